```python
import math
import jax, jax.numpy as jnp
from jax import lax
import numpy as np

D_MODEL = 2048
BATCH = 2
SEQ = 4096
DEPTH = 2
DEC_BATCH = 8
DEC_SEQ = 1
PAST_LEN = 16384
PAGE_SIZE = 128

HEAD_DIM = 64
HEADS_PER_GROUP = 4
ATTN_GROUPS = ((128, 1), (512, 4), (2048, 16))
N_ATTN_GROUPS = len(ATTN_GROUPS)
N_HEADS = N_ATTN_GROUPS * HEADS_PER_GROUP
ATTN_WIDTH = N_HEADS * HEAD_DIM
ATTN_OUT = HEADS_PER_GROUP * HEAD_DIM
N_DIL_KEYS = 128
N_BUCKETS = 32
MAX_DISTANCE = 2048
ATTN_SCALE = HEAD_DIM ** -0.5
POOL_WINDOWS = (2, 4, 8, 16)
POOL_GROUP = D_MODEL // 16
POOL_WIDTH = len(POOL_WINDOWS) * POOL_GROUP
POOL_STATE = max(POOL_WINDOWS) - 1
CONV_CH = D_MODEL // 4
CONV_K = 3
D_FF = 5632
FFN_CONV_K = 3
N_BRANCH = 3
N_IN = 3 * ATTN_WIDTH + POOL_WIDTH + 3 * CONV_CH + N_BRANCH * D_MODEL
SPLITS = (ATTN_WIDTH, 2 * ATTN_WIDTH, 3 * ATTN_WIDTH,
          3 * ATTN_WIDTH + POOL_WIDTH,
          3 * ATTN_WIDTH + POOL_WIDTH + CONV_CH,
          3 * ATTN_WIDTH + POOL_WIDTH + 2 * CONV_CH,
          3 * ATTN_WIDTH + POOL_WIDTH + 3 * CONV_CH)
N_MOD = 6
EPS = 1e-6
NEG_INF = -1e30

kernel_name = 'gated_hybrid_pool_conv_dilated_attn_step'


def rms_norm(x, g):
    xf = x.astype(jnp.float32)
    y = xf * lax.rsqrt(jnp.mean(xf * xf, axis=-1, keepdims=True) + EPS)
    return (y * g.astype(jnp.float32)).astype(x.dtype)


def causal_dwconv(u_ext, w):
    k = w.shape[0]
    n = u_ext.shape[1] - (k - 1)
    return sum(u_ext[:, i:i + n] * w[i] for i in range(k))


def group_rel_bias(rel_bias):
    n = np.arange(N_DIL_KEYS + 1)
    max_exact = N_BUCKETS // 2
    out = []
    for g, (_, d) in enumerate(ATTN_GROUPS):
        dist = n * d
        large = max_exact + (np.log(np.maximum(dist, 1) / max_exact) / math.log(MAX_DISTANCE / max_exact)
                             * (N_BUCKETS - max_exact)).astype(np.int32)
        bucket = np.where(dist < max_exact, dist, np.minimum(large, N_BUCKETS - 1))
        out.append(rel_bias[bucket, g * HEADS_PER_GROUP:(g + 1) * HEADS_PER_GROUP].T)
    return jnp.stack(out)


def dilated_band_attention(q, k, v, bias, dilation):
    b, s, h, dh = q.shape
    n = N_DIL_KEYS
    L = s // dilation
    nb = -(-L // n)
    lp = nb * n

    def split(t):
        t = t.reshape(b, L, dilation, h, dh).transpose(0, 2, 1, 3, 4)
        return jnp.pad(t, ((0, 0), (0, 0), (0, lp - L), (0, 0), (0, 0)))

    def kblocks(t):
        t = jnp.pad(split(t), ((0, 0), (0, 0), (n, 0), (0, 0), (0, 0))).reshape(b, dilation, nb + 1, n, h, dh)
        return jnp.concatenate([t[:, :, :-1], t[:, :, 1:]], axis=3)

    qs = split(q).reshape(b, dilation, nb, n, h, dh)
    kb, vb = kblocks(k), kblocks(v)
    a_idx = np.arange(n)[:, None]
    c_idx = np.arange(2 * n)[None, :]
    dist = a_idx + n - c_idx
    band = (dist >= 0) & (dist <= n)
    key_idx = np.arange(nb)[:, None] * n - n + np.arange(2 * n)[None, :]
    valid = band[None] & (key_idx >= 0)[:, None, :]
    bias_qk = bias[:, np.clip(dist, 0, n)].astype(jnp.float32)
    s_ = jnp.einsum('brnqhd,brnkhd->brnhqk', qs, kb, preferred_element_type=jnp.float32) * ATTN_SCALE
    s_ = jnp.where(valid[None, None, :, None], s_ + bias_qk, NEG_INF)
    m = jnp.max(s_, axis=-1, keepdims=True)
    p = jnp.exp(s_ - m)
    l = jnp.sum(p, axis=-1, keepdims=True)
    o = jnp.einsum('brnhqk,brnkhd->brnqhd', (p / l).astype(v.dtype), vb)
    lse = (m + jnp.log(l))[..., 0]
    o = o.reshape(b, dilation, lp, h, dh)[:, :, :L].transpose(0, 2, 1, 3, 4).reshape(b, s, h, dh)
    lse = lse.transpose(0, 1, 2, 4, 3).reshape(b, dilation, lp, h)[:, :, :L].transpose(0, 2, 1, 3).reshape(b, s, h)
    return o, lse


def dilated_cache_attention(q, k_ext, v_ext, bias, dilation, n_past):
    t = q.shape[1]
    idx = n_past + np.arange(t)[:, None] - dilation * np.arange(N_DIL_KEYS + 1)[None, :]
    valid = idx >= 0
    idx = np.maximum(idx, 0)
    kg, vg = k_ext[:, idx], v_ext[:, idx]
    s_ = jnp.einsum('bthd,btnhd->bhtn', q, kg, preferred_element_type=jnp.float32) * ATTN_SCALE
    s_ = jnp.where(valid[None, None], s_ + bias[:, None, :].astype(jnp.float32), NEG_INF)
    m = jnp.max(s_, axis=-1, keepdims=True)
    p = jnp.exp(s_ - m)
    l = jnp.sum(p, axis=-1, keepdims=True)
    o = jnp.einsum('bhtn,btnhd->bthd', (p / l).astype(v_ext.dtype), vg)
    lse = (m + jnp.log(l))[..., 0].transpose(0, 2, 1)
    return o, lse


def combine_groups(outs, lses):
    o = jnp.stack(outs, axis=2)
    wgt = jax.nn.softmax(jnp.stack(lses, axis=2), axis=2)
    return jnp.einsum('btgh,btghd->bthd', wgt.astype(o.dtype), o)


def pool_prompt(p):
    _, s, _ = p.shape
    cs = jnp.pad(jnp.cumsum(p.astype(jnp.float32), axis=1), ((0, 0), (1, 0), (0, 0)))
    t = np.arange(s)
    means = []
    for gi, w in enumerate(POOL_WINDOWS):
        lo = np.maximum(t + 1 - w, 0)
        cg = cs[..., gi * POOL_GROUP:(gi + 1) * POOL_GROUP]
        cnt = (t + 1 - lo).astype(np.float32)
        means.append((cg[:, t + 1] - cg[:, lo]) / cnt[None, :, None])
    return jnp.concatenate(means, axis=-1) - p.astype(jnp.float32)


def pool_step(state, p):
    ext = jnp.concatenate([state, p], axis=1)
    cs = jnp.pad(jnp.cumsum(ext.astype(jnp.float32), axis=1), ((0, 0), (1, 0), (0, 0)))
    m = POOL_STATE + np.arange(p.shape[1])
    means = []
    for gi, w in enumerate(POOL_WINDOWS):
        cg = cs[..., gi * POOL_GROUP:(gi + 1) * POOL_GROUP]
        means.append((cg[:, m + 1] - cg[:, m + 1 - w]) / w)
    return jnp.concatenate(means, axis=-1) - p.astype(jnp.float32), ext[:, -POOL_STATE:]


def run_layer(x, c, lw, bias_g, hist):
    (norm_g, w_ada, b_ada, w_in, w_attn_br, w_pool_grp, pool_scale, w_pool_br,
     conv_w, w_conv_br, w_o, w_up, ffn_conv_w, w_down) = lw
    nb, t, _ = x.shape
    mod = (jax.nn.silu(c) @ w_ada + b_ada).reshape(nb, N_MOD, 1, D_MODEL)
    h = rms_norm(x, norm_g[0]) * (1 + mod[:, 1]) + mod[:, 0]
    q, k, v, p, gb, gc, hc, gl = jnp.split(h @ w_in, SPLITS, axis=-1)
    hs = (nb, t, N_ATTN_GROUPS, HEADS_PER_GROUP, HEAD_DIM)
    q, k, v = q.reshape(hs), k.reshape(hs), v.reshape(hs)
    u = gc * hc
    outs, lses = [], []
    if hist is None:
        for g, (_, d) in enumerate(ATTN_GROUPS):
            o, lse = dilated_band_attention(q[:, :, g], k[:, :, g], v[:, :, g], bias_g[g], d)
            outs.append(o)
            lses.append(lse)
        new_kv = [jnp.stack([k[:, -min(w, t):, g], v[:, -min(w, t):, g]], axis=2)
                  for g, (w, _) in enumerate(ATTN_GROUPS)]
        pm = pool_prompt(p)
        new_pool = p[:, -POOL_STATE:]
        u_ext = jnp.pad(u, ((0, 0), (CONV_K - 1, 0), (0, 0)))
    else:
        kv_caches, pool_state, conv_state, ffn_state = hist
        for g, (_, d) in enumerate(ATTN_GROUPS):
            cache = kv_caches[g]
            k_ext = jnp.concatenate([cache[:, :, 0], k[:, :, g]], axis=1)
            v_ext = jnp.concatenate([cache[:, :, 1], v[:, :, g]], axis=1)
            o, lse = dilated_cache_attention(q[:, :, g], k_ext, v_ext, bias_g[g], d, cache.shape[1])
            outs.append(o)
            lses.append(lse)
        new_kv = [jnp.stack([k[:, :, g], v[:, :, g]], axis=2) for g in range(N_ATTN_GROUPS)]
        pm, new_pool = pool_step(pool_state, p)
        u_ext = jnp.concatenate([conv_state, u], axis=1)
    br_c = combine_groups(outs, lses).reshape(nb, t, ATTN_OUT) @ w_attn_br
    pg = pm.astype(x.dtype).reshape(nb, t, len(POOL_WINDOWS), POOL_GROUP)
    pz = jnp.einsum('btgc,gce->btge', pg, w_pool_grp).reshape(nb, t, POOL_WIDTH)
    br_a = (pz * pool_scale) @ w_pool_br
    br_b = (gb * causal_dwconv(u_ext, conv_w)) @ w_conv_br
    new_conv = u_ext[:, -(CONV_K - 1):]
    gates = jax.nn.sigmoid(gl.reshape(nb, t, N_BRANCH, D_MODEL))
    merged = gates[:, :, 0] * br_a + gates[:, :, 1] * br_b + gates[:, :, 2] * br_c
    x = x + mod[:, 2] * rms_norm(merged @ w_o, norm_g[1])
    h2 = rms_norm(x, norm_g[2]) * (1 + mod[:, 4]) + mod[:, 3]
    up = h2 @ w_up
    if hist is None:
        up_ext = jnp.pad(up, ((0, 0), (FFN_CONV_K - 1, 0), (0, 0)))
    else:
        up_ext = jnp.concatenate([hist[3], up], axis=1)
    uc = causal_dwconv(up_ext, ffn_conv_w)
    f = (jax.nn.gelu(uc[..., :D_FF], approximate=True) * uc[..., D_FF:]) @ w_down
    new_ffn = up_ext[:, -(FFN_CONV_K - 1):]
    x = x + mod[:, 5] * rms_norm(f, norm_g[3])
    return x, (new_kv[0], new_kv[1], new_kv[2], new_pool, new_conv, new_ffn)


def setup_inputs(seed: int = 0) -> dict:
    key = jax.random.key(seed)
    ks = jax.random.split(key, 24)
    f32 = jnp.float32
    nrm = lambda k, shape, scale: jax.random.normal(k, shape, f32) * scale
    lc = [min(w, PAST_LEN) for w, _ in ATTN_GROUPS]
    return {
        'x_prompt': nrm(ks[0], (BATCH, SEQ, D_MODEL), 1.0),
        'x_sample': nrm(ks[1], (DEC_BATCH, DEC_SEQ, D_MODEL), 1.0),
        'c_prompt': nrm(ks[2], (BATCH, D_MODEL), 1.0),
        'c_sample': nrm(ks[3], (DEC_BATCH, D_MODEL), 1.0),
        'cache_kv_w128': nrm(ks[4], (DEPTH, DEC_BATCH, lc[0], 2, HEADS_PER_GROUP, HEAD_DIM), 1.0),
        'cache_kv_w512': nrm(ks[5], (DEPTH, DEC_BATCH, lc[1], 2, HEADS_PER_GROUP, HEAD_DIM), 1.0),
        'cache_kv_w2048': nrm(ks[6], (DEPTH, DEC_BATCH, lc[2], 2, HEADS_PER_GROUP, HEAD_DIM), 1.0),
        'state_pool': nrm(ks[7], (DEPTH, DEC_BATCH, POOL_STATE, POOL_WIDTH), 1.0),
        'state_conv': nrm(ks[8], (DEPTH, DEC_BATCH, CONV_K - 1, CONV_CH), 1.0),
        'state_ffn_conv': nrm(ks[9], (DEPTH, DEC_BATCH, FFN_CONV_K - 1, 2 * D_FF), 1.0),
        'rel_bias': nrm(ks[10], (N_BUCKETS, N_HEADS), 0.5),
        'norm_g': 1.0 + nrm(ks[11], (DEPTH, 4, D_MODEL), 0.02),
        'w_ada': nrm(ks[12], (DEPTH, D_MODEL, N_MOD * D_MODEL), 0.5 * D_MODEL ** -0.5),
        'b_ada': nrm(ks[13], (DEPTH, N_MOD * D_MODEL), 0.01),
        'w_in': nrm(ks[14], (DEPTH, D_MODEL, N_IN), D_MODEL ** -0.5),
        'w_attn_br': nrm(ks[15], (DEPTH, ATTN_OUT, D_MODEL), ATTN_OUT ** -0.5),
        'w_pool_grp': nrm(ks[16], (DEPTH, len(POOL_WINDOWS), POOL_GROUP, POOL_GROUP), POOL_GROUP ** -0.5),
        'pool_scale': 1.0 + nrm(ks[17], (DEPTH, POOL_WIDTH), 0.1),
        'w_pool_br': nrm(ks[18], (DEPTH, POOL_WIDTH, D_MODEL), POOL_WIDTH ** -0.5),
        'conv_w': nrm(ks[19], (DEPTH, CONV_K, CONV_CH), CONV_K ** -0.5),
        'w_conv_br': nrm(ks[20], (DEPTH, CONV_CH, D_MODEL), CONV_CH ** -0.5),
        'w_o': nrm(ks[21], (DEPTH, D_MODEL, D_MODEL), D_MODEL ** -0.5),
        'w_up': nrm(ks[22], (DEPTH, D_MODEL, 2 * D_FF), D_MODEL ** -0.5),
        'ffn_conv_w': nrm(jax.random.fold_in(ks[23], 1), (DEPTH, FFN_CONV_K, 2 * D_FF), FFN_CONV_K ** -0.5),
        'w_down': nrm(jax.random.fold_in(ks[23], 2), (DEPTH, D_FF, D_MODEL), D_FF ** -0.5),
    }


def reference(x_prompt, x_sample, c_prompt, c_sample, cache_kv_w128, cache_kv_w512, cache_kv_w2048,
              state_pool, state_conv, state_ffn_conv, rel_bias, norm_g, w_ada, b_ada, w_in, w_attn_br,
              w_pool_grp, pool_scale, w_pool_br, conv_w, w_conv_br, w_o, w_up, ffn_conv_w, w_down):
    bias_g = group_rel_bias(rel_bias)
    yp, ys = x_prompt, x_sample
    st_p, st_s = [], []
    for l in range(DEPTH):
        lw = (norm_g[l], w_ada[l], b_ada[l], w_in[l], w_attn_br[l], w_pool_grp[l], pool_scale[l],
              w_pool_br[l], conv_w[l], w_conv_br[l], w_o[l], w_up[l], ffn_conv_w[l], w_down[l])
        yp, sp = run_layer(yp, c_prompt, lw, bias_g, None)
        hist = ((cache_kv_w128[l], cache_kv_w512[l], cache_kv_w2048[l]), state_pool[l], state_conv[l], state_ffn_conv[l])
        ys, ss = run_layer(ys, c_sample, lw, bias_g, hist)
        st_p.append(sp)
        st_s.append(ss)
    kv128_p, kv512_p, kv2048_p, pool_p, conv_p, ffn_p = [jnp.stack([s[i] for s in st_p]) for i in range(6)]
    kv128_s, kv512_s, kv2048_s, pool_s, conv_s, ffn_s = [jnp.stack([s[i] for s in st_s]) for i in range(6)]
    return (yp, ys, kv128_p, kv512_p, kv2048_p, pool_p, conv_p, ffn_p,
            kv128_s, kv512_s, kv2048_s, pool_s, conv_s, ffn_s)
```

```python
import functools

import numpy as np
import jax
import jax.numpy as jnp
from jax import lax
from jax.experimental import pallas as pl
from jax.experimental.pallas import tpu as pltpu

F32 = jnp.float32
BF16 = jnp.bfloat16

D_MODEL = 2048
DEPTH = 2
HEAD_DIM = 64
HEADS_PER_GROUP = 4
ATTN_GROUPS = ((128, 1), (512, 4), (2048, 16))
N_ATTN_GROUPS = len(ATTN_GROUPS)
ATTN_WIDTH = N_ATTN_GROUPS * HEADS_PER_GROUP * HEAD_DIM
GROUP_WIDTH = HEADS_PER_GROUP * HEAD_DIM
N_DIL_KEYS = 128
N_BUCKETS = 32
MAX_DISTANCE = 2048
ATTN_SCALE = HEAD_DIM ** -0.5
POOL_WINDOWS = (2, 4, 8, 16)
POOL_GROUP = 128
POOL_WIDTH = 512
POOL_STATE = 15
CONV_CH = 512
CONV_K = 3
D_FF = 5632
N_MOD = 6
EPS = 1e-6
NEG_INF = -1e30

LANES = 128
HIST = 16
SAMPLE_ROWS = 16
QKV_WIDTH = 3 * ATTN_WIDTH
QKV_SLABS = QKV_WIDTH // LANES
REST_WIDTH = POOL_WIDTH + 3 * CONV_CH + 3 * D_MODEL
ACT_WIDTH = POOL_WIDTH + 3 * CONV_CH
VMEM_CAP = 56 * 1024 * 1024


def _vmem_limit(block_bytes, scratch_bytes=0):
    est = 2 * block_bytes + scratch_bytes
    return int(min(VMEM_CAP, est + est // 4 + (4 << 20)))


def _nbytes(shape, dtype):
    return int(np.prod(shape)) * jnp.dtype(dtype).itemsize


def _rms(x, g):
    return x * lax.rsqrt(jnp.mean(x * x, axis=-1, keepdims=True) + EPS) * g


def _sigmoid(x):
    return 1.0 / (1.0 + jnp.exp(-x))


def _gelu_tanh(x):
    return 0.5 * x * (1.0 + jnp.tanh(np.sqrt(2.0 / np.pi) * (x + 0.044715 * (x * x * x))))


def _dot(a, b):
    return jnp.dot(a, b, preferred_element_type=F32)


def _ada_kernel(c_ref, w_ref, b_ref, o_ref):
    c = c_ref[...]
    s = (c * _sigmoid(c)).astype(BF16)
    o_ref[0] = _dot(s, w_ref[0].astype(BF16)) + b_ref[0]


def _ada(c_all, w_ada, b_ada):
    rows = c_all.shape[0]
    n = w_ada.shape[-1]
    tn = 1024
    blocks = _nbytes((rows, D_MODEL), F32) + _nbytes((D_MODEL, tn), F32) + _nbytes((rows + 1, tn), F32)
    return pl.pallas_call(
        _ada_kernel,
        out_shape=jax.ShapeDtypeStruct((DEPTH, rows, n), F32),
        grid=(DEPTH, n // tn),
        in_specs=[
            pl.BlockSpec((rows, D_MODEL), lambda l, j: (0, 0)),
            pl.BlockSpec((1, D_MODEL, tn), lambda l, j: (l, 0, j)),
            pl.BlockSpec((1, 1, tn), lambda l, j: (l, 0, j)),
        ],
        out_specs=pl.BlockSpec((1, rows, tn), lambda l, j: (l, 0, j)),
        compiler_params=pltpu.CompilerParams(
            dimension_semantics=("parallel", "parallel"),
            vmem_limit_bytes=_vmem_limit(blocks, _nbytes((D_MODEL, tn), BF16))),
        name="ada",
    )(c_all, w_ada, b_ada.reshape(DEPTH, 1, n))


def _proj_kernel(x_ref, g_ref, sc_ref, sh_ref, w_ref, o_ref, h_ref, *, slabs):
    @pl.when(pl.program_id(2) == 0)
    def _():
        y = _rms(x_ref[0], g_ref[...])
        h_ref[...] = (y * (1.0 + sc_ref[0]) + sh_ref[0]).astype(BF16)

    res = _dot(h_ref[...], w_ref[...])
    if slabs:
        for s in range(slabs):
            o_ref[0, s] = res[:, s * LANES:(s + 1) * LANES]
    else:
        o_ref[0] = res


def _proj(x, g, sc, sh, w, *, tm, tn, slab_out):
    b, s, d = x.shape
    n = w.shape[1]
    r = sc.shape[1]
    slabs = tn // LANES if slab_out else 0
    if slab_out:
        out_shape = jax.ShapeDtypeStruct((b, n // LANES, s, LANES), F32)
        out_spec = pl.BlockSpec((1, slabs, tm, LANES), lambda bi, i, j: (bi, j, i, 0))
    else:
        out_shape = jax.ShapeDtypeStruct((b, s, n), F32)
        out_spec = pl.BlockSpec((1, tm, tn), lambda bi, i, j: (bi, i, j))
    blocks = (_nbytes((tm, d), F32) + _nbytes((2 * r + 1, d), F32) + _nbytes((d, tn), BF16)
              + _nbytes((tm, tn), F32))
    return pl.pallas_call(
        functools.partial(_proj_kernel, slabs=slabs),
        out_shape=out_shape,
        grid=(b, s // tm, n // tn),
        in_specs=[
            pl.BlockSpec((1, tm, d), lambda bi, i, j: (bi, i, 0)),
            pl.BlockSpec((1, d), lambda bi, i, j: (0, 0)),
            pl.BlockSpec((1, r, d), lambda bi, i, j: (bi, 0, 0)),
            pl.BlockSpec((1, r, d), lambda bi, i, j: (bi, 0, 0)),
            pl.BlockSpec((d, tn), lambda bi, i, j: (0, j)),
        ],
        out_specs=out_spec,
        scratch_shapes=[pltpu.VMEM((tm, d), BF16)],
        compiler_params=pltpu.CompilerParams(
            dimension_semantics=("parallel", "parallel", "arbitrary"),
            vmem_limit_bytes=_vmem_limit(blocks, _nbytes((tm, d), BF16) + _nbytes((tm, tn), F32))),
        name="proj_slab" if slab_out else "proj",
    )(x, g, sc, sh, w)


def _head_masks(rows):
    lane = lax.broadcasted_iota(jnp.int32, (rows, GROUP_WIDTH), 1)
    return [(lane >= h * HEAD_DIM) & (lane < (h + 1) * HEAD_DIM) for h in range(HEADS_PER_GROUP)]


def _attn_kernel(q_ref, kc_ref, kp_ref, vc_ref, vp_ref, bias_ref, o_ref, lse_ref, *, d, sb):
    i = pl.program_id(1)
    nq = N_DIL_KEYS
    span = nq * d
    ncb = sb // span
    hm = _head_masks(1)
    col = lax.broadcasted_iota(jnp.int32, (1, 2 * nq), 1)
    bias = bias_ref[0]

    def rows(start):
        return pl.ds(start, nq, stride=d) if d > 1 else pl.ds(start, nq)

    def load(ref, start):
        return jnp.concatenate([ref[0, s, rows(start), :] for s in range(2)], axis=1)

    def one_block(r, jb):
        qs = jb * span + r
        q = load(q_ref, qs) * ATTN_SCALE
        if jb == 0:
            lo = sb - span + r
            k_lo, v_lo = load(kp_ref, lo), load(vp_ref, lo)
        else:
            lo = (jb - 1) * span + r
            k_lo, v_lo = load(kc_ref, lo), load(vc_ref, lo)
        kcat = jnp.concatenate([k_lo, load(kc_ref, qs)], axis=0).astype(BF16)
        vcat = jnp.concatenate([v_lo, load(vc_ref, qs)], axis=0).astype(BF16)
        qm = jnp.concatenate([jnp.where(hm[h], q, 0.0) for h in range(HEADS_PER_GROUP)], axis=0).astype(BF16)
        s = lax.dot_general(qm, kcat, (((1,), (1,)), ((), ())), preferred_element_type=F32) + bias
        if jb == 0:
            s = jnp.where((col < nq) & (i == 0), NEG_INF, s)
        m = jnp.max(s, axis=-1, keepdims=True)
        p = jnp.exp(s - m)
        l = jnp.sum(p, axis=-1, keepdims=True)
        oall = _dot((p * (1.0 / l)).astype(BF16), vcat)
        lse = m + jnp.log(l)
        o = jnp.zeros((nq, GROUP_WIDTH), F32)
        ls = jnp.zeros((nq, GROUP_WIDTH), F32)
        for h in range(HEADS_PER_GROUP):
            o = jnp.where(hm[h], oall[h * nq:(h + 1) * nq], o)
            ls = jnp.where(hm[h], lse[h * nq:(h + 1) * nq], ls)
        for sl in range(2):
            o_ref[0, sl, rows(qs), :] = o[:, sl * LANES:(sl + 1) * LANES]
            lse_ref[0, sl, rows(qs), :] = ls[:, sl * LANES:(sl + 1) * LANES]

    if d == 1:
        for jb in range(ncb):
            one_block(0, jb)
    else:
        def body(r, carry):
            for jb in range(ncb):
                one_block(r, jb)
            return carry
        lax.fori_loop(0, d, body, 0)


def _attention(qkv, bias_tab, g):
    b, _, s, _ = qkv.shape
    d = ATTN_GROUPS[g][1]
    sb = max(N_DIL_KEYS * d, 512)
    blk = (1, 2, sb, LANES)
    kslab, vslab = ATTN_WIDTH // GROUP_WIDTH + g, 2 * ATTN_WIDTH // GROUP_WIDTH + g
    prev = lambda i: jnp.maximum(i - 1, 0)
    blocks = 7 * _nbytes(blk, F32) + _nbytes((4 * N_DIL_KEYS, 2 * N_DIL_KEYS), F32)
    out = jax.ShapeDtypeStruct((b, 2, s, LANES), F32)
    return pl.pallas_call(
        functools.partial(_attn_kernel, d=d, sb=sb),
        out_shape=(out, out),
        grid=(b, s // sb),
        in_specs=[
            pl.BlockSpec(blk, lambda bi, i: (bi, g, i, 0)),
            pl.BlockSpec(blk, lambda bi, i: (bi, kslab, i, 0)),
            pl.BlockSpec(blk, lambda bi, i: (bi, kslab, prev(i), 0)),
            pl.BlockSpec(blk, lambda bi, i: (bi, vslab, i, 0)),
            pl.BlockSpec(blk, lambda bi, i: (bi, vslab, prev(i), 0)),
            pl.BlockSpec((1, 4 * N_DIL_KEYS, 2 * N_DIL_KEYS), lambda bi, i: (g, 0, 0)),
        ],
        out_specs=(pl.BlockSpec(blk, lambda bi, i: (bi, 0, i, 0)),
                   pl.BlockSpec(blk, lambda bi, i: (bi, 0, i, 0))),
        compiler_params=pltpu.CompilerParams(
            dimension_semantics=("parallel", "parallel"),
            vmem_limit_bytes=_vmem_limit(blocks, 8 << 20)),
        name=f"attn_d{d}",
    )(qkv, qkv, qkv, qkv, qkv, bias_tab)


def _attn_step_kernel(qkv_ref, c0_ref, c1_ref, c2_ref, bc_ref, bn_ref, o_ref):
    b = pl.program_id(0)
    rows = SAMPLE_ROWS
    lane = lax.broadcasted_iota(jnp.int32, (rows, GROUP_WIDTH), 1)
    row = lax.broadcasted_iota(jnp.int32, (rows, GROUP_WIDTH), 0)
    sel = (lane >= row * HEAD_DIM) & (lane < (row + 1) * HEAD_DIM)

    def rowvec(slab):
        return jnp.concatenate([qkv_ref[0, slab + s, pl.ds(b, 1), :] for s in range(2)], axis=1)

    outs, lses = [], []
    for g, c_ref in enumerate((c0_ref, c1_ref, c2_ref)):
        q = rowvec(2 * g) * ATTN_SCALE
        kn = rowvec(QKV_SLABS // 3 + 2 * g).astype(BF16).astype(F32)
        vn = rowvec(2 * QKV_SLABS // 3 + 2 * g).astype(BF16).astype(F32)
        q4 = jnp.where(sel, jnp.broadcast_to(q, (rows, GROUP_WIDTH)), 0.0).astype(BF16)
        kc = c_ref[0, 0, :, 0:GROUP_WIDTH].astype(BF16)
        vc = c_ref[0, 0, :, GROUP_WIDTH:2 * GROUP_WIDTH].astype(BF16)
        s_c = lax.dot_general(q4, kc, (((1,), (1,)), ((), ())), preferred_element_type=F32) + bc_ref[g]
        s_n = jnp.sum(q4.astype(F32) * kn, axis=-1, keepdims=True) + bn_ref[g][:, 0:1]
        m = jnp.maximum(jnp.max(s_c, axis=-1, keepdims=True), s_n)
        p_c = jnp.exp(s_c - m)
        p_n = jnp.exp(s_n - m)
        l = jnp.sum(p_c, axis=-1, keepdims=True) + p_n
        inv = 1.0 / l
        oc = _dot((p_c * inv).astype(BF16), vc) + (p_n * inv).astype(BF16).astype(F32) * vn
        lse = m + jnp.log(l)
        outs.append(jnp.sum(jnp.where(sel, oc, 0.0), axis=0, keepdims=True))
        lses.append(jnp.sum(jnp.where(sel, lse, 0.0), axis=0, keepdims=True))
    mx = jnp.maximum(jnp.maximum(lses[0], lses[1]), lses[2])
    w = [jnp.exp(ls - mx) for ls in lses]
    tot = w[0] + w[1] + w[2]
    o_ref[0] = (w[0] * outs[0] + w[1] * outs[1] + w[2] * outs[2]) * (1.0 / tot)


def _attention_step(qkv, caches, bias_c, bias_n, nb):
    cache_specs = [pl.BlockSpec((1, 1, N_DIL_KEYS, 2 * GROUP_WIDTH), lambda bi: (bi, 0, 0, 0))
                   for _ in range(N_ATTN_GROUPS)]
    blocks = (_nbytes(qkv.shape, F32) + 3 * _nbytes((N_DIL_KEYS, 2 * GROUP_WIDTH), F32)
              + 2 * _nbytes(bias_c.shape, F32))
    return pl.pallas_call(
        _attn_step_kernel,
        out_shape=jax.ShapeDtypeStruct((nb, 1, GROUP_WIDTH), F32),
        grid=(nb,),
        in_specs=[pl.BlockSpec(qkv.shape, lambda bi: (0, 0, 0, 0))] + cache_specs + [
            pl.BlockSpec(bias_c.shape, lambda bi: (0, 0, 0)),
            pl.BlockSpec(bias_n.shape, lambda bi: (0, 0, 0)),
        ],
        out_specs=pl.BlockSpec((1, 1, GROUP_WIDTH), lambda bi: (bi, 0, 0)),
        compiler_params=pltpu.CompilerParams(
            dimension_semantics=("parallel",),
            vmem_limit_bytes=_vmem_limit(blocks, 4 << 20)),
        name="attn_step",
    )(qkv, *[c.reshape(nb, 1, N_DIL_KEYS, -1) for c in caches], bias_c, bias_n)


def _branch_kernel(*refs, tm, step):
    if step:
        (act_ref, pst_ref, cst_ref, ga_ref, gb_ref, gc_ref, at_ref,
         wgrp_ref, pscale_ref, wpb_ref, cw_ref, wcb_ref, wab_ref,
         out_ref, u_ref, pzs_ref, cb_ref, comb_ref) = refs
    else:
        (act_ref, hist_ref, ga_ref, gb_ref, gc_ref, o0_ref, o1_ref, o2_ref, l0_ref, l1_ref, l2_ref,
         wgrp_ref, pscale_ref, wpb_ref, cw_ref, wcb_ref, wab_ref,
         out_ref, u_ref, pzs_ref, cb_ref, comb_ref, pe_ref, ue_ref) = refs
    i = pl.program_id(1)

    @pl.when(pl.program_id(2) == 0)
    def _():
        p = act_ref[0, :, 0:POOL_WIDTH]
        gate_b = act_ref[0, :, POOL_WIDTH:POOL_WIDTH + CONV_CH]
        u = act_ref[0, :, POOL_WIDTH + CONV_CH:POOL_WIDTH + 2 * CONV_CH] * \
            act_ref[0, :, POOL_WIDTH + 2 * CONV_CH:POOL_WIDTH + 3 * CONV_CH]
        cw = cw_ref[...]
        if step:
            acc = p
            sums = {}
            for k in range(1, max(POOL_WINDOWS)):
                acc = acc + pst_ref[POOL_STATE - k]
                sums[k + 1] = acc
            means = [sums[w][:, gi * POOL_GROUP:(gi + 1) * POOL_GROUP] * (1.0 / w)
                     for gi, w in enumerate(POOL_WINDOWS)]
            conv = cst_ref[0] * cw[0:1] + cst_ref[1] * cw[1:2] + u * cw[2:3]
            u_ref[0, 0] = u
            comb_ref[...] = at_ref[0].astype(BF16)
        else:
            first = i == 0
            hist_p = hist_ref[0, :, 0:POOL_WIDTH]
            hist_u = hist_ref[0, :, POOL_WIDTH + CONV_CH:POOL_WIDTH + 2 * CONV_CH] * \
                hist_ref[0, :, POOL_WIDTH + 2 * CONV_CH:POOL_WIDTH + 3 * CONV_CH]
            pe_ref[0:HIST] = jnp.where(first, 0.0, hist_p)
            ue_ref[0:HIST] = jnp.where(first, 0.0, hist_u)
            pe_ref[HIST:HIST + tm] = p
            ue_ref[HIST:HIST + tm] = u
            t = i * tm + lax.broadcasted_iota(jnp.int32, (tm, 1), 0)
            means = []
            for gi, w in enumerate(POOL_WINDOWS):
                cs = slice(gi * POOL_GROUP, (gi + 1) * POOL_GROUP)
                acc = pe_ref[HIST:HIST + tm, cs]
                for k in range(1, w):
                    acc = acc + pe_ref[HIST - k:HIST - k + tm, cs]
                cnt = jnp.minimum(t + 1, w).astype(F32)
                means.append(acc * (1.0 / cnt))
            conv = (ue_ref[HIST - 2:HIST - 2 + tm] * cw[0:1] + ue_ref[HIST - 1:HIST - 1 + tm] * cw[1:2]
                    + u * cw[2:3])
            u_ref[0, 0] = ue_ref[HIST + tm - 8:HIST + tm]
            for sl in range(2):
                ls = [r[0, sl] for r in (l0_ref, l1_ref, l2_ref)]
                os_ = [r[0, sl] for r in (o0_ref, o1_ref, o2_ref)]
                mx = jnp.maximum(jnp.maximum(ls[0], ls[1]), ls[2])
                w_ = [jnp.exp(x - mx) for x in ls]
                tot = w_[0] + w_[1] + w_[2]
                comb = (w_[0] * os_[0] + w_[1] * os_[1] + w_[2] * os_[2]) * (1.0 / tot)
                comb_ref[:, sl * LANES:(sl + 1) * LANES] = comb.astype(BF16)
        for gi in range(len(POOL_WINDOWS)):
            cs = slice(gi * POOL_GROUP, (gi + 1) * POOL_GROUP)
            pm = (means[gi] - p[:, cs]).astype(BF16)
            pz = _dot(pm, wgrp_ref[gi])
            pzs_ref[:, cs] = (pz * pscale_ref[:, cs]).astype(BF16)
        cb_ref[...] = (gate_b * conv).astype(BF16)

    br_a = _dot(pzs_ref[...], wpb_ref[...])
    br_b = _dot(cb_ref[...], wcb_ref[...])
    br_c = _dot(comb_ref[...], wab_ref[...])
    merged = _sigmoid(ga_ref[0]) * br_a + _sigmoid(gb_ref[0]) * br_b + _sigmoid(gc_ref[0]) * br_c
    out_ref[0] = merged.astype(BF16)


def _branches(rest, attn, states, weights, *, tm, tc, step):
    b, s, _ = rest.shape
    wgrp, pscale, wpb, cw, wcb, wab = weights
    ni = s // tm
    gblk = ACT_WIDTH // tc
    gate_specs = [pl.BlockSpec((1, tm, tc), lambda bi, i, c, k=k: (bi, i, gblk + k * (D_MODEL // tc) + c))
                  for k in range(3)]
    w_specs = [
        pl.BlockSpec(wgrp.shape, lambda bi, i, c: (0, 0, 0)),
        pl.BlockSpec(pscale.shape, lambda bi, i, c: (0, 0)),
        pl.BlockSpec((POOL_WIDTH, tc), lambda bi, i, c: (0, c)),
        pl.BlockSpec(cw.shape, lambda bi, i, c: (0, 0)),
        pl.BlockSpec((CONV_CH, tc), lambda bi, i, c: (0, c)),
        pl.BlockSpec((GROUP_WIDTH, tc), lambda bi, i, c: (0, c)),
    ]
    act_spec = pl.BlockSpec((1, tm, ACT_WIDTH), lambda bi, i, c: (bi, i, 0))
    scratch = [pltpu.VMEM((tm, POOL_WIDTH), BF16), pltpu.VMEM((tm, CONV_CH), BF16),
               pltpu.VMEM((tm, GROUP_WIDTH), BF16)]
    blocks = (_nbytes((tm, ACT_WIDTH), F32) + 3 * _nbytes((tm, tc), F32) + _nbytes(wgrp.shape, BF16)
              + _nbytes((POOL_WIDTH + CONV_CH + GROUP_WIDTH, tc), BF16) + _nbytes((tm, tc), BF16))
    if step:
        pst, cst = states
        ins = [rest, pst, cst, rest, rest, rest, attn]
        in_specs = [act_spec,
                    pl.BlockSpec(pst.shape, lambda bi, i, c: (0, 0, 0)),
                    pl.BlockSpec(cst.shape, lambda bi, i, c: (0, 0, 0))] + gate_specs + [
                    pl.BlockSpec((1, tm, GROUP_WIDTH), lambda bi, i, c: (bi, i, 0))]
        urows = tm
        blocks += _nbytes(pst.shape, F32) + _nbytes(cst.shape, F32)
    else:
        hblk = tm // HIST
        slab = pl.BlockSpec((1, 2, tm, LANES), lambda bi, i, c: (bi, 0, i, 0))
        ins = [rest, rest, rest, rest, rest] + list(attn)
        in_specs = [act_spec,
                    pl.BlockSpec((1, HIST, ACT_WIDTH), lambda bi, i, c: (bi, jnp.maximum(i * hblk - 1, 0), 0))
                    ] + gate_specs + [slab] * 6
        scratch += [pltpu.VMEM((tm + HIST, POOL_WIDTH), F32), pltpu.VMEM((tm + HIST, CONV_CH), F32)]
        urows = 8
        blocks += 6 * _nbytes((2, tm, LANES), F32) + _nbytes((HIST, ACT_WIDTH), F32)
    scratch_bytes = 3 * _nbytes((tm + HIST, POOL_WIDTH), F32) + 4 * _nbytes((tm, tc), F32)
    return pl.pallas_call(
        functools.partial(_branch_kernel, tm=tm, step=step),
        out_shape=(jax.ShapeDtypeStruct((b, s, D_MODEL), BF16),
                   jax.ShapeDtypeStruct((b, ni, urows, CONV_CH), F32)),
        grid=(b, ni, D_MODEL // tc),
        in_specs=in_specs + w_specs,
        out_specs=(pl.BlockSpec((1, tm, tc), lambda bi, i, c: (bi, i, c)),
                   pl.BlockSpec((1, 1, urows, CONV_CH), lambda bi, i, c: (bi, i, 0, 0))),
        scratch_shapes=scratch,
        compiler_params=pltpu.CompilerParams(
            dimension_semantics=("parallel", "parallel", "arbitrary"),
            vmem_limit_bytes=_vmem_limit(blocks, scratch_bytes)),
        name="branches_step" if step else "branches",
    )(*ins, wgrp, pscale, wpb, cw, wcb, wab)


def _oproj_kernel(m_ref, w_ref, x_ref, g_ref, gate_ref, o_ref):
    mix = _dot(m_ref[0], w_ref[...])
    o_ref[0] = x_ref[0] + gate_ref[0] * _rms(mix, g_ref[...])


def _oproj(merged, w_o, x, g, gate, *, tm):
    b, s, d = x.shape
    r = gate.shape[1]
    blocks = (_nbytes((tm, d), BF16) + _nbytes((d, d), BF16) + 2 * _nbytes((tm, d), F32)
              + _nbytes((r + 1, d), F32))
    return pl.pallas_call(
        _oproj_kernel,
        out_shape=jax.ShapeDtypeStruct((b, s, d), F32),
        grid=(b, s // tm),
        in_specs=[
            pl.BlockSpec((1, tm, d), lambda bi, i: (bi, i, 0)),
            pl.BlockSpec((d, d), lambda bi, i: (0, 0)),
            pl.BlockSpec((1, tm, d), lambda bi, i: (bi, i, 0)),
            pl.BlockSpec((1, d), lambda bi, i: (0, 0)),
            pl.BlockSpec((1, r, d), lambda bi, i: (bi, 0, 0)),
        ],
        out_specs=pl.BlockSpec((1, tm, d), lambda bi, i: (bi, i, 0)),
        compiler_params=pltpu.CompilerParams(
            dimension_semantics=("parallel", "parallel"),
            vmem_limit_bytes=_vmem_limit(blocks, _nbytes((tm, d), F32))),
        name="oproj",
    )(merged, w_o, x, g, gate)


def _ffn_kernel(*refs, tm, step):
    if step:
        (x_ref, g2_ref, sc_ref, sh_ref, wg_ref, wv_ref, cwg_ref, cwv_ref, wd_ref, g3_ref, gate_ref,
         stg_ref, stv_ref, o_ref, tg_ref, tv_ref, h_ref, acc_ref) = refs
        hist = 0
    else:
        (x_ref, xp_ref, g2_ref, sc_ref, sh_ref, wg_ref, wv_ref, cwg_ref, cwv_ref, wd_ref, g3_ref, gate_ref,
         o_ref, tg_ref, tv_ref, h_ref, acc_ref, ug_ref, uv_ref) = refs
        hist = HIST
    i = pl.program_id(1)
    j = pl.program_id(2)

    def modulate(x):
        return (_rms(x, g2_ref[...]) * (1.0 + sc_ref[0]) + sh_ref[0]).astype(BF16)

    @pl.when(j == 0)
    def _():
        if not step:
            h_ref[0:HIST] = modulate(xp_ref[0])
        h_ref[hist:hist + tm] = modulate(x_ref[0])
        acc_ref[...] = jnp.zeros_like(acc_ref)

    h = h_ref[...]
    up_g = _dot(h, wg_ref[...])
    up_v = _dot(h, wv_ref[...])
    cwg = cwg_ref[...]
    cwv = cwv_ref[...]
    if step:
        uc_g = stg_ref[0] * cwg[0:1] + stg_ref[1] * cwg[1:2] + up_g * cwg[2:3]
        uc_v = stv_ref[0] * cwv[0:1] + stv_ref[1] * cwv[1:2] + up_v * cwv[2:3]
        tg_ref[0, 0] = up_g
        tv_ref[0, 0] = up_v
    else:
        ug_ref[...] = up_g
        uv_ref[...] = up_v

        @pl.when(i == 0)
        def _():
            ug_ref[0:HIST] = jnp.zeros((HIST, ug_ref.shape[1]), F32)
            uv_ref[0:HIST] = jnp.zeros((HIST, uv_ref.shape[1]), F32)

        def conv(ref, cw):
            return (ref[HIST - 2:HIST - 2 + tm] * cw[0:1] + ref[HIST - 1:HIST - 1 + tm] * cw[1:2]
                    + ref[HIST:HIST + tm] * cw[2:3])
        uc_g = conv(ug_ref, cwg)
        uc_v = conv(uv_ref, cwv)
        tg_ref[0, 0] = ug_ref[HIST + tm - 8:HIST + tm]
        tv_ref[0, 0] = uv_ref[HIST + tm - 8:HIST + tm]
    act = (_gelu_tanh(uc_g) * uc_v).astype(BF16)
    acc_ref[...] += _dot(act, wd_ref[...])

    @pl.when(j == pl.num_programs(2) - 1)
    def _():
        o_ref[0] = x_ref[0] + gate_ref[0] * _rms(acc_ref[...], g3_ref[...])


def _ffn(x, g2, sc, sh, w_up, cw, w_down, g3, gate, state, *, tm, tf, step):
    b, s, d = x.shape
    r = sc.shape[1]
    ni, nj = s // tm, D_FF // tf
    trows = tm if step else 8
    vec = lambda: pl.BlockSpec((1, d), lambda bi, i, j: (0, 0))
    mod = lambda: pl.BlockSpec((1, r, d), lambda bi, i, j: (bi, 0, 0))
    w_specs = [
        pl.BlockSpec((d, tf), lambda bi, i, j: (0, j)),
        pl.BlockSpec((d, tf), lambda bi, i, j: (0, nj + j)),
        pl.BlockSpec((CONV_K, tf), lambda bi, i, j: (0, j)),
        pl.BlockSpec((CONV_K, tf), lambda bi, i, j: (0, nj + j)),
        pl.BlockSpec((tf, d), lambda bi, i, j: (j, 0)),
    ]
    x_spec = pl.BlockSpec((1, tm, d), lambda bi, i, j: (bi, i, 0))
    hist = 0 if step else HIST
    scratch = [pltpu.VMEM((tm + hist, d), BF16), pltpu.VMEM((tm, d), F32)]
    blocks = (2 * _nbytes((tm, d), F32) + 3 * _nbytes((d, tf), BF16) + 2 * _nbytes((trows, tf), F32)
              + _nbytes((3 * r + 2, d), F32))
    if step:
        stg, stv = state, state
        ins = [x, g2, sc, sh, w_up, w_up, cw, cw, w_down, g3, gate, stg, stv]
        in_specs = [x_spec, vec(), mod(), mod()] + w_specs + [vec(), mod(),
                    pl.BlockSpec((CONV_K - 1, tm, tf), lambda bi, i, j: (0, 0, j)),
                    pl.BlockSpec((CONV_K - 1, tm, tf), lambda bi, i, j: (0, 0, nj + j))]
        blocks += 2 * _nbytes((CONV_K - 1, tm, tf), F32)
    else:
        hblk = tm // HIST
        ins = [x, x, g2, sc, sh, w_up, w_up, cw, cw, w_down, g3, gate]
        in_specs = [x_spec,
                    pl.BlockSpec((1, HIST, d), lambda bi, i, j: (bi, jnp.maximum(i * hblk - 1, 0), 0)),
                    vec(), mod(), mod()] + w_specs + [vec(), mod()]
        scratch += [pltpu.VMEM((tm + HIST, tf), F32), pltpu.VMEM((tm + HIST, tf), F32)]
        blocks += _nbytes((HIST, d), F32)
    scratch_bytes = (_nbytes((tm + hist, d), BF16) + _nbytes((tm, d), F32)
                     + 6 * _nbytes((tm + hist, tf), F32))
    tail = jax.ShapeDtypeStruct((b, ni, trows, D_FF), F32)
    tail_spec = lambda: pl.BlockSpec((1, 1, trows, tf), lambda bi, i, j: (bi, i, 0, j))
    return pl.pallas_call(
        functools.partial(_ffn_kernel, tm=tm, step=step),
        out_shape=(jax.ShapeDtypeStruct((b, s, d), F32), tail, tail),
        grid=(b, ni, nj),
        in_specs=in_specs,
        out_specs=(pl.BlockSpec((1, tm, d), lambda bi, i, j: (bi, i, 0)), tail_spec(), tail_spec()),
        scratch_shapes=scratch,
        compiler_params=pltpu.CompilerParams(
            dimension_semantics=("parallel", "parallel", "arbitrary"),
            vmem_limit_bytes=_vmem_limit(blocks, scratch_bytes)),
        name="ffn_step" if step else "ffn",
    )(*ins)


def _group_rel_bias(rel_bias):
    n = np.arange(N_DIL_KEYS + 1)
    max_exact = N_BUCKETS // 2
    out = []
    for g, (_, d) in enumerate(ATTN_GROUPS):
        dist = n * d
        large = max_exact + (np.log(np.maximum(dist, 1) / max_exact) / np.log(MAX_DISTANCE / max_exact)
                             * (N_BUCKETS - max_exact)).astype(np.int32)
        bucket = np.where(dist < max_exact, dist, np.minimum(large, N_BUCKETS - 1))
        out.append(rel_bias[bucket, g * HEADS_PER_GROUP:(g + 1) * HEADS_PER_GROUP].T)
    return jnp.stack(out)


def _bias_tables(rel_bias):
    bias_g = _group_rel_bias(rel_bias).astype(F32)
    n = N_DIL_KEYS
    a = np.arange(n)[:, None]
    c = np.arange(2 * n)[None, :]
    dist = a + n - c
    band = (dist >= 0) & (dist <= n)
    tab = jnp.where(band[None, None], bias_g[:, :, np.clip(dist, 0, n)], NEG_INF)
    tab = tab.reshape(N_ATTN_GROUPS, HEADS_PER_GROUP * n, 2 * n)
    pad = ((0, 0), (0, SAMPLE_ROWS - HEADS_PER_GROUP), (0, 0))
    step_c = jnp.pad(bias_g[:, :, n - np.arange(n)], pad)
    step_n = jnp.pad(jnp.broadcast_to(bias_g[:, :, 0:1], (N_ATTN_GROUPS, HEADS_PER_GROUP, LANES)), pad)
    return tab, step_c, step_n


def _split_mod(mod, rows):
    m = mod.reshape(mod.shape[0], N_MOD, D_MODEL)
    if rows:
        return [m[None, :, k] for k in range(N_MOD)]
    return [m[:, k][:, None, :] for k in range(N_MOD)]


def _layer_weights(l, norm_g, w_in, w_attn_br, w_pool_grp, pool_scale, w_pool_br, conv_w, w_conv_br,
                   w_o, w_up, ffn_conv_w, w_down):
    return dict(
        g=[norm_g[l, k][None] for k in range(4)],
        w_qkv=w_in[l, :, :QKV_WIDTH].astype(BF16),
        w_rest=w_in[l, :, QKV_WIDTH:].astype(BF16),
        branch=(w_pool_grp[l].astype(BF16), pool_scale[l][None], w_pool_br[l].astype(BF16),
                conv_w[l], w_conv_br[l].astype(BF16), w_attn_br[l].astype(BF16)),
        w_o=w_o[l].astype(BF16),
        w_up=w_up[l].astype(BF16),
        ffn_cw=ffn_conv_w[l],
        w_down=w_down[l].astype(BF16),
    )


def _kv_rows(qkv_slab, g, nrows):
    b = qkv_slab.shape[0]
    k0 = QKV_SLABS // 3 + 2 * g
    v0 = 2 * QKV_SLABS // 3 + 2 * g
    k = qkv_slab[:, k0:k0 + 2, -nrows:].transpose(0, 2, 1, 3).reshape(b, nrows, HEADS_PER_GROUP, HEAD_DIM)
    v = qkv_slab[:, v0:v0 + 2, -nrows:].transpose(0, 2, 1, 3).reshape(b, nrows, HEADS_PER_GROUP, HEAD_DIM)
    return jnp.stack([k, v], axis=2)


def _prompt_layer(x, mod, lw, bias_tab):
    b, s, _ = x.shape
    sh1, sc1, gt1, sh2, sc2, gt2 = _split_mod(mod, rows=False)
    tm = 512
    qkv = _proj(x, lw["g"][0], sc1, sh1, lw["w_qkv"], tm=1024, tn=ATTN_WIDTH, slab_out=True)
    rest = _proj(x, lw["g"][0], sc1, sh1, lw["w_rest"], tm=1024, tn=1024, slab_out=False)
    ols = [_attention(qkv, bias_tab, g) for g in range(N_ATTN_GROUPS)]
    attn = [o for o, _ in ols] + [l for _, l in ols]
    merged, u_tail = _branches(rest, attn, None, lw["branch"], tm=tm, tc=512, step=False)
    x1 = _oproj(merged, lw["w_o"], x, lw["g"][1], gt1, tm=tm)
    x2, tail_g, tail_v = _ffn(x1, lw["g"][2], sc2, sh2, lw["w_up"], lw["ffn_cw"], lw["w_down"],
                              lw["g"][3], gt2, None, tm=tm, tf=512, step=False)
    new_kv = [_kv_rows(qkv, g, min(w, s)) for g, (w, _) in enumerate(ATTN_GROUPS)]
    new_pool = rest[:, -POOL_STATE:, :POOL_WIDTH]
    new_conv = u_tail[:, -1, -(CONV_K - 1):]
    new_ffn = jnp.concatenate([tail_g[:, -1, -(FFN_K - 1):], tail_v[:, -1, -(FFN_K - 1):]], axis=-1)
    return x2, (new_kv[0], new_kv[1], new_kv[2], new_pool, new_conv, new_ffn)


FFN_K = 3


def _pad_rows(a, axis):
    pad = [(0, 0)] * a.ndim
    pad[axis] = (0, SAMPLE_ROWS - a.shape[axis])
    return jnp.pad(a, pad)


def _sample_layer(x, mod, lw, bias_c, bias_n, hist, nb):
    caches, pool_state, conv_state, ffn_state = hist
    sh1, sc1, gt1, sh2, sc2, gt2 = _split_mod(mod, rows=True)
    tm = SAMPLE_ROWS
    qkv = _proj(x, lw["g"][0], sc1, sh1, lw["w_qkv"], tm=tm, tn=ATTN_WIDTH, slab_out=True)
    rest = _proj(x, lw["g"][0], sc1, sh1, lw["w_rest"], tm=tm, tn=1024, slab_out=False)
    for (w, d), c in zip(ATTN_GROUPS, caches):
        assert c.shape[1] == w == N_DIL_KEYS * d, "cache must hold exactly one window"
    strided = [c.reshape(nb, N_DIL_KEYS, -1) for c in caches]
    attn = _attention_step(qkv, strided, bias_c, bias_n, nb)
    attn = _pad_rows(attn.reshape(1, nb, GROUP_WIDTH), 1)
    pst = _pad_rows(pool_state.transpose(1, 0, 2), 1)
    cst = _pad_rows(conv_state.transpose(1, 0, 2), 1)
    fst = _pad_rows(ffn_state.transpose(1, 0, 2), 1)
    merged, u = _branches(rest, attn, (pst, cst), lw["branch"], tm=tm, tc=512, step=True)
    x1 = _oproj(merged, lw["w_o"], x, lw["g"][1], gt1, tm=tm)
    x2, up_g, up_v = _ffn(x1, lw["g"][2], sc2, sh2, lw["w_up"], lw["ffn_cw"], lw["w_down"],
                          lw["g"][3], gt2, fst, tm=tm, tf=512, step=True)
    new_kv = [_kv_rows(qkv[:, :, :nb].transpose(2, 1, 0, 3), g, 1) for g in range(N_ATTN_GROUPS)]
    p_new = rest[0, :nb, :POOL_WIDTH]
    new_pool = jnp.concatenate([pool_state[:, 1:], p_new[:, None]], axis=1)
    new_conv = jnp.concatenate([conv_state[:, 1:], u[0, 0, :nb, None]], axis=1)
    up_new = jnp.concatenate([up_g[0, 0, :nb], up_v[0, 0, :nb]], axis=-1)
    new_ffn = jnp.concatenate([ffn_state[:, 1:], up_new[:, None]], axis=1)
    return x2, (new_kv[0], new_kv[1], new_kv[2], new_pool, new_conv, new_ffn)


def kernel(x_prompt, x_sample, c_prompt, c_sample, cache_kv_w128, cache_kv_w512, cache_kv_w2048, state_pool, state_conv, state_ffn_conv, rel_bias, norm_g, w_ada, b_ada, w_in, w_attn_br, w_pool_grp, pool_scale, w_pool_br, conv_w, w_conv_br, w_o, w_up, ffn_conv_w, w_down):
    nbp = x_prompt.shape[0]
    nbs, tdec, _ = x_sample.shape
    assert tdec == 1 and nbs <= SAMPLE_ROWS
    bias_tab, bias_c, bias_n = _bias_tables(rel_bias)
    c_rows = -(-(SAMPLE_ROWS + nbp) // 8) * 8
    c_all = jnp.zeros((c_rows, D_MODEL), F32).at[:nbs].set(c_sample).at[SAMPLE_ROWS:SAMPLE_ROWS + nbp].set(c_prompt)
    mod_all = _ada(c_all, w_ada, b_ada)
    yp = x_prompt
    ys = _pad_rows(x_sample.reshape(1, nbs, D_MODEL), 1)
    st_p, st_s = [], []
    for l in range(DEPTH):
        lw = _layer_weights(l, norm_g, w_in, w_attn_br, w_pool_grp, pool_scale, w_pool_br, conv_w,
                            w_conv_br, w_o, w_up, ffn_conv_w, w_down)
        yp, sp = _prompt_layer(yp, mod_all[l, SAMPLE_ROWS:SAMPLE_ROWS + nbp], lw, bias_tab)
        hist = ((cache_kv_w128[l], cache_kv_w512[l], cache_kv_w2048[l]), state_pool[l], state_conv[l],
                state_ffn_conv[l])
        ys, ss = _sample_layer(ys, mod_all[l, :SAMPLE_ROWS], lw, bias_c, bias_n, hist, nbs)
        st_p.append(sp)
        st_s.append(ss)
    outs_p = [jnp.stack([s[k] for s in st_p]) for k in range(6)]
    outs_s = [jnp.stack([s[k] for s in st_s]) for k in range(6)]
    return (yp, ys[0, :nbs, None, :], *outs_p, *outs_s)
```

```python
import functools

import numpy as np
import jax
import jax.numpy as jnp
from jax import lax
from jax.experimental import pallas as pl
from jax.experimental.pallas import tpu as pltpu

F32 = jnp.float32
BF16 = jnp.bfloat16

D_MODEL = 2048
DEPTH = 2
HEAD_DIM = 64
HEADS_PER_GROUP = 4
ATTN_GROUPS = ((128, 1), (512, 4), (2048, 16))
N_ATTN_GROUPS = len(ATTN_GROUPS)
ATTN_WIDTH = N_ATTN_GROUPS * HEADS_PER_GROUP * HEAD_DIM
GROUP_WIDTH = HEADS_PER_GROUP * HEAD_DIM
N_DIL_KEYS = 128
N_BUCKETS = 32
MAX_DISTANCE = 2048
ATTN_SCALE = HEAD_DIM ** -0.5
POOL_WINDOWS = (2, 4, 8, 16)
POOL_GROUP = 128
POOL_WIDTH = 512
POOL_STATE = 15
CONV_CH = 512
CONV_K = 3
D_FF = 5632
FFN_K = 3
N_MOD = 6
N_NORM = 4
EPS = 1e-6
NEG_INF = -1e30

LANES = 128
HIST = 16
SAMPLE_ROWS = 16
QKV_WIDTH = 3 * ATTN_WIDTH
QKV_SLABS = QKV_WIDTH // LANES
REST_WIDTH = POOL_WIDTH + 3 * CONV_CH + 3 * D_MODEL
ACT_WIDTH = POOL_WIDTH + 3 * CONV_CH
FFN_ROW_CHUNKS = 2
VMEM_CAP = 56 * 1024 * 1024


def _vmem_limit(block_bytes, scratch_bytes=0):
    est = 2 * block_bytes + scratch_bytes
    return int(min(VMEM_CAP, est + est // 4 + (4 << 20)))


def _nbytes(shape, dtype):
    return int(np.prod(shape)) * jnp.dtype(dtype).itemsize


def _rms(x, g):
    return x * lax.rsqrt(jnp.mean(x * x, axis=-1, keepdims=True) + EPS) * g


def _sigmoid(x):
    return 1.0 / (1.0 + jnp.exp(-x))


def _gelu_tanh(x):
    return 0.5 * x * (1.0 + jnp.tanh(np.sqrt(2.0 / np.pi) * (x + 0.044715 * (x * x * x))))


def _dot(a, b):
    return jnp.dot(a, b, preferred_element_type=F32)


def _tail2(ref):
    return ref[(0,) * (len(ref.shape) - 2)]


def _norm_view(norm_g, l):
    arr = norm_g.reshape(DEPTH * N_NORM, 1, D_MODEL)
    return arr, lambda k: pl.BlockSpec((1, 1, D_MODEL), lambda *_: (l * N_NORM + k, 0, 0))


def _mod_view(mod_all, l, prompt):
    if prompt:
        arr = mod_all.reshape(DEPTH, mod_all.shape[1], N_MOD, 1, D_MODEL)
        return arr, lambda k: pl.BlockSpec((1, 1, 1, 1, D_MODEL), lambda bi, *_: (l, SAMPLE_ROWS + bi, k, 0, 0))
    return mod_all, lambda k: pl.BlockSpec((1, SAMPLE_ROWS, D_MODEL), lambda bi, *_: (l, 0, k))


def _mod_rows(mod):
    return mod[1](0).block_shape[-2]


def _ada_kernel(c_ref, w_ref, b_ref, o_ref):
    c = c_ref[...]
    s = (c * _sigmoid(c)).astype(BF16)
    o_ref[0] = _dot(s, w_ref[0].astype(BF16)) + b_ref[0]


def _ada(c_all, w_ada, b_ada):
    rows = c_all.shape[0]
    n = w_ada.shape[-1]
    tn = 1024
    blocks = _nbytes((rows, D_MODEL), F32) + _nbytes((D_MODEL, tn), F32) + _nbytes((rows + 1, tn), F32)
    return pl.pallas_call(
        _ada_kernel,
        out_shape=jax.ShapeDtypeStruct((DEPTH, rows, n), F32),
        grid=(DEPTH, n // tn),
        in_specs=[
            pl.BlockSpec((rows, D_MODEL), lambda l, j: (0, 0)),
            pl.BlockSpec((1, D_MODEL, tn), lambda l, j: (l, 0, j)),
            pl.BlockSpec((1, 1, tn), lambda l, j: (l, 0, j)),
        ],
        out_specs=pl.BlockSpec((1, rows, tn), lambda l, j: (l, 0, j)),
        compiler_params=pltpu.CompilerParams(
            dimension_semantics=("parallel", "parallel"),
            vmem_limit_bytes=_vmem_limit(blocks, _nbytes((D_MODEL, tn), BF16))),
        name="ada",
    )(c_all, w_ada, b_ada.reshape(DEPTH, 1, n))


def _proj_kernel(x_ref, g_ref, sc_ref, sh_ref, w_ref, *refs, slabs):
    h_ref = refs[-1]

    @pl.when(pl.program_id(2) == 0)
    def _():
        y = _rms(x_ref[0], g_ref[0])
        h_ref[...] = (y * (1.0 + _tail2(sc_ref)) + _tail2(sh_ref)).astype(BF16)

    res = _dot(h_ref[...], w_ref[0])
    refs[0][0] = res
    if slabs:
        for s in range(slabs):
            refs[1][0, s] = res[:, s * LANES:(s + 1) * LANES]


def _proj(x, norm, mod, w, l, *, tm, tn, slab_out):
    b, s, d = x.shape
    n = w.shape[2]
    slabs = tn // LANES if slab_out else 0
    out_shape = [jax.ShapeDtypeStruct((b, s, n), F32)]
    out_specs = [pl.BlockSpec((1, tm, tn), lambda bi, i, j: (bi, i, j))]
    if slab_out:
        out_shape.append(jax.ShapeDtypeStruct((b, n // LANES, s, LANES), F32))
        out_specs.append(pl.BlockSpec((1, slabs, tm, LANES), lambda bi, i, j: (bi, j, i, 0)))
    blocks = (_nbytes((tm, d), F32) + _nbytes((2 * _mod_rows(mod) + 1, d), F32) + _nbytes((d, tn), BF16)
              + (2 if slab_out else 1) * _nbytes((tm, tn), F32))
    res = pl.pallas_call(
        functools.partial(_proj_kernel, slabs=slabs),
        out_shape=out_shape,
        grid=(b, s // tm, n // tn),
        in_specs=[
            pl.BlockSpec((1, tm, d), lambda bi, i, j: (bi, i, 0)),
            norm[1](0), mod[1](1), mod[1](0),
            pl.BlockSpec((1, d, tn), lambda bi, i, j: (l, 0, j)),
        ],
        out_specs=out_specs,
        scratch_shapes=[pltpu.VMEM((tm, d), BF16)],
        compiler_params=pltpu.CompilerParams(
            dimension_semantics=("parallel", "parallel", "arbitrary"),
            vmem_limit_bytes=_vmem_limit(blocks, _nbytes((tm, d), BF16) + _nbytes((tm, tn), F32))),
        name="proj_slab" if slab_out else "proj",
    )(x, norm[0], mod[0], mod[0], w)
    return res if slab_out else res[0]


def _head_masks(rows):
    lane = lax.broadcasted_iota(jnp.int32, (rows, GROUP_WIDTH), 1)
    return [(lane >= h * HEAD_DIM) & (lane < (h + 1) * HEAD_DIM) for h in range(HEADS_PER_GROUP)]


def _attn_kernel(q_ref, kc_ref, kp_ref, vc_ref, vp_ref, bias_ref, o_ref, lse_ref, *, d, sb):
    i = pl.program_id(1)
    nq = N_DIL_KEYS
    span = nq * d
    ncb = sb // span
    hm = _head_masks(1)
    col = lax.broadcasted_iota(jnp.int32, (1, 2 * nq), 1)
    bias = bias_ref[0]

    def rows(start):
        return pl.ds(start, nq, stride=d) if d > 1 else pl.ds(start, nq)

    def load(ref, start):
        return jnp.concatenate([ref[0, s, rows(start), :] for s in range(2)], axis=1)

    def one_block(r, jb):
        qs = jb * span + r
        q = load(q_ref, qs) * ATTN_SCALE
        if jb == 0:
            lo = sb - span + r
            k_lo, v_lo = load(kp_ref, lo), load(vp_ref, lo)
        else:
            lo = (jb - 1) * span + r
            k_lo, v_lo = load(kc_ref, lo), load(vc_ref, lo)
        kcat = jnp.concatenate([k_lo, load(kc_ref, qs)], axis=0).astype(BF16)
        vcat = jnp.concatenate([v_lo, load(vc_ref, qs)], axis=0).astype(BF16)
        qm = jnp.concatenate([jnp.where(hm[h], q, 0.0) for h in range(HEADS_PER_GROUP)], axis=0).astype(BF16)
        s = lax.dot_general(qm, kcat, (((1,), (1,)), ((), ())), preferred_element_type=F32) + bias
        if jb == 0:
            s = jnp.where((col < nq) & (i == 0), NEG_INF, s)
        m = jnp.max(s, axis=-1, keepdims=True)
        p = jnp.exp(s - m)
        l = jnp.sum(p, axis=-1, keepdims=True)
        oall = _dot((p * (1.0 / l)).astype(BF16), vcat)
        lse = m + jnp.log(l)
        o = jnp.zeros((nq, GROUP_WIDTH), F32)
        ls = jnp.zeros((nq, GROUP_WIDTH), F32)
        for h in range(HEADS_PER_GROUP):
            o = jnp.where(hm[h], oall[h * nq:(h + 1) * nq], o)
            ls = jnp.where(hm[h], lse[h * nq:(h + 1) * nq], ls)
        for sl in range(2):
            o_ref[0, sl, rows(qs), :] = o[:, sl * LANES:(sl + 1) * LANES]
            lse_ref[0, sl, rows(qs), :] = ls[:, sl * LANES:(sl + 1) * LANES]

    if d == 1:
        for jb in range(ncb):
            one_block(0, jb)
    else:
        def body(r, carry):
            for jb in range(ncb):
                one_block(r, jb)
            return carry
        lax.fori_loop(0, d, body, 0)


def _attention(qkv, bias_tab, g):
    b, _, s, _ = qkv.shape
    d = ATTN_GROUPS[g][1]
    sb = max(N_DIL_KEYS * d, 512)
    blk = (1, 2, sb, LANES)
    kslab, vslab = ATTN_WIDTH // GROUP_WIDTH + g, 2 * ATTN_WIDTH // GROUP_WIDTH + g
    prev = lambda i: jnp.maximum(i - 1, 0)
    blocks = 7 * _nbytes(blk, F32) + _nbytes((4 * N_DIL_KEYS, 2 * N_DIL_KEYS), F32)
    out = jax.ShapeDtypeStruct((b, 2, s, LANES), F32)
    return pl.pallas_call(
        functools.partial(_attn_kernel, d=d, sb=sb),
        out_shape=(out, out),
        grid=(b, s // sb),
        in_specs=[
            pl.BlockSpec(blk, lambda bi, i: (bi, g, i, 0)),
            pl.BlockSpec(blk, lambda bi, i: (bi, kslab, i, 0)),
            pl.BlockSpec(blk, lambda bi, i: (bi, kslab, prev(i), 0)),
            pl.BlockSpec(blk, lambda bi, i: (bi, vslab, i, 0)),
            pl.BlockSpec(blk, lambda bi, i: (bi, vslab, prev(i), 0)),
            pl.BlockSpec((1, 4 * N_DIL_KEYS, 2 * N_DIL_KEYS), lambda bi, i: (g, 0, 0)),
        ],
        out_specs=(pl.BlockSpec(blk, lambda bi, i: (bi, 0, i, 0)),
                   pl.BlockSpec(blk, lambda bi, i: (bi, 0, i, 0))),
        compiler_params=pltpu.CompilerParams(
            dimension_semantics=("parallel", "parallel"),
            vmem_limit_bytes=_vmem_limit(blocks, 8 << 20)),
        name=f"attn_d{d}",
    )(qkv, qkv, qkv, qkv, qkv, bias_tab)


def _attn_step_kernel(qkv_ref, c0_ref, c1_ref, c2_ref, bc_ref, bn_ref, o_ref):
    b = pl.program_id(0)
    rows = SAMPLE_ROWS
    lane = lax.broadcasted_iota(jnp.int32, (rows, GROUP_WIDTH), 1)
    row = lax.broadcasted_iota(jnp.int32, (rows, GROUP_WIDTH), 0)
    sel = (lane >= row * HEAD_DIM) & (lane < (row + 1) * HEAD_DIM)

    def rowvec(col0):
        return qkv_ref[0, pl.ds(b, 1), col0:col0 + GROUP_WIDTH]

    outs, lses = [], []
    for g, c_ref in enumerate((c0_ref, c1_ref, c2_ref)):
        q = rowvec(g * GROUP_WIDTH) * ATTN_SCALE
        kn = rowvec(ATTN_WIDTH + g * GROUP_WIDTH).astype(BF16).astype(F32)
        vn = rowvec(2 * ATTN_WIDTH + g * GROUP_WIDTH).astype(BF16).astype(F32)
        q4 = jnp.where(sel, jnp.broadcast_to(q, (rows, GROUP_WIDTH)), 0.0).astype(BF16)
        kc = c_ref[0, 0, :, 0:GROUP_WIDTH].astype(BF16)
        vc = c_ref[0, 0, :, GROUP_WIDTH:2 * GROUP_WIDTH].astype(BF16)
        s_c = lax.dot_general(q4, kc, (((1,), (1,)), ((), ())), preferred_element_type=F32) + bc_ref[g]
        s_n = jnp.sum(q4.astype(F32) * kn, axis=-1, keepdims=True) + bn_ref[g][:, 0:1]
        m = jnp.maximum(jnp.max(s_c, axis=-1, keepdims=True), s_n)
        p_c = jnp.exp(s_c - m)
        p_n = jnp.exp(s_n - m)
        l = jnp.sum(p_c, axis=-1, keepdims=True) + p_n
        inv = 1.0 / l
        oc = _dot((p_c * inv).astype(BF16), vc) + (p_n * inv).astype(BF16).astype(F32) * vn
        lse = m + jnp.log(l)
        outs.append(jnp.sum(jnp.where(sel, oc, 0.0), axis=0, keepdims=True))
        lses.append(jnp.sum(jnp.where(sel, lse, 0.0), axis=0, keepdims=True))
    mx = jnp.maximum(jnp.maximum(lses[0], lses[1]), lses[2])
    w = [jnp.exp(ls - mx) for ls in lses]
    tot = w[0] + w[1] + w[2]
    o_ref[0] = (w[0] * outs[0] + w[1] * outs[1] + w[2] * outs[2]) * (1.0 / tot)


def _attention_step(qkv, caches, bias_c, bias_n, l, nb):
    cache_specs = [pl.BlockSpec((1, 1, N_DIL_KEYS, 2 * GROUP_WIDTH), lambda bi: (l, bi, 0, 0))
                   for _ in range(N_ATTN_GROUPS)]
    blocks = (_nbytes(qkv.shape, F32) + 3 * _nbytes((N_DIL_KEYS, 2 * GROUP_WIDTH), F32)
              + 2 * _nbytes(bias_c.shape, F32))
    return pl.pallas_call(
        _attn_step_kernel,
        out_shape=jax.ShapeDtypeStruct((nb, 1, GROUP_WIDTH), F32),
        grid=(nb,),
        in_specs=[pl.BlockSpec(qkv.shape, lambda bi: (0, 0, 0))] + cache_specs + [
            pl.BlockSpec(bias_c.shape, lambda bi: (0, 0, 0)),
            pl.BlockSpec(bias_n.shape, lambda bi: (0, 0, 0)),
        ],
        out_specs=pl.BlockSpec((1, 1, GROUP_WIDTH), lambda bi: (bi, 0, 0)),
        compiler_params=pltpu.CompilerParams(
            dimension_semantics=("parallel",),
            vmem_limit_bytes=_vmem_limit(blocks, 4 << 20)),
        name="attn_step",
    )(qkv, *caches, bias_c, bias_n)


def _branch_kernel(*refs, tm, step):
    if step:
        (act_ref, pst_ref, cst_ref, ga_ref, gb_ref, gc_ref, at_ref,
         wgrp_ref, pscale_ref, wpb_ref, cw_ref, wcb_ref, wab_ref,
         out_ref, u_ref, pzs_ref, cb_ref, comb_ref) = refs
    else:
        (act_ref, hist_ref, ga_ref, gb_ref, gc_ref, o0_ref, o1_ref, o2_ref, l0_ref, l1_ref, l2_ref,
         wgrp_ref, pscale_ref, wpb_ref, cw_ref, wcb_ref, wab_ref,
         out_ref, u_ref, pzs_ref, cb_ref, comb_ref, pe_ref, ue_ref) = refs
    i = pl.program_id(1)

    @pl.when(pl.program_id(2) == 0)
    def _():
        p = act_ref[0, :, 0:POOL_WIDTH]
        gate_b = act_ref[0, :, POOL_WIDTH:POOL_WIDTH + CONV_CH]
        u = act_ref[0, :, POOL_WIDTH + CONV_CH:POOL_WIDTH + 2 * CONV_CH] * \
            act_ref[0, :, POOL_WIDTH + 2 * CONV_CH:POOL_WIDTH + 3 * CONV_CH]
        cw = cw_ref[0]
        if step:
            acc = p
            sums = {}
            for k in range(1, max(POOL_WINDOWS)):
                acc = acc + pst_ref[0, POOL_STATE - k]
                sums[k + 1] = acc
            means = [sums[w][:, gi * POOL_GROUP:(gi + 1) * POOL_GROUP] * (1.0 / w)
                     for gi, w in enumerate(POOL_WINDOWS)]
            conv = cst_ref[0, 0] * cw[0:1] + cst_ref[0, 1] * cw[1:2] + u * cw[2:3]
            u_ref[0, 0] = u
            comb_ref[...] = at_ref[0].astype(BF16)
        else:
            first = i == 0
            hist_p = hist_ref[0, :, 0:POOL_WIDTH]
            hist_u = hist_ref[0, :, POOL_WIDTH + CONV_CH:POOL_WIDTH + 2 * CONV_CH] * \
                hist_ref[0, :, POOL_WIDTH + 2 * CONV_CH:POOL_WIDTH + 3 * CONV_CH]
            pe_ref[0:HIST] = jnp.where(first, 0.0, hist_p)
            ue_ref[0:HIST] = jnp.where(first, 0.0, hist_u)
            pe_ref[HIST:HIST + tm] = p
            ue_ref[HIST:HIST + tm] = u
            t = i * tm + lax.broadcasted_iota(jnp.int32, (tm, 1), 0)
            means = []
            for gi, w in enumerate(POOL_WINDOWS):
                cs = slice(gi * POOL_GROUP, (gi + 1) * POOL_GROUP)
                acc = pe_ref[HIST:HIST + tm, cs]
                for k in range(1, w):
                    acc = acc + pe_ref[HIST - k:HIST - k + tm, cs]
                cnt = jnp.minimum(t + 1, w).astype(F32)
                means.append(acc * (1.0 / cnt))
            conv = (ue_ref[HIST - 2:HIST - 2 + tm] * cw[0:1] + ue_ref[HIST - 1:HIST - 1 + tm] * cw[1:2]
                    + u * cw[2:3])
            u_ref[0, 0] = ue_ref[HIST + tm - 8:HIST + tm]
            for sl in range(2):
                ls = [r[0, sl] for r in (l0_ref, l1_ref, l2_ref)]
                os_ = [r[0, sl] for r in (o0_ref, o1_ref, o2_ref)]
                mx = jnp.maximum(jnp.maximum(ls[0], ls[1]), ls[2])
                w_ = [jnp.exp(x - mx) for x in ls]
                tot = w_[0] + w_[1] + w_[2]
                comb = (w_[0] * os_[0] + w_[1] * os_[1] + w_[2] * os_[2]) * (1.0 / tot)
                comb_ref[:, sl * LANES:(sl + 1) * LANES] = comb.astype(BF16)
        pscale = pscale_ref[0]
        for gi in range(len(POOL_WINDOWS)):
            cs = slice(gi * POOL_GROUP, (gi + 1) * POOL_GROUP)
            pm = (means[gi] - p[:, cs]).astype(BF16)
            pz = _dot(pm, wgrp_ref[0, gi])
            pzs_ref[:, cs] = (pz * pscale[:, cs]).astype(BF16)
        cb_ref[...] = (gate_b * conv).astype(BF16)

    br_a = _dot(pzs_ref[...], wpb_ref[0])
    br_b = _dot(cb_ref[...], wcb_ref[0])
    br_c = _dot(comb_ref[...], wab_ref[0])
    merged = _sigmoid(ga_ref[0]) * br_a + _sigmoid(gb_ref[0]) * br_b + _sigmoid(gc_ref[0]) * br_c
    out_ref[0] = merged.astype(BF16)


def _branches(rest, attn, states, weights, l, *, tm, tc, step):
    b, s, _ = rest.shape
    wgrp, pscale, wpb, cw, wcb, wab = weights
    ni = s // tm
    gblk = ACT_WIDTH // tc
    gate_specs = [pl.BlockSpec((1, tm, tc), lambda bi, i, c, k=k: (bi, i, gblk + k * (D_MODEL // tc) + c))
                  for k in range(3)]
    w_specs = [
        pl.BlockSpec((1,) + wgrp.shape[1:], lambda bi, i, c: (l, 0, 0, 0)),
        pl.BlockSpec((1,) + pscale.shape[1:], lambda bi, i, c: (l, 0, 0)),
        pl.BlockSpec((1, POOL_WIDTH, tc), lambda bi, i, c: (l, 0, c)),
        pl.BlockSpec((1,) + cw.shape[1:], lambda bi, i, c: (l, 0, 0)),
        pl.BlockSpec((1, CONV_CH, tc), lambda bi, i, c: (l, 0, c)),
        pl.BlockSpec((1, GROUP_WIDTH, tc), lambda bi, i, c: (l, 0, c)),
    ]
    act_spec = pl.BlockSpec((1, tm, ACT_WIDTH), lambda bi, i, c: (bi, i, 0))
    scratch = [pltpu.VMEM((tm, POOL_WIDTH), BF16), pltpu.VMEM((tm, CONV_CH), BF16),
               pltpu.VMEM((tm, GROUP_WIDTH), BF16)]
    blocks = (_nbytes((tm, ACT_WIDTH), F32) + 3 * _nbytes((tm, tc), F32) + _nbytes(wgrp.shape[1:], BF16)
              + _nbytes((POOL_WIDTH + CONV_CH + GROUP_WIDTH, tc), BF16) + _nbytes((tm, tc), BF16))
    if step:
        pst, cst = states
        ins = [rest, pst, cst, rest, rest, rest, attn]
        in_specs = [act_spec,
                    pl.BlockSpec((1,) + pst.shape[1:], lambda bi, i, c: (l, 0, 0, 0)),
                    pl.BlockSpec((1,) + cst.shape[1:], lambda bi, i, c: (l, 0, 0, 0))] + gate_specs + [
                    pl.BlockSpec((1, tm, GROUP_WIDTH), lambda bi, i, c: (bi, i, 0))]
        urows = tm
        blocks += _nbytes(pst.shape[1:], F32) + _nbytes(cst.shape[1:], F32)
    else:
        hblk = tm // HIST
        slab = pl.BlockSpec((1, 2, tm, LANES), lambda bi, i, c: (bi, 0, i, 0))
        ins = [rest, rest, rest, rest, rest] + list(attn)
        in_specs = [act_spec,
                    pl.BlockSpec((1, HIST, ACT_WIDTH), lambda bi, i, c: (bi, jnp.maximum(i * hblk - 1, 0), 0))
                    ] + gate_specs + [slab] * 6
        scratch += [pltpu.VMEM((tm + HIST, POOL_WIDTH), F32), pltpu.VMEM((tm + HIST, CONV_CH), F32)]
        urows = 8
        blocks += 6 * _nbytes((2, tm, LANES), F32) + _nbytes((HIST, ACT_WIDTH), F32)
    scratch_bytes = 3 * _nbytes((tm + HIST, POOL_WIDTH), F32) + 4 * _nbytes((tm, tc), F32)
    return pl.pallas_call(
        functools.partial(_branch_kernel, tm=tm, step=step),
        out_shape=(jax.ShapeDtypeStruct((b, s, D_MODEL), BF16),
                   jax.ShapeDtypeStruct((b, ni, urows, CONV_CH), F32)),
        grid=(b, ni, D_MODEL // tc),
        in_specs=in_specs + w_specs,
        out_specs=(pl.BlockSpec((1, tm, tc), lambda bi, i, c: (bi, i, c)),
                   pl.BlockSpec((1, 1, urows, CONV_CH), lambda bi, i, c: (bi, i, 0, 0))),
        scratch_shapes=scratch,
        compiler_params=pltpu.CompilerParams(
            dimension_semantics=("parallel", "parallel", "arbitrary"),
            vmem_limit_bytes=_vmem_limit(blocks, scratch_bytes)),
        name="branches_step" if step else "branches",
    )(*ins, wgrp, pscale, wpb, cw, wcb, wab)


def _oproj_kernel(m_ref, w_ref, x_ref, g_ref, gate_ref, o_ref):
    mix = _dot(m_ref[0], w_ref[0])
    o_ref[0] = x_ref[0] + _tail2(gate_ref) * _rms(mix, g_ref[0])


def _oproj(merged, w_o, x, norm, mod, l, *, tm):
    b, s, d = x.shape
    blocks = (_nbytes((tm, d), BF16) + _nbytes((d, d), BF16) + 2 * _nbytes((tm, d), F32)
              + _nbytes((_mod_rows(mod) + 1, d), F32))
    return pl.pallas_call(
        _oproj_kernel,
        out_shape=jax.ShapeDtypeStruct((b, s, d), F32),
        grid=(b, s // tm),
        in_specs=[
            pl.BlockSpec((1, tm, d), lambda bi, i: (bi, i, 0)),
            pl.BlockSpec((1, d, d), lambda bi, i: (l, 0, 0)),
            pl.BlockSpec((1, tm, d), lambda bi, i: (bi, i, 0)),
            norm[1](1), mod[1](2),
        ],
        out_specs=pl.BlockSpec((1, tm, d), lambda bi, i: (bi, i, 0)),
        compiler_params=pltpu.CompilerParams(
            dimension_semantics=("parallel", "parallel"),
            vmem_limit_bytes=_vmem_limit(blocks, _nbytes((tm, d), F32))),
        name="oproj",
    )(merged, w_o, x, norm[0], mod[0])


def _ffn_kernel(*refs, tm, step, nchunk):
    if step:
        (x_ref, g2_ref, sc_ref, sh_ref, wg_ref, wv_ref, cwg_ref, cwv_ref, wd_ref, g3_ref, gate_ref,
         stg_ref, stv_ref, o_ref, tg_ref, tv_ref, h_ref, acc_ref) = refs
        hist = 0
    else:
        (x_ref, xp_ref, g2_ref, sc_ref, sh_ref, wg_ref, wv_ref, cwg_ref, cwv_ref, wd_ref, g3_ref, gate_ref,
         o_ref, tg_ref, tv_ref, h_ref, acc_ref, ug_ref, uv_ref) = refs
        hist = HIST
    i = pl.program_id(1)
    j = pl.program_id(2)

    def modulate(x):
        return (_rms(x, g2_ref[0]) * (1.0 + _tail2(sc_ref)) + _tail2(sh_ref)).astype(BF16)

    @pl.when(j == 0)
    def _():
        if not step:
            h_ref[0:HIST] = modulate(xp_ref[0])
        h_ref[hist:hist + tm] = modulate(x_ref[0])
        acc_ref[...] = jnp.zeros_like(acc_ref)

    cwg = cwg_ref[0]
    cwv = cwv_ref[0]
    if step:
        h = h_ref[...]
        up_g = _dot(h, wg_ref[0])
        up_v = _dot(h, wv_ref[0])
        uc_g = stg_ref[0, 0] * cwg[0:1] + stg_ref[0, 1] * cwg[1:2] + up_g * cwg[2:3]
        uc_v = stv_ref[0, 0] * cwv[0:1] + stv_ref[0, 1] * cwv[1:2] + up_v * cwv[2:3]
        tg_ref[0, 0] = up_g
        tv_ref[0, 0] = up_v
        acc_ref[...] += _dot((_gelu_tanh(uc_g) * uc_v).astype(BF16), wd_ref[0])
    else:
        first = i == 0
        rc = tm // nchunk
        for c in range(nchunk):
            lo = 0 if c == 0 else HIST + c * rc
            hi = HIST + (c + 1) * rc
            h = h_ref[lo:hi]
            for u_ref, w_ref in ((ug_ref, wg_ref), (uv_ref, wv_ref)):
                up = _dot(h, w_ref[0])
                if c == 0:
                    u_ref[0:HIST] = jnp.where(first, 0.0, up[0:HIST])
                    u_ref[HIST:hi] = up[HIST:]
                else:
                    u_ref[lo:hi] = up

            def conv(ref, cw):
                base = HIST + c * rc
                return (ref[base - 2:base - 2 + rc] * cw[0:1] + ref[base - 1:base - 1 + rc] * cw[1:2]
                        + ref[base:base + rc] * cw[2:3])
            act = (_gelu_tanh(conv(ug_ref, cwg)) * conv(uv_ref, cwv)).astype(BF16)
            acc_ref[c * rc:(c + 1) * rc] += _dot(act, wd_ref[0])
        tg_ref[0, 0] = ug_ref[HIST + tm - 8:HIST + tm]
        tv_ref[0, 0] = uv_ref[HIST + tm - 8:HIST + tm]

    @pl.when(j == pl.num_programs(2) - 1)
    def _():
        o_ref[0] = x_ref[0] + _tail2(gate_ref) * _rms(acc_ref[...], g3_ref[0])


def _ffn(x, norm, mod, w_up, cw, w_down, state, l, *, tm, tf, step):
    b, s, d = x.shape
    ni, nj = s // tm, D_FF // tf
    trows = tm if step else 8
    w_specs = [
        pl.BlockSpec((1, d, tf), lambda bi, i, j: (l, 0, j)),
        pl.BlockSpec((1, d, tf), lambda bi, i, j: (l, 0, nj + j)),
        pl.BlockSpec((1, FFN_K, tf), lambda bi, i, j: (l, 0, j)),
        pl.BlockSpec((1, FFN_K, tf), lambda bi, i, j: (l, 0, nj + j)),
        pl.BlockSpec((1, tf, d), lambda bi, i, j: (l, j, 0)),
    ]
    x_spec = pl.BlockSpec((1, tm, d), lambda bi, i, j: (bi, i, 0))
    hist = 0 if step else HIST
    scratch = [pltpu.VMEM((tm + hist, d), BF16), pltpu.VMEM((tm, d), F32)]
    blocks = (2 * _nbytes((tm, d), F32) + 3 * _nbytes((d, tf), BF16) + 2 * _nbytes((trows, tf), F32)
              + _nbytes((3 * _mod_rows(mod) + 2, d), F32))
    mods = [norm[1](2), mod[1](4), mod[1](3)]
    tailp = [norm[1](3), mod[1](5)]
    if step:
        ins = [x, norm[0], mod[0], mod[0], w_up, w_up, cw, cw, w_down, norm[0], mod[0], state, state]
        in_specs = [x_spec] + mods + w_specs + tailp + [
            pl.BlockSpec((1, FFN_K - 1, tm, tf), lambda bi, i, j: (l, 0, 0, j)),
            pl.BlockSpec((1, FFN_K - 1, tm, tf), lambda bi, i, j: (l, 0, 0, nj + j))]
        blocks += 2 * _nbytes((FFN_K - 1, tm, tf), F32)
    else:
        hblk = tm // HIST
        ins = [x, x, norm[0], mod[0], mod[0], w_up, w_up, cw, cw, w_down, norm[0], mod[0]]
        in_specs = [x_spec,
                    pl.BlockSpec((1, HIST, d), lambda bi, i, j: (bi, jnp.maximum(i * hblk - 1, 0), 0))
                    ] + mods + w_specs + tailp
        scratch += [pltpu.VMEM((tm + HIST, tf), F32), pltpu.VMEM((tm + HIST, tf), F32)]
        blocks += _nbytes((HIST, d), F32)
    scratch_bytes = (_nbytes((tm + hist, d), BF16) + _nbytes((tm, d), F32)
                     + 6 * _nbytes((tm + hist, tf), F32))
    tail = jax.ShapeDtypeStruct((b, ni, trows, D_FF), F32)
    tail_spec = lambda: pl.BlockSpec((1, 1, trows, tf), lambda bi, i, j: (bi, i, 0, j))
    return pl.pallas_call(
        functools.partial(_ffn_kernel, tm=tm, step=step, nchunk=FFN_ROW_CHUNKS),
        out_shape=(jax.ShapeDtypeStruct((b, s, d), F32), tail, tail),
        grid=(b, ni, nj),
        in_specs=in_specs,
        out_specs=(pl.BlockSpec((1, tm, d), lambda bi, i, j: (bi, i, 0)), tail_spec(), tail_spec()),
        scratch_shapes=scratch,
        compiler_params=pltpu.CompilerParams(
            dimension_semantics=("parallel", "parallel", "arbitrary"),
            vmem_limit_bytes=_vmem_limit(blocks, scratch_bytes)),
        name="ffn_step" if step else "ffn",
    )(*ins)


def _group_rel_bias(rel_bias):
    n = np.arange(N_DIL_KEYS + 1)
    max_exact = N_BUCKETS // 2
    out = []
    for g, (_, d) in enumerate(ATTN_GROUPS):
        dist = n * d
        large = max_exact + (np.log(np.maximum(dist, 1) / max_exact) / np.log(MAX_DISTANCE / max_exact)
                             * (N_BUCKETS - max_exact)).astype(np.int32)
        bucket = np.where(dist < max_exact, dist, np.minimum(large, N_BUCKETS - 1))
        out.append(rel_bias[bucket, g * HEADS_PER_GROUP:(g + 1) * HEADS_PER_GROUP].T)
    return jnp.stack(out)


def _bias_tables(rel_bias):
    bias_g = _group_rel_bias(rel_bias).astype(F32)
    n = N_DIL_KEYS
    gh = (N_ATTN_GROUPS, HEADS_PER_GROUP)
    ext = jnp.concatenate([bias_g[:, :, ::-1], jnp.full(gh + (n,), NEG_INF, F32)], axis=-1)
    tab = jnp.tile(ext, (1, 1, n))[:, :, :n * 2 * n].reshape(N_ATTN_GROUPS, HEADS_PER_GROUP * n, 2 * n)
    pad = ((0, 0), (0, SAMPLE_ROWS - HEADS_PER_GROUP), (0, 0))
    step_c = jnp.pad(bias_g[:, :, :0:-1], pad)
    step_n = jnp.pad(jnp.broadcast_to(bias_g[:, :, 0:1], gh + (LANES,)), pad)
    return tab, step_c, step_n


def _kv_pack(qkv, g, nrows):
    b = qkv.shape[0]
    k0 = ATTN_WIDTH + g * GROUP_WIDTH
    v0 = 2 * ATTN_WIDTH + g * GROUP_WIDTH
    k = qkv[:, -nrows:, k0:k0 + GROUP_WIDTH].reshape(b, nrows, HEADS_PER_GROUP, HEAD_DIM)
    v = qkv[:, -nrows:, v0:v0 + GROUP_WIDTH].reshape(b, nrows, HEADS_PER_GROUP, HEAD_DIM)
    return jnp.stack([k, v], axis=2)


def _prompt_layer(x, l, P):
    b, s, _ = x.shape
    norm, mod = P["norm"](l), P["mod_p"](l)
    tm = 512
    qkv, qkv_slab = _proj(x, norm, mod, P["w_qkv"], l, tm=1024, tn=ATTN_WIDTH, slab_out=True)
    rest = _proj(x, norm, mod, P["w_rest"], l, tm=1024, tn=1024, slab_out=False)
    ols = [_attention(qkv_slab, P["bias_tab"], g) for g in range(N_ATTN_GROUPS)]
    attn = [o for o, _ in ols] + [ls for _, ls in ols]
    merged, u_tail = _branches(rest, attn, None, P["branch"], l, tm=tm, tc=512, step=False)
    x1 = _oproj(merged, P["w_o"], x, norm, mod, l, tm=tm)
    x2, tail_g, tail_v = _ffn(x1, norm, mod, P["w_up"], P["ffn_cw"], P["w_down"], None, l,
                              tm=tm, tf=512, step=False)
    new_kv = [_kv_pack(qkv, g, min(w, s)) for g, (w, _) in enumerate(ATTN_GROUPS)]
    new_pool = rest[:, -POOL_STATE:, :POOL_WIDTH]
    new_conv = u_tail[:, -1, -(CONV_K - 1):]
    new_ffn = jnp.concatenate([tail_g[:, -1, -(FFN_K - 1):], tail_v[:, -1, -(FFN_K - 1):]], axis=-1)
    return x2, (new_kv[0], new_kv[1], new_kv[2], new_pool, new_conv, new_ffn)


def _pad_rows(a, axis):
    pad = [(0, 0)] * a.ndim
    pad[axis] = (0, SAMPLE_ROWS - a.shape[axis])
    return jnp.pad(a, pad)


def _sample_layer(x, l, P, S, nb):
    norm, mod = P["norm"](l), P["mod_s"](l)
    tm = SAMPLE_ROWS
    qkv = _proj(x, norm, mod, P["w_qkv"], l, tm=tm, tn=ATTN_WIDTH, slab_out=False)
    rest = _proj(x, norm, mod, P["w_rest"], l, tm=tm, tn=1024, slab_out=False)
    attn = _attention_step(qkv, S["caches"], P["bias_c"], P["bias_n"], l, nb)
    attn = _pad_rows(attn.reshape(1, nb, GROUP_WIDTH), 1)
    merged, u = _branches(rest, attn, (S["pool_t"], S["conv_t"]), P["branch"], l, tm=tm, tc=512, step=True)
    x1 = _oproj(merged, P["w_o"], x, norm, mod, l, tm=tm)
    x2, up_g, up_v = _ffn(x1, norm, mod, P["w_up"], P["ffn_cw"], P["w_down"], S["ffn_t"], l,
                          tm=tm, tf=512, step=True)
    new_kv = [_kv_pack(qkv[0, :nb, None], g, 1) for g in range(N_ATTN_GROUPS)]
    new_pool = jnp.concatenate([S["pool"][l][:, 1:], rest[0, :nb, None, :POOL_WIDTH]], axis=1)
    new_conv = jnp.concatenate([S["conv"][l][:, 1:], u[0, 0, :nb, None]], axis=1)
    up_new = jnp.concatenate([up_g[0, 0, :nb], up_v[0, 0, :nb]], axis=-1)
    new_ffn = jnp.concatenate([S["ffn"][l][:, 1:], up_new[:, None]], axis=1)
    return x2, (new_kv[0], new_kv[1], new_kv[2], new_pool, new_conv, new_ffn)


def kernel(x_prompt, x_sample, c_prompt, c_sample, cache_kv_w128, cache_kv_w512, cache_kv_w2048, state_pool, state_conv, state_ffn_conv, rel_bias, norm_g, w_ada, b_ada, w_in, w_attn_br, w_pool_grp, pool_scale, w_pool_br, conv_w, w_conv_br, w_o, w_up, ffn_conv_w, w_down):
    nbp = x_prompt.shape[0]
    nbs, tdec, _ = x_sample.shape
    assert tdec == 1 and nbs <= SAMPLE_ROWS
    caches = (cache_kv_w128, cache_kv_w512, cache_kv_w2048)
    for (w, d), c in zip(ATTN_GROUPS, caches):
        assert c.shape[2] == w == N_DIL_KEYS * d, "cache must hold exactly one window"

    bias_tab, bias_c, bias_n = _bias_tables(rel_bias)
    c_rows = -(-(SAMPLE_ROWS + nbp) // 8) * 8
    c_all = jnp.zeros((c_rows, D_MODEL), F32).at[:nbs].set(c_sample).at[SAMPLE_ROWS:SAMPLE_ROWS + nbp].set(c_prompt)
    mod_all = _ada(c_all, w_ada, b_ada)

    P = dict(
        norm=functools.partial(_norm_view, norm_g),
        mod_p=functools.partial(_mod_view, mod_all, prompt=True),
        mod_s=functools.partial(_mod_view, mod_all, prompt=False),
        w_qkv=w_in[:, :, :QKV_WIDTH].astype(BF16),
        w_rest=w_in[:, :, QKV_WIDTH:].astype(BF16),
        branch=(w_pool_grp.astype(BF16), pool_scale[:, None, :], w_pool_br.astype(BF16),
                conv_w, w_conv_br.astype(BF16), w_attn_br.astype(BF16)),
        w_o=w_o.astype(BF16), w_up=w_up.astype(BF16), ffn_cw=ffn_conv_w, w_down=w_down.astype(BF16),
        bias_tab=bias_tab, bias_c=bias_c, bias_n=bias_n,
    )
    S = dict(
        caches=[c.reshape(DEPTH, nbs, N_DIL_KEYS, -1) for c in caches],
        pool=state_pool, conv=state_conv, ffn=state_ffn_conv,
        pool_t=_pad_rows(state_pool.transpose(0, 2, 1, 3), 2),
        conv_t=_pad_rows(state_conv.transpose(0, 2, 1, 3), 2),
        ffn_t=_pad_rows(state_ffn_conv.transpose(0, 2, 1, 3), 2),
    )
    yp = x_prompt
    ys = _pad_rows(x_sample.reshape(1, nbs, D_MODEL), 1)
    st_p, st_s = [], []
    for l in range(DEPTH):
        yp, sp = _prompt_layer(yp, l, P)
        ys, ss = _sample_layer(ys, l, P, S, nbs)
        st_p.append(sp)
        st_s.append(ss)
    outs_p = [jnp.stack([s[k] for s in st_p]) for k in range(6)]
    outs_s = [jnp.stack([s[k] for s in st_s]) for k in range(6)]
    return (yp, ys[0, :nbs, None, :], *outs_p, *outs_s)
```

```python
import functools

import numpy as np
import jax
import jax.numpy as jnp
from jax import lax
from jax.experimental import pallas as pl
from jax.experimental.pallas import tpu as pltpu

F32 = jnp.float32
BF16 = jnp.bfloat16

D_MODEL = 2048
DEPTH = 2
HEAD_DIM = 64
HEADS_PER_GROUP = 4
ATTN_GROUPS = ((128, 1), (512, 4), (2048, 16))
N_ATTN_GROUPS = len(ATTN_GROUPS)
ATTN_WIDTH = N_ATTN_GROUPS * HEADS_PER_GROUP * HEAD_DIM
GROUP_WIDTH = HEADS_PER_GROUP * HEAD_DIM
N_DIL_KEYS = 128
N_BUCKETS = 32
MAX_DISTANCE = 2048
ATTN_SCALE = HEAD_DIM ** -0.5
POOL_WINDOWS = (2, 4, 8, 16)
POOL_GROUP = 128
POOL_WIDTH = 512
POOL_STATE = 15
CONV_CH = 512
CONV_K = 3
D_FF = 5632
FFN_K = 3
N_MOD = 6
N_NORM = 4
EPS = 1e-6
NEG_INF = -1e30

LANES = 128
HIST = 16
SAMPLE_ROWS = 16
QKV_WIDTH = 3 * ATTN_WIDTH
QKV_SLABS = QKV_WIDTH // LANES
REST_WIDTH = POOL_WIDTH + 3 * CONV_CH + 3 * D_MODEL
ACT_WIDTH = POOL_WIDTH + 3 * CONV_CH
FFN_ROW_CHUNKS = 2
VMEM_CAP = 56 * 1024 * 1024


def _vmem_limit(block_bytes, scratch_bytes=0):
    est = 2 * block_bytes + scratch_bytes
    return int(min(VMEM_CAP, est + est // 4 + (4 << 20)))


def _nbytes(shape, dtype):
    return int(np.prod(shape)) * jnp.dtype(dtype).itemsize


def _rms(x, g):
    return x * lax.rsqrt(jnp.mean(x * x, axis=-1, keepdims=True) + EPS) * g


def _sigmoid(x):
    return 1.0 / (1.0 + jnp.exp(-x))


def _gelu_tanh(x):
    return 0.5 * x * (1.0 + jnp.tanh(np.sqrt(2.0 / np.pi) * (x + 0.044715 * (x * x * x))))


def _dot(a, b):
    return jnp.dot(a, b, preferred_element_type=F32)


def _tail2(ref):
    return ref[(0,) * (len(ref.shape) - 2)]


def _norm_view(norm_g, l):
    arr = norm_g.reshape(DEPTH * N_NORM, 1, D_MODEL)
    return arr, lambda k: pl.BlockSpec((1, 1, D_MODEL), lambda *_: (l * N_NORM + k, 0, 0))


def _mod_view(mod_all, l, prompt):
    if prompt:
        arr = mod_all.reshape(DEPTH, mod_all.shape[1], N_MOD, 1, D_MODEL)
        return arr, lambda k: pl.BlockSpec((1, 1, 1, 1, D_MODEL), lambda bi, *_: (l, SAMPLE_ROWS + bi, k, 0, 0))
    return mod_all, lambda k: pl.BlockSpec((1, SAMPLE_ROWS, D_MODEL), lambda bi, *_: (l, 0, k))


def _mod_rows(mod):
    return mod[1](0).block_shape[-2]


def _ada_kernel(c_ref, w_ref, b_ref, o_ref):
    c = c_ref[...]
    s = (c * _sigmoid(c)).astype(BF16)
    o_ref[0] = _dot(s, w_ref[0].astype(BF16)) + b_ref[0]


def _ada(c_all, w_ada, b_ada):
    rows = c_all.shape[0]
    n = w_ada.shape[-1]
    tn = 1024
    blocks = _nbytes((rows, D_MODEL), F32) + _nbytes((D_MODEL, tn), F32) + _nbytes((rows + 1, tn), F32)
    return pl.pallas_call(
        _ada_kernel,
        out_shape=jax.ShapeDtypeStruct((DEPTH, rows, n), F32),
        grid=(DEPTH, n // tn),
        in_specs=[
            pl.BlockSpec((rows, D_MODEL), lambda l, j: (0, 0)),
            pl.BlockSpec((1, D_MODEL, tn), lambda l, j: (l, 0, j)),
            pl.BlockSpec((1, 1, tn), lambda l, j: (l, 0, j)),
        ],
        out_specs=pl.BlockSpec((1, rows, tn), lambda l, j: (l, 0, j)),
        compiler_params=pltpu.CompilerParams(
            dimension_semantics=("parallel", "parallel"),
            vmem_limit_bytes=_vmem_limit(blocks, _nbytes((D_MODEL, tn), BF16))),
        name="ada",
    )(c_all, w_ada, b_ada.reshape(DEPTH, 1, n))


def _proj_kernel(x_ref, g_ref, sc_ref, sh_ref, w_ref, *refs, slabs):
    h_ref = refs[-1]

    @pl.when(pl.program_id(2) == 0)
    def _():
        y = _rms(x_ref[0], g_ref[0])
        h_ref[...] = (y * (1.0 + _tail2(sc_ref)) + _tail2(sh_ref)).astype(BF16)

    res = _dot(h_ref[...], w_ref[0])
    refs[0][0] = res
    if slabs:
        for s in range(slabs):
            refs[1][0, s] = res[:, s * LANES:(s + 1) * LANES]


def _proj(x, norm, mod, w, l, *, tm, tn, slab_out):
    b, s, d = x.shape
    n = w.shape[2]
    slabs = tn // LANES if slab_out else 0
    out_shape = [jax.ShapeDtypeStruct((b, s, n), F32)]
    out_specs = [pl.BlockSpec((1, tm, tn), lambda bi, i, j: (bi, i, j))]
    if slab_out:
        out_shape.append(jax.ShapeDtypeStruct((b, n // LANES, s, LANES), F32))
        out_specs.append(pl.BlockSpec((1, slabs, tm, LANES), lambda bi, i, j: (bi, j, i, 0)))
    blocks = (_nbytes((tm, d), F32) + _nbytes((2 * _mod_rows(mod) + 1, d), F32) + _nbytes((d, tn), BF16)
              + (2 if slab_out else 1) * _nbytes((tm, tn), F32))
    res = pl.pallas_call(
        functools.partial(_proj_kernel, slabs=slabs),
        out_shape=out_shape,
        grid=(b, s // tm, n // tn),
        in_specs=[
            pl.BlockSpec((1, tm, d), lambda bi, i, j: (bi, i, 0)),
            norm[1](0), mod[1](1), mod[1](0),
            pl.BlockSpec((1, d, tn), lambda bi, i, j: (l, 0, j)),
        ],
        out_specs=out_specs,
        scratch_shapes=[pltpu.VMEM((tm, d), BF16)],
        compiler_params=pltpu.CompilerParams(
            dimension_semantics=("parallel", "parallel", "arbitrary"),
            vmem_limit_bytes=_vmem_limit(blocks, _nbytes((tm, d), BF16) + _nbytes((tm, tn), F32))),
        name="proj_slab" if slab_out else "proj",
    )(x, norm[0], mod[0], mod[0], w)
    return res if slab_out else res[0]


def _head_masks(rows):
    lane = lax.broadcasted_iota(jnp.int32, (rows, GROUP_WIDTH), 1)
    return [(lane >= h * HEAD_DIM) & (lane < (h + 1) * HEAD_DIM) for h in range(HEADS_PER_GROUP)]


def _attn_kernel(q_ref, kc_ref, kp_ref, vc_ref, vp_ref, bias_ref, o_ref, lse_ref, *, d, sb):
    i = pl.program_id(1)
    nq = N_DIL_KEYS
    span = nq * d
    ncb = sb // span
    hm = _head_masks(1)
    col = lax.broadcasted_iota(jnp.int32, (1, 2 * nq), 1)
    bias = bias_ref[0]

    def rows(start):
        return pl.ds(start, nq, stride=d) if d > 1 else pl.ds(start, nq)

    def load(ref, start):
        return jnp.concatenate([ref[0, s, rows(start), :] for s in range(2)], axis=1)

    def one_block(r, jb):
        qs = jb * span + r
        q = load(q_ref, qs) * ATTN_SCALE
        if jb == 0:
            lo = sb - span + r
            k_lo, v_lo = load(kp_ref, lo), load(vp_ref, lo)
        else:
            lo = (jb - 1) * span + r
            k_lo, v_lo = load(kc_ref, lo), load(vc_ref, lo)
        kcat = jnp.concatenate([k_lo, load(kc_ref, qs)], axis=0).astype(BF16)
        vcat = jnp.concatenate([v_lo, load(vc_ref, qs)], axis=0).astype(BF16)
        qm = jnp.concatenate([jnp.where(hm[h], q, 0.0) for h in range(HEADS_PER_GROUP)], axis=0).astype(BF16)
        s = lax.dot_general(qm, kcat, (((1,), (1,)), ((), ())), preferred_element_type=F32) + bias
        if jb == 0:
            s = jnp.where((col < nq) & (i == 0), NEG_INF, s)
        m = jnp.max(s, axis=-1, keepdims=True)
        p = jnp.exp(s - m)
        l = jnp.sum(p, axis=-1, keepdims=True)
        oall = _dot((p * (1.0 / l)).astype(BF16), vcat)
        lse = m + jnp.log(l)
        o = jnp.zeros((nq, GROUP_WIDTH), F32)
        ls = jnp.zeros((nq, GROUP_WIDTH), F32)
        for h in range(HEADS_PER_GROUP):
            o = jnp.where(hm[h], oall[h * nq:(h + 1) * nq], o)
            ls = jnp.where(hm[h], lse[h * nq:(h + 1) * nq], ls)
        for sl in range(2):
            o_ref[0, sl, rows(qs), :] = o[:, sl * LANES:(sl + 1) * LANES]
            lse_ref[0, sl, rows(qs), :] = ls[:, sl * LANES:(sl + 1) * LANES]

    if d == 1:
        for jb in range(ncb):
            one_block(0, jb)
    else:
        def body(r, carry):
            for jb in range(ncb):
                one_block(r, jb)
            return carry
        lax.fori_loop(0, d, body, 0)


def _attention(qkv, bias_tab, g):
    b, _, s, _ = qkv.shape
    d = ATTN_GROUPS[g][1]
    sb = max(N_DIL_KEYS * d, 512)
    blk = (1, 2, sb, LANES)
    kslab, vslab = ATTN_WIDTH // GROUP_WIDTH + g, 2 * ATTN_WIDTH // GROUP_WIDTH + g
    prev = lambda i: jnp.maximum(i - 1, 0)
    blocks = 7 * _nbytes(blk, F32) + _nbytes((4 * N_DIL_KEYS, 2 * N_DIL_KEYS), F32)
    out = jax.ShapeDtypeStruct((b, 2, s, LANES), F32)
    return pl.pallas_call(
        functools.partial(_attn_kernel, d=d, sb=sb),
        out_shape=(out, out),
        grid=(b, s // sb),
        in_specs=[
            pl.BlockSpec(blk, lambda bi, i: (bi, g, i, 0)),
            pl.BlockSpec(blk, lambda bi, i: (bi, kslab, i, 0)),
            pl.BlockSpec(blk, lambda bi, i: (bi, kslab, prev(i), 0)),
            pl.BlockSpec(blk, lambda bi, i: (bi, vslab, i, 0)),
            pl.BlockSpec(blk, lambda bi, i: (bi, vslab, prev(i), 0)),
            pl.BlockSpec((1, 4 * N_DIL_KEYS, 2 * N_DIL_KEYS), lambda bi, i: (g, 0, 0)),
        ],
        out_specs=(pl.BlockSpec(blk, lambda bi, i: (bi, 0, i, 0)),
                   pl.BlockSpec(blk, lambda bi, i: (bi, 0, i, 0))),
        compiler_params=pltpu.CompilerParams(
            dimension_semantics=("parallel", "parallel"),
            vmem_limit_bytes=_vmem_limit(blocks, 8 << 20)),
        name=f"attn_d{d}",
    )(qkv, qkv, qkv, qkv, qkv, bias_tab)


def _attn_step_kernel(qkv_ref, c0_ref, c1_ref, c2_ref, bc_ref, bn_ref, o_ref):
    outs, lses = [], []
    for g, c_ref in enumerate((c0_ref, c1_ref, c2_ref)):
        q = qkv_ref[0, g] * ATTN_SCALE
        kn = qkv_ref[0, N_ATTN_GROUPS + g]
        vn = qkv_ref[0, 2 * N_ATTN_GROUPS + g]
        k = c_ref[0, 0, :, 0, 0]
        v = c_ref[0, 0, :, 0, 1]
        s_c = jnp.sum(k * q[None], axis=-1, keepdims=True) + bc_ref[g]
        s_n = jnp.sum(kn * q, axis=-1, keepdims=True) + bn_ref[g]
        m = jnp.maximum(jnp.max(s_c, axis=0), s_n)
        p_c = jnp.exp(s_c - m[None])
        p_n = jnp.exp(s_n - m)
        l = jnp.sum(p_c, axis=0) + p_n
        inv = 1.0 / l
        outs.append(jnp.sum((p_c * inv[None]) * v, axis=0) + (p_n * inv) * vn)
        lses.append(m + jnp.log(l))
    mx = jnp.maximum(jnp.maximum(lses[0], lses[1]), lses[2])
    w = [jnp.exp(ls - mx) for ls in lses]
    tot = w[0] + w[1] + w[2]
    o_ref[0] = (w[0] * outs[0] + w[1] * outs[1] + w[2] * outs[2]) * (1.0 / tot)


def _attention_step(qkv, caches, bias_c, bias_n, l, nb):
    cblk = (1, 1, N_DIL_KEYS, 1, 2, HEADS_PER_GROUP, HEAD_DIM)
    cache_specs = [pl.BlockSpec(cblk, lambda bi: (l, bi, 0, 0, 0, 0, 0)) for _ in range(N_ATTN_GROUPS)]
    tile = 8 * LANES
    blocks = (3 * 2 * N_DIL_KEYS + 3 * N_DIL_KEYS + 16) * _nbytes((tile,), F32)
    return pl.pallas_call(
        _attn_step_kernel,
        out_shape=jax.ShapeDtypeStruct((nb, HEADS_PER_GROUP, HEAD_DIM), F32),
        grid=(nb,),
        in_specs=[pl.BlockSpec((1,) + qkv.shape[1:], lambda bi: (bi, 0, 0, 0))] + cache_specs + [
            pl.BlockSpec(bias_c.shape, lambda bi: (0, 0, 0, 0)),
            pl.BlockSpec(bias_n.shape, lambda bi: (0, 0, 0)),
        ],
        out_specs=pl.BlockSpec((1, HEADS_PER_GROUP, HEAD_DIM), lambda bi: (bi, 0, 0)),
        compiler_params=pltpu.CompilerParams(
            dimension_semantics=("parallel",),
            vmem_limit_bytes=_vmem_limit(blocks, 8 << 20)),
        name="attn_step",
    )(qkv, *caches, bias_c, bias_n)


def _branch_kernel(*refs, tm, step):
    if step:
        (act_ref, pst_ref, cst_ref, ga_ref, gb_ref, gc_ref, at_ref,
         wgrp_ref, pscale_ref, wpb_ref, cw_ref, wcb_ref, wab_ref,
         out_ref, u_ref, pzs_ref, cb_ref, comb_ref) = refs
    else:
        (act_ref, hist_ref, ga_ref, gb_ref, gc_ref, o0_ref, o1_ref, o2_ref, l0_ref, l1_ref, l2_ref,
         wgrp_ref, pscale_ref, wpb_ref, cw_ref, wcb_ref, wab_ref,
         out_ref, u_ref, pzs_ref, cb_ref, comb_ref, pe_ref, ue_ref) = refs
    i = pl.program_id(1)

    @pl.when(pl.program_id(2) == 0)
    def _():
        p = act_ref[0, :, 0:POOL_WIDTH]
        gate_b = act_ref[0, :, POOL_WIDTH:POOL_WIDTH + CONV_CH]
        u = act_ref[0, :, POOL_WIDTH + CONV_CH:POOL_WIDTH + 2 * CONV_CH] * \
            act_ref[0, :, POOL_WIDTH + 2 * CONV_CH:POOL_WIDTH + 3 * CONV_CH]
        cw = cw_ref[0]
        if step:
            acc = p
            sums = {}
            for k in range(1, max(POOL_WINDOWS)):
                acc = acc + pst_ref[0, POOL_STATE - k]
                sums[k + 1] = acc
            means = [sums[w][:, gi * POOL_GROUP:(gi + 1) * POOL_GROUP] * (1.0 / w)
                     for gi, w in enumerate(POOL_WINDOWS)]
            conv = cst_ref[0, 0] * cw[0:1] + cst_ref[0, 1] * cw[1:2] + u * cw[2:3]
            u_ref[0, 0] = u
            comb_ref[...] = at_ref[0].astype(BF16)
        else:
            first = i == 0
            hist_p = hist_ref[0, :, 0:POOL_WIDTH]
            hist_u = hist_ref[0, :, POOL_WIDTH + CONV_CH:POOL_WIDTH + 2 * CONV_CH] * \
                hist_ref[0, :, POOL_WIDTH + 2 * CONV_CH:POOL_WIDTH + 3 * CONV_CH]
            pe_ref[0:HIST] = jnp.where(first, 0.0, hist_p)
            ue_ref[0:HIST] = jnp.where(first, 0.0, hist_u)
            pe_ref[HIST:HIST + tm] = p
            ue_ref[HIST:HIST + tm] = u
            t = i * tm + lax.broadcasted_iota(jnp.int32, (tm, 1), 0)
            means = []
            for gi, w in enumerate(POOL_WINDOWS):
                cs = slice(gi * POOL_GROUP, (gi + 1) * POOL_GROUP)
                acc = pe_ref[HIST:HIST + tm, cs]
                for k in range(1, w):
                    acc = acc + pe_ref[HIST - k:HIST - k + tm, cs]
                cnt = jnp.minimum(t + 1, w).astype(F32)
                means.append(acc * (1.0 / cnt))
            conv = (ue_ref[HIST - 2:HIST - 2 + tm] * cw[0:1] + ue_ref[HIST - 1:HIST - 1 + tm] * cw[1:2]
                    + u * cw[2:3])
            u_ref[0, 0] = ue_ref[HIST + tm - 8:HIST + tm]
            for sl in range(2):
                ls = [r[0, sl] for r in (l0_ref, l1_ref, l2_ref)]
                os_ = [r[0, sl] for r in (o0_ref, o1_ref, o2_ref)]
                mx = jnp.maximum(jnp.maximum(ls[0], ls[1]), ls[2])
                w_ = [jnp.exp(x - mx) for x in ls]
                tot = w_[0] + w_[1] + w_[2]
                comb = (w_[0] * os_[0] + w_[1] * os_[1] + w_[2] * os_[2]) * (1.0 / tot)
                comb_ref[:, sl * LANES:(sl + 1) * LANES] = comb.astype(BF16)
        pscale = pscale_ref[0]
        for gi in range(len(POOL_WINDOWS)):
            cs = slice(gi * POOL_GROUP, (gi + 1) * POOL_GROUP)
            pm = (means[gi] - p[:, cs]).astype(BF16)
            pz = _dot(pm, wgrp_ref[0, gi])
            pzs_ref[:, cs] = (pz * pscale[:, cs]).astype(BF16)
        cb_ref[...] = (gate_b * conv).astype(BF16)

    br_a = _dot(pzs_ref[...], wpb_ref[0])
    br_b = _dot(cb_ref[...], wcb_ref[0])
    br_c = _dot(comb_ref[...], wab_ref[0])
    merged = _sigmoid(ga_ref[0]) * br_a + _sigmoid(gb_ref[0]) * br_b + _sigmoid(gc_ref[0]) * br_c
    out_ref[0] = merged.astype(BF16)


def _branches(rest, attn, states, weights, l, *, tm, tc, step):
    b, s, _ = rest.shape
    wgrp, pscale, wpb, cw, wcb, wab = weights
    ni = s // tm
    gblk = ACT_WIDTH // tc
    gate_specs = [pl.BlockSpec((1, tm, tc), lambda bi, i, c, k=k: (bi, i, gblk + k * (D_MODEL // tc) + c))
                  for k in range(3)]
    w_specs = [
        pl.BlockSpec((1,) + wgrp.shape[1:], lambda bi, i, c: (l, 0, 0, 0)),
        pl.BlockSpec((1,) + pscale.shape[1:], lambda bi, i, c: (l, 0, 0)),
        pl.BlockSpec((1, POOL_WIDTH, tc), lambda bi, i, c: (l, 0, c)),
        pl.BlockSpec((1,) + cw.shape[1:], lambda bi, i, c: (l, 0, 0)),
        pl.BlockSpec((1, CONV_CH, tc), lambda bi, i, c: (l, 0, c)),
        pl.BlockSpec((1, GROUP_WIDTH, tc), lambda bi, i, c: (l, 0, c)),
    ]
    act_spec = pl.BlockSpec((1, tm, ACT_WIDTH), lambda bi, i, c: (bi, i, 0))
    scratch = [pltpu.VMEM((tm, POOL_WIDTH), BF16), pltpu.VMEM((tm, CONV_CH), BF16),
               pltpu.VMEM((tm, GROUP_WIDTH), BF16)]
    blocks = (_nbytes((tm, ACT_WIDTH), F32) + 3 * _nbytes((tm, tc), F32) + _nbytes(wgrp.shape[1:], BF16)
              + _nbytes((POOL_WIDTH + CONV_CH + GROUP_WIDTH, tc), BF16) + _nbytes((tm, tc), BF16))
    if step:
        pst, cst = states
        ins = [rest, pst, cst, rest, rest, rest, attn]
        in_specs = [act_spec,
                    pl.BlockSpec((1,) + pst.shape[1:], lambda bi, i, c: (l, 0, 0, 0)),
                    pl.BlockSpec((1,) + cst.shape[1:], lambda bi, i, c: (l, 0, 0, 0))] + gate_specs + [
                    pl.BlockSpec((1, tm, GROUP_WIDTH), lambda bi, i, c: (bi, i, 0))]
        urows = tm
        blocks += _nbytes(pst.shape[1:], F32) + _nbytes(cst.shape[1:], F32)
    else:
        hblk = tm // HIST
        slab = pl.BlockSpec((1, 2, tm, LANES), lambda bi, i, c: (bi, 0, i, 0))
        ins = [rest, rest, rest, rest, rest] + list(attn)
        in_specs = [act_spec,
                    pl.BlockSpec((1, HIST, ACT_WIDTH), lambda bi, i, c: (bi, jnp.maximum(i * hblk - 1, 0), 0))
                    ] + gate_specs + [slab] * 6
        scratch += [pltpu.VMEM((tm + HIST, POOL_WIDTH), F32), pltpu.VMEM((tm + HIST, CONV_CH), F32)]
        urows = 8
        blocks += 6 * _nbytes((2, tm, LANES), F32) + _nbytes((HIST, ACT_WIDTH), F32)
    scratch_bytes = 3 * _nbytes((tm + HIST, POOL_WIDTH), F32) + 4 * _nbytes((tm, tc), F32)
    return pl.pallas_call(
        functools.partial(_branch_kernel, tm=tm, step=step),
        out_shape=(jax.ShapeDtypeStruct((b, s, D_MODEL), BF16),
                   jax.ShapeDtypeStruct((b, ni, urows, CONV_CH), F32)),
        grid=(b, ni, D_MODEL // tc),
        in_specs=in_specs + w_specs,
        out_specs=(pl.BlockSpec((1, tm, tc), lambda bi, i, c: (bi, i, c)),
                   pl.BlockSpec((1, 1, urows, CONV_CH), lambda bi, i, c: (bi, i, 0, 0))),
        scratch_shapes=scratch,
        compiler_params=pltpu.CompilerParams(
            dimension_semantics=("parallel", "parallel", "arbitrary"),
            vmem_limit_bytes=_vmem_limit(blocks, scratch_bytes)),
        name="branches_step" if step else "branches",
    )(*ins, wgrp, pscale, wpb, cw, wcb, wab)


def _oproj_kernel(m_ref, w_ref, x_ref, g_ref, gate_ref, o_ref):
    mix = _dot(m_ref[0], w_ref[0])
    o_ref[0] = x_ref[0] + _tail2(gate_ref) * _rms(mix, g_ref[0])


def _oproj(merged, w_o, x, norm, mod, l, *, tm):
    b, s, d = x.shape
    blocks = (_nbytes((tm, d), BF16) + _nbytes((d, d), BF16) + 2 * _nbytes((tm, d), F32)
              + _nbytes((_mod_rows(mod) + 1, d), F32))
    return pl.pallas_call(
        _oproj_kernel,
        out_shape=jax.ShapeDtypeStruct((b, s, d), F32),
        grid=(b, s // tm),
        in_specs=[
            pl.BlockSpec((1, tm, d), lambda bi, i: (bi, i, 0)),
            pl.BlockSpec((1, d, d), lambda bi, i: (l, 0, 0)),
            pl.BlockSpec((1, tm, d), lambda bi, i: (bi, i, 0)),
            norm[1](1), mod[1](2),
        ],
        out_specs=pl.BlockSpec((1, tm, d), lambda bi, i: (bi, i, 0)),
        compiler_params=pltpu.CompilerParams(
            dimension_semantics=("parallel", "parallel"),
            vmem_limit_bytes=_vmem_limit(blocks, _nbytes((tm, d), F32))),
        name="oproj",
    )(merged, w_o, x, norm[0], mod[0])


def _ffn_kernel(*refs, tm, step, nchunk):
    if step:
        (x_ref, g2_ref, sc_ref, sh_ref, wg_ref, wv_ref, cwg_ref, cwv_ref, wd_ref, g3_ref, gate_ref,
         stg_ref, stv_ref, o_ref, tg_ref, tv_ref, h_ref, acc_ref) = refs
        hist = 0
    else:
        (x_ref, xp_ref, g2_ref, sc_ref, sh_ref, wg_ref, wv_ref, cwg_ref, cwv_ref, wd_ref, g3_ref, gate_ref,
         o_ref, tg_ref, tv_ref, h_ref, acc_ref, ug_ref, uv_ref) = refs
        hist = HIST
    i = pl.program_id(1)
    j = pl.program_id(2)

    def modulate(x):
        return (_rms(x, g2_ref[0]) * (1.0 + _tail2(sc_ref)) + _tail2(sh_ref)).astype(BF16)

    @pl.when(j == 0)
    def _():
        if not step:
            h_ref[0:HIST] = modulate(xp_ref[0])
        h_ref[hist:hist + tm] = modulate(x_ref[0])
        acc_ref[...] = jnp.zeros_like(acc_ref)

    cwg = cwg_ref[0]
    cwv = cwv_ref[0]
    if step:
        h = h_ref[...]
        up_g = _dot(h, wg_ref[0])
        up_v = _dot(h, wv_ref[0])
        uc_g = stg_ref[0, 0] * cwg[0:1] + stg_ref[0, 1] * cwg[1:2] + up_g * cwg[2:3]
        uc_v = stv_ref[0, 0] * cwv[0:1] + stv_ref[0, 1] * cwv[1:2] + up_v * cwv[2:3]
        tg_ref[0, 0] = up_g
        tv_ref[0, 0] = up_v
        acc_ref[...] += _dot((_gelu_tanh(uc_g) * uc_v).astype(BF16), wd_ref[0])
    else:
        first = i == 0
        rc = tm // nchunk
        for c in range(nchunk):
            lo = 0 if c == 0 else HIST + c * rc
            hi = HIST + (c + 1) * rc
            h = h_ref[lo:hi]
            for u_ref, w_ref in ((ug_ref, wg_ref), (uv_ref, wv_ref)):
                up = _dot(h, w_ref[0])
                if c == 0:
                    u_ref[0:HIST] = jnp.where(first, 0.0, up[0:HIST])
                    u_ref[HIST:hi] = up[HIST:]
                else:
                    u_ref[lo:hi] = up

            def conv(ref, cw):
                base = HIST + c * rc
                return (ref[base - 2:base - 2 + rc] * cw[0:1] + ref[base - 1:base - 1 + rc] * cw[1:2]
                        + ref[base:base + rc] * cw[2:3])
            act = (_gelu_tanh(conv(ug_ref, cwg)) * conv(uv_ref, cwv)).astype(BF16)
            acc_ref[c * rc:(c + 1) * rc] += _dot(act, wd_ref[0])
        tg_ref[0, 0] = ug_ref[HIST + tm - 8:HIST + tm]
        tv_ref[0, 0] = uv_ref[HIST + tm - 8:HIST + tm]

    @pl.when(j == pl.num_programs(2) - 1)
    def _():
        o_ref[0] = x_ref[0] + _tail2(gate_ref) * _rms(acc_ref[...], g3_ref[0])


def _ffn(x, norm, mod, w_up, cw, w_down, state, l, *, tm, tf, step):
    b, s, d = x.shape
    ni, nj = s // tm, D_FF // tf
    trows = tm if step else 8
    w_specs = [
        pl.BlockSpec((1, d, tf), lambda bi, i, j: (l, 0, j)),
        pl.BlockSpec((1, d, tf), lambda bi, i, j: (l, 0, nj + j)),
        pl.BlockSpec((1, FFN_K, tf), lambda bi, i, j: (l, 0, j)),
        pl.BlockSpec((1, FFN_K, tf), lambda bi, i, j: (l, 0, nj + j)),
        pl.BlockSpec((1, tf, d), lambda bi, i, j: (l, j, 0)),
    ]
    x_spec = pl.BlockSpec((1, tm, d), lambda bi, i, j: (bi, i, 0))
    hist = 0 if step else HIST
    scratch = [pltpu.VMEM((tm + hist, d), BF16), pltpu.VMEM((tm, d), F32)]
    blocks = (2 * _nbytes((tm, d), F32) + 3 * _nbytes((d, tf), BF16) + 2 * _nbytes((trows, tf), F32)
              + _nbytes((3 * _mod_rows(mod) + 2, d), F32))
    mods = [norm[1](2), mod[1](4), mod[1](3)]
    tailp = [norm[1](3), mod[1](5)]
    if step:
        ins = [x, norm[0], mod[0], mod[0], w_up, w_up, cw, cw, w_down, norm[0], mod[0], state, state]
        in_specs = [x_spec] + mods + w_specs + tailp + [
            pl.BlockSpec((1, FFN_K - 1, tm, tf), lambda bi, i, j: (l, 0, 0, j)),
            pl.BlockSpec((1, FFN_K - 1, tm, tf), lambda bi, i, j: (l, 0, 0, nj + j))]
        blocks += 2 * _nbytes((FFN_K - 1, tm, tf), F32)
    else:
        hblk = tm // HIST
        ins = [x, x, norm[0], mod[0], mod[0], w_up, w_up, cw, cw, w_down, norm[0], mod[0]]
        in_specs = [x_spec,
                    pl.BlockSpec((1, HIST, d), lambda bi, i, j: (bi, jnp.maximum(i * hblk - 1, 0), 0))
                    ] + mods + w_specs + tailp
        scratch += [pltpu.VMEM((tm + HIST, tf), F32), pltpu.VMEM((tm + HIST, tf), F32)]
        blocks += _nbytes((HIST, d), F32)
    scratch_bytes = (_nbytes((tm + hist, d), BF16) + _nbytes((tm, d), F32)
                     + 6 * _nbytes((tm + hist, tf), F32))
    tail = jax.ShapeDtypeStruct((b, ni, trows, D_FF), F32)
    tail_spec = lambda: pl.BlockSpec((1, 1, trows, tf), lambda bi, i, j: (bi, i, 0, j))
    return pl.pallas_call(
        functools.partial(_ffn_kernel, tm=tm, step=step, nchunk=FFN_ROW_CHUNKS),
        out_shape=(jax.ShapeDtypeStruct((b, s, d), F32), tail, tail),
        grid=(b, ni, nj),
        in_specs=in_specs,
        out_specs=(pl.BlockSpec((1, tm, d), lambda bi, i, j: (bi, i, 0)), tail_spec(), tail_spec()),
        scratch_shapes=scratch,
        compiler_params=pltpu.CompilerParams(
            dimension_semantics=("parallel", "parallel", "arbitrary"),
            vmem_limit_bytes=_vmem_limit(blocks, scratch_bytes)),
        name="ffn_step" if step else "ffn",
    )(*ins)


def _group_rel_bias(rel_bias):
    n = np.arange(N_DIL_KEYS + 1)
    max_exact = N_BUCKETS // 2
    out = []
    for g, (_, d) in enumerate(ATTN_GROUPS):
        dist = n * d
        large = max_exact + (np.log(np.maximum(dist, 1) / max_exact) / np.log(MAX_DISTANCE / max_exact)
                             * (N_BUCKETS - max_exact)).astype(np.int32)
        bucket = np.where(dist < max_exact, dist, np.minimum(large, N_BUCKETS - 1))
        out.append(rel_bias[bucket, g * HEADS_PER_GROUP:(g + 1) * HEADS_PER_GROUP].T)
    return jnp.stack(out)


def _bias_tables(rel_bias):
    bias_g = _group_rel_bias(rel_bias).astype(F32)
    n = N_DIL_KEYS
    gh = (N_ATTN_GROUPS, HEADS_PER_GROUP)
    ext = jnp.concatenate([bias_g[:, :, ::-1], jnp.full(gh + (n,), NEG_INF, F32)], axis=-1)
    tab = jnp.tile(ext, (1, 1, n))[:, :, :n * 2 * n].reshape(N_ATTN_GROUPS, HEADS_PER_GROUP * n, 2 * n)
    step_c = bias_g[:, :, :0:-1].transpose(0, 2, 1)[..., None]
    step_n = bias_g[:, :, 0:1]
    return tab, step_c, step_n


def _kv_pack(qkv, g, nrows):
    b = qkv.shape[0]
    k0 = ATTN_WIDTH + g * GROUP_WIDTH
    v0 = 2 * ATTN_WIDTH + g * GROUP_WIDTH
    k = qkv[:, -nrows:, k0:k0 + GROUP_WIDTH].reshape(b, nrows, HEADS_PER_GROUP, HEAD_DIM)
    v = qkv[:, -nrows:, v0:v0 + GROUP_WIDTH].reshape(b, nrows, HEADS_PER_GROUP, HEAD_DIM)
    return jnp.stack([k, v], axis=2)


def _prompt_layer(x, l, P):
    b, s, _ = x.shape
    norm, mod = P["norm"](l), P["mod_p"](l)
    tm = 512
    qkv, qkv_slab = _proj(x, norm, mod, P["w_qkv"], l, tm=1024, tn=ATTN_WIDTH, slab_out=True)
    rest = _proj(x, norm, mod, P["w_rest"], l, tm=1024, tn=1024, slab_out=False)
    ols = [_attention(qkv_slab, P["bias_tab"], g) for g in range(N_ATTN_GROUPS)]
    attn = [o for o, _ in ols] + [ls for _, ls in ols]
    merged, u_tail = _branches(rest, attn, None, P["branch"], l, tm=tm, tc=512, step=False)
    x1 = _oproj(merged, P["w_o"], x, norm, mod, l, tm=tm)
    x2, tail_g, tail_v = _ffn(x1, norm, mod, P["w_up"], P["ffn_cw"], P["w_down"], None, l,
                              tm=tm, tf=512, step=False)
    new_kv = [_kv_pack(qkv, g, min(w, s)) for g, (w, _) in enumerate(ATTN_GROUPS)]
    new_pool = rest[:, -POOL_STATE:, :POOL_WIDTH]
    new_conv = u_tail[:, -1, -(CONV_K - 1):]
    new_ffn = jnp.concatenate([tail_g[:, -1, -(FFN_K - 1):], tail_v[:, -1, -(FFN_K - 1):]], axis=-1)
    return x2, (new_kv[0], new_kv[1], new_kv[2], new_pool, new_conv, new_ffn)


def _pad_rows(a, axis):
    pad = [(0, 0)] * a.ndim
    pad[axis] = (0, SAMPLE_ROWS - a.shape[axis])
    return jnp.pad(a, pad)


def _sample_layer(x, l, P, S, nb):
    norm, mod = P["norm"](l), P["mod_s"](l)
    tm = SAMPLE_ROWS
    qkv = _proj(x, norm, mod, P["w_qkv"], l, tm=tm, tn=ATTN_WIDTH, slab_out=False)
    rest = _proj(x, norm, mod, P["w_rest"], l, tm=tm, tn=1024, slab_out=False)
    heads = qkv[0, :nb].reshape(nb, 3 * N_ATTN_GROUPS, HEADS_PER_GROUP, HEAD_DIM)
    attn = _attention_step(heads, S["caches"], P["bias_c"], P["bias_n"], l, nb)
    attn = _pad_rows(attn.reshape(1, nb, GROUP_WIDTH), 1)
    merged, u = _branches(rest, attn, (S["pool_t"], S["conv_t"]), P["branch"], l, tm=tm, tc=512, step=True)
    x1 = _oproj(merged, P["w_o"], x, norm, mod, l, tm=tm)
    x2, up_g, up_v = _ffn(x1, norm, mod, P["w_up"], P["ffn_cw"], P["w_down"], S["ffn_t"], l,
                          tm=tm, tf=512, step=True)
    new_kv = [_kv_pack(qkv[0, :nb, None], g, 1) for g in range(N_ATTN_GROUPS)]
    new_pool = jnp.concatenate([S["pool"][l][:, 1:], rest[0, :nb, None, :POOL_WIDTH]], axis=1)
    new_conv = jnp.concatenate([S["conv"][l][:, 1:], u[0, 0, :nb, None]], axis=1)
    up_new = jnp.concatenate([up_g[0, 0, :nb], up_v[0, 0, :nb]], axis=-1)
    new_ffn = jnp.concatenate([S["ffn"][l][:, 1:], up_new[:, None]], axis=1)
    return x2, (new_kv[0], new_kv[1], new_kv[2], new_pool, new_conv, new_ffn)


def kernel(x_prompt, x_sample, c_prompt, c_sample, cache_kv_w128, cache_kv_w512, cache_kv_w2048, state_pool, state_conv, state_ffn_conv, rel_bias, norm_g, w_ada, b_ada, w_in, w_attn_br, w_pool_grp, pool_scale, w_pool_br, conv_w, w_conv_br, w_o, w_up, ffn_conv_w, w_down):
    nbp = x_prompt.shape[0]
    nbs, tdec, _ = x_sample.shape
    assert tdec == 1 and nbs <= SAMPLE_ROWS
    caches = (cache_kv_w128, cache_kv_w512, cache_kv_w2048)
    for (w, d), c in zip(ATTN_GROUPS, caches):
        assert c.shape[2] == w == N_DIL_KEYS * d, "cache must hold exactly one window"

    bias_tab, bias_c, bias_n = _bias_tables(rel_bias)
    c_rows = -(-(SAMPLE_ROWS + nbp) // 8) * 8
    c_all = jnp.zeros((c_rows, D_MODEL), F32).at[:nbs].set(c_sample).at[SAMPLE_ROWS:SAMPLE_ROWS + nbp].set(c_prompt)
    mod_all = _ada(c_all, w_ada, b_ada)

    P = dict(
        norm=functools.partial(_norm_view, norm_g),
        mod_p=functools.partial(_mod_view, mod_all, prompt=True),
        mod_s=functools.partial(_mod_view, mod_all, prompt=False),
        w_qkv=w_in[:, :, :QKV_WIDTH].astype(BF16),
        w_rest=w_in[:, :, QKV_WIDTH:].astype(BF16),
        branch=(w_pool_grp.astype(BF16), pool_scale[:, None, :], w_pool_br.astype(BF16),
                conv_w, w_conv_br.astype(BF16), w_attn_br.astype(BF16)),
        w_o=w_o.astype(BF16), w_up=w_up.astype(BF16), ffn_cw=ffn_conv_w, w_down=w_down.astype(BF16),
        bias_tab=bias_tab, bias_c=bias_c, bias_n=bias_n,
    )
    S = dict(
        caches=[c.reshape(DEPTH, nbs, N_DIL_KEYS, d, 2, HEADS_PER_GROUP, HEAD_DIM)
                for c, (_, d) in zip(caches, ATTN_GROUPS)],
        pool=state_pool, conv=state_conv, ffn=state_ffn_conv,
        pool_t=_pad_rows(state_pool.transpose(0, 2, 1, 3), 2),
        conv_t=_pad_rows(state_conv.transpose(0, 2, 1, 3), 2),
        ffn_t=_pad_rows(state_ffn_conv.transpose(0, 2, 1, 3), 2),
    )
    yp = x_prompt
    ys = _pad_rows(x_sample.reshape(1, nbs, D_MODEL), 1)
    st_p, st_s = [], []
    for l in range(DEPTH):
        yp, sp = _prompt_layer(yp, l, P)
        ys, ss = _sample_layer(ys, l, P, S, nbs)
        st_p.append(sp)
        st_s.append(ss)
    outs_p = [jnp.stack([s[k] for s in st_p]) for k in range(6)]
    outs_s = [jnp.stack([s[k] for s in st_s]) for k in range(6)]
    return (yp, ys[0, :nbs, None, :], *outs_p, *outs_s)
```

```python
import functools

import numpy as np
import jax
import jax.numpy as jnp
from jax import lax
from jax.experimental import pallas as pl
from jax.experimental.pallas import tpu as pltpu

F32 = jnp.float32
BF16 = jnp.bfloat16

D_MODEL = 2048
DEPTH = 2
HEAD_DIM = 64
HEADS_PER_GROUP = 4
ATTN_GROUPS = ((128, 1), (512, 4), (2048, 16))
N_ATTN_GROUPS = len(ATTN_GROUPS)
ATTN_WIDTH = N_ATTN_GROUPS * HEADS_PER_GROUP * HEAD_DIM
GROUP_WIDTH = HEADS_PER_GROUP * HEAD_DIM
N_DIL_KEYS = 128
N_BUCKETS = 32
MAX_DISTANCE = 2048
ATTN_SCALE = HEAD_DIM ** -0.5
POOL_WINDOWS = (2, 4, 8, 16)
POOL_GROUP = 128
POOL_WIDTH = 512
POOL_STATE = 15
CONV_CH = 512
CONV_K = 3
D_FF = 5632
FFN_K = 3
N_MOD = 6
N_NORM = 4
EPS = 1e-6
NEG_INF = -1e30

LANES = 128
HIST = 16
SAMPLE_ROWS = 16
QKV_WIDTH = 3 * ATTN_WIDTH
QKV_SLABS = QKV_WIDTH // LANES
REST_WIDTH = POOL_WIDTH + 3 * CONV_CH + 3 * D_MODEL
ACT_WIDTH = POOL_WIDTH + 3 * CONV_CH
FFN_ROW_CHUNKS = 2
VMEM_CAP = 56 * 1024 * 1024


def _vmem_limit(block_bytes, scratch_bytes=0):
    est = 2 * block_bytes + scratch_bytes
    return int(min(VMEM_CAP, est + est // 4 + (4 << 20)))


def _nbytes(shape, dtype):
    return int(np.prod(shape)) * jnp.dtype(dtype).itemsize


def _rms(x, g):
    return x * lax.rsqrt(jnp.mean(x * x, axis=-1, keepdims=True) + EPS) * g


def _sigmoid(x):
    return 0.5 + 0.5 * jnp.tanh(0.5 * x)


def _gelu_tanh(x):
    return 0.5 * x * (1.0 + jnp.tanh(np.sqrt(2.0 / np.pi) * (x + 0.044715 * (x * x * x))))


def _dot(a, b):
    return jnp.dot(a, b, preferred_element_type=F32)


def _tail2(ref):
    return ref[(0,) * (len(ref.shape) - 2)]


def _norm_view(norm_g, l):
    arr = norm_g.reshape(DEPTH * N_NORM, 1, D_MODEL)
    return arr, lambda k: pl.BlockSpec((1, 1, D_MODEL), lambda *_: (l * N_NORM + k, 0, 0))


def _mod_view(mod_all, l, prompt):
    if prompt:
        arr = mod_all.reshape(DEPTH, mod_all.shape[1], N_MOD, 1, D_MODEL)
        return arr, lambda k: pl.BlockSpec((1, 1, 1, 1, D_MODEL), lambda bi, *_: (l, SAMPLE_ROWS + bi, k, 0, 0))
    return mod_all, lambda k: pl.BlockSpec((1, SAMPLE_ROWS, D_MODEL), lambda bi, *_: (l, 0, k))


def _mod_rows(mod):
    return mod[1](0).block_shape[-2]


def _ada_kernel(c_ref, w_ref, b_ref, o_ref):
    c = c_ref[...]
    s = (c * _sigmoid(c)).astype(BF16)
    o_ref[0] = _dot(s, w_ref[0].astype(BF16)) + b_ref[0]


def _ada(c_all, w_ada, b_ada):
    rows = c_all.shape[0]
    n = w_ada.shape[-1]
    tn = 1024
    blocks = _nbytes((rows, D_MODEL), F32) + _nbytes((D_MODEL, tn), F32) + _nbytes((rows + 1, tn), F32)
    return pl.pallas_call(
        _ada_kernel,
        out_shape=jax.ShapeDtypeStruct((DEPTH, rows, n), F32),
        grid=(DEPTH, n // tn),
        in_specs=[
            pl.BlockSpec((rows, D_MODEL), lambda l, j: (0, 0)),
            pl.BlockSpec((1, D_MODEL, tn), lambda l, j: (l, 0, j)),
            pl.BlockSpec((1, 1, tn), lambda l, j: (l, 0, j)),
        ],
        out_specs=pl.BlockSpec((1, rows, tn), lambda l, j: (l, 0, j)),
        compiler_params=pltpu.CompilerParams(
            dimension_semantics=("parallel", "parallel"),
            vmem_limit_bytes=_vmem_limit(blocks, _nbytes((D_MODEL, tn), BF16))),
        name="ada",
    )(c_all, w_ada, b_ada.reshape(DEPTH, 1, n))


def _modulate_rows(x_ref, g_ref, sc_ref, sh_ref, h_ref, row0, tm):
    y = _rms(x_ref[0], g_ref[0])
    h_ref[row0:row0 + tm] = (y * (1.0 + _tail2(sc_ref)) + _tail2(sh_ref)).astype(BF16)


def _proj_kernel(x_ref, g_ref, sc_ref, sh_ref, w_ref, o_ref, slab_ref, h_ref, *, slabs, tm):
    @pl.when(pl.program_id(2) == 0)
    def _():
        _modulate_rows(x_ref, g_ref, sc_ref, sh_ref, h_ref.at[0], 0, tm)

    res = _dot(h_ref[0], w_ref[0])
    o_ref[0] = res
    for s in range(slabs):
        slab_ref[0, s] = res[:, s * LANES:(s + 1) * LANES]


def _proj(x, norm, mod, w, l, *, tm, tn):
    b, s, d = x.shape
    n = w.shape[2]
    slabs = tn // LANES
    r = _mod_rows(mod)
    blocks = (_nbytes((tm, d), F32) + _nbytes((2 * r + 1, d), F32) + _nbytes((d, tn), BF16)
              + 2 * _nbytes((tm, tn), F32) + _nbytes((tm, d), BF16))
    return pl.pallas_call(
        functools.partial(_proj_kernel, slabs=slabs, tm=tm),
        out_shape=[jax.ShapeDtypeStruct((b, s, n), F32),
                   jax.ShapeDtypeStruct((b, n // LANES, s, LANES), F32),
                   jax.ShapeDtypeStruct((b, s, d), BF16)],
        grid=(b, s // tm, n // tn),
        in_specs=[
            pl.BlockSpec((1, tm, d), lambda bi, i, j: (bi, i, 0)),
            norm[1](0), mod[1](1), mod[1](0),
            pl.BlockSpec((1, d, tn), lambda bi, i, j: (l, 0, j)),
        ],
        out_specs=[pl.BlockSpec((1, tm, tn), lambda bi, i, j: (bi, i, j)),
                   pl.BlockSpec((1, slabs, tm, LANES), lambda bi, i, j: (bi, j, i, 0)),
                   pl.BlockSpec((1, tm, d), lambda bi, i, j: (bi, i, 0))],
        compiler_params=pltpu.CompilerParams(
            dimension_semantics=("parallel", "parallel", "arbitrary"),
            vmem_limit_bytes=_vmem_limit(blocks, _nbytes((tm, tn), F32) + _nbytes((tm, d), F32))),
        name="proj_norm",
    )(x, norm[0], mod[0], mod[0], w)


def _matmul_kernel(h_ref, w_ref, o_ref):
    o_ref[0] = _dot(h_ref[0], w_ref[0])


def _matmul(h, w, l, *, tm, tn):
    b, s, d = h.shape
    n = w.shape[2]
    blocks = _nbytes((tm, d), BF16) + _nbytes((d, tn), BF16) + _nbytes((tm, tn), F32)
    return pl.pallas_call(
        _matmul_kernel,
        out_shape=jax.ShapeDtypeStruct((b, s, n), F32),
        grid=(b, s // tm, n // tn),
        in_specs=[pl.BlockSpec((1, tm, d), lambda bi, i, j: (bi, i, 0)),
                  pl.BlockSpec((1, d, tn), lambda bi, i, j: (l, 0, j))],
        out_specs=pl.BlockSpec((1, tm, tn), lambda bi, i, j: (bi, i, j)),
        compiler_params=pltpu.CompilerParams(
            dimension_semantics=("parallel", "parallel", "parallel"),
            vmem_limit_bytes=_vmem_limit(blocks, _nbytes((tm, tn), F32))),
        name="proj_rest",
    )(h, w)


def _head_masks(rows):
    lane = lax.broadcasted_iota(jnp.int32, (rows, GROUP_WIDTH), 1)
    return [(lane >= h * HEAD_DIM) & (lane < (h + 1) * HEAD_DIM) for h in range(HEADS_PER_GROUP)]


def _attn_kernel(q_ref, kc_ref, kp_ref, vc_ref, vp_ref, bias_ref, o_ref, lse_ref, *, d, sb):
    i = pl.program_id(1)
    nq = N_DIL_KEYS
    span = nq * d
    ncb = sb // span
    hm = _head_masks(1)
    col = lax.broadcasted_iota(jnp.int32, (1, 2 * nq), 1)
    bias = bias_ref[0]

    def rows(start):
        return pl.ds(start, nq, stride=d) if d > 1 else pl.ds(start, nq)

    def load(ref, start):
        return jnp.concatenate([ref[0, s, rows(start), :] for s in range(2)], axis=1)

    def one_block(r, jb):
        qs = jb * span + r
        q = load(q_ref, qs) * ATTN_SCALE
        if jb == 0:
            lo = sb - span + r
            k_lo, v_lo = load(kp_ref, lo), load(vp_ref, lo)
        else:
            lo = (jb - 1) * span + r
            k_lo, v_lo = load(kc_ref, lo), load(vc_ref, lo)
        kcat = jnp.concatenate([k_lo, load(kc_ref, qs)], axis=0).astype(BF16)
        vcat = jnp.concatenate([v_lo, load(vc_ref, qs)], axis=0).astype(BF16)
        qm = jnp.concatenate([jnp.where(hm[h], q, 0.0) for h in range(HEADS_PER_GROUP)], axis=0).astype(BF16)
        s = lax.dot_general(qm, kcat, (((1,), (1,)), ((), ())), preferred_element_type=F32) + bias
        if jb == 0:
            s = jnp.where((col < nq) & (i == 0), NEG_INF, s)
        m = jnp.max(s, axis=-1, keepdims=True)
        p = jnp.exp(s - m)
        l = jnp.sum(p, axis=-1, keepdims=True)
        oall = _dot((p * (1.0 / l)).astype(BF16), vcat)
        lse = m + jnp.log(l)
        o = jnp.zeros((nq, GROUP_WIDTH), F32)
        ls = jnp.zeros((nq, GROUP_WIDTH), F32)
        for h in range(HEADS_PER_GROUP):
            o = jnp.where(hm[h], oall[h * nq:(h + 1) * nq], o)
            ls = jnp.where(hm[h], lse[h * nq:(h + 1) * nq], ls)
        for sl in range(2):
            o_ref[0, sl, rows(qs), :] = o[:, sl * LANES:(sl + 1) * LANES]
            lse_ref[0, sl, rows(qs), :] = ls[:, sl * LANES:(sl + 1) * LANES]

    if d == 1:
        for jb in range(ncb):
            one_block(0, jb)
    else:
        def body(r, carry):
            for jb in range(ncb):
                one_block(r, jb)
            return carry
        lax.fori_loop(0, d, body, 0)


def _attention(qkv, bias_tab, g):
    b, _, s, _ = qkv.shape
    d = ATTN_GROUPS[g][1]
    sb = max(N_DIL_KEYS * d, 512)
    blk = (1, 2, sb, LANES)
    kslab, vslab = ATTN_WIDTH // GROUP_WIDTH + g, 2 * ATTN_WIDTH // GROUP_WIDTH + g
    prev = lambda i: jnp.maximum(i - 1, 0)
    blocks = 7 * _nbytes(blk, F32) + _nbytes((4 * N_DIL_KEYS, 2 * N_DIL_KEYS), F32)
    out = jax.ShapeDtypeStruct((b, 2, s, LANES), F32)
    return pl.pallas_call(
        functools.partial(_attn_kernel, d=d, sb=sb),
        out_shape=(out, out),
        grid=(b, s // sb),
        in_specs=[
            pl.BlockSpec(blk, lambda bi, i: (bi, g, i, 0)),
            pl.BlockSpec(blk, lambda bi, i: (bi, kslab, i, 0)),
            pl.BlockSpec(blk, lambda bi, i: (bi, kslab, prev(i), 0)),
            pl.BlockSpec(blk, lambda bi, i: (bi, vslab, i, 0)),
            pl.BlockSpec(blk, lambda bi, i: (bi, vslab, prev(i), 0)),
            pl.BlockSpec((1, 4 * N_DIL_KEYS, 2 * N_DIL_KEYS), lambda bi, i: (g, 0, 0)),
        ],
        out_specs=(pl.BlockSpec(blk, lambda bi, i: (bi, 0, i, 0)),
                   pl.BlockSpec(blk, lambda bi, i: (bi, 0, i, 0))),
        compiler_params=pltpu.CompilerParams(
            dimension_semantics=("parallel", "parallel"),
            vmem_limit_bytes=_vmem_limit(blocks, 8 << 20)),
        name=f"attn_d{d}",
    )(qkv, qkv, qkv, qkv, qkv, bias_tab)


def _attn_step_kernel(qkv_ref, c0_ref, c1_ref, c2_ref, b0_ref, b1_ref, b2_ref, o_ref):
    outs = [[None] * N_ATTN_GROUPS for _ in range(HEADS_PER_GROUP)]
    lses = [[None] * N_ATTN_GROUPS for _ in range(HEADS_PER_GROUP)]
    for g, (c_ref, b_ref) in enumerate(((c0_ref, b0_ref), (c1_ref, b1_ref), (c2_ref, b2_ref))):
        for h in range(HEADS_PER_GROUP):
            q = qkv_ref[0, g, h] * ATTN_SCALE
            kn = qkv_ref[0, N_ATTN_GROUPS + g, h]
            vn = qkv_ref[0, 2 * N_ATTN_GROUPS + g, h]
            s_c = jnp.sum(c_ref[0, 0, 0, h] * q, axis=0, keepdims=True) + b_ref[h:h + 1, :]
            s_n = jnp.sum(kn * q, axis=0, keepdims=True) + b_ref[HEADS_PER_GROUP + h:HEADS_PER_GROUP + h + 1, 0:1]
            m = jnp.maximum(jnp.max(s_c, axis=-1, keepdims=True), s_n)
            p_c = jnp.exp(s_c - m)
            p_n = jnp.exp(s_n - m)
            l = jnp.sum(p_c, axis=-1, keepdims=True) + p_n
            inv = 1.0 / l
            outs[h][g] = jnp.sum(c_ref[0, 0, 1, h] * (p_c * inv), axis=-1, keepdims=True) + (p_n * inv) * vn
            lses[h][g] = m + jnp.log(l)
    for h in range(HEADS_PER_GROUP):
        ls, os_ = lses[h], outs[h]
        mx = jnp.maximum(jnp.maximum(ls[0], ls[1]), ls[2])
        w = [jnp.exp(x - mx) for x in ls]
        tot = w[0] + w[1] + w[2]
        o_ref[0, h] = (w[0] * os_[0] + w[1] * os_[1] + w[2] * os_[2]) * (1.0 / tot)


def _attention_step(qkv, caches, biases, l, nb):
    cache_specs = [pl.BlockSpec((1, 1) + c.shape[2:], lambda bi: (l, bi, 0, 0, 0, 0)) for c in caches]
    bias_specs = [pl.BlockSpec(b.shape, lambda bi: (0, 0)) for b in biases]
    blocks = (sum(_nbytes(c.shape[2:], F32) for c in caches) + sum(_nbytes(b.shape, F32) for b in biases)
              + 13 * HEADS_PER_GROUP * HEAD_DIM * LANES * 4)
    return pl.pallas_call(
        _attn_step_kernel,
        out_shape=jax.ShapeDtypeStruct((nb, HEADS_PER_GROUP, HEAD_DIM, 1), F32),
        grid=(nb,),
        in_specs=[pl.BlockSpec((1,) + qkv.shape[1:], lambda bi: (bi, 0, 0, 0, 0))] + cache_specs + bias_specs,
        out_specs=pl.BlockSpec((1, HEADS_PER_GROUP, HEAD_DIM, 1), lambda bi: (bi, 0, 0, 0)),
        compiler_params=pltpu.CompilerParams(
            dimension_semantics=("parallel",),
            vmem_limit_bytes=_vmem_limit(blocks, 8 << 20)),
        name="attn_step",
    )(qkv, *caches, *biases)


def _branch_kernel(*refs, tm, step):
    if step:
        (act_ref, pst_ref, cst_ref, ga_ref, gb_ref, gc_ref, at_ref,
         wgrp_ref, pscale_ref, wpb_ref, cw_ref, wcb_ref, wab_ref,
         out_ref, u_ref, pzs_ref, cb_ref, comb_ref) = refs
    else:
        (act_ref, hist_ref, ga_ref, gb_ref, gc_ref, o0_ref, o1_ref, o2_ref, l0_ref, l1_ref, l2_ref,
         wgrp_ref, pscale_ref, wpb_ref, cw_ref, wcb_ref, wab_ref,
         out_ref, u_ref, pzs_ref, cb_ref, comb_ref, pe_ref, ue_ref) = refs
    i = pl.program_id(1)

    @pl.when(pl.program_id(2) == 0)
    def _():
        p = act_ref[0, :, 0:POOL_WIDTH]
        gate_b = act_ref[0, :, POOL_WIDTH:POOL_WIDTH + CONV_CH]
        u = act_ref[0, :, POOL_WIDTH + CONV_CH:POOL_WIDTH + 2 * CONV_CH] * \
            act_ref[0, :, POOL_WIDTH + 2 * CONV_CH:POOL_WIDTH + 3 * CONV_CH]
        cw = cw_ref[0]
        if step:
            acc = p
            sums = {}
            for k in range(1, max(POOL_WINDOWS)):
                acc = acc + pst_ref[0, POOL_STATE - k]
                sums[k + 1] = acc
            means = [sums[w][:, gi * POOL_GROUP:(gi + 1) * POOL_GROUP] * (1.0 / w)
                     for gi, w in enumerate(POOL_WINDOWS)]
            conv = cst_ref[0, 0] * cw[0:1] + cst_ref[0, 1] * cw[1:2] + u * cw[2:3]
            u_ref[0, 0] = u
            comb_ref[...] = at_ref[0].astype(BF16)
        else:
            first = i == 0
            hist_p = hist_ref[0, :, 0:POOL_WIDTH]
            hist_u = hist_ref[0, :, POOL_WIDTH + CONV_CH:POOL_WIDTH + 2 * CONV_CH] * \
                hist_ref[0, :, POOL_WIDTH + 2 * CONV_CH:POOL_WIDTH + 3 * CONV_CH]
            pe_ref[0:HIST] = jnp.where(first, 0.0, hist_p)
            ue_ref[0:HIST] = jnp.where(first, 0.0, hist_u)
            pe_ref[HIST:HIST + tm] = p
            ue_ref[HIST:HIST + tm] = u
            t = i * tm + lax.broadcasted_iota(jnp.int32, (tm, 1), 0)
            means = []
            for gi, w in enumerate(POOL_WINDOWS):
                cs = slice(gi * POOL_GROUP, (gi + 1) * POOL_GROUP)
                acc = pe_ref[HIST:HIST + tm, cs]
                for k in range(1, w):
                    acc = acc + pe_ref[HIST - k:HIST - k + tm, cs]
                cnt = jnp.minimum(t + 1, w).astype(F32)
                means.append(acc * (1.0 / cnt))
            conv = (ue_ref[HIST - 2:HIST - 2 + tm] * cw[0:1] + ue_ref[HIST - 1:HIST - 1 + tm] * cw[1:2]
                    + u * cw[2:3])
            u_ref[0, 0] = ue_ref[HIST + tm - 8:HIST + tm]
            for sl in range(2):
                ls = [r[0, sl] for r in (l0_ref, l1_ref, l2_ref)]
                os_ = [r[0, sl] for r in (o0_ref, o1_ref, o2_ref)]
                mx = jnp.maximum(jnp.maximum(ls[0], ls[1]), ls[2])
                w_ = [jnp.exp(x - mx) for x in ls]
                tot = w_[0] + w_[1] + w_[2]
                comb = (w_[0] * os_[0] + w_[1] * os_[1] + w_[2] * os_[2]) * (1.0 / tot)
                comb_ref[:, sl * LANES:(sl + 1) * LANES] = comb.astype(BF16)
        pscale = pscale_ref[0]
        for gi in range(len(POOL_WINDOWS)):
            cs = slice(gi * POOL_GROUP, (gi + 1) * POOL_GROUP)
            pm = (means[gi] - p[:, cs]).astype(BF16)
            pz = _dot(pm, wgrp_ref[0, gi])
            pzs_ref[:, cs] = (pz * pscale[:, cs]).astype(BF16)
        cb_ref[...] = (gate_b * conv).astype(BF16)

    br_a = _dot(pzs_ref[...], wpb_ref[0])
    br_b = _dot(cb_ref[...], wcb_ref[0])
    br_c = _dot(comb_ref[...], wab_ref[0])
    merged = _sigmoid(ga_ref[0]) * br_a + _sigmoid(gb_ref[0]) * br_b + _sigmoid(gc_ref[0]) * br_c
    out_ref[0] = merged.astype(BF16)


def _branches(rest, attn, states, weights, l, *, tm, tc, step):
    b, s, _ = rest.shape
    wgrp, pscale, wpb, cw, wcb, wab = weights
    ni = s // tm
    gblk = ACT_WIDTH // tc
    gate_specs = [pl.BlockSpec((1, tm, tc), lambda bi, i, c, k=k: (bi, i, gblk + k * (D_MODEL // tc) + c))
                  for k in range(3)]
    w_specs = [
        pl.BlockSpec((1,) + wgrp.shape[1:], lambda bi, i, c: (l, 0, 0, 0)),
        pl.BlockSpec((1,) + pscale.shape[1:], lambda bi, i, c: (l, 0, 0)),
        pl.BlockSpec((1, POOL_WIDTH, tc), lambda bi, i, c: (l, 0, c)),
        pl.BlockSpec((1,) + cw.shape[1:], lambda bi, i, c: (l, 0, 0)),
        pl.BlockSpec((1, CONV_CH, tc), lambda bi, i, c: (l, 0, c)),
        pl.BlockSpec((1, GROUP_WIDTH, tc), lambda bi, i, c: (l, 0, c)),
    ]
    act_spec = pl.BlockSpec((1, tm, ACT_WIDTH), lambda bi, i, c: (bi, i, 0))
    scratch = [pltpu.VMEM((tm, POOL_WIDTH), BF16), pltpu.VMEM((tm, CONV_CH), BF16),
               pltpu.VMEM((tm, GROUP_WIDTH), BF16)]
    blocks = (_nbytes((tm, ACT_WIDTH), F32) + 3 * _nbytes((tm, tc), F32) + _nbytes(wgrp.shape[1:], BF16)
              + _nbytes((POOL_WIDTH + CONV_CH + GROUP_WIDTH, tc), BF16) + _nbytes((tm, tc), BF16))
    if step:
        pst, cst = states
        ins = [rest, pst, cst, rest, rest, rest, attn]
        in_specs = [act_spec,
                    pl.BlockSpec((1,) + pst.shape[1:], lambda bi, i, c: (l, 0, 0, 0)),
                    pl.BlockSpec((1,) + cst.shape[1:], lambda bi, i, c: (l, 0, 0, 0))] + gate_specs + [
                    pl.BlockSpec((1, tm, GROUP_WIDTH), lambda bi, i, c: (bi, i, 0))]
        urows = tm
        blocks += _nbytes(pst.shape[1:], F32) + _nbytes(cst.shape[1:], F32)
    else:
        hblk = tm // HIST
        slab = pl.BlockSpec((1, 2, tm, LANES), lambda bi, i, c: (bi, 0, i, 0))
        ins = [rest, rest, rest, rest, rest] + list(attn)
        in_specs = [act_spec,
                    pl.BlockSpec((1, HIST, ACT_WIDTH), lambda bi, i, c: (bi, jnp.maximum(i * hblk - 1, 0), 0))
                    ] + gate_specs + [slab] * 6
        scratch += [pltpu.VMEM((tm + HIST, POOL_WIDTH), F32), pltpu.VMEM((tm + HIST, CONV_CH), F32)]
        urows = 8
        blocks += 6 * _nbytes((2, tm, LANES), F32) + _nbytes((HIST, ACT_WIDTH), F32)
    scratch_bytes = 3 * _nbytes((tm + HIST, POOL_WIDTH), F32) + 4 * _nbytes((tm, tc), F32)
    return pl.pallas_call(
        functools.partial(_branch_kernel, tm=tm, step=step),
        out_shape=(jax.ShapeDtypeStruct((b, s, D_MODEL), BF16),
                   jax.ShapeDtypeStruct((b, ni, urows, CONV_CH), F32)),
        grid=(b, ni, D_MODEL // tc),
        in_specs=in_specs + w_specs,
        out_specs=(pl.BlockSpec((1, tm, tc), lambda bi, i, c: (bi, i, c)),
                   pl.BlockSpec((1, 1, urows, CONV_CH), lambda bi, i, c: (bi, i, 0, 0))),
        scratch_shapes=scratch,
        compiler_params=pltpu.CompilerParams(
            dimension_semantics=("parallel", "parallel", "arbitrary"),
            vmem_limit_bytes=_vmem_limit(blocks, scratch_bytes)),
        name="branches_step" if step else "branches",
    )(*ins, wgrp, pscale, wpb, cw, wcb, wab)


def _oproj_kernel(m_ref, w_ref, x_ref, g_ref, gate_ref, o_ref):
    mix = _dot(m_ref[0], w_ref[0])
    o_ref[0] = x_ref[0] + _tail2(gate_ref) * _rms(mix, g_ref[0])


def _oproj(merged, w_o, x, norm, mod, l, *, tm):
    b, s, d = x.shape
    blocks = (_nbytes((tm, d), BF16) + _nbytes((d, d), BF16) + 2 * _nbytes((tm, d), F32)
              + _nbytes((_mod_rows(mod) + 1, d), F32))
    return pl.pallas_call(
        _oproj_kernel,
        out_shape=jax.ShapeDtypeStruct((b, s, d), F32),
        grid=(b, s // tm),
        in_specs=[
            pl.BlockSpec((1, tm, d), lambda bi, i: (bi, i, 0)),
            pl.BlockSpec((1, d, d), lambda bi, i: (l, 0, 0)),
            pl.BlockSpec((1, tm, d), lambda bi, i: (bi, i, 0)),
            norm[1](1), mod[1](2),
        ],
        out_specs=pl.BlockSpec((1, tm, d), lambda bi, i: (bi, i, 0)),
        compiler_params=pltpu.CompilerParams(
            dimension_semantics=("parallel", "parallel"),
            vmem_limit_bytes=_vmem_limit(blocks, _nbytes((tm, d), F32))),
        name="oproj",
    )(merged, w_o, x, norm[0], mod[0])


def _ffn_kernel(*refs, tm, step, nchunk):
    if step:
        (x_ref, g2_ref, sc_ref, sh_ref, wg_ref, wv_ref, cwg_ref, cwv_ref, wd_ref, g3_ref, gate_ref,
         stg_ref, stv_ref, o_ref, tg_ref, tv_ref, h_ref, acc_ref) = refs
        hist = 0
    else:
        (x_ref, xp_ref, g2_ref, sc_ref, sh_ref, wg_ref, wv_ref, cwg_ref, cwv_ref, wd_ref, g3_ref, gate_ref,
         o_ref, tg_ref, tv_ref, h_ref, acc_ref, ug_ref, uv_ref) = refs
        hist = HIST
    i = pl.program_id(1)
    j = pl.program_id(2)

    @pl.when(j == 0)
    def _():
        if not step:
            _modulate_rows(xp_ref, g2_ref, sc_ref, sh_ref, h_ref, 0, HIST)
        _modulate_rows(x_ref, g2_ref, sc_ref, sh_ref, h_ref, hist, tm)
        acc_ref[...] = jnp.zeros_like(acc_ref)

    cwg = cwg_ref[0]
    cwv = cwv_ref[0]
    if step:
        h = h_ref[...]
        up_g = _dot(h, wg_ref[0])
        up_v = _dot(h, wv_ref[0])
        uc_g = stg_ref[0, 0] * cwg[0:1] + stg_ref[0, 1] * cwg[1:2] + up_g * cwg[2:3]
        uc_v = stv_ref[0, 0] * cwv[0:1] + stv_ref[0, 1] * cwv[1:2] + up_v * cwv[2:3]
        tg_ref[0, 0] = up_g
        tv_ref[0, 0] = up_v
        acc_ref[...] += _dot((_gelu_tanh(uc_g) * uc_v).astype(BF16), wd_ref[0])
    else:
        first = i == 0
        rc = tm // nchunk
        for c in range(nchunk):
            lo = 0 if c == 0 else HIST + c * rc
            hi = HIST + (c + 1) * rc
            h = h_ref[lo:hi]
            for u_ref, w_ref in ((ug_ref, wg_ref), (uv_ref, wv_ref)):
                up = _dot(h, w_ref[0])
                if c == 0:
                    u_ref[0:HIST] = jnp.where(first, 0.0, up[0:HIST])
                    u_ref[HIST:hi] = up[HIST:]
                else:
                    u_ref[lo:hi] = up

            def conv(ref, cw):
                base = HIST + c * rc
                return (ref[base - 2:base - 2 + rc] * cw[0:1] + ref[base - 1:base - 1 + rc] * cw[1:2]
                        + ref[base:base + rc] * cw[2:3])
            act = (_gelu_tanh(conv(ug_ref, cwg)) * conv(uv_ref, cwv)).astype(BF16)
            acc_ref[c * rc:(c + 1) * rc] += _dot(act, wd_ref[0])
        tg_ref[0, 0] = ug_ref[HIST + tm - 8:HIST + tm]
        tv_ref[0, 0] = uv_ref[HIST + tm - 8:HIST + tm]

    @pl.when(j == pl.num_programs(2) - 1)
    def _():
        o_ref[0] = x_ref[0] + _tail2(gate_ref) * _rms(acc_ref[...], g3_ref[0])


def _ffn(x, norm, mod, w_up, cw, w_down, state, l, *, tm, tf, step):
    b, s, d = x.shape
    ni, nj = s // tm, D_FF // tf
    trows = tm if step else 8
    w_specs = [
        pl.BlockSpec((1, d, tf), lambda bi, i, j: (l, 0, j)),
        pl.BlockSpec((1, d, tf), lambda bi, i, j: (l, 0, nj + j)),
        pl.BlockSpec((1, FFN_K, tf), lambda bi, i, j: (l, 0, j)),
        pl.BlockSpec((1, FFN_K, tf), lambda bi, i, j: (l, 0, nj + j)),
        pl.BlockSpec((1, tf, d), lambda bi, i, j: (l, j, 0)),
    ]
    x_spec = pl.BlockSpec((1, tm, d), lambda bi, i, j: (bi, i, 0))
    hist = 0 if step else HIST
    scratch = [pltpu.VMEM((tm + hist, d), BF16), pltpu.VMEM((tm, d), F32)]
    blocks = (2 * _nbytes((tm, d), F32) + 3 * _nbytes((d, tf), BF16) + 2 * _nbytes((trows, tf), F32)
              + _nbytes((3 * _mod_rows(mod) + 2, d), F32))
    mods = [norm[1](2), mod[1](4), mod[1](3)]
    tailp = [norm[1](3), mod[1](5)]
    if step:
        ins = [x, norm[0], mod[0], mod[0], w_up, w_up, cw, cw, w_down, norm[0], mod[0], state, state]
        in_specs = [x_spec] + mods + w_specs + tailp + [
            pl.BlockSpec((1, FFN_K - 1, tm, tf), lambda bi, i, j: (l, 0, 0, j)),
            pl.BlockSpec((1, FFN_K - 1, tm, tf), lambda bi, i, j: (l, 0, 0, nj + j))]
        blocks += 2 * _nbytes((FFN_K - 1, tm, tf), F32)
    else:
        hblk = tm // HIST
        ins = [x, x, norm[0], mod[0], mod[0], w_up, w_up, cw, cw, w_down, norm[0], mod[0]]
        in_specs = [x_spec,
                    pl.BlockSpec((1, HIST, d), lambda bi, i, j: (bi, jnp.maximum(i * hblk - 1, 0), 0))
                    ] + mods + w_specs + tailp
        scratch += [pltpu.VMEM((tm + HIST, tf), F32), pltpu.VMEM((tm + HIST, tf), F32)]
        blocks += _nbytes((HIST, d), F32)
    scratch_bytes = (_nbytes((tm + hist, d), BF16) + _nbytes((tm, d), F32)
                     + 6 * _nbytes((tm + hist, tf), F32))
    tail = jax.ShapeDtypeStruct((b, ni, trows, D_FF), F32)
    tail_spec = lambda: pl.BlockSpec((1, 1, trows, tf), lambda bi, i, j: (bi, i, 0, j))
    return pl.pallas_call(
        functools.partial(_ffn_kernel, tm=tm, step=step, nchunk=FFN_ROW_CHUNKS),
        out_shape=(jax.ShapeDtypeStruct((b, s, d), F32), tail, tail),
        grid=(b, ni, nj),
        in_specs=in_specs,
        out_specs=(pl.BlockSpec((1, tm, d), lambda bi, i, j: (bi, i, 0)), tail_spec(), tail_spec()),
        scratch_shapes=scratch,
        compiler_params=pltpu.CompilerParams(
            dimension_semantics=("parallel", "parallel", "arbitrary"),
            vmem_limit_bytes=_vmem_limit(blocks, scratch_bytes)),
        name="ffn_step" if step else "ffn",
    )(*ins)


def _group_rel_bias(rel_bias):
    n = np.arange(N_DIL_KEYS + 1)
    max_exact = N_BUCKETS // 2
    out = []
    for g, (_, d) in enumerate(ATTN_GROUPS):
        dist = n * d
        large = max_exact + (np.log(np.maximum(dist, 1) / max_exact) / np.log(MAX_DISTANCE / max_exact)
                             * (N_BUCKETS - max_exact)).astype(np.int32)
        bucket = np.where(dist < max_exact, dist, np.minimum(large, N_BUCKETS - 1))
        out.append(rel_bias[bucket, g * HEADS_PER_GROUP:(g + 1) * HEADS_PER_GROUP].T)
    return jnp.stack(out)


def _bias_tables(rel_bias):
    bias_g = _group_rel_bias(rel_bias).astype(F32)
    n = N_DIL_KEYS
    gh = (N_ATTN_GROUPS, HEADS_PER_GROUP)
    ext = jnp.concatenate([bias_g[:, :, ::-1], jnp.full(gh + (n,), NEG_INF, F32)], axis=-1)
    tab = jnp.tile(ext, (1, 1, n))[:, :, :n * 2 * n].reshape(N_ATTN_GROUPS, HEADS_PER_GROUP * n, 2 * n)
    step = []
    for g, (_, d) in enumerate(ATTN_GROUPS):
        hit = bias_g[g, :, :0:-1, None]
        row = jnp.concatenate([hit, jnp.full((HEADS_PER_GROUP, n, d - 1), NEG_INF, F32)], axis=-1)
        new = jnp.broadcast_to(bias_g[g, :, 0:1], (HEADS_PER_GROUP, n * d))
        step.append(jnp.concatenate([row.reshape(HEADS_PER_GROUP, n * d), new], axis=0))
    return tab, step


def _kv_pack(qkv, g, nrows):
    b = qkv.shape[0]
    k0 = ATTN_WIDTH + g * GROUP_WIDTH
    v0 = 2 * ATTN_WIDTH + g * GROUP_WIDTH
    k = qkv[:, -nrows:, k0:k0 + GROUP_WIDTH].reshape(b, nrows, HEADS_PER_GROUP, HEAD_DIM)
    v = qkv[:, -nrows:, v0:v0 + GROUP_WIDTH].reshape(b, nrows, HEADS_PER_GROUP, HEAD_DIM)
    return jnp.stack([k, v], axis=2)


def _prompt_layer(x, l, P):
    b, s, _ = x.shape
    norm, mod = P["norm"](l), P["mod_p"](l)
    tm = 512
    qkv, qkv_slab, h = _proj(x, norm, mod, P["w_qkv"], l, tm=1024, tn=ATTN_WIDTH)
    rest = _matmul(h, P["w_rest"], l, tm=1024, tn=1024)
    ols = [_attention(qkv_slab, P["bias_tab"], g) for g in range(N_ATTN_GROUPS)]
    attn = [o for o, _ in ols] + [ls for _, ls in ols]
    merged, u_tail = _branches(rest, attn, None, P["branch"], l, tm=tm, tc=512, step=False)
    x1 = _oproj(merged, P["w_o"], x, norm, mod, l, tm=tm)
    x2, tail_g, tail_v = _ffn(x1, norm, mod, P["w_up"], P["ffn_cw"], P["w_down"], None, l,
                              tm=tm, tf=512, step=False)
    new_kv = [_kv_pack(qkv, g, min(w, s)) for g, (w, _) in enumerate(ATTN_GROUPS)]
    new_pool = rest[:, -POOL_STATE:, :POOL_WIDTH]
    new_conv = u_tail[:, -1, -(CONV_K - 1):]
    new_ffn = jnp.concatenate([tail_g[:, -1, -(FFN_K - 1):], tail_v[:, -1, -(FFN_K - 1):]], axis=-1)
    return x2, (new_kv[0], new_kv[1], new_kv[2], new_pool, new_conv, new_ffn)


def _pad_rows(a, axis):
    pad = [(0, 0)] * a.ndim
    pad[axis] = (0, SAMPLE_ROWS - a.shape[axis])
    return jnp.pad(a, pad)


def _sample_layer(x, l, P, S, nb):
    norm, mod = P["norm"](l), P["mod_s"](l)
    tm = SAMPLE_ROWS
    qkv, _, h = _proj(x, norm, mod, P["w_qkv"], l, tm=tm, tn=ATTN_WIDTH)
    rest = _matmul(h, P["w_rest"], l, tm=tm, tn=1024)
    heads = qkv[0, :nb].reshape(nb, 3 * N_ATTN_GROUPS, HEADS_PER_GROUP, HEAD_DIM, 1)
    attn = _attention_step(heads, S["caches"], P["bias_step"], l, nb)
    attn = _pad_rows(attn.reshape(1, nb, GROUP_WIDTH), 1)
    merged, u = _branches(rest, attn, (S["pool_t"], S["conv_t"]), P["branch"], l, tm=tm, tc=512, step=True)
    x1 = _oproj(merged, P["w_o"], x, norm, mod, l, tm=tm)
    x2, up_g, up_v = _ffn(x1, norm, mod, P["w_up"], P["ffn_cw"], P["w_down"], S["ffn_t"], l,
                          tm=tm, tf=512, step=True)
    new_kv = [_kv_pack(qkv[0, :nb, None], g, 1) for g in range(N_ATTN_GROUPS)]
    new_pool = jnp.concatenate([S["pool"][l][:, 1:], rest[0, :nb, None, :POOL_WIDTH]], axis=1)
    new_conv = jnp.concatenate([S["conv"][l][:, 1:], u[0, 0, :nb, None]], axis=1)
    up_new = jnp.concatenate([up_g[0, 0, :nb], up_v[0, 0, :nb]], axis=-1)
    new_ffn = jnp.concatenate([S["ffn"][l][:, 1:], up_new[:, None]], axis=1)
    return x2, (new_kv[0], new_kv[1], new_kv[2], new_pool, new_conv, new_ffn)


def kernel(x_prompt, x_sample, c_prompt, c_sample, cache_kv_w128, cache_kv_w512, cache_kv_w2048, state_pool, state_conv, state_ffn_conv, rel_bias, norm_g, w_ada, b_ada, w_in, w_attn_br, w_pool_grp, pool_scale, w_pool_br, conv_w, w_conv_br, w_o, w_up, ffn_conv_w, w_down):
    nbp = x_prompt.shape[0]
    nbs, tdec, _ = x_sample.shape
    assert tdec == 1 and nbs <= SAMPLE_ROWS
    caches = (cache_kv_w128, cache_kv_w512, cache_kv_w2048)
    for (w, d), c in zip(ATTN_GROUPS, caches):
        assert c.shape[2] == w == N_DIL_KEYS * d, "cache must hold exactly one window"

    bias_tab, bias_step = _bias_tables(rel_bias)
    c_rows = -(-(SAMPLE_ROWS + nbp) // 8) * 8
    c_all = jnp.zeros((c_rows, D_MODEL), F32).at[:nbs].set(c_sample).at[SAMPLE_ROWS:SAMPLE_ROWS + nbp].set(c_prompt)
    mod_all = _ada(c_all, w_ada, b_ada)

    P = dict(
        norm=functools.partial(_norm_view, norm_g),
        mod_p=functools.partial(_mod_view, mod_all, prompt=True),
        mod_s=functools.partial(_mod_view, mod_all, prompt=False),
        w_qkv=w_in[:, :, :QKV_WIDTH].astype(BF16),
        w_rest=w_in[:, :, QKV_WIDTH:].astype(BF16),
        branch=(w_pool_grp.astype(BF16), pool_scale[:, None, :], w_pool_br.astype(BF16),
                conv_w, w_conv_br.astype(BF16), w_attn_br.astype(BF16)),
        w_o=w_o.astype(BF16), w_up=w_up.astype(BF16), ffn_cw=ffn_conv_w, w_down=w_down.astype(BF16),
        bias_tab=bias_tab, bias_step=bias_step,
    )
    S = dict(
        caches=[c.transpose(0, 1, 3, 4, 5, 2) for c in caches],
        pool=state_pool, conv=state_conv, ffn=state_ffn_conv,
        pool_t=_pad_rows(state_pool.transpose(0, 2, 1, 3), 2),
        conv_t=_pad_rows(state_conv.transpose(0, 2, 1, 3), 2),
        ffn_t=_pad_rows(state_ffn_conv.transpose(0, 2, 1, 3), 2),
    )
    yp = x_prompt
    ys = _pad_rows(x_sample.reshape(1, nbs, D_MODEL), 1)
    st_p, st_s = [], []
    for l in range(DEPTH):
        yp, sp = _prompt_layer(yp, l, P)
        ys, ss = _sample_layer(ys, l, P, S, nbs)
        st_p.append(sp)
        st_s.append(ss)
    outs_p = [jnp.stack([s[k] for s in st_p]) for k in range(6)]
    outs_s = [jnp.stack([s[k] for s in st_s]) for k in range(6)]
    return (yp, ys[0, :nbs, None, :], *outs_p, *outs_s)
```

```python
import functools

import numpy as np
import jax
import jax.numpy as jnp
from jax import lax
from jax.experimental import pallas as pl
from jax.experimental.pallas import tpu as pltpu

F32 = jnp.float32
BF16 = jnp.bfloat16

D_MODEL = 2048
DEPTH = 2
HEAD_DIM = 64
HEADS_PER_GROUP = 4
ATTN_GROUPS = ((128, 1), (512, 4), (2048, 16))
N_ATTN_GROUPS = len(ATTN_GROUPS)
ATTN_WIDTH = N_ATTN_GROUPS * HEADS_PER_GROUP * HEAD_DIM
GROUP_WIDTH = HEADS_PER_GROUP * HEAD_DIM
N_DIL_KEYS = 128
N_BUCKETS = 32
MAX_DISTANCE = 2048
ATTN_SCALE = HEAD_DIM ** -0.5
POOL_WINDOWS = (2, 4, 8, 16)
POOL_GROUP = 128
POOL_WIDTH = 512
POOL_STATE = 15
CONV_CH = 512
CONV_K = 3
D_FF = 5632
FFN_K = 3
N_MOD = 6
N_NORM = 4
EPS = 1e-6
NEG_INF = -1e30

LANES = 128
HIST = 16
SAMPLE_ROWS = 16
QKV_WIDTH = 3 * ATTN_WIDTH
QKV_SLABS = QKV_WIDTH // LANES
REST_WIDTH = POOL_WIDTH + 3 * CONV_CH + 3 * D_MODEL
ACT_WIDTH = POOL_WIDTH + 3 * CONV_CH
ATTN_CLASS_UNROLL = 4
FFN_ROW_CHUNKS = 2
VMEM_CAP = 56 * 1024 * 1024


def _vmem_limit(block_bytes, scratch_bytes=0):
    est = 2 * block_bytes + scratch_bytes
    return int(min(VMEM_CAP, est + est // 4 + (4 << 20)))


def _nbytes(shape, dtype):
    return int(np.prod(shape)) * jnp.dtype(dtype).itemsize


def _rms(x, g):
    return x * lax.rsqrt(jnp.mean(x * x, axis=-1, keepdims=True) + EPS) * g


def _sigmoid(x):
    return 0.5 + 0.5 * jnp.tanh(0.5 * x)


def _gelu_tanh(x):
    return 0.5 * x * (1.0 + jnp.tanh(np.sqrt(2.0 / np.pi) * (x + 0.044715 * (x * x * x))))


def _dot(a, b):
    return jnp.dot(a, b, preferred_element_type=F32)


def _tail2(ref):
    return ref[(0,) * (len(ref.shape) - 2)]


def _norm_view(norm_g, l):
    arr = norm_g.reshape(DEPTH * N_NORM, 1, D_MODEL)
    return arr, lambda k: pl.BlockSpec((1, 1, D_MODEL), lambda *_: (l * N_NORM + k, 0, 0))


def _mod_view(mod_all, l, prompt):
    if prompt:
        arr = mod_all.reshape(DEPTH, mod_all.shape[1], N_MOD, 1, D_MODEL)
        return arr, lambda k: pl.BlockSpec((1, 1, 1, 1, D_MODEL), lambda bi, *_: (l, SAMPLE_ROWS + bi, k, 0, 0))
    return mod_all, lambda k: pl.BlockSpec((1, SAMPLE_ROWS, D_MODEL), lambda bi, *_: (l, 0, k))


def _mod_rows(mod):
    return mod[1](0).block_shape[-2]


def _ada_kernel(c_ref, w_ref, b_ref, o_ref):
    c = c_ref[...]
    s = (c * _sigmoid(c)).astype(BF16)
    o_ref[0] = _dot(s, w_ref[0].astype(BF16)) + b_ref[0]


def _ada(c_all, w_ada, b_ada):
    rows = c_all.shape[0]
    n = w_ada.shape[-1]
    tn = 1024
    blocks = _nbytes((rows, D_MODEL), F32) + _nbytes((D_MODEL, tn), F32) + _nbytes((rows + 1, tn), F32)
    return pl.pallas_call(
        _ada_kernel,
        out_shape=jax.ShapeDtypeStruct((DEPTH, rows, n), F32),
        grid=(DEPTH, n // tn),
        in_specs=[
            pl.BlockSpec((rows, D_MODEL), lambda l, j: (0, 0)),
            pl.BlockSpec((1, D_MODEL, tn), lambda l, j: (l, 0, j)),
            pl.BlockSpec((1, 1, tn), lambda l, j: (l, 0, j)),
        ],
        out_specs=pl.BlockSpec((1, rows, tn), lambda l, j: (l, 0, j)),
        compiler_params=pltpu.CompilerParams(
            dimension_semantics=("parallel", "parallel"),
            vmem_limit_bytes=_vmem_limit(blocks, _nbytes((D_MODEL, tn), BF16))),
        name="ada",
    )(c_all, w_ada, b_ada.reshape(DEPTH, 1, n))


def _modulate_rows(x_ref, g_ref, sc_ref, sh_ref, h_ref, row0, tm):
    y = _rms(x_ref[0], g_ref[0])
    h_ref[row0:row0 + tm] = (y * (1.0 + _tail2(sc_ref)) + _tail2(sh_ref)).astype(BF16)


def _proj_kernel(x_ref, g_ref, sc_ref, sh_ref, w_ref, o_ref, slab_ref, h_ref, *, slabs, tm):
    @pl.when(pl.program_id(2) == 0)
    def _():
        _modulate_rows(x_ref, g_ref, sc_ref, sh_ref, h_ref.at[0], 0, tm)

    res = _dot(h_ref[0], w_ref[0])
    o_ref[0] = res
    for s in range(slabs):
        slab_ref[0, s] = res[:, s * LANES:(s + 1) * LANES]


def _proj(x, norm, mod, w, l, *, tm, tn):
    b, s, d = x.shape
    n = w.shape[2]
    slabs = tn // LANES
    r = _mod_rows(mod)
    blocks = (_nbytes((tm, d), F32) + _nbytes((2 * r + 1, d), F32) + _nbytes((d, tn), BF16)
              + 2 * _nbytes((tm, tn), F32) + _nbytes((tm, d), BF16))
    return pl.pallas_call(
        functools.partial(_proj_kernel, slabs=slabs, tm=tm),
        out_shape=[jax.ShapeDtypeStruct((b, s, n), F32),
                   jax.ShapeDtypeStruct((b, n // LANES, s, LANES), F32),
                   jax.ShapeDtypeStruct((b, s, d), BF16)],
        grid=(b, s // tm, n // tn),
        in_specs=[
            pl.BlockSpec((1, tm, d), lambda bi, i, j: (bi, i, 0)),
            norm[1](0), mod[1](1), mod[1](0),
            pl.BlockSpec((1, d, tn), lambda bi, i, j: (l, 0, j)),
        ],
        out_specs=[pl.BlockSpec((1, tm, tn), lambda bi, i, j: (bi, i, j)),
                   pl.BlockSpec((1, slabs, tm, LANES), lambda bi, i, j: (bi, j, i, 0)),
                   pl.BlockSpec((1, tm, d), lambda bi, i, j: (bi, i, 0))],
        compiler_params=pltpu.CompilerParams(
            dimension_semantics=("parallel", "parallel", "arbitrary"),
            vmem_limit_bytes=_vmem_limit(blocks, _nbytes((tm, tn), F32) + _nbytes((tm, d), F32))),
        name="proj_norm",
    )(x, norm[0], mod[0], mod[0], w)


def _matmul_kernel(h_ref, w_ref, o_ref):
    o_ref[0] = _dot(h_ref[0], w_ref[0])


def _matmul(h, w, l, *, tm, tn):
    b, s, d = h.shape
    n = w.shape[2]
    blocks = _nbytes((tm, d), BF16) + _nbytes((d, tn), BF16) + _nbytes((tm, tn), F32)
    return pl.pallas_call(
        _matmul_kernel,
        out_shape=jax.ShapeDtypeStruct((b, s, n), F32),
        grid=(b, s // tm, n // tn),
        in_specs=[pl.BlockSpec((1, tm, d), lambda bi, i, j: (bi, i, 0)),
                  pl.BlockSpec((1, d, tn), lambda bi, i, j: (l, 0, j))],
        out_specs=pl.BlockSpec((1, tm, tn), lambda bi, i, j: (bi, i, j)),
        compiler_params=pltpu.CompilerParams(
            dimension_semantics=("parallel", "parallel", "parallel"),
            vmem_limit_bytes=_vmem_limit(blocks, _nbytes((tm, tn), F32))),
        name="proj_rest",
    )(h, w)


def _head_masks(rows):
    lane = lax.broadcasted_iota(jnp.int32, (rows, GROUP_WIDTH), 1)
    return [(lane >= h * HEAD_DIM) & (lane < (h + 1) * HEAD_DIM) for h in range(HEADS_PER_GROUP)]


def _attn_kernel(q_ref, kc_ref, kp_ref, vc_ref, vp_ref, bias_ref, o_ref, lse_ref, edge_ref, *, d, sb):
    i = pl.program_id(1)
    nq = N_DIL_KEYS
    span = nq * d
    ncb = sb // span
    hm = _head_masks(1)
    col = lax.broadcasted_iota(jnp.int32, (1, 2 * nq), 1)
    edge_ref[...] = jnp.where((col < nq) & (i == 0), NEG_INF, bias_ref[0])

    def rows(start):
        return pl.ds(start, nq, stride=d) if d > 1 else pl.ds(start, nq)

    def load(ref, start):
        return jnp.concatenate([ref[0, s, rows(start), :] for s in range(2)], axis=1)

    def one_block(r, jb):
        qs = jb * span + r
        q = load(q_ref, qs) * ATTN_SCALE
        if jb == 0:
            lo = sb - span + r
            k_lo, v_lo = load(kp_ref, lo), load(vp_ref, lo)
        else:
            lo = (jb - 1) * span + r
            k_lo, v_lo = load(kc_ref, lo), load(vc_ref, lo)
        kcat = jnp.concatenate([k_lo, load(kc_ref, qs)], axis=0).astype(BF16)
        vcat = jnp.concatenate([v_lo, load(vc_ref, qs)], axis=0).astype(BF16)
        qm = jnp.concatenate([jnp.where(hm[h], q, 0.0) for h in range(HEADS_PER_GROUP)], axis=0).astype(BF16)
        s = lax.dot_general(qm, kcat, (((1,), (1,)), ((), ())), preferred_element_type=F32)
        s = s + (edge_ref[...] if jb == 0 else bias_ref[0])
        m = jnp.max(s, axis=-1, keepdims=True)
        p = jnp.exp(s - m)
        l = jnp.sum(p, axis=-1, keepdims=True)
        oall = _dot((p * (1.0 / l)).astype(BF16), vcat)
        lse = m + jnp.log(l)
        o = jnp.zeros((nq, GROUP_WIDTH), F32)
        ls = jnp.zeros((nq, GROUP_WIDTH), F32)
        for h in range(HEADS_PER_GROUP):
            o = jnp.where(hm[h], oall[h * nq:(h + 1) * nq], o)
            ls = jnp.where(hm[h], lse[h * nq:(h + 1) * nq], ls)
        for sl in range(2):
            o_ref[0, sl, rows(qs), :] = o[:, sl * LANES:(sl + 1) * LANES]
            lse_ref[0, sl, rows(qs), :] = ls[:, sl * LANES:(sl + 1) * LANES]

    if d == 1:
        for jb in range(ncb):
            one_block(0, jb)
    else:
        def body(r, carry):
            for jb in range(ncb):
                one_block(r, jb)
            return carry
        lax.fori_loop(0, d, body, 0, unroll=ATTN_CLASS_UNROLL)


def _attention(qkv, bias_tab, g):
    b, _, s, _ = qkv.shape
    d = ATTN_GROUPS[g][1]
    sb = max(N_DIL_KEYS * d, 512)
    blk = (1, 2, sb, LANES)
    kslab, vslab = ATTN_WIDTH // GROUP_WIDTH + g, 2 * ATTN_WIDTH // GROUP_WIDTH + g
    prev = lambda i: jnp.maximum(i - 1, 0)
    blocks = 7 * _nbytes(blk, F32) + _nbytes((4 * N_DIL_KEYS, 2 * N_DIL_KEYS), F32)
    out = jax.ShapeDtypeStruct((b, 2, s, LANES), F32)
    return pl.pallas_call(
        functools.partial(_attn_kernel, d=d, sb=sb),
        out_shape=(out, out),
        grid=(b, s // sb),
        in_specs=[
            pl.BlockSpec(blk, lambda bi, i: (bi, g, i, 0)),
            pl.BlockSpec(blk, lambda bi, i: (bi, kslab, i, 0)),
            pl.BlockSpec(blk, lambda bi, i: (bi, kslab, prev(i), 0)),
            pl.BlockSpec(blk, lambda bi, i: (bi, vslab, i, 0)),
            pl.BlockSpec(blk, lambda bi, i: (bi, vslab, prev(i), 0)),
            pl.BlockSpec((1, 4 * N_DIL_KEYS, 2 * N_DIL_KEYS), lambda bi, i: (g, 0, 0)),
        ],
        out_specs=(pl.BlockSpec(blk, lambda bi, i: (bi, 0, i, 0)),
                   pl.BlockSpec(blk, lambda bi, i: (bi, 0, i, 0))),
        scratch_shapes=[pltpu.VMEM((HEADS_PER_GROUP * N_DIL_KEYS, 2 * N_DIL_KEYS), F32)],
        compiler_params=pltpu.CompilerParams(
            dimension_semantics=("parallel", "parallel"),
            vmem_limit_bytes=_vmem_limit(blocks, 8 << 20)),
        name=f"attn_d{d}",
    )(qkv, qkv, qkv, qkv, qkv, bias_tab)


def _attn_step_kernel(qkv_ref, c0_ref, c1_ref, c2_ref, b0_ref, b1_ref, b2_ref, o_ref):
    outs = [[None] * N_ATTN_GROUPS for _ in range(HEADS_PER_GROUP)]
    lses = [[None] * N_ATTN_GROUPS for _ in range(HEADS_PER_GROUP)]
    for g, (c_ref, b_ref) in enumerate(((c0_ref, b0_ref), (c1_ref, b1_ref), (c2_ref, b2_ref))):
        for h in range(HEADS_PER_GROUP):
            q = qkv_ref[0, g, h] * ATTN_SCALE
            kn = qkv_ref[0, N_ATTN_GROUPS + g, h]
            vn = qkv_ref[0, 2 * N_ATTN_GROUPS + g, h]
            s_c = jnp.sum(c_ref[0, 0, 0, h] * q, axis=0, keepdims=True) + b_ref[h:h + 1, :]
            s_n = jnp.sum(kn * q, axis=0, keepdims=True) + b_ref[HEADS_PER_GROUP + h:HEADS_PER_GROUP + h + 1, 0:1]
            m = jnp.maximum(jnp.max(s_c, axis=-1, keepdims=True), s_n)
            p_c = jnp.exp(s_c - m)
            p_n = jnp.exp(s_n - m)
            l = jnp.sum(p_c, axis=-1, keepdims=True) + p_n
            inv = 1.0 / l
            outs[h][g] = jnp.sum(c_ref[0, 0, 1, h] * (p_c * inv), axis=-1, keepdims=True) + (p_n * inv) * vn
            lses[h][g] = m + jnp.log(l)
    for h in range(HEADS_PER_GROUP):
        ls, os_ = lses[h], outs[h]
        mx = jnp.maximum(jnp.maximum(ls[0], ls[1]), ls[2])
        w = [jnp.exp(x - mx) for x in ls]
        tot = w[0] + w[1] + w[2]
        o_ref[0, h] = (w[0] * os_[0] + w[1] * os_[1] + w[2] * os_[2]) * (1.0 / tot)


def _attention_step(qkv, caches, biases, l, nb):
    cache_specs = [pl.BlockSpec((1, 1) + c.shape[2:], lambda bi: (l, bi, 0, 0, 0, 0)) for c in caches]
    bias_specs = [pl.BlockSpec(b.shape, lambda bi: (0, 0)) for b in biases]
    blocks = (sum(_nbytes(c.shape[2:], F32) for c in caches) + sum(_nbytes(b.shape, F32) for b in biases)
              + 13 * HEADS_PER_GROUP * HEAD_DIM * LANES * 4)
    return pl.pallas_call(
        _attn_step_kernel,
        out_shape=jax.ShapeDtypeStruct((nb, HEADS_PER_GROUP, HEAD_DIM, 1), F32),
        grid=(nb,),
        in_specs=[pl.BlockSpec((1,) + qkv.shape[1:], lambda bi: (bi, 0, 0, 0, 0))] + cache_specs + bias_specs,
        out_specs=pl.BlockSpec((1, HEADS_PER_GROUP, HEAD_DIM, 1), lambda bi: (bi, 0, 0, 0)),
        compiler_params=pltpu.CompilerParams(
            dimension_semantics=("parallel",),
            vmem_limit_bytes=_vmem_limit(blocks, 8 << 20)),
        name="attn_step",
    )(qkv, *caches, *biases)


def _branch_kernel(*refs, tm, step):
    if step:
        (act_ref, pst_ref, cst_ref, ga_ref, gb_ref, gc_ref, at_ref,
         wgrp_ref, pscale_ref, wpb_ref, cw_ref, wcb_ref, wab_ref,
         out_ref, u_ref, pzs_ref, cb_ref, comb_ref) = refs
    else:
        (act_ref, hist_ref, ga_ref, gb_ref, gc_ref, o0_ref, o1_ref, o2_ref, l0_ref, l1_ref, l2_ref,
         wgrp_ref, pscale_ref, wpb_ref, cw_ref, wcb_ref, wab_ref,
         out_ref, u_ref, pzs_ref, cb_ref, comb_ref, pe_ref, ue_ref) = refs
    i = pl.program_id(1)

    @pl.when(pl.program_id(2) == 0)
    def _():
        p = act_ref[0, :, 0:POOL_WIDTH]
        gate_b = act_ref[0, :, POOL_WIDTH:POOL_WIDTH + CONV_CH]
        u = act_ref[0, :, POOL_WIDTH + CONV_CH:POOL_WIDTH + 2 * CONV_CH] * \
            act_ref[0, :, POOL_WIDTH + 2 * CONV_CH:POOL_WIDTH + 3 * CONV_CH]
        cw = cw_ref[0]
        if step:
            acc = p
            sums = {}
            for k in range(1, max(POOL_WINDOWS)):
                acc = acc + pst_ref[0, POOL_STATE - k]
                sums[k + 1] = acc
            means = [sums[w][:, gi * POOL_GROUP:(gi + 1) * POOL_GROUP] * (1.0 / w)
                     for gi, w in enumerate(POOL_WINDOWS)]
            conv = cst_ref[0, 0] * cw[0:1] + cst_ref[0, 1] * cw[1:2] + u * cw[2:3]
            u_ref[0, 0] = u
            comb_ref[...] = at_ref[0].astype(BF16)
        else:
            first = i == 0
            hist_p = hist_ref[0, :, 0:POOL_WIDTH]
            hist_u = hist_ref[0, :, POOL_WIDTH + CONV_CH:POOL_WIDTH + 2 * CONV_CH] * \
                hist_ref[0, :, POOL_WIDTH + 2 * CONV_CH:POOL_WIDTH + 3 * CONV_CH]
            pe_ref[0:HIST] = jnp.where(first, 0.0, hist_p)
            ue_ref[0:HIST] = jnp.where(first, 0.0, hist_u)
            pe_ref[HIST:HIST + tm] = p
            ue_ref[HIST:HIST + tm] = u
            t = i * tm + lax.broadcasted_iota(jnp.int32, (tm, 1), 0)
            means = []
            for gi, w in enumerate(POOL_WINDOWS):
                cs = slice(gi * POOL_GROUP, (gi + 1) * POOL_GROUP)
                acc = pe_ref[HIST:HIST + tm, cs]
                for k in range(1, w):
                    acc = acc + pe_ref[HIST - k:HIST - k + tm, cs]
                cnt = jnp.minimum(t + 1, w).astype(F32)
                means.append(acc * (1.0 / cnt))
            conv = (ue_ref[HIST - 2:HIST - 2 + tm] * cw[0:1] + ue_ref[HIST - 1:HIST - 1 + tm] * cw[1:2]
                    + u * cw[2:3])
            u_ref[0, 0] = ue_ref[HIST + tm - 8:HIST + tm]
            for sl in range(2):
                ls = [r[0, sl] for r in (l0_ref, l1_ref, l2_ref)]
                os_ = [r[0, sl] for r in (o0_ref, o1_ref, o2_ref)]
                mx = jnp.maximum(jnp.maximum(ls[0], ls[1]), ls[2])
                w_ = [jnp.exp(x - mx) for x in ls]
                tot = w_[0] + w_[1] + w_[2]
                comb = (w_[0] * os_[0] + w_[1] * os_[1] + w_[2] * os_[2]) * (1.0 / tot)
                comb_ref[:, sl * LANES:(sl + 1) * LANES] = comb.astype(BF16)
        pscale = pscale_ref[0]
        for gi in range(len(POOL_WINDOWS)):
            cs = slice(gi * POOL_GROUP, (gi + 1) * POOL_GROUP)
            pm = (means[gi] - p[:, cs]).astype(BF16)
            pz = _dot(pm, wgrp_ref[0, gi])
            pzs_ref[:, cs] = (pz * pscale[:, cs]).astype(BF16)
        cb_ref[...] = (gate_b * conv).astype(BF16)

    br_a = _dot(pzs_ref[...], wpb_ref[0])
    br_b = _dot(cb_ref[...], wcb_ref[0])
    br_c = _dot(comb_ref[...], wab_ref[0])
    merged = _sigmoid(ga_ref[0]) * br_a + _sigmoid(gb_ref[0]) * br_b + _sigmoid(gc_ref[0]) * br_c
    out_ref[0] = merged.astype(BF16)


def _branches(rest, attn, states, weights, l, *, tm, tc, step):
    b, s, _ = rest.shape
    wgrp, pscale, wpb, cw, wcb, wab = weights
    ni = s // tm
    gblk = ACT_WIDTH // tc
    gate_specs = [pl.BlockSpec((1, tm, tc), lambda bi, i, c, k=k: (bi, i, gblk + k * (D_MODEL // tc) + c))
                  for k in range(3)]
    w_specs = [
        pl.BlockSpec((1,) + wgrp.shape[1:], lambda bi, i, c: (l, 0, 0, 0)),
        pl.BlockSpec((1,) + pscale.shape[1:], lambda bi, i, c: (l, 0, 0)),
        pl.BlockSpec((1, POOL_WIDTH, tc), lambda bi, i, c: (l, 0, c)),
        pl.BlockSpec((1,) + cw.shape[1:], lambda bi, i, c: (l, 0, 0)),
        pl.BlockSpec((1, CONV_CH, tc), lambda bi, i, c: (l, 0, c)),
        pl.BlockSpec((1, GROUP_WIDTH, tc), lambda bi, i, c: (l, 0, c)),
    ]
    act_spec = pl.BlockSpec((1, tm, ACT_WIDTH), lambda bi, i, c: (bi, i, 0))
    scratch = [pltpu.VMEM((tm, POOL_WIDTH), BF16), pltpu.VMEM((tm, CONV_CH), BF16),
               pltpu.VMEM((tm, GROUP_WIDTH), BF16)]
    blocks = (_nbytes((tm, ACT_WIDTH), F32) + 3 * _nbytes((tm, tc), F32) + _nbytes(wgrp.shape[1:], BF16)
              + _nbytes((POOL_WIDTH + CONV_CH + GROUP_WIDTH, tc), BF16) + _nbytes((tm, tc), BF16))
    if step:
        pst, cst = states
        ins = [rest, pst, cst, rest, rest, rest, attn]
        in_specs = [act_spec,
                    pl.BlockSpec((1,) + pst.shape[1:], lambda bi, i, c: (l, 0, 0, 0)),
                    pl.BlockSpec((1,) + cst.shape[1:], lambda bi, i, c: (l, 0, 0, 0))] + gate_specs + [
                    pl.BlockSpec((1, tm, GROUP_WIDTH), lambda bi, i, c: (bi, i, 0))]
        urows = tm
        blocks += _nbytes(pst.shape[1:], F32) + _nbytes(cst.shape[1:], F32)
    else:
        hblk = tm // HIST
        slab = pl.BlockSpec((1, 2, tm, LANES), lambda bi, i, c: (bi, 0, i, 0))
        ins = [rest, rest, rest, rest, rest] + list(attn)
        in_specs = [act_spec,
                    pl.BlockSpec((1, HIST, ACT_WIDTH), lambda bi, i, c: (bi, jnp.maximum(i * hblk - 1, 0), 0))
                    ] + gate_specs + [slab] * 6
        scratch += [pltpu.VMEM((tm + HIST, POOL_WIDTH), F32), pltpu.VMEM((tm + HIST, CONV_CH), F32)]
        urows = 8
        blocks += 6 * _nbytes((2, tm, LANES), F32) + _nbytes((HIST, ACT_WIDTH), F32)
    scratch_bytes = 3 * _nbytes((tm + HIST, POOL_WIDTH), F32) + 4 * _nbytes((tm, tc), F32)
    return pl.pallas_call(
        functools.partial(_branch_kernel, tm=tm, step=step),
        out_shape=(jax.ShapeDtypeStruct((b, s, D_MODEL), BF16),
                   jax.ShapeDtypeStruct((b, ni, urows, CONV_CH), F32)),
        grid=(b, ni, D_MODEL // tc),
        in_specs=in_specs + w_specs,
        out_specs=(pl.BlockSpec((1, tm, tc), lambda bi, i, c: (bi, i, c)),
                   pl.BlockSpec((1, 1, urows, CONV_CH), lambda bi, i, c: (bi, i, 0, 0))),
        scratch_shapes=scratch,
        compiler_params=pltpu.CompilerParams(
            dimension_semantics=("parallel", "parallel", "arbitrary"),
            vmem_limit_bytes=_vmem_limit(blocks, scratch_bytes)),
        name="branches_step" if step else "branches",
    )(*ins, wgrp, pscale, wpb, cw, wcb, wab)


def _oproj_kernel(m_ref, w_ref, x_ref, g_ref, gate_ref, o_ref):
    mix = _dot(m_ref[0], w_ref[0])
    o_ref[0] = x_ref[0] + _tail2(gate_ref) * _rms(mix, g_ref[0])


def _oproj(merged, w_o, x, norm, mod, l, *, tm):
    b, s, d = x.shape
    blocks = (_nbytes((tm, d), BF16) + _nbytes((d, d), BF16) + 2 * _nbytes((tm, d), F32)
              + _nbytes((_mod_rows(mod) + 1, d), F32))
    return pl.pallas_call(
        _oproj_kernel,
        out_shape=jax.ShapeDtypeStruct((b, s, d), F32),
        grid=(b, s // tm),
        in_specs=[
            pl.BlockSpec((1, tm, d), lambda bi, i: (bi, i, 0)),
            pl.BlockSpec((1, d, d), lambda bi, i: (l, 0, 0)),
            pl.BlockSpec((1, tm, d), lambda bi, i: (bi, i, 0)),
            norm[1](1), mod[1](2),
        ],
        out_specs=pl.BlockSpec((1, tm, d), lambda bi, i: (bi, i, 0)),
        compiler_params=pltpu.CompilerParams(
            dimension_semantics=("parallel", "parallel"),
            vmem_limit_bytes=_vmem_limit(blocks, _nbytes((tm, d), F32))),
        name="oproj",
    )(merged, w_o, x, norm[0], mod[0])


def _ffn_kernel(*refs, tm, step, nchunk):
    if step:
        (x_ref, g2_ref, sc_ref, sh_ref, wg_ref, wv_ref, cwg_ref, cwv_ref, wd_ref, g3_ref, gate_ref,
         stg_ref, stv_ref, o_ref, tg_ref, tv_ref, h_ref) = refs
        hist = 0
    else:
        (x_ref, xp_ref, g2_ref, sc_ref, sh_ref, wg_ref, wv_ref, cwg_ref, cwv_ref, wd_ref, g3_ref, gate_ref,
         o_ref, tg_ref, tv_ref, h_ref, ug_ref, uv_ref) = refs
        hist = HIST
    i = pl.program_id(1)
    j = pl.program_id(2)

    @pl.when(j == 0)
    def _():
        if not step:
            _modulate_rows(xp_ref, g2_ref, sc_ref, sh_ref, h_ref, 0, HIST)
        _modulate_rows(x_ref, g2_ref, sc_ref, sh_ref, h_ref, hist, tm)
        o_ref[...] = jnp.zeros_like(o_ref)

    cwg = cwg_ref[0]
    cwv = cwv_ref[0]
    if step:
        h = h_ref[...]
        up_g = _dot(h, wg_ref[0])
        up_v = _dot(h, wv_ref[0])
        uc_g = stg_ref[0, 0] * cwg[0:1] + stg_ref[0, 1] * cwg[1:2] + up_g * cwg[2:3]
        uc_v = stv_ref[0, 0] * cwv[0:1] + stv_ref[0, 1] * cwv[1:2] + up_v * cwv[2:3]
        tg_ref[0, 0] = up_g
        tv_ref[0, 0] = up_v
        o_ref[0] += _dot((_gelu_tanh(uc_g) * uc_v).astype(BF16), wd_ref[0])
    else:
        first = i == 0
        rc = tm // nchunk
        for c in range(nchunk):
            lo = 0 if c == 0 else HIST + c * rc
            hi = HIST + (c + 1) * rc
            h = h_ref[lo:hi]
            for u_ref, w_ref in ((ug_ref, wg_ref), (uv_ref, wv_ref)):
                up = _dot(h, w_ref[0])
                if c == 0:
                    u_ref[0:HIST] = jnp.where(first, 0.0, up[0:HIST])
                    u_ref[HIST:hi] = up[HIST:]
                else:
                    u_ref[lo:hi] = up

            def conv(ref, cw):
                base = HIST + c * rc
                return (ref[base - 2:base - 2 + rc] * cw[0:1] + ref[base - 1:base - 1 + rc] * cw[1:2]
                        + ref[base:base + rc] * cw[2:3])
            act = (_gelu_tanh(conv(ug_ref, cwg)) * conv(uv_ref, cwv)).astype(BF16)
            o_ref[0, c * rc:(c + 1) * rc] += _dot(act, wd_ref[0])
        tg_ref[0, 0] = ug_ref[HIST + tm - 8:HIST + tm]
        tv_ref[0, 0] = uv_ref[HIST + tm - 8:HIST + tm]

    @pl.when(j == pl.num_programs(2) - 1)
    def _():
        o_ref[0] = x_ref[0] + _tail2(gate_ref) * _rms(o_ref[0], g3_ref[0])


def _ffn(x, norm, mod, w_up, cw, w_down, state, l, *, tm, tf, step):
    b, s, d = x.shape
    ni, nj = s // tm, D_FF // tf
    trows = tm if step else 8
    w_specs = [
        pl.BlockSpec((1, d, tf), lambda bi, i, j: (l, 0, j)),
        pl.BlockSpec((1, d, tf), lambda bi, i, j: (l, 0, nj + j)),
        pl.BlockSpec((1, FFN_K, tf), lambda bi, i, j: (l, 0, j)),
        pl.BlockSpec((1, FFN_K, tf), lambda bi, i, j: (l, 0, nj + j)),
        pl.BlockSpec((1, tf, d), lambda bi, i, j: (l, j, 0)),
    ]
    x_spec = pl.BlockSpec((1, tm, d), lambda bi, i, j: (bi, i, 0), pipeline_mode=pl.Buffered(1))
    hist = 0 if step else HIST
    scratch = [pltpu.VMEM((tm + hist, d), BF16)]
    blocks = (2 * _nbytes((tm, d), F32) + 3 * _nbytes((d, tf), BF16) + 2 * _nbytes((trows, tf), F32)
              + _nbytes((3 * _mod_rows(mod) + 2, d), F32))
    mods = [norm[1](2), mod[1](4), mod[1](3)]
    tailp = [norm[1](3), mod[1](5)]
    if step:
        ins = [x, norm[0], mod[0], mod[0], w_up, w_up, cw, cw, w_down, norm[0], mod[0], state, state]
        in_specs = [x_spec] + mods + w_specs + tailp + [
            pl.BlockSpec((1, FFN_K - 1, tm, tf), lambda bi, i, j: (l, 0, 0, j)),
            pl.BlockSpec((1, FFN_K - 1, tm, tf), lambda bi, i, j: (l, 0, 0, nj + j))]
        blocks += 2 * _nbytes((FFN_K - 1, tm, tf), F32)
    else:
        hblk = tm // HIST
        ins = [x, x, norm[0], mod[0], mod[0], w_up, w_up, cw, cw, w_down, norm[0], mod[0]]
        in_specs = [x_spec,
                    pl.BlockSpec((1, HIST, d), lambda bi, i, j: (bi, jnp.maximum(i * hblk - 1, 0), 0))
                    ] + mods + w_specs + tailp
        scratch += [pltpu.VMEM((tm + HIST, tf), F32), pltpu.VMEM((tm + HIST, tf), F32)]
        blocks += _nbytes((HIST, d), F32)
    scratch_bytes = _nbytes((tm + hist, d), BF16) + 6 * _nbytes((tm + hist, tf), F32)
    tail = jax.ShapeDtypeStruct((b, ni, trows, D_FF), F32)
    tail_spec = lambda: pl.BlockSpec((1, 1, trows, tf), lambda bi, i, j: (bi, i, 0, j))
    return pl.pallas_call(
        functools.partial(_ffn_kernel, tm=tm, step=step, nchunk=FFN_ROW_CHUNKS),
        out_shape=(jax.ShapeDtypeStruct((b, s, d), F32), tail, tail),
        grid=(b, ni, nj),
        in_specs=in_specs,
        out_specs=(pl.BlockSpec((1, tm, d), lambda bi, i, j: (bi, i, 0)), tail_spec(), tail_spec()),
        scratch_shapes=scratch,
        compiler_params=pltpu.CompilerParams(
            dimension_semantics=("parallel", "parallel", "arbitrary"),
            vmem_limit_bytes=_vmem_limit(blocks, scratch_bytes)),
        name="ffn_step" if step else "ffn",
    )(*ins)


def _group_rel_bias(rel_bias):
    n = np.arange(N_DIL_KEYS + 1)
    max_exact = N_BUCKETS // 2
    out = []
    for g, (_, d) in enumerate(ATTN_GROUPS):
        dist = n * d
        large = max_exact + (np.log(np.maximum(dist, 1) / max_exact) / np.log(MAX_DISTANCE / max_exact)
                             * (N_BUCKETS - max_exact)).astype(np.int32)
        bucket = np.where(dist < max_exact, dist, np.minimum(large, N_BUCKETS - 1))
        out.append(rel_bias[bucket, g * HEADS_PER_GROUP:(g + 1) * HEADS_PER_GROUP].T)
    return jnp.stack(out)


def _bias_tables(rel_bias):
    bias_g = _group_rel_bias(rel_bias).astype(F32)
    n = N_DIL_KEYS
    gh = (N_ATTN_GROUPS, HEADS_PER_GROUP)
    ext = jnp.concatenate([bias_g[:, :, ::-1], jnp.full(gh + (n,), NEG_INF, F32)], axis=-1)
    tab = jnp.tile(ext, (1, 1, n))[:, :, :n * 2 * n].reshape(N_ATTN_GROUPS, HEADS_PER_GROUP * n, 2 * n)
    step = []
    for g, (_, d) in enumerate(ATTN_GROUPS):
        hit = bias_g[g, :, :0:-1, None]
        row = jnp.concatenate([hit, jnp.full((HEADS_PER_GROUP, n, d - 1), NEG_INF, F32)], axis=-1)
        new = jnp.broadcast_to(bias_g[g, :, 0:1], (HEADS_PER_GROUP, n * d))
        step.append(jnp.concatenate([row.reshape(HEADS_PER_GROUP, n * d), new], axis=0))
    return tab, step


def _kv_pack(qkv, g, nrows):
    b = qkv.shape[0]
    k0 = ATTN_WIDTH + g * GROUP_WIDTH
    v0 = 2 * ATTN_WIDTH + g * GROUP_WIDTH
    k = qkv[:, -nrows:, k0:k0 + GROUP_WIDTH].reshape(b, nrows, HEADS_PER_GROUP, HEAD_DIM)
    v = qkv[:, -nrows:, v0:v0 + GROUP_WIDTH].reshape(b, nrows, HEADS_PER_GROUP, HEAD_DIM)
    return jnp.stack([k, v], axis=2)


def _prompt_layer(x, l, P):
    b, s, _ = x.shape
    norm, mod = P["norm"](l), P["mod_p"](l)
    tm = 512
    qkv, qkv_slab, h = _proj(x, norm, mod, P["w_qkv"], l, tm=1024, tn=ATTN_WIDTH)
    rest = _matmul(h, P["w_rest"], l, tm=1024, tn=1024)
    ols = [_attention(qkv_slab, P["bias_tab"], g) for g in range(N_ATTN_GROUPS)]
    attn = [o for o, _ in ols] + [ls for _, ls in ols]
    merged, u_tail = _branches(rest, attn, None, P["branch"], l, tm=tm, tc=512, step=False)
    x1 = _oproj(merged, P["w_o"], x, norm, mod, l, tm=tm)
    x2, tail_g, tail_v = _ffn(x1, norm, mod, P["w_up"], P["ffn_cw"], P["w_down"], None, l,
                              tm=1024, tf=512, step=False)
    new_kv = [_kv_pack(qkv, g, min(w, s)) for g, (w, _) in enumerate(ATTN_GROUPS)]
    new_pool = rest[:, -POOL_STATE:, :POOL_WIDTH]
    new_conv = u_tail[:, -1, -(CONV_K - 1):]
    new_ffn = jnp.concatenate([tail_g[:, -1, -(FFN_K - 1):], tail_v[:, -1, -(FFN_K - 1):]], axis=-1)
    return x2, (new_kv[0], new_kv[1], new_kv[2], new_pool, new_conv, new_ffn)


def _pad_rows(a, axis):
    pad = [(0, 0)] * a.ndim
    pad[axis] = (0, SAMPLE_ROWS - a.shape[axis])
    return jnp.pad(a, pad)


def _sample_layer(x, l, P, S, nb):
    norm, mod = P["norm"](l), P["mod_s"](l)
    tm = SAMPLE_ROWS
    qkv, _, h = _proj(x, norm, mod, P["w_qkv"], l, tm=tm, tn=ATTN_WIDTH)
    rest = _matmul(h, P["w_rest"], l, tm=tm, tn=1024)
    heads = qkv[0, :nb].reshape(nb, 3 * N_ATTN_GROUPS, HEADS_PER_GROUP, HEAD_DIM, 1)
    attn = _attention_step(heads, S["caches"], P["bias_step"], l, nb)
    attn = _pad_rows(attn.reshape(1, nb, GROUP_WIDTH), 1)
    merged, u = _branches(rest, attn, (S["pool_t"], S["conv_t"]), P["branch"], l, tm=tm, tc=512, step=True)
    x1 = _oproj(merged, P["w_o"], x, norm, mod, l, tm=tm)
    x2, up_g, up_v = _ffn(x1, norm, mod, P["w_up"], P["ffn_cw"], P["w_down"], S["ffn_t"], l,
                          tm=tm, tf=512, step=True)
    new_kv = [_kv_pack(qkv[0, :nb, None], g, 1) for g in range(N_ATTN_GROUPS)]
    new_pool = jnp.concatenate([S["pool"][l][:, 1:], rest[0, :nb, None, :POOL_WIDTH]], axis=1)
    new_conv = jnp.concatenate([S["conv"][l][:, 1:], u[0, 0, :nb, None]], axis=1)
    up_new = jnp.concatenate([up_g[0, 0, :nb], up_v[0, 0, :nb]], axis=-1)
    new_ffn = jnp.concatenate([S["ffn"][l][:, 1:], up_new[:, None]], axis=1)
    return x2, (new_kv[0], new_kv[1], new_kv[2], new_pool, new_conv, new_ffn)


def kernel(x_prompt, x_sample, c_prompt, c_sample, cache_kv_w128, cache_kv_w512, cache_kv_w2048, state_pool, state_conv, state_ffn_conv, rel_bias, norm_g, w_ada, b_ada, w_in, w_attn_br, w_pool_grp, pool_scale, w_pool_br, conv_w, w_conv_br, w_o, w_up, ffn_conv_w, w_down):
    nbp = x_prompt.shape[0]
    nbs, tdec, _ = x_sample.shape
    assert tdec == 1 and nbs <= SAMPLE_ROWS
    caches = (cache_kv_w128, cache_kv_w512, cache_kv_w2048)
    for (w, d), c in zip(ATTN_GROUPS, caches):
        assert c.shape[2] == w == N_DIL_KEYS * d, "cache must hold exactly one window"

    bias_tab, bias_step = _bias_tables(rel_bias)
    c_rows = -(-(SAMPLE_ROWS + nbp) // 8) * 8
    c_all = jnp.zeros((c_rows, D_MODEL), F32).at[:nbs].set(c_sample).at[SAMPLE_ROWS:SAMPLE_ROWS + nbp].set(c_prompt)
    mod_all = _ada(c_all, w_ada, b_ada)

    P = dict(
        norm=functools.partial(_norm_view, norm_g),
        mod_p=functools.partial(_mod_view, mod_all, prompt=True),
        mod_s=functools.partial(_mod_view, mod_all, prompt=False),
        w_qkv=w_in[:, :, :QKV_WIDTH].astype(BF16),
        w_rest=w_in[:, :, QKV_WIDTH:].astype(BF16),
        branch=(w_pool_grp.astype(BF16), pool_scale[:, None, :], w_pool_br.astype(BF16),
                conv_w, w_conv_br.astype(BF16), w_attn_br.astype(BF16)),
        w_o=w_o.astype(BF16), w_up=w_up.astype(BF16), ffn_cw=ffn_conv_w, w_down=w_down.astype(BF16),
        bias_tab=bias_tab, bias_step=bias_step,
    )
    S = dict(
        caches=[c.transpose(0, 1, 3, 4, 5, 2) for c in caches],
        pool=state_pool, conv=state_conv, ffn=state_ffn_conv,
        pool_t=_pad_rows(state_pool.transpose(0, 2, 1, 3), 2),
        conv_t=_pad_rows(state_conv.transpose(0, 2, 1, 3), 2),
        ffn_t=_pad_rows(state_ffn_conv.transpose(0, 2, 1, 3), 2),
    )
    yp = x_prompt
    ys = _pad_rows(x_sample.reshape(1, nbs, D_MODEL), 1)
    st_p, st_s = [], []
    for l in range(DEPTH):
        yp, sp = _prompt_layer(yp, l, P)
        ys, ss = _sample_layer(ys, l, P, S, nbs)
        st_p.append(sp)
        st_s.append(ss)
    outs_p = [jnp.stack([s[k] for s in st_p]) for k in range(6)]
    outs_s = [jnp.stack([s[k] for s in st_s]) for k in range(6)]
    return (yp, ys[0, :nbs, None, :], *outs_p, *outs_s)
```

```python
import functools

import numpy as np
import jax
import jax.numpy as jnp
from jax import lax
from jax.experimental import pallas as pl
from jax.experimental.pallas import tpu as pltpu

F32 = jnp.float32
BF16 = jnp.bfloat16

D_MODEL = 2048
DEPTH = 2
HEAD_DIM = 64
HEADS_PER_GROUP = 4
ATTN_GROUPS = ((128, 1), (512, 4), (2048, 16))
N_ATTN_GROUPS = len(ATTN_GROUPS)
ATTN_WIDTH = N_ATTN_GROUPS * HEADS_PER_GROUP * HEAD_DIM
GROUP_WIDTH = HEADS_PER_GROUP * HEAD_DIM
N_DIL_KEYS = 128
N_BUCKETS = 32
MAX_DISTANCE = 2048
ATTN_SCALE = HEAD_DIM ** -0.5
POOL_WINDOWS = (2, 4, 8, 16)
POOL_GROUP = 128
POOL_WIDTH = 512
POOL_STATE = 15
CONV_CH = 512
CONV_K = 3
D_FF = 5632
FFN_K = 3
N_MOD = 6
N_NORM = 4
EPS = 1e-6
NEG_INF = -1e30

LANES = 128
HIST = 16
SAMPLE_ROWS = 16
QKV_WIDTH = 3 * ATTN_WIDTH
QKV_SLABS = QKV_WIDTH // LANES
ACT_WIDTH = POOL_WIDTH + 3 * CONV_CH
ATTN_CLASS_UNROLL = 4
FFN_ROW_CHUNKS = 2
VMEM_CAP = 56 * 1024 * 1024


def _vmem_limit(block_bytes, scratch_bytes=0):
    est = 2 * block_bytes + scratch_bytes
    return int(min(VMEM_CAP, est + est // 4 + (4 << 20)))


def _nbytes(shape, dtype):
    return int(np.prod(shape)) * jnp.dtype(dtype).itemsize


def _rms(x, g):
    return x * lax.rsqrt(jnp.mean(x * x, axis=-1, keepdims=True) + EPS) * g


def _sigmoid(x):
    return 0.5 + 0.5 * jnp.tanh(0.5 * x)


def _gelu_tanh(x):
    return 0.5 * x * (1.0 + jnp.tanh(np.sqrt(2.0 / np.pi) * (x + 0.044715 * (x * x * x))))


def _dot(a, b):
    return jnp.dot(a, b, preferred_element_type=F32)


def _tail2(ref):
    return ref[(0,) * (len(ref.shape) - 2)]


def _norm_view(norm_g, l):
    arr = norm_g.reshape(DEPTH * N_NORM, 1, D_MODEL)
    return arr, lambda k: pl.BlockSpec((1, 1, D_MODEL), lambda *_: (l * N_NORM + k, 0, 0))


def _mod_view(mod_all, l, prompt):
    if prompt:
        arr = mod_all.reshape(DEPTH, mod_all.shape[1], N_MOD, 1, D_MODEL)
        return arr, lambda k: pl.BlockSpec((1, 1, 1, 1, D_MODEL), lambda bi, *_: (l, SAMPLE_ROWS + bi, k, 0, 0))
    return mod_all, lambda k: pl.BlockSpec((1, SAMPLE_ROWS, D_MODEL), lambda bi, *_: (l, 0, k))


def _mod_rows(mod):
    return mod[1](0).block_shape[-2]


def _ada_kernel(c_ref, w_ref, b_ref, o_ref):
    c = c_ref[...]
    s = (c * _sigmoid(c)).astype(BF16)
    o_ref[0] = _dot(s, w_ref[0].astype(BF16)) + b_ref[0]


def _ada(c_all, w_ada, b_ada):
    rows = c_all.shape[0]
    n = w_ada.shape[-1]
    tn = 1024
    blocks = _nbytes((rows, D_MODEL), F32) + _nbytes((D_MODEL, tn), F32) + _nbytes((rows + 1, tn), F32)
    return pl.pallas_call(
        _ada_kernel,
        out_shape=jax.ShapeDtypeStruct((DEPTH, rows, n), F32),
        grid=(DEPTH, n // tn),
        in_specs=[
            pl.BlockSpec((rows, D_MODEL), lambda l, j: (0, 0)),
            pl.BlockSpec((1, D_MODEL, tn), lambda l, j: (l, 0, j)),
            pl.BlockSpec((1, 1, tn), lambda l, j: (l, 0, j)),
        ],
        out_specs=pl.BlockSpec((1, rows, tn), lambda l, j: (l, 0, j)),
        compiler_params=pltpu.CompilerParams(
            dimension_semantics=("parallel", "parallel"),
            vmem_limit_bytes=_vmem_limit(blocks, _nbytes((D_MODEL, tn), BF16))),
        name="ada",
    )(c_all, w_ada, b_ada.reshape(DEPTH, 1, n))


def _modulate_rows(x_ref, g_ref, sc_ref, sh_ref, h_ref, row0, tm):
    y = _rms(x_ref[0], g_ref[0])
    h_ref[row0:row0 + tm] = (y * (1.0 + _tail2(sc_ref)) + _tail2(sh_ref)).astype(BF16)


def _proj_kernel(x_ref, g_ref, sc_ref, sh_ref, w_ref, o_ref, slab_ref, h_ref, *, slabs, tm):
    @pl.when(pl.program_id(2) == 0)
    def _():
        _modulate_rows(x_ref, g_ref, sc_ref, sh_ref, h_ref.at[0], 0, tm)

    res = _dot(h_ref[0], w_ref[0])
    o_ref[0] = res
    for s in range(slabs):
        slab_ref[0, s] = res[:, s * LANES:(s + 1) * LANES]


def _proj(x, norm, mod, w, l, *, tm, tn):
    b, s, d = x.shape
    n = w.shape[2]
    slabs = tn // LANES
    r = _mod_rows(mod)
    blocks = (_nbytes((tm, d), F32) + _nbytes((2 * r + 1, d), F32) + _nbytes((d, tn), BF16)
              + 2 * _nbytes((tm, tn), F32) + _nbytes((tm, d), BF16))
    return pl.pallas_call(
        functools.partial(_proj_kernel, slabs=slabs, tm=tm),
        out_shape=[jax.ShapeDtypeStruct((b, s, n), F32),
                   jax.ShapeDtypeStruct((b, n // LANES, s, LANES), F32),
                   jax.ShapeDtypeStruct((b, s, d), BF16)],
        grid=(b, s // tm, n // tn),
        in_specs=[
            pl.BlockSpec((1, tm, d), lambda bi, i, j: (bi, i, 0)),
            norm[1](0), mod[1](1), mod[1](0),
            pl.BlockSpec((1, d, tn), lambda bi, i, j: (l, 0, j)),
        ],
        out_specs=[pl.BlockSpec((1, tm, tn), lambda bi, i, j: (bi, i, j)),
                   pl.BlockSpec((1, slabs, tm, LANES), lambda bi, i, j: (bi, j, i, 0)),
                   pl.BlockSpec((1, tm, d), lambda bi, i, j: (bi, i, 0))],
        compiler_params=pltpu.CompilerParams(
            dimension_semantics=("parallel", "parallel", "arbitrary"),
            vmem_limit_bytes=_vmem_limit(blocks, _nbytes((tm, tn), F32) + _nbytes((tm, d), F32))),
        name="proj_norm",
    )(x, norm[0], mod[0], mod[0], w)


def _matmul_kernel(h_ref, w_ref, o_ref):
    o_ref[0] = _dot(h_ref[0], w_ref[0])


def _matmul(h, w, l, *, tm, tn):
    b, s, d = h.shape
    n = w.shape[2]
    blocks = _nbytes((tm, d), BF16) + _nbytes((d, tn), BF16) + _nbytes((tm, tn), F32)
    return pl.pallas_call(
        _matmul_kernel,
        out_shape=jax.ShapeDtypeStruct((b, s, n), F32),
        grid=(b, s // tm, n // tn),
        in_specs=[pl.BlockSpec((1, tm, d), lambda bi, i, j: (bi, i, 0)),
                  pl.BlockSpec((1, d, tn), lambda bi, i, j: (l, 0, j))],
        out_specs=pl.BlockSpec((1, tm, tn), lambda bi, i, j: (bi, i, j)),
        compiler_params=pltpu.CompilerParams(
            dimension_semantics=("parallel", "parallel", "parallel"),
            vmem_limit_bytes=_vmem_limit(blocks, _nbytes((tm, tn), F32))),
        name="proj_act",
    )(h, w)


def _head_masks(rows):
    lane = lax.broadcasted_iota(jnp.int32, (rows, GROUP_WIDTH), 1)
    return [(lane >= h * HEAD_DIM) & (lane < (h + 1) * HEAD_DIM) for h in range(HEADS_PER_GROUP)]


def _attn_kernel(q_ref, kc_ref, kp_ref, vc_ref, vp_ref, bias_ref, o_ref, lse_ref, edge_ref, *, d, sb):
    i = pl.program_id(1)
    nq = N_DIL_KEYS
    span = nq * d
    ncb = sb // span
    hm = _head_masks(1)
    col = lax.broadcasted_iota(jnp.int32, (1, 2 * nq), 1)
    edge_ref[...] = jnp.where((col < nq) & (i == 0), NEG_INF, bias_ref[0])

    def rows(start):
        return pl.ds(start, nq, stride=d) if d > 1 else pl.ds(start, nq)

    def load(ref, start):
        return jnp.concatenate([ref[0, s, rows(start), :] for s in range(2)], axis=1)

    def one_block(r, jb):
        qs = jb * span + r
        q = load(q_ref, qs) * ATTN_SCALE
        if jb == 0:
            lo = sb - span + r
            k_lo, v_lo = load(kp_ref, lo), load(vp_ref, lo)
        else:
            lo = (jb - 1) * span + r
            k_lo, v_lo = load(kc_ref, lo), load(vc_ref, lo)
        kcat = jnp.concatenate([k_lo, load(kc_ref, qs)], axis=0).astype(BF16)
        vcat = jnp.concatenate([v_lo, load(vc_ref, qs)], axis=0).astype(BF16)
        qm = jnp.concatenate([jnp.where(hm[h], q, 0.0) for h in range(HEADS_PER_GROUP)], axis=0).astype(BF16)
        s = lax.dot_general(qm, kcat, (((1,), (1,)), ((), ())), preferred_element_type=F32)
        s = s + (edge_ref[...] if jb == 0 else bias_ref[0])
        m = jnp.max(s, axis=-1, keepdims=True)
        p = jnp.exp(s - m)
        l = jnp.sum(p, axis=-1, keepdims=True)
        oall = _dot((p * (1.0 / l)).astype(BF16), vcat)
        lse = m + jnp.log(l)
        o = jnp.zeros((nq, GROUP_WIDTH), F32)
        ls = jnp.zeros((nq, GROUP_WIDTH), F32)
        for h in range(HEADS_PER_GROUP):
            o = jnp.where(hm[h], oall[h * nq:(h + 1) * nq], o)
            ls = jnp.where(hm[h], lse[h * nq:(h + 1) * nq], ls)
        for sl in range(2):
            o_ref[0, sl, rows(qs), :] = o[:, sl * LANES:(sl + 1) * LANES]
            lse_ref[0, sl, rows(qs), :] = ls[:, sl * LANES:(sl + 1) * LANES]

    if d == 1:
        for jb in range(ncb):
            one_block(0, jb)
    else:
        def body(r, carry):
            for jb in range(ncb):
                one_block(r, jb)
            return carry
        lax.fori_loop(0, d, body, 0, unroll=ATTN_CLASS_UNROLL)


def _attention(qkv, bias_tab, g):
    b, _, s, _ = qkv.shape
    d = ATTN_GROUPS[g][1]
    sb = max(N_DIL_KEYS * d, 512)
    blk = (1, 2, sb, LANES)
    kslab, vslab = ATTN_WIDTH // GROUP_WIDTH + g, 2 * ATTN_WIDTH // GROUP_WIDTH + g
    prev = lambda i: jnp.maximum(i - 1, 0)
    blocks = 7 * _nbytes(blk, F32) + _nbytes((4 * N_DIL_KEYS, 2 * N_DIL_KEYS), F32)
    out = jax.ShapeDtypeStruct((b, 2, s, LANES), F32)
    return pl.pallas_call(
        functools.partial(_attn_kernel, d=d, sb=sb),
        out_shape=(out, out),
        grid=(b, s // sb),
        in_specs=[
            pl.BlockSpec(blk, lambda bi, i: (bi, g, i, 0)),
            pl.BlockSpec(blk, lambda bi, i: (bi, kslab, i, 0)),
            pl.BlockSpec(blk, lambda bi, i: (bi, kslab, prev(i), 0)),
            pl.BlockSpec(blk, lambda bi, i: (bi, vslab, i, 0)),
            pl.BlockSpec(blk, lambda bi, i: (bi, vslab, prev(i), 0)),
            pl.BlockSpec((1, 4 * N_DIL_KEYS, 2 * N_DIL_KEYS), lambda bi, i: (g, 0, 0)),
        ],
        out_specs=(pl.BlockSpec(blk, lambda bi, i: (bi, 0, i, 0)),
                   pl.BlockSpec(blk, lambda bi, i: (bi, 0, i, 0))),
        scratch_shapes=[pltpu.VMEM((HEADS_PER_GROUP * N_DIL_KEYS, 2 * N_DIL_KEYS), F32)],
        compiler_params=pltpu.CompilerParams(
            dimension_semantics=("parallel", "parallel"),
            vmem_limit_bytes=_vmem_limit(blocks, 8 << 20)),
        name=f"attn_d{d}",
    )(qkv, qkv, qkv, qkv, qkv, bias_tab)


def _attn_step_kernel(qkv_ref, c0_ref, c1_ref, c2_ref, b0_ref, b1_ref, b2_ref, o_ref):
    outs = [[None] * N_ATTN_GROUPS for _ in range(HEADS_PER_GROUP)]
    lses = [[None] * N_ATTN_GROUPS for _ in range(HEADS_PER_GROUP)]
    for g, (c_ref, b_ref) in enumerate(((c0_ref, b0_ref), (c1_ref, b1_ref), (c2_ref, b2_ref))):
        for h in range(HEADS_PER_GROUP):
            q = qkv_ref[0, g, h] * ATTN_SCALE
            kn = qkv_ref[0, N_ATTN_GROUPS + g, h]
            vn = qkv_ref[0, 2 * N_ATTN_GROUPS + g, h]
            s_c = jnp.sum(c_ref[0, 0, 0, h] * q, axis=0, keepdims=True) + b_ref[h:h + 1, :]
            s_n = jnp.sum(kn * q, axis=0, keepdims=True) + b_ref[HEADS_PER_GROUP + h:HEADS_PER_GROUP + h + 1, 0:1]
            m = jnp.maximum(jnp.max(s_c, axis=-1, keepdims=True), s_n)
            p_c = jnp.exp(s_c - m)
            p_n = jnp.exp(s_n - m)
            l = jnp.sum(p_c, axis=-1, keepdims=True) + p_n
            inv = 1.0 / l
            outs[h][g] = jnp.sum(c_ref[0, 0, 1, h] * (p_c * inv), axis=-1, keepdims=True) + (p_n * inv) * vn
            lses[h][g] = m + jnp.log(l)
    for h in range(HEADS_PER_GROUP):
        ls, os_ = lses[h], outs[h]
        mx = jnp.maximum(jnp.maximum(ls[0], ls[1]), ls[2])
        w = [jnp.exp(x - mx) for x in ls]
        tot = w[0] + w[1] + w[2]
        o_ref[0, h] = (w[0] * os_[0] + w[1] * os_[1] + w[2] * os_[2]) * (1.0 / tot)


def _attention_step(qkv, caches, biases, l, nb):
    cache_specs = [pl.BlockSpec((1, 1) + c.shape[2:], lambda bi: (l, bi, 0, 0, 0, 0)) for c in caches]
    bias_specs = [pl.BlockSpec(b.shape, lambda bi: (0, 0)) for b in biases]
    blocks = (sum(_nbytes(c.shape[2:], F32) for c in caches) + sum(_nbytes(b.shape, F32) for b in biases)
              + 13 * HEADS_PER_GROUP * HEAD_DIM * LANES * 4)
    return pl.pallas_call(
        _attn_step_kernel,
        out_shape=jax.ShapeDtypeStruct((nb, HEADS_PER_GROUP, HEAD_DIM, 1), F32),
        grid=(nb,),
        in_specs=[pl.BlockSpec((1,) + qkv.shape[1:], lambda bi: (bi, 0, 0, 0, 0))] + cache_specs + bias_specs,
        out_specs=pl.BlockSpec((1, HEADS_PER_GROUP, HEAD_DIM, 1), lambda bi: (bi, 0, 0, 0)),
        compiler_params=pltpu.CompilerParams(
            dimension_semantics=("parallel",),
            vmem_limit_bytes=_vmem_limit(blocks, 8 << 20)),
        name="attn_step",
    )(qkv, *caches, *biases)


def _branch_kernel(*refs, tm, step):
    if step:
        (act_ref, pst_ref, cst_ref, h_ref, wga_ref, wgb_ref, wgc_ref, at_ref,
         wgrp_ref, pscale_ref, wpb_ref, cw_ref, wcb_ref, wab_ref,
         out_ref, u_ref, pzs_ref, cb_ref, comb_ref) = refs
    else:
        (act_ref, hist_ref, h_ref, wga_ref, wgb_ref, wgc_ref, o0_ref, o1_ref, o2_ref, l0_ref, l1_ref, l2_ref,
         wgrp_ref, pscale_ref, wpb_ref, cw_ref, wcb_ref, wab_ref,
         out_ref, u_ref, pzs_ref, cb_ref, comb_ref, pe_ref, ue_ref) = refs
    i = pl.program_id(1)
    c_tile = pl.program_id(2)

    def sequence_mixers():
        p = act_ref[0, :, 0:POOL_WIDTH]
        gate_b = act_ref[0, :, POOL_WIDTH:POOL_WIDTH + CONV_CH]
        u = act_ref[0, :, POOL_WIDTH + CONV_CH:POOL_WIDTH + 2 * CONV_CH] * \
            act_ref[0, :, POOL_WIDTH + 2 * CONV_CH:POOL_WIDTH + 3 * CONV_CH]
        cw = cw_ref[0]
        if step:
            acc = p
            sums = {}
            for k in range(1, max(POOL_WINDOWS)):
                acc = acc + pst_ref[0, POOL_STATE - k]
                sums[k + 1] = acc
            means = [sums[w][:, gi * POOL_GROUP:(gi + 1) * POOL_GROUP] * (1.0 / w)
                     for gi, w in enumerate(POOL_WINDOWS)]
            conv = cst_ref[0, 0] * cw[0:1] + cst_ref[0, 1] * cw[1:2] + u * cw[2:3]
            u_ref[0, 0] = u
            comb_ref[...] = at_ref[0].astype(BF16)
        else:
            first = i == 0
            hist_p = hist_ref[0, :, 0:POOL_WIDTH]
            hist_u = hist_ref[0, :, POOL_WIDTH + CONV_CH:POOL_WIDTH + 2 * CONV_CH] * \
                hist_ref[0, :, POOL_WIDTH + 2 * CONV_CH:POOL_WIDTH + 3 * CONV_CH]
            pe_ref[0:HIST] = jnp.where(first, 0.0, hist_p)
            ue_ref[0:HIST] = jnp.where(first, 0.0, hist_u)
            pe_ref[HIST:HIST + tm] = p
            ue_ref[HIST:HIST + tm] = u
            t = i * tm + lax.broadcasted_iota(jnp.int32, (tm, 1), 0)
            means = []
            for gi, w in enumerate(POOL_WINDOWS):
                cs = slice(gi * POOL_GROUP, (gi + 1) * POOL_GROUP)
                acc = pe_ref[HIST:HIST + tm, cs]
                for k in range(1, w):
                    acc = acc + pe_ref[HIST - k:HIST - k + tm, cs]
                cnt = jnp.minimum(t + 1, w).astype(F32)
                means.append(acc * (1.0 / cnt))
            conv = (ue_ref[HIST - 2:HIST - 2 + tm] * cw[0:1] + ue_ref[HIST - 1:HIST - 1 + tm] * cw[1:2]
                    + u * cw[2:3])
            u_ref[0, 0] = ue_ref[HIST + tm - 8:HIST + tm]
            for sl in range(2):
                ls = [r[0, sl] for r in (l0_ref, l1_ref, l2_ref)]
                os_ = [r[0, sl] for r in (o0_ref, o1_ref, o2_ref)]
                mx = jnp.maximum(jnp.maximum(ls[0], ls[1]), ls[2])
                w_ = [jnp.exp(x - mx) for x in ls]
                tot = w_[0] + w_[1] + w_[2]
                comb = (w_[0] * os_[0] + w_[1] * os_[1] + w_[2] * os_[2]) * (1.0 / tot)
                comb_ref[:, sl * LANES:(sl + 1) * LANES] = comb.astype(BF16)
        pscale = pscale_ref[0]
        for gi in range(len(POOL_WINDOWS)):
            cs = slice(gi * POOL_GROUP, (gi + 1) * POOL_GROUP)
            pm = (means[gi] - p[:, cs]).astype(BF16)
            pz = _dot(pm, wgrp_ref[0, gi])
            pzs_ref[:, cs] = (pz * pscale[:, cs]).astype(BF16)
        cb_ref[...] = (gate_b * conv).astype(BF16)

    def merge_tile():
        h = h_ref[0]
        merged = _sigmoid(_dot(h, wga_ref[0])) * _dot(pzs_ref[...], wpb_ref[0])
        merged += _sigmoid(_dot(h, wgb_ref[0])) * _dot(cb_ref[...], wcb_ref[0])
        merged += _sigmoid(_dot(h, wgc_ref[0])) * _dot(comb_ref[...], wab_ref[0])
        out_ref[0] = merged.astype(BF16)

    @pl.when(c_tile == 0)
    def _():
        sequence_mixers()
        merge_tile()

    @pl.when(c_tile > 0)
    def _():
        merge_tile()


def _branches(act, h, attn, states, weights, l, *, tm, tc, step):
    b, s, _ = act.shape
    wgrp, pscale, wpb, cw, wcb, wab, wgate = weights
    ni = s // tm
    gate_specs = [pl.BlockSpec((1, tm, D_MODEL), lambda bi, i, c: (bi, i, 0))] + [
        pl.BlockSpec((1, D_MODEL, tc), lambda bi, i, c, k=k: (l, 0, k * (D_MODEL // tc) + c)) for k in range(3)]
    w_specs = [
        pl.BlockSpec((1,) + wgrp.shape[1:], lambda bi, i, c: (l, 0, 0, 0)),
        pl.BlockSpec((1,) + pscale.shape[1:], lambda bi, i, c: (l, 0, 0)),
        pl.BlockSpec((1, POOL_WIDTH, tc), lambda bi, i, c: (l, 0, c)),
        pl.BlockSpec((1,) + cw.shape[1:], lambda bi, i, c: (l, 0, 0)),
        pl.BlockSpec((1, CONV_CH, tc), lambda bi, i, c: (l, 0, c)),
        pl.BlockSpec((1, GROUP_WIDTH, tc), lambda bi, i, c: (l, 0, c)),
    ]
    act_spec = pl.BlockSpec((1, tm, ACT_WIDTH), lambda bi, i, c: (bi, i, 0))
    scratch = [pltpu.VMEM((tm, POOL_WIDTH), BF16), pltpu.VMEM((tm, CONV_CH), BF16),
               pltpu.VMEM((tm, GROUP_WIDTH), BF16)]
    blocks = (_nbytes((tm, ACT_WIDTH), F32) + _nbytes((tm + 3 * tc, D_MODEL), BF16) + _nbytes(wgrp.shape[1:], BF16)
              + _nbytes((POOL_WIDTH + CONV_CH + GROUP_WIDTH, tc), BF16) + _nbytes((tm, tc), BF16))
    if step:
        pst, cst = states
        ins = [act, pst, cst, h, wgate, wgate, wgate, attn]
        in_specs = [act_spec,
                    pl.BlockSpec((1,) + pst.shape[1:], lambda bi, i, c: (l, 0, 0, 0)),
                    pl.BlockSpec((1,) + cst.shape[1:], lambda bi, i, c: (l, 0, 0, 0))] + gate_specs + [
                    pl.BlockSpec((1, tm, GROUP_WIDTH), lambda bi, i, c: (bi, i, 0))]
        urows = tm
        blocks += _nbytes(pst.shape[1:], F32) + _nbytes(cst.shape[1:], F32)
    else:
        hblk = tm // HIST
        slab = pl.BlockSpec((1, 2, tm, LANES), lambda bi, i, c: (bi, 0, i, 0))
        ins = [act, act, h, wgate, wgate, wgate] + list(attn)
        in_specs = [act_spec,
                    pl.BlockSpec((1, HIST, ACT_WIDTH), lambda bi, i, c: (bi, jnp.maximum(i * hblk - 1, 0), 0))
                    ] + gate_specs + [slab] * 6
        scratch += [pltpu.VMEM((tm + HIST, POOL_WIDTH), F32), pltpu.VMEM((tm + HIST, CONV_CH), F32)]
        urows = 8
        blocks += 6 * _nbytes((2, tm, LANES), F32) + _nbytes((HIST, ACT_WIDTH), F32)
    scratch_bytes = 3 * _nbytes((tm + HIST, POOL_WIDTH), F32) + 8 * _nbytes((tm, tc), F32)
    return pl.pallas_call(
        functools.partial(_branch_kernel, tm=tm, step=step),
        out_shape=(jax.ShapeDtypeStruct((b, s, D_MODEL), BF16),
                   jax.ShapeDtypeStruct((b, ni, urows, CONV_CH), F32)),
        grid=(b, ni, D_MODEL // tc),
        in_specs=in_specs + w_specs,
        out_specs=(pl.BlockSpec((1, tm, tc), lambda bi, i, c: (bi, i, c)),
                   pl.BlockSpec((1, 1, urows, CONV_CH), lambda bi, i, c: (bi, i, 0, 0))),
        scratch_shapes=scratch,
        compiler_params=pltpu.CompilerParams(
            dimension_semantics=("parallel", "parallel", "arbitrary"),
            vmem_limit_bytes=_vmem_limit(blocks, scratch_bytes)),
        name="branches_step" if step else "branches",
    )(*ins, wgrp, pscale, wpb, cw, wcb, wab)


def _oproj_kernel(m_ref, w_ref, x_ref, g_ref, gate_ref, o_ref):
    mix = _dot(m_ref[0], w_ref[0])
    o_ref[0] = x_ref[0] + _tail2(gate_ref) * _rms(mix, g_ref[0])


def _oproj(merged, w_o, x, norm, mod, l, *, tm):
    b, s, d = x.shape
    blocks = (_nbytes((tm, d), BF16) + _nbytes((d, d), BF16) + 2 * _nbytes((tm, d), F32)
              + _nbytes((_mod_rows(mod) + 1, d), F32))
    return pl.pallas_call(
        _oproj_kernel,
        out_shape=jax.ShapeDtypeStruct((b, s, d), F32),
        grid=(b, s // tm),
        in_specs=[
            pl.BlockSpec((1, tm, d), lambda bi, i: (bi, i, 0)),
            pl.BlockSpec((1, d, d), lambda bi, i: (l, 0, 0)),
            pl.BlockSpec((1, tm, d), lambda bi, i: (bi, i, 0)),
            norm[1](1), mod[1](2),
        ],
        out_specs=pl.BlockSpec((1, tm, d), lambda bi, i: (bi, i, 0)),
        compiler_params=pltpu.CompilerParams(
            dimension_semantics=("parallel", "parallel"),
            vmem_limit_bytes=_vmem_limit(blocks, _nbytes((tm, d), F32))),
        name="oproj",
    )(merged, w_o, x, norm[0], mod[0])


def _ffn_kernel(*refs, tm, step, nchunk):
    if step:
        (x_ref, g2_ref, sc_ref, sh_ref, wg_ref, wv_ref, cwg_ref, cwv_ref, wd_ref, g3_ref, gate_ref,
         stg_ref, stv_ref, o_ref, tg_ref, tv_ref, h_ref) = refs
        hist = 0
    else:
        (x_ref, xp_ref, g2_ref, sc_ref, sh_ref, wg_ref, wv_ref, cwg_ref, cwv_ref, wd_ref, g3_ref, gate_ref,
         o_ref, tg_ref, tv_ref, h_ref, ug_ref, uv_ref) = refs
        hist = HIST
    i = pl.program_id(1)
    j = pl.program_id(2)

    @pl.when(j == 0)
    def _():
        if not step:
            _modulate_rows(xp_ref, g2_ref, sc_ref, sh_ref, h_ref, 0, HIST)
        _modulate_rows(x_ref, g2_ref, sc_ref, sh_ref, h_ref, hist, tm)
        o_ref[...] = jnp.zeros_like(o_ref)

    cwg = cwg_ref[0]
    cwv = cwv_ref[0]
    if step:
        h = h_ref[...]
        up_g = _dot(h, wg_ref[0])
        up_v = _dot(h, wv_ref[0])
        uc_g = stg_ref[0, 0] * cwg[0:1] + stg_ref[0, 1] * cwg[1:2] + up_g * cwg[2:3]
        uc_v = stv_ref[0, 0] * cwv[0:1] + stv_ref[0, 1] * cwv[1:2] + up_v * cwv[2:3]
        tg_ref[0, 0] = up_g
        tv_ref[0, 0] = up_v
        o_ref[0] += _dot((_gelu_tanh(uc_g) * uc_v).astype(BF16), wd_ref[0])
    else:
        first = i == 0
        rc = tm // nchunk
        for c in range(nchunk):
            lo = 0 if c == 0 else HIST + c * rc
            hi = HIST + (c + 1) * rc
            h = h_ref[lo:hi]
            for u_ref, w_ref in ((ug_ref, wg_ref), (uv_ref, wv_ref)):
                up = _dot(h, w_ref[0])
                if c == 0:
                    u_ref[0:HIST] = jnp.where(first, 0.0, up[0:HIST])
                    u_ref[HIST:hi] = up[HIST:]
                else:
                    u_ref[lo:hi] = up

            def conv(ref, cw):
                base = HIST + c * rc
                return (ref[base - 2:base - 2 + rc] * cw[0:1] + ref[base - 1:base - 1 + rc] * cw[1:2]
                        + ref[base:base + rc] * cw[2:3])
            act = (_gelu_tanh(conv(ug_ref, cwg)) * conv(uv_ref, cwv)).astype(BF16)
            o_ref[0, c * rc:(c + 1) * rc] += _dot(act, wd_ref[0])
        tg_ref[0, 0] = ug_ref[HIST + tm - 8:HIST + tm]
        tv_ref[0, 0] = uv_ref[HIST + tm - 8:HIST + tm]

    @pl.when(j == pl.num_programs(2) - 1)
    def _():
        o_ref[0] = x_ref[0] + _tail2(gate_ref) * _rms(o_ref[0], g3_ref[0])


def _ffn(x, norm, mod, w_up, cw, w_down, state, l, *, tm, tf, step):
    b, s, d = x.shape
    ni, nj = s // tm, D_FF // tf
    trows = tm if step else 8
    w_specs = [
        pl.BlockSpec((1, d, tf), lambda bi, i, j: (l, 0, j)),
        pl.BlockSpec((1, d, tf), lambda bi, i, j: (l, 0, nj + j)),
        pl.BlockSpec((1, FFN_K, tf), lambda bi, i, j: (l, 0, j)),
        pl.BlockSpec((1, FFN_K, tf), lambda bi, i, j: (l, 0, nj + j)),
        pl.BlockSpec((1, tf, d), lambda bi, i, j: (l, j, 0)),
    ]
    x_spec = pl.BlockSpec((1, tm, d), lambda bi, i, j: (bi, i, 0), pipeline_mode=pl.Buffered(1))
    hist = 0 if step else HIST
    scratch = [pltpu.VMEM((tm + hist, d), BF16)]
    blocks = (2 * _nbytes((tm, d), F32) + 3 * _nbytes((d, tf), BF16) + 2 * _nbytes((trows, tf), F32)
              + _nbytes((3 * _mod_rows(mod) + 2, d), F32))
    mods = [norm[1](2), mod[1](4), mod[1](3)]
    tailp = [norm[1](3), mod[1](5)]
    if step:
        ins = [x, norm[0], mod[0], mod[0], w_up, w_up, cw, cw, w_down, norm[0], mod[0], state, state]
        in_specs = [x_spec] + mods + w_specs + tailp + [
            pl.BlockSpec((1, FFN_K - 1, tm, tf), lambda bi, i, j: (l, 0, 0, j)),
            pl.BlockSpec((1, FFN_K - 1, tm, tf), lambda bi, i, j: (l, 0, 0, nj + j))]
        blocks += 2 * _nbytes((FFN_K - 1, tm, tf), F32)
    else:
        hblk = tm // HIST
        ins = [x, x, norm[0], mod[0], mod[0], w_up, w_up, cw, cw, w_down, norm[0], mod[0]]
        in_specs = [x_spec,
                    pl.BlockSpec((1, HIST, d), lambda bi, i, j: (bi, jnp.maximum(i * hblk - 1, 0), 0))
                    ] + mods + w_specs + tailp
        scratch += [pltpu.VMEM((tm + HIST, tf), F32), pltpu.VMEM((tm + HIST, tf), F32)]
        blocks += _nbytes((HIST, d), F32)
    scratch_bytes = _nbytes((tm + hist, d), BF16) + 6 * _nbytes((tm + hist, tf), F32)
    tail = jax.ShapeDtypeStruct((b, ni, trows, D_FF), F32)
    tail_spec = lambda: pl.BlockSpec((1, 1, trows, tf), lambda bi, i, j: (bi, i, 0, j))
    return pl.pallas_call(
        functools.partial(_ffn_kernel, tm=tm, step=step, nchunk=FFN_ROW_CHUNKS),
        out_shape=(jax.ShapeDtypeStruct((b, s, d), F32), tail, tail),
        grid=(b, ni, nj),
        in_specs=in_specs,
        out_specs=(pl.BlockSpec((1, tm, d), lambda bi, i, j: (bi, i, 0)), tail_spec(), tail_spec()),
        scratch_shapes=scratch,
        compiler_params=pltpu.CompilerParams(
            dimension_semantics=("parallel", "parallel", "arbitrary"),
            vmem_limit_bytes=_vmem_limit(blocks, scratch_bytes)),
        name="ffn_step" if step else "ffn",
    )(*ins)


def _group_rel_bias(rel_bias):
    n = np.arange(N_DIL_KEYS + 1)
    max_exact = N_BUCKETS // 2
    out = []
    for g, (_, d) in enumerate(ATTN_GROUPS):
        dist = n * d
        large = max_exact + (np.log(np.maximum(dist, 1) / max_exact) / np.log(MAX_DISTANCE / max_exact)
                             * (N_BUCKETS - max_exact)).astype(np.int32)
        bucket = np.where(dist < max_exact, dist, np.minimum(large, N_BUCKETS - 1))
        out.append(rel_bias[bucket, g * HEADS_PER_GROUP:(g + 1) * HEADS_PER_GROUP].T)
    return jnp.stack(out)


def _bias_tables(rel_bias):
    bias_g = _group_rel_bias(rel_bias).astype(F32)
    n = N_DIL_KEYS
    gh = (N_ATTN_GROUPS, HEADS_PER_GROUP)
    ext = jnp.concatenate([bias_g[:, :, ::-1], jnp.full(gh + (n,), NEG_INF, F32)], axis=-1)
    tab = jnp.tile(ext, (1, 1, n))[:, :, :n * 2 * n].reshape(N_ATTN_GROUPS, HEADS_PER_GROUP * n, 2 * n)
    step = []
    for g, (_, d) in enumerate(ATTN_GROUPS):
        hit = bias_g[g, :, :0:-1, None]
        row = jnp.concatenate([hit, jnp.full((HEADS_PER_GROUP, n, d - 1), NEG_INF, F32)], axis=-1)
        new = jnp.broadcast_to(bias_g[g, :, 0:1], (HEADS_PER_GROUP, n * d))
        step.append(jnp.concatenate([row.reshape(HEADS_PER_GROUP, n * d), new], axis=0))
    return tab, step


def _kv_pack(qkv, g, nrows):
    b = qkv.shape[0]
    k0 = ATTN_WIDTH + g * GROUP_WIDTH
    v0 = 2 * ATTN_WIDTH + g * GROUP_WIDTH
    k = qkv[:, -nrows:, k0:k0 + GROUP_WIDTH].reshape(b, nrows, HEADS_PER_GROUP, HEAD_DIM)
    v = qkv[:, -nrows:, v0:v0 + GROUP_WIDTH].reshape(b, nrows, HEADS_PER_GROUP, HEAD_DIM)
    return jnp.stack([k, v], axis=2)


def _prompt_layer(x, l, P):
    b, s, _ = x.shape
    norm, mod = P["norm"](l), P["mod_p"](l)
    tm = 512
    qkv, qkv_slab, h = _proj(x, norm, mod, P["w_qkv"], l, tm=1024, tn=ATTN_WIDTH)
    act = _matmul(h, P["w_act"], l, tm=1024, tn=1024)
    ols = [_attention(qkv_slab, P["bias_tab"], g) for g in range(N_ATTN_GROUPS)]
    attn = [o for o, _ in ols] + [ls for _, ls in ols]
    merged, u_tail = _branches(act, h, attn, None, P["branch"], l, tm=tm, tc=512, step=False)
    x1 = _oproj(merged, P["w_o"], x, norm, mod, l, tm=tm)
    x2, tail_g, tail_v = _ffn(x1, norm, mod, P["w_up"], P["ffn_cw"], P["w_down"], None, l,
                              tm=1024, tf=512, step=False)
    new_kv = [_kv_pack(qkv, g, min(w, s)) for g, (w, _) in enumerate(ATTN_GROUPS)]
    new_pool = act[:, -POOL_STATE:, :POOL_WIDTH]
    new_conv = u_tail[:, -1, -(CONV_K - 1):]
    new_ffn = jnp.concatenate([tail_g[:, -1, -(FFN_K - 1):], tail_v[:, -1, -(FFN_K - 1):]], axis=-1)
    return x2, (new_kv[0], new_kv[1], new_kv[2], new_pool, new_conv, new_ffn)


def _pad_rows(a, axis):
    pad = [(0, 0)] * a.ndim
    pad[axis] = (0, SAMPLE_ROWS - a.shape[axis])
    return jnp.pad(a, pad)


def _sample_layer(x, l, P, S, nb):
    norm, mod = P["norm"](l), P["mod_s"](l)
    tm = SAMPLE_ROWS
    qkv, _, h = _proj(x, norm, mod, P["w_qkv"], l, tm=tm, tn=ATTN_WIDTH)
    act = _matmul(h, P["w_act"], l, tm=tm, tn=1024)
    heads = qkv[0, :nb].reshape(nb, 3 * N_ATTN_GROUPS, HEADS_PER_GROUP, HEAD_DIM, 1)
    attn = _attention_step(heads, S["caches"], P["bias_step"], l, nb)
    attn = _pad_rows(attn.reshape(1, nb, GROUP_WIDTH), 1)
    merged, u = _branches(act, h, attn, (S["pool_t"], S["conv_t"]), P["branch"], l, tm=tm, tc=512, step=True)
    x1 = _oproj(merged, P["w_o"], x, norm, mod, l, tm=tm)
    x2, up_g, up_v = _ffn(x1, norm, mod, P["w_up"], P["ffn_cw"], P["w_down"], S["ffn_t"], l,
                          tm=tm, tf=512, step=True)
    new_kv = [_kv_pack(qkv[0, :nb, None], g, 1) for g in range(N_ATTN_GROUPS)]
    new_pool = jnp.concatenate([S["pool"][l][:, 1:], act[0, :nb, None, :POOL_WIDTH]], axis=1)
    new_conv = jnp.concatenate([S["conv"][l][:, 1:], u[0, 0, :nb, None]], axis=1)
    up_new = jnp.concatenate([up_g[0, 0, :nb], up_v[0, 0, :nb]], axis=-1)
    new_ffn = jnp.concatenate([S["ffn"][l][:, 1:], up_new[:, None]], axis=1)
    return x2, (new_kv[0], new_kv[1], new_kv[2], new_pool, new_conv, new_ffn)


def kernel(x_prompt, x_sample, c_prompt, c_sample, cache_kv_w128, cache_kv_w512, cache_kv_w2048, state_pool, state_conv, state_ffn_conv, rel_bias, norm_g, w_ada, b_ada, w_in, w_attn_br, w_pool_grp, pool_scale, w_pool_br, conv_w, w_conv_br, w_o, w_up, ffn_conv_w, w_down):
    nbp = x_prompt.shape[0]
    nbs, tdec, _ = x_sample.shape
    assert tdec == 1 and nbs <= SAMPLE_ROWS
    caches = (cache_kv_w128, cache_kv_w512, cache_kv_w2048)
    for (w, d), c in zip(ATTN_GROUPS, caches):
        assert c.shape[2] == w == N_DIL_KEYS * d, "cache must hold exactly one window"

    bias_tab, bias_step = _bias_tables(rel_bias)
    c_rows = -(-(SAMPLE_ROWS + nbp) // 8) * 8
    c_all = jnp.zeros((c_rows, D_MODEL), F32).at[:nbs].set(c_sample).at[SAMPLE_ROWS:SAMPLE_ROWS + nbp].set(c_prompt)
    mod_all = _ada(c_all, w_ada, b_ada)

    P = dict(
        norm=functools.partial(_norm_view, norm_g),
        mod_p=functools.partial(_mod_view, mod_all, prompt=True),
        mod_s=functools.partial(_mod_view, mod_all, prompt=False),
        w_qkv=w_in[:, :, :QKV_WIDTH].astype(BF16),
        w_act=w_in[:, :, QKV_WIDTH:QKV_WIDTH + ACT_WIDTH].astype(BF16),
        branch=(w_pool_grp.astype(BF16), pool_scale[:, None, :], w_pool_br.astype(BF16),
                conv_w, w_conv_br.astype(BF16), w_attn_br.astype(BF16),
                w_in[:, :, QKV_WIDTH + ACT_WIDTH:].astype(BF16)),
        w_o=w_o.astype(BF16), w_up=w_up.astype(BF16), ffn_cw=ffn_conv_w, w_down=w_down.astype(BF16),
        bias_tab=bias_tab, bias_step=bias_step,
    )
    S = dict(
        caches=[c.transpose(0, 1, 3, 4, 5, 2) for c in caches],
        pool=state_pool, conv=state_conv, ffn=state_ffn_conv,
        pool_t=_pad_rows(state_pool.transpose(0, 2, 1, 3), 2),
        conv_t=_pad_rows(state_conv.transpose(0, 2, 1, 3), 2),
        ffn_t=_pad_rows(state_ffn_conv.transpose(0, 2, 1, 3), 2),
    )
    yp = x_prompt
    ys = _pad_rows(x_sample.reshape(1, nbs, D_MODEL), 1)
    st_p, st_s = [], []
    for l in range(DEPTH):
        yp, sp = _prompt_layer(yp, l, P)
        ys, ss = _sample_layer(ys, l, P, S, nbs)
        st_p.append(sp)
        st_s.append(ss)
    outs_p = [jnp.stack([s[k] for s in st_p]) for k in range(6)]
    outs_s = [jnp.stack([s[k] for s in st_s]) for k in range(6)]
    return (yp, ys[0, :nbs, None, :], *outs_p, *outs_s)
```

```python
import functools

import numpy as np
import jax
import jax.numpy as jnp
from jax import lax
from jax.experimental import pallas as pl
from jax.experimental.pallas import tpu as pltpu

F32 = jnp.float32
BF16 = jnp.bfloat16

D_MODEL = 2048
DEPTH = 2
HEAD_DIM = 64
HEADS_PER_GROUP = 4
ATTN_GROUPS = ((128, 1), (512, 4), (2048, 16))
N_ATTN_GROUPS = len(ATTN_GROUPS)
ATTN_WIDTH = N_ATTN_GROUPS * HEADS_PER_GROUP * HEAD_DIM
GROUP_WIDTH = HEADS_PER_GROUP * HEAD_DIM
N_DIL_KEYS = 128
N_BUCKETS = 32
MAX_DISTANCE = 2048
ATTN_SCALE = HEAD_DIM ** -0.5
POOL_WINDOWS = (2, 4, 8, 16)
POOL_GROUP = 128
POOL_WIDTH = 512
POOL_STATE = 15
CONV_CH = 512
CONV_K = 3
D_FF = 5632
FFN_K = 3
N_MOD = 6
N_NORM = 4
EPS = 1e-6
NEG_INF = -1e30

LANES = 128
HIST = 16
SAMPLE_ROWS = 16
QKV_WIDTH = 3 * ATTN_WIDTH
QKV_SLABS = QKV_WIDTH // LANES
ACT_WIDTH = POOL_WIDTH + 3 * CONV_CH
ATTN_CLASS_UNROLL = 4
ROW_CHUNKS = 2
VMEM_CAP = 56 * 1024 * 1024


def _vmem_limit(block_bytes, scratch_bytes=0):
    est = 2 * block_bytes + scratch_bytes
    return int(min(VMEM_CAP, est + est // 4 + (4 << 20)))


def _nbytes(shape, dtype):
    return int(np.prod(shape)) * jnp.dtype(dtype).itemsize


def _rms(x, g):
    return x * lax.rsqrt(jnp.mean(x * x, axis=-1, keepdims=True) + EPS) * g


def _sigmoid(x):
    return 0.5 + 0.5 * jnp.tanh(0.5 * x)


def _gelu_tanh(x):
    return 0.5 * x * (1.0 + jnp.tanh(np.sqrt(2.0 / np.pi) * (x + 0.044715 * (x * x * x))))


def _dot(a, b):
    return jnp.dot(a, b, preferred_element_type=F32)


def _tail2(ref):
    return ref[(0,) * (len(ref.shape) - 2)]


def _norm_view(norm_g, l):
    arr = norm_g.reshape(DEPTH * N_NORM, 1, D_MODEL)
    return arr, lambda k: pl.BlockSpec((1, 1, D_MODEL), lambda *_: (l * N_NORM + k, 0, 0))


def _mod_view(mod_all, l, prompt):
    if prompt:
        arr = mod_all.reshape(DEPTH, mod_all.shape[1], N_MOD, 1, D_MODEL)
        return arr, lambda k: pl.BlockSpec((1, 1, 1, 1, D_MODEL), lambda bi, *_: (l, SAMPLE_ROWS + bi, k, 0, 0))
    return mod_all, lambda k: pl.BlockSpec((1, SAMPLE_ROWS, D_MODEL), lambda bi, *_: (l, 0, k))


def _mod_rows(mod):
    return mod[1](0).block_shape[-2]


def _ada_kernel(c_ref, w_ref, b_ref, o_ref):
    c = c_ref[...]
    s = (c * _sigmoid(c)).astype(BF16)
    o_ref[0] = _dot(s, w_ref[0].astype(BF16)) + b_ref[0]


def _ada(c_all, w_ada, b_ada):
    rows = c_all.shape[0]
    n = w_ada.shape[-1]
    tn = 1024
    blocks = _nbytes((rows, D_MODEL), F32) + _nbytes((D_MODEL, tn), F32) + _nbytes((rows + 1, tn), F32)
    return pl.pallas_call(
        _ada_kernel,
        out_shape=jax.ShapeDtypeStruct((DEPTH, rows, n), F32),
        grid=(DEPTH, n // tn),
        in_specs=[
            pl.BlockSpec((rows, D_MODEL), lambda l, j: (0, 0)),
            pl.BlockSpec((1, D_MODEL, tn), lambda l, j: (l, 0, j)),
            pl.BlockSpec((1, 1, tn), lambda l, j: (l, 0, j)),
        ],
        out_specs=pl.BlockSpec((1, rows, tn), lambda l, j: (l, 0, j)),
        compiler_params=pltpu.CompilerParams(
            dimension_semantics=("parallel", "parallel"),
            vmem_limit_bytes=_vmem_limit(blocks, _nbytes((D_MODEL, tn), BF16))),
        name="ada",
    )(c_all, w_ada, b_ada.reshape(DEPTH, 1, n))


def _modulate_rows(x_ref, g_ref, sc_ref, sh_ref, h_ref, row0, tm):
    y = _rms(x_ref[0], g_ref[0])
    h_ref[row0:row0 + tm] = (y * (1.0 + _tail2(sc_ref)) + _tail2(sh_ref)).astype(BF16)


def _proj_kernel(x_ref, g_ref, sc_ref, sh_ref, w_ref, o_ref, slab_ref, h_ref, *, slabs, tm, nchunk):
    def project(r):
        res = _dot(h_ref[0, r], w_ref[0])
        o_ref[0, r] = res
        for s in range(slabs):
            slab_ref[0, s, r] = res[:, s * LANES:(s + 1) * LANES]

    @pl.when(pl.program_id(2) == 0)
    def _():
        rc = tm // nchunk
        for c in range(nchunk):
            r = slice(c * rc, (c + 1) * rc)
            y = _rms(x_ref[0, r], g_ref[0])
            h_ref[0, r] = (y * (1.0 + _tail2(sc_ref)) + _tail2(sh_ref)).astype(BF16)
            project(r)

    @pl.when(pl.program_id(2) > 0)
    def _():
        project(slice(None))


def _proj(x, norm, mod, w, l, *, tm, tn):
    b, s, d = x.shape
    n = w.shape[2]
    slabs = tn // LANES
    r = _mod_rows(mod)
    blocks = (_nbytes((tm, d), F32) + _nbytes((2 * r + 1, d), F32) + _nbytes((d, tn), BF16)
              + 2 * _nbytes((tm, tn), F32) + _nbytes((tm, d), BF16))
    return pl.pallas_call(
        functools.partial(_proj_kernel, slabs=slabs, tm=tm, nchunk=2 * ROW_CHUNKS if r == 1 else 1),
        out_shape=[jax.ShapeDtypeStruct((b, s, n), F32),
                   jax.ShapeDtypeStruct((b, n // LANES, s, LANES), F32),
                   jax.ShapeDtypeStruct((b, s, d), BF16)],
        grid=(b, s // tm, n // tn),
        in_specs=[
            pl.BlockSpec((1, tm, d), lambda bi, i, j: (bi, i, 0)),
            norm[1](0), mod[1](1), mod[1](0),
            pl.BlockSpec((1, d, tn), lambda bi, i, j: (l, 0, j)),
        ],
        out_specs=[pl.BlockSpec((1, tm, tn), lambda bi, i, j: (bi, i, j)),
                   pl.BlockSpec((1, slabs, tm, LANES), lambda bi, i, j: (bi, j, i, 0)),
                   pl.BlockSpec((1, tm, d), lambda bi, i, j: (bi, i, 0))],
        compiler_params=pltpu.CompilerParams(
            dimension_semantics=("parallel", "parallel", "arbitrary"),
            vmem_limit_bytes=_vmem_limit(blocks, _nbytes((tm, tn), F32) + _nbytes((tm, d), F32))),
        name="proj_norm",
    )(x, norm[0], mod[0], mod[0], w)


def _matmul_kernel(h_ref, w_ref, o_ref):
    o_ref[0] = _dot(h_ref[0], w_ref[0])


def _matmul(h, w, l, *, tm, tn):
    b, s, d = h.shape
    n = w.shape[2]
    blocks = _nbytes((tm, d), BF16) + _nbytes((d, tn), BF16) + _nbytes((tm, tn), F32)
    return pl.pallas_call(
        _matmul_kernel,
        out_shape=jax.ShapeDtypeStruct((b, s, n), F32),
        grid=(b, s // tm, n // tn),
        in_specs=[pl.BlockSpec((1, tm, d), lambda bi, i, j: (bi, i, 0)),
                  pl.BlockSpec((1, d, tn), lambda bi, i, j: (l, 0, j))],
        out_specs=pl.BlockSpec((1, tm, tn), lambda bi, i, j: (bi, i, j)),
        compiler_params=pltpu.CompilerParams(
            dimension_semantics=("parallel", "parallel", "parallel"),
            vmem_limit_bytes=_vmem_limit(blocks, _nbytes((tm, tn), F32))),
        name="proj_act",
    )(h, w)


def _head_masks(rows):
    lane = lax.broadcasted_iota(jnp.int32, (rows, GROUP_WIDTH), 1)
    return [(lane >= h * HEAD_DIM) & (lane < (h + 1) * HEAD_DIM) for h in range(HEADS_PER_GROUP)]


def _attn_kernel(q_ref, kc_ref, kp_ref, vc_ref, vp_ref, bias_ref, o_ref, lse_ref, edge_ref, *, d, sb):
    i = pl.program_id(1)
    nq = N_DIL_KEYS
    span = nq * d
    ncb = sb // span
    hm = _head_masks(1)
    col = lax.broadcasted_iota(jnp.int32, (1, 2 * nq), 1)
    edge_ref[...] = jnp.where((col < nq) & (i == 0), NEG_INF, bias_ref[0])

    def rows(start):
        return pl.ds(start, nq, stride=d) if d > 1 else pl.ds(start, nq)

    def load(ref, start):
        return jnp.concatenate([ref[0, s, rows(start), :] for s in range(2)], axis=1)

    def one_block(r, jb):
        qs = jb * span + r
        q = load(q_ref, qs) * ATTN_SCALE
        if jb == 0:
            lo = sb - span + r
            k_lo, v_lo = load(kp_ref, lo), load(vp_ref, lo)
        else:
            lo = (jb - 1) * span + r
            k_lo, v_lo = load(kc_ref, lo), load(vc_ref, lo)
        kcat = jnp.concatenate([k_lo, load(kc_ref, qs)], axis=0).astype(BF16)
        vcat = jnp.concatenate([v_lo, load(vc_ref, qs)], axis=0).astype(BF16)
        qm = jnp.concatenate([jnp.where(hm[h], q, 0.0) for h in range(HEADS_PER_GROUP)], axis=0).astype(BF16)
        s = lax.dot_general(qm, kcat, (((1,), (1,)), ((), ())), preferred_element_type=F32)
        s = s + (edge_ref[...] if jb == 0 else bias_ref[0])
        m = jnp.max(s, axis=-1, keepdims=True)
        p = jnp.exp(s - m)
        l = jnp.sum(p, axis=-1, keepdims=True)
        oall = _dot((p * (1.0 / l)).astype(BF16), vcat)
        lse = m + jnp.log(l)
        o = jnp.zeros((nq, GROUP_WIDTH), F32)
        ls = jnp.zeros((nq, GROUP_WIDTH), F32)
        for h in range(HEADS_PER_GROUP):
            o = jnp.where(hm[h], oall[h * nq:(h + 1) * nq], o)
            ls = jnp.where(hm[h], lse[h * nq:(h + 1) * nq], ls)
        for sl in range(2):
            o_ref[0, sl, rows(qs), :] = o[:, sl * LANES:(sl + 1) * LANES]
            lse_ref[0, sl, rows(qs), :] = ls[:, sl * LANES:(sl + 1) * LANES]

    if d == 1:
        for jb in range(ncb):
            one_block(0, jb)
    else:
        def body(r, carry):
            for jb in range(ncb):
                one_block(r, jb)
            return carry
        lax.fori_loop(0, d, body, 0, unroll=ATTN_CLASS_UNROLL)


def _attention(qkv, bias_tab, g):
    b, _, s, _ = qkv.shape
    d = ATTN_GROUPS[g][1]
    sb = max(N_DIL_KEYS * d, 512)
    blk = (1, 2, sb, LANES)
    kslab, vslab = ATTN_WIDTH // GROUP_WIDTH + g, 2 * ATTN_WIDTH // GROUP_WIDTH + g
    prev = lambda i: jnp.maximum(i - 1, 0)
    blocks = 7 * _nbytes(blk, F32) + _nbytes((4 * N_DIL_KEYS, 2 * N_DIL_KEYS), F32)
    out = jax.ShapeDtypeStruct((b, 2, s, LANES), F32)
    return pl.pallas_call(
        functools.partial(_attn_kernel, d=d, sb=sb),
        out_shape=(out, out),
        grid=(b, s // sb),
        in_specs=[
            pl.BlockSpec(blk, lambda bi, i: (bi, g, i, 0)),
            pl.BlockSpec(blk, lambda bi, i: (bi, kslab, i, 0)),
            pl.BlockSpec(blk, lambda bi, i: (bi, kslab, prev(i), 0)),
            pl.BlockSpec(blk, lambda bi, i: (bi, vslab, i, 0)),
            pl.BlockSpec(blk, lambda bi, i: (bi, vslab, prev(i), 0)),
            pl.BlockSpec((1, 4 * N_DIL_KEYS, 2 * N_DIL_KEYS), lambda bi, i: (g, 0, 0)),
        ],
        out_specs=(pl.BlockSpec(blk, lambda bi, i: (bi, 0, i, 0)),
                   pl.BlockSpec(blk, lambda bi, i: (bi, 0, i, 0))),
        scratch_shapes=[pltpu.VMEM((HEADS_PER_GROUP * N_DIL_KEYS, 2 * N_DIL_KEYS), F32)],
        compiler_params=pltpu.CompilerParams(
            dimension_semantics=("parallel", "parallel"),
            vmem_limit_bytes=_vmem_limit(blocks, 8 << 20)),
        name=f"attn_d{d}",
    )(qkv, qkv, qkv, qkv, qkv, bias_tab)


def _attn_step_kernel(qkv_ref, c0_ref, c1_ref, c2_ref, b0_ref, b1_ref, b2_ref, o_ref):
    outs = [[None] * N_ATTN_GROUPS for _ in range(HEADS_PER_GROUP)]
    lses = [[None] * N_ATTN_GROUPS for _ in range(HEADS_PER_GROUP)]
    for g, (c_ref, b_ref) in enumerate(((c0_ref, b0_ref), (c1_ref, b1_ref), (c2_ref, b2_ref))):
        for h in range(HEADS_PER_GROUP):
            q = qkv_ref[0, g, h] * ATTN_SCALE
            kn = qkv_ref[0, N_ATTN_GROUPS + g, h]
            vn = qkv_ref[0, 2 * N_ATTN_GROUPS + g, h]
            s_c = jnp.sum(c_ref[0, 0, 0, h] * q, axis=0, keepdims=True) + b_ref[h:h + 1, :]
            s_n = jnp.sum(kn * q, axis=0, keepdims=True) + b_ref[HEADS_PER_GROUP + h:HEADS_PER_GROUP + h + 1, 0:1]
            m = jnp.maximum(jnp.max(s_c, axis=-1, keepdims=True), s_n)
            p_c = jnp.exp(s_c - m)
            p_n = jnp.exp(s_n - m)
            l = jnp.sum(p_c, axis=-1, keepdims=True) + p_n
            inv = 1.0 / l
            outs[h][g] = jnp.sum(c_ref[0, 0, 1, h] * (p_c * inv), axis=-1, keepdims=True) + (p_n * inv) * vn
            lses[h][g] = m + jnp.log(l)
    for h in range(HEADS_PER_GROUP):
        ls, os_ = lses[h], outs[h]
        mx = jnp.maximum(jnp.maximum(ls[0], ls[1]), ls[2])
        w = [jnp.exp(x - mx) for x in ls]
        tot = w[0] + w[1] + w[2]
        o_ref[0, h] = (w[0] * os_[0] + w[1] * os_[1] + w[2] * os_[2]) * (1.0 / tot)


def _attention_step(qkv, caches, biases, l, nb):
    cache_specs = [pl.BlockSpec((1, 1) + c.shape[2:], lambda bi: (l, bi, 0, 0, 0, 0)) for c in caches]
    bias_specs = [pl.BlockSpec(b.shape, lambda bi: (0, 0)) for b in biases]
    blocks = (sum(_nbytes(c.shape[2:], F32) for c in caches) + sum(_nbytes(b.shape, F32) for b in biases)
              + 13 * HEADS_PER_GROUP * HEAD_DIM * LANES * 4)
    return pl.pallas_call(
        _attn_step_kernel,
        out_shape=jax.ShapeDtypeStruct((nb, HEADS_PER_GROUP, HEAD_DIM, 1), F32),
        grid=(nb,),
        in_specs=[pl.BlockSpec((1,) + qkv.shape[1:], lambda bi: (bi, 0, 0, 0, 0))] + cache_specs + bias_specs,
        out_specs=pl.BlockSpec((1, HEADS_PER_GROUP, HEAD_DIM, 1), lambda bi: (bi, 0, 0, 0)),
        compiler_params=pltpu.CompilerParams(
            dimension_semantics=("parallel",),
            vmem_limit_bytes=_vmem_limit(blocks, 8 << 20)),
        name="attn_step",
    )(qkv, *caches, *biases)


def _branch_kernel(*refs, tm, step):
    if step:
        (act_ref, pst_ref, cst_ref, h_ref, wga_ref, wgb_ref, wgc_ref, at_ref,
         wgrp_ref, pscale_ref, wpb_ref, cw_ref, wcb_ref, wab_ref,
         out_ref, u_ref, pzs_ref, cb_ref, comb_ref) = refs
    else:
        (act_ref, hist_ref, h_ref, wga_ref, wgb_ref, wgc_ref, o0_ref, o1_ref, o2_ref, l0_ref, l1_ref, l2_ref,
         wgrp_ref, pscale_ref, wpb_ref, cw_ref, wcb_ref, wab_ref,
         out_ref, u_ref, pzs_ref, cb_ref, comb_ref, pe_ref, ue_ref) = refs
    c_tile = pl.program_id(1)
    i = pl.program_id(2)
    rows = pl.ds(pl.multiple_of(i * tm, tm), tm)
    pzs_ref, cb_ref, comb_ref = pzs_ref.at[rows], cb_ref.at[rows], comb_ref.at[rows]

    def sequence_mixers():
        p = act_ref[0, :, 0:POOL_WIDTH]
        gate_b = act_ref[0, :, POOL_WIDTH:POOL_WIDTH + CONV_CH]
        u = act_ref[0, :, POOL_WIDTH + CONV_CH:POOL_WIDTH + 2 * CONV_CH] * \
            act_ref[0, :, POOL_WIDTH + 2 * CONV_CH:POOL_WIDTH + 3 * CONV_CH]
        cw = cw_ref[0]
        if step:
            acc = p
            sums = {}
            for k in range(1, max(POOL_WINDOWS)):
                acc = acc + pst_ref[0, POOL_STATE - k]
                sums[k + 1] = acc
            means = [sums[w][:, gi * POOL_GROUP:(gi + 1) * POOL_GROUP] * (1.0 / w)
                     for gi, w in enumerate(POOL_WINDOWS)]
            conv = cst_ref[0, 0] * cw[0:1] + cst_ref[0, 1] * cw[1:2] + u * cw[2:3]
            u_ref[0, 0] = u
            comb_ref[...] = at_ref[0].astype(BF16)
        else:
            first = i == 0
            hist_p = hist_ref[0, :, 0:POOL_WIDTH]
            hist_u = hist_ref[0, :, POOL_WIDTH + CONV_CH:POOL_WIDTH + 2 * CONV_CH] * \
                hist_ref[0, :, POOL_WIDTH + 2 * CONV_CH:POOL_WIDTH + 3 * CONV_CH]
            pe_ref[0:HIST] = jnp.where(first, 0.0, hist_p)
            ue_ref[0:HIST] = jnp.where(first, 0.0, hist_u)
            pe_ref[HIST:HIST + tm] = p
            ue_ref[HIST:HIST + tm] = u
            t = i * tm + lax.broadcasted_iota(jnp.int32, (tm, 1), 0)
            means = []
            for gi, w in enumerate(POOL_WINDOWS):
                cs = slice(gi * POOL_GROUP, (gi + 1) * POOL_GROUP)
                acc = pe_ref[HIST:HIST + tm, cs]
                for k in range(1, w):
                    acc = acc + pe_ref[HIST - k:HIST - k + tm, cs]
                cnt = jnp.minimum(t + 1, w).astype(F32)
                means.append(acc * (1.0 / cnt))
            conv = (ue_ref[HIST - 2:HIST - 2 + tm] * cw[0:1] + ue_ref[HIST - 1:HIST - 1 + tm] * cw[1:2]
                    + u * cw[2:3])
            u_ref[0, 0] = ue_ref[HIST + tm - 8:HIST + tm]
            for sl in range(2):
                ls = [r[0, sl] for r in (l0_ref, l1_ref, l2_ref)]
                os_ = [r[0, sl] for r in (o0_ref, o1_ref, o2_ref)]
                mx = jnp.maximum(jnp.maximum(ls[0], ls[1]), ls[2])
                w_ = [jnp.exp(x - mx) for x in ls]
                tot = w_[0] + w_[1] + w_[2]
                comb = (w_[0] * os_[0] + w_[1] * os_[1] + w_[2] * os_[2]) * (1.0 / tot)
                comb_ref[:, sl * LANES:(sl + 1) * LANES] = comb.astype(BF16)
        pscale = pscale_ref[0]
        for gi in range(len(POOL_WINDOWS)):
            cs = slice(gi * POOL_GROUP, (gi + 1) * POOL_GROUP)
            pm = (means[gi] - p[:, cs]).astype(BF16)
            pz = _dot(pm, wgrp_ref[0, gi])
            pzs_ref[:, cs] = (pz * pscale[:, cs]).astype(BF16)
        cb_ref[...] = (gate_b * conv).astype(BF16)

    def merge_tile():
        h = h_ref[0]
        merged = _sigmoid(_dot(h, wga_ref[0])) * _dot(pzs_ref[...], wpb_ref[0])
        merged += _sigmoid(_dot(h, wgb_ref[0])) * _dot(cb_ref[...], wcb_ref[0])
        merged += _sigmoid(_dot(h, wgc_ref[0])) * _dot(comb_ref[...], wab_ref[0])
        out_ref[0] = merged.astype(BF16)

    @pl.when(c_tile == 0)
    def _():
        sequence_mixers()
        merge_tile()

    @pl.when(c_tile > 0)
    def _():
        merge_tile()


def _branches(act, h, attn, states, weights, l, *, tm, tc, step):
    b, s, _ = act.shape
    wgrp, pscale, wpb, cw, wcb, wab, wgate = weights
    ni = s // tm
    once = lambda c, i: jnp.where(c == 0, i, ni - 1)
    gate_specs = [pl.BlockSpec((1, tm, D_MODEL), lambda bi, c, i: (bi, i, 0))] + [
        pl.BlockSpec((1, D_MODEL, tc), lambda bi, c, i, k=k: (l, 0, k * (D_MODEL // tc) + c)) for k in range(3)]
    w_specs = [
        pl.BlockSpec((1,) + wgrp.shape[1:], lambda bi, c, i: (l, 0, 0, 0)),
        pl.BlockSpec((1,) + pscale.shape[1:], lambda bi, c, i: (l, 0, 0)),
        pl.BlockSpec((1, POOL_WIDTH, tc), lambda bi, c, i: (l, 0, c)),
        pl.BlockSpec((1,) + cw.shape[1:], lambda bi, c, i: (l, 0, 0)),
        pl.BlockSpec((1, CONV_CH, tc), lambda bi, c, i: (l, 0, c)),
        pl.BlockSpec((1, GROUP_WIDTH, tc), lambda bi, c, i: (l, 0, c)),
    ]
    act_spec = pl.BlockSpec((1, tm, ACT_WIDTH), lambda bi, c, i: (bi, once(c, i), 0))
    scratch = [pltpu.VMEM((s, POOL_WIDTH), BF16), pltpu.VMEM((s, CONV_CH), BF16),
               pltpu.VMEM((s, GROUP_WIDTH), BF16)]
    blocks = (_nbytes((tm, ACT_WIDTH), F32) + _nbytes((tm + 3 * tc, D_MODEL), BF16) + _nbytes(wgrp.shape[1:], BF16)
              + _nbytes((POOL_WIDTH + CONV_CH + GROUP_WIDTH, tc), BF16) + _nbytes((tm, tc), BF16))
    if step:
        pst, cst = states
        ins = [act, pst, cst, h, wgate, wgate, wgate, attn]
        in_specs = [act_spec,
                    pl.BlockSpec((1,) + pst.shape[1:], lambda bi, c, i: (l, 0, 0, 0)),
                    pl.BlockSpec((1,) + cst.shape[1:], lambda bi, c, i: (l, 0, 0, 0))] + gate_specs + [
                    pl.BlockSpec((1, tm, GROUP_WIDTH), lambda bi, c, i: (bi, once(c, i), 0))]
        urows = tm
        blocks += _nbytes(pst.shape[1:], F32) + _nbytes(cst.shape[1:], F32)
    else:
        hblk = tm // HIST
        slab = pl.BlockSpec((1, 2, tm, LANES), lambda bi, c, i: (bi, 0, once(c, i), 0))
        ins = [act, act, h, wgate, wgate, wgate] + list(attn)
        in_specs = [act_spec,
                    pl.BlockSpec((1, HIST, ACT_WIDTH),
                                 lambda bi, c, i: (bi, jnp.maximum(once(c, i) * hblk - 1, 0), 0))
                    ] + gate_specs + [slab] * 6
        scratch += [pltpu.VMEM((tm + HIST, POOL_WIDTH), F32), pltpu.VMEM((tm + HIST, CONV_CH), F32)]
        urows = 8
        blocks += 6 * _nbytes((2, tm, LANES), F32) + _nbytes((HIST, ACT_WIDTH), F32)
    scratch_bytes = (_nbytes((s, POOL_WIDTH + CONV_CH + GROUP_WIDTH), BF16) + 2 * _nbytes((tm + HIST, POOL_WIDTH), F32)
                     + 8 * _nbytes((tm, tc), F32))
    return pl.pallas_call(
        functools.partial(_branch_kernel, tm=tm, step=step),
        out_shape=(jax.ShapeDtypeStruct((b, s, D_MODEL), BF16),
                   jax.ShapeDtypeStruct((b, ni, urows, CONV_CH), F32)),
        grid=(b, D_MODEL // tc, ni),
        in_specs=in_specs + w_specs,
        out_specs=(pl.BlockSpec((1, tm, tc), lambda bi, c, i: (bi, i, c)),
                   pl.BlockSpec((1, 1, urows, CONV_CH), lambda bi, c, i: (bi, once(c, i), 0, 0))),
        scratch_shapes=scratch,
        compiler_params=pltpu.CompilerParams(
            dimension_semantics=("parallel", "arbitrary", "arbitrary"),
            vmem_limit_bytes=_vmem_limit(blocks, scratch_bytes)),
        name="branches_step" if step else "branches",
    )(*ins, wgrp, pscale, wpb, cw, wcb, wab)


def _oproj_kernel(m_ref, w_ref, x_ref, g_ref, gate_ref, o_ref):
    mix = _dot(m_ref[0], w_ref[0])
    o_ref[0] = x_ref[0] + _tail2(gate_ref) * _rms(mix, g_ref[0])


def _oproj(merged, w_o, x, norm, mod, l, *, tm):
    b, s, d = x.shape
    blocks = (_nbytes((tm, d), BF16) + _nbytes((d, d), BF16) + 2 * _nbytes((tm, d), F32)
              + _nbytes((_mod_rows(mod) + 1, d), F32))
    return pl.pallas_call(
        _oproj_kernel,
        out_shape=jax.ShapeDtypeStruct((b, s, d), F32),
        grid=(b, s // tm),
        in_specs=[
            pl.BlockSpec((1, tm, d), lambda bi, i: (bi, i, 0)),
            pl.BlockSpec((1, d, d), lambda bi, i: (l, 0, 0)),
            pl.BlockSpec((1, tm, d), lambda bi, i: (bi, i, 0)),
            norm[1](1), mod[1](2),
        ],
        out_specs=pl.BlockSpec((1, tm, d), lambda bi, i: (bi, i, 0)),
        compiler_params=pltpu.CompilerParams(
            dimension_semantics=("parallel", "parallel"),
            vmem_limit_bytes=_vmem_limit(blocks, _nbytes((tm, d), F32))),
        name="oproj",
    )(merged, w_o, x, norm[0], mod[0])


def _ffn_kernel(*refs, tm, step, nchunk):
    if step:
        (x_ref, g2_ref, sc_ref, sh_ref, wg_ref, wv_ref, cwg_ref, cwv_ref, wd_ref, g3_ref, gate_ref,
         stg_ref, stv_ref, o_ref, tg_ref, tv_ref, h_ref) = refs
        hist = 0
    else:
        (x_ref, xp_ref, g2_ref, sc_ref, sh_ref, wg_ref, wv_ref, cwg_ref, cwv_ref, wd_ref, g3_ref, gate_ref,
         o_ref, tg_ref, tv_ref, h_ref, ug_ref, uv_ref) = refs
        hist = HIST
    i = pl.program_id(1)
    j = pl.program_id(2)

    @pl.when(j == 0)
    def _():
        if not step:
            _modulate_rows(xp_ref, g2_ref, sc_ref, sh_ref, h_ref, 0, HIST)
        _modulate_rows(x_ref, g2_ref, sc_ref, sh_ref, h_ref, hist, tm)
        o_ref[...] = jnp.zeros_like(o_ref)

    cwg = cwg_ref[0]
    cwv = cwv_ref[0]
    if step:
        h = h_ref[...]
        up_g = _dot(h, wg_ref[0])
        up_v = _dot(h, wv_ref[0])
        uc_g = stg_ref[0, 0] * cwg[0:1] + stg_ref[0, 1] * cwg[1:2] + up_g * cwg[2:3]
        uc_v = stv_ref[0, 0] * cwv[0:1] + stv_ref[0, 1] * cwv[1:2] + up_v * cwv[2:3]
        tg_ref[0, 0] = up_g
        tv_ref[0, 0] = up_v
        o_ref[0] += _dot((_gelu_tanh(uc_g) * uc_v).astype(BF16), wd_ref[0])
    else:
        first = i == 0
        rc = tm // nchunk
        for c in range(nchunk):
            lo = 0 if c == 0 else HIST + c * rc
            hi = HIST + (c + 1) * rc
            h = h_ref[lo:hi]
            for u_ref, w_ref in ((ug_ref, wg_ref), (uv_ref, wv_ref)):
                up = _dot(h, w_ref[0])
                if c == 0:
                    u_ref[0:HIST] = jnp.where(first, 0.0, up[0:HIST])
                    u_ref[HIST:hi] = up[HIST:]
                else:
                    u_ref[lo:hi] = up

            def conv(ref, cw):
                base = HIST + c * rc
                return (ref[base - 2:base - 2 + rc] * cw[0:1] + ref[base - 1:base - 1 + rc] * cw[1:2]
                        + ref[base:base + rc] * cw[2:3])
            act = (_gelu_tanh(conv(ug_ref, cwg)) * conv(uv_ref, cwv)).astype(BF16)
            o_ref[0, c * rc:(c + 1) * rc] += _dot(act, wd_ref[0])
        tg_ref[0, 0] = ug_ref[HIST + tm - 8:HIST + tm]
        tv_ref[0, 0] = uv_ref[HIST + tm - 8:HIST + tm]

    @pl.when(j == pl.num_programs(2) - 1)
    def _():
        o_ref[0] = x_ref[0] + _tail2(gate_ref) * _rms(o_ref[0], g3_ref[0])


def _ffn(x, norm, mod, w_up, cw, w_down, state, l, *, tm, tf, step):
    b, s, d = x.shape
    ni, nj = s // tm, D_FF // tf
    trows = tm if step else 8
    w_specs = [
        pl.BlockSpec((1, d, tf), lambda bi, i, j: (l, 0, j)),
        pl.BlockSpec((1, d, tf), lambda bi, i, j: (l, 0, nj + j)),
        pl.BlockSpec((1, FFN_K, tf), lambda bi, i, j: (l, 0, j)),
        pl.BlockSpec((1, FFN_K, tf), lambda bi, i, j: (l, 0, nj + j)),
        pl.BlockSpec((1, tf, d), lambda bi, i, j: (l, j, 0)),
    ]
    x_spec = pl.BlockSpec((1, tm, d), lambda bi, i, j: (bi, i, 0), pipeline_mode=pl.Buffered(1))
    hist = 0 if step else HIST
    scratch = [pltpu.VMEM((tm + hist, d), BF16)]
    blocks = (2 * _nbytes((tm, d), F32) + 3 * _nbytes((d, tf), BF16) + 2 * _nbytes((trows, tf), F32)
              + _nbytes((3 * _mod_rows(mod) + 2, d), F32))
    mods = [norm[1](2), mod[1](4), mod[1](3)]
    tailp = [norm[1](3), mod[1](5)]
    if step:
        ins = [x, norm[0], mod[0], mod[0], w_up, w_up, cw, cw, w_down, norm[0], mod[0], state, state]
        in_specs = [x_spec] + mods + w_specs + tailp + [
            pl.BlockSpec((1, FFN_K - 1, tm, tf), lambda bi, i, j: (l, 0, 0, j)),
            pl.BlockSpec((1, FFN_K - 1, tm, tf), lambda bi, i, j: (l, 0, 0, nj + j))]
        blocks += 2 * _nbytes((FFN_K - 1, tm, tf), F32)
    else:
        hblk = tm // HIST
        ins = [x, x, norm[0], mod[0], mod[0], w_up, w_up, cw, cw, w_down, norm[0], mod[0]]
        in_specs = [x_spec,
                    pl.BlockSpec((1, HIST, d), lambda bi, i, j: (bi, jnp.maximum(i * hblk - 1, 0), 0))
                    ] + mods + w_specs + tailp
        scratch += [pltpu.VMEM((tm + HIST, tf), F32), pltpu.VMEM((tm + HIST, tf), F32)]
        blocks += _nbytes((HIST, d), F32)
    scratch_bytes = _nbytes((tm + hist, d), BF16) + 6 * _nbytes((tm + hist, tf), F32)
    tail = jax.ShapeDtypeStruct((b, ni, trows, D_FF), F32)
    tail_spec = lambda: pl.BlockSpec((1, 1, trows, tf), lambda bi, i, j: (bi, i, 0, j))
    return pl.pallas_call(
        functools.partial(_ffn_kernel, tm=tm, step=step, nchunk=ROW_CHUNKS),
        out_shape=(jax.ShapeDtypeStruct((b, s, d), F32), tail, tail),
        grid=(b, ni, nj),
        in_specs=in_specs,
        out_specs=(pl.BlockSpec((1, tm, d), lambda bi, i, j: (bi, i, 0)), tail_spec(), tail_spec()),
        scratch_shapes=scratch,
        compiler_params=pltpu.CompilerParams(
            dimension_semantics=("parallel", "parallel", "arbitrary"),
            vmem_limit_bytes=_vmem_limit(blocks, scratch_bytes)),
        name="ffn_step" if step else "ffn",
    )(*ins)


def _group_rel_bias(rel_bias):
    n = np.arange(N_DIL_KEYS + 1)
    max_exact = N_BUCKETS // 2
    out = []
    for g, (_, d) in enumerate(ATTN_GROUPS):
        dist = n * d
        large = max_exact + (np.log(np.maximum(dist, 1) / max_exact) / np.log(MAX_DISTANCE / max_exact)
                             * (N_BUCKETS - max_exact)).astype(np.int32)
        bucket = np.where(dist < max_exact, dist, np.minimum(large, N_BUCKETS - 1))
        out.append(rel_bias[bucket, g * HEADS_PER_GROUP:(g + 1) * HEADS_PER_GROUP].T)
    return jnp.stack(out)


def _bias_tables(rel_bias):
    bias_g = _group_rel_bias(rel_bias).astype(F32)
    n = N_DIL_KEYS
    gh = (N_ATTN_GROUPS, HEADS_PER_GROUP)
    ext = jnp.concatenate([bias_g[:, :, ::-1], jnp.full(gh + (n,), NEG_INF, F32)], axis=-1)
    tab = jnp.tile(ext, (1, 1, n))[:, :, :n * 2 * n].reshape(N_ATTN_GROUPS, HEADS_PER_GROUP * n, 2 * n)
    step = []
    for g, (_, d) in enumerate(ATTN_GROUPS):
        hit = bias_g[g, :, :0:-1, None]
        row = jnp.concatenate([hit, jnp.full((HEADS_PER_GROUP, n, d - 1), NEG_INF, F32)], axis=-1)
        new = jnp.broadcast_to(bias_g[g, :, 0:1], (HEADS_PER_GROUP, n * d))
        step.append(jnp.concatenate([row.reshape(HEADS_PER_GROUP, n * d), new], axis=0))
    return tab, step


def _kv_pack(qkv, g, nrows):
    b = qkv.shape[0]
    k0 = ATTN_WIDTH + g * GROUP_WIDTH
    v0 = 2 * ATTN_WIDTH + g * GROUP_WIDTH
    k = qkv[:, -nrows:, k0:k0 + GROUP_WIDTH].reshape(b, nrows, HEADS_PER_GROUP, HEAD_DIM)
    v = qkv[:, -nrows:, v0:v0 + GROUP_WIDTH].reshape(b, nrows, HEADS_PER_GROUP, HEAD_DIM)
    return jnp.stack([k, v], axis=2)


def _prompt_layer(x, l, P):
    b, s, _ = x.shape
    norm, mod = P["norm"](l), P["mod_p"](l)
    tm = 512
    qkv, qkv_slab, h = _proj(x, norm, mod, P["w_qkv"], l, tm=1024, tn=ATTN_WIDTH)
    act = _matmul(h, P["w_act"], l, tm=1024, tn=1024)
    ols = [_attention(qkv_slab, P["bias_tab"], g) for g in range(N_ATTN_GROUPS)]
    attn = [o for o, _ in ols] + [ls for _, ls in ols]
    merged, u_tail = _branches(act, h, attn, None, P["branch"], l, tm=tm, tc=512, step=False)
    x1 = _oproj(merged, P["w_o"], x, norm, mod, l, tm=tm)
    x2, tail_g, tail_v = _ffn(x1, norm, mod, P["w_up"], P["ffn_cw"], P["w_down"], None, l,
                              tm=1024, tf=512, step=False)
    new_kv = [_kv_pack(qkv, g, min(w, s)) for g, (w, _) in enumerate(ATTN_GROUPS)]
    new_pool = act[:, -POOL_STATE:, :POOL_WIDTH]
    new_conv = u_tail[:, -1, -(CONV_K - 1):]
    new_ffn = jnp.concatenate([tail_g[:, -1, -(FFN_K - 1):], tail_v[:, -1, -(FFN_K - 1):]], axis=-1)
    return x2, (new_kv[0], new_kv[1], new_kv[2], new_pool, new_conv, new_ffn)


def _pad_rows(a, axis):
    pad = [(0, 0)] * a.ndim
    pad[axis] = (0, SAMPLE_ROWS - a.shape[axis])
    return jnp.pad(a, pad)


def _sample_layer(x, l, P, S, nb):
    norm, mod = P["norm"](l), P["mod_s"](l)
    tm = SAMPLE_ROWS
    qkv, _, h = _proj(x, norm, mod, P["w_qkv"], l, tm=tm, tn=ATTN_WIDTH)
    act = _matmul(h, P["w_act"], l, tm=tm, tn=1024)
    heads = qkv[0, :nb].reshape(nb, 3 * N_ATTN_GROUPS, HEADS_PER_GROUP, HEAD_DIM, 1)
    attn = _attention_step(heads, S["caches"], P["bias_step"], l, nb)
    attn = _pad_rows(attn.reshape(1, nb, GROUP_WIDTH), 1)
    merged, u = _branches(act, h, attn, (S["pool_t"], S["conv_t"]), P["branch"], l, tm=tm, tc=512, step=True)
    x1 = _oproj(merged, P["w_o"], x, norm, mod, l, tm=tm)
    x2, up_g, up_v = _ffn(x1, norm, mod, P["w_up"], P["ffn_cw"], P["w_down"], S["ffn_t"], l,
                          tm=tm, tf=512, step=True)
    new_kv = [_kv_pack(qkv[0, :nb, None], g, 1) for g in range(N_ATTN_GROUPS)]
    new_pool = jnp.concatenate([S["pool"][l][:, 1:], act[0, :nb, None, :POOL_WIDTH]], axis=1)
    new_conv = jnp.concatenate([S["conv"][l][:, 1:], u[0, 0, :nb, None]], axis=1)
    up_new = jnp.concatenate([up_g[0, 0, :nb], up_v[0, 0, :nb]], axis=-1)
    new_ffn = jnp.concatenate([S["ffn"][l][:, 1:], up_new[:, None]], axis=1)
    return x2, (new_kv[0], new_kv[1], new_kv[2], new_pool, new_conv, new_ffn)


def kernel(x_prompt, x_sample, c_prompt, c_sample, cache_kv_w128, cache_kv_w512, cache_kv_w2048, state_pool, state_conv, state_ffn_conv, rel_bias, norm_g, w_ada, b_ada, w_in, w_attn_br, w_pool_grp, pool_scale, w_pool_br, conv_w, w_conv_br, w_o, w_up, ffn_conv_w, w_down):
    nbp = x_prompt.shape[0]
    nbs, tdec, _ = x_sample.shape
    assert tdec == 1 and nbs <= SAMPLE_ROWS
    caches = (cache_kv_w128, cache_kv_w512, cache_kv_w2048)
    for (w, d), c in zip(ATTN_GROUPS, caches):
        assert c.shape[2] == w == N_DIL_KEYS * d, "cache must hold exactly one window"

    bias_tab, bias_step = _bias_tables(rel_bias)
    c_rows = -(-(SAMPLE_ROWS + nbp) // 8) * 8
    c_all = jnp.zeros((c_rows, D_MODEL), F32).at[:nbs].set(c_sample).at[SAMPLE_ROWS:SAMPLE_ROWS + nbp].set(c_prompt)
    mod_all = _ada(c_all, w_ada, b_ada)

    P = dict(
        norm=functools.partial(_norm_view, norm_g),
        mod_p=functools.partial(_mod_view, mod_all, prompt=True),
        mod_s=functools.partial(_mod_view, mod_all, prompt=False),
        w_qkv=w_in[:, :, :QKV_WIDTH].astype(BF16),
        w_act=w_in[:, :, QKV_WIDTH:QKV_WIDTH + ACT_WIDTH].astype(BF16),
        branch=(w_pool_grp.astype(BF16), pool_scale[:, None, :], w_pool_br.astype(BF16),
                conv_w, w_conv_br.astype(BF16), w_attn_br.astype(BF16),
                w_in[:, :, QKV_WIDTH + ACT_WIDTH:].astype(BF16)),
        w_o=w_o.astype(BF16), w_up=w_up.astype(BF16), ffn_cw=ffn_conv_w, w_down=w_down.astype(BF16),
        bias_tab=bias_tab, bias_step=bias_step,
    )
    S = dict(
        caches=[c.transpose(0, 1, 3, 4, 5, 2) for c in caches],
        pool=state_pool, conv=state_conv, ffn=state_ffn_conv,
        pool_t=_pad_rows(state_pool.transpose(0, 2, 1, 3), 2),
        conv_t=_pad_rows(state_conv.transpose(0, 2, 1, 3), 2),
        ffn_t=_pad_rows(state_ffn_conv.transpose(0, 2, 1, 3), 2),
    )
    yp = x_prompt
    ys = _pad_rows(x_sample.reshape(1, nbs, D_MODEL), 1)
    st_p, st_s = [], []
    for l in range(DEPTH):
        yp, sp = _prompt_layer(yp, l, P)
        ys, ss = _sample_layer(ys, l, P, S, nbs)
        st_p.append(sp)
        st_s.append(ss)
    outs_p = [jnp.stack([s[k] for s in st_p]) for k in range(6)]
    outs_s = [jnp.stack([s[k] for s in st_s]) for k in range(6)]
    return (yp, ys[0, :nbs, None, :], *outs_p, *outs_s)
```

```python
import functools

import numpy as np
import jax
import jax.numpy as jnp
from jax import lax
from jax.experimental import pallas as pl
from jax.experimental.pallas import tpu as pltpu

F32 = jnp.float32
BF16 = jnp.bfloat16

D_MODEL = 2048
DEPTH = 2
HEAD_DIM = 64
HEADS_PER_GROUP = 4
ATTN_GROUPS = ((128, 1), (512, 4), (2048, 16))
N_ATTN_GROUPS = len(ATTN_GROUPS)
ATTN_WIDTH = N_ATTN_GROUPS * HEADS_PER_GROUP * HEAD_DIM
GROUP_WIDTH = HEADS_PER_GROUP * HEAD_DIM
N_DIL_KEYS = 128
N_BUCKETS = 32
MAX_DISTANCE = 2048
ATTN_SCALE = HEAD_DIM ** -0.5
POOL_WINDOWS = (2, 4, 8, 16)
POOL_GROUP = 128
POOL_WIDTH = 512
POOL_STATE = 15
CONV_CH = 512
CONV_K = 3
D_FF = 5632
FFN_K = 3
N_MOD = 6
N_NORM = 4
EPS = 1e-6
NEG_INF = -1e30

LANES = 128
HIST = 16
SAMPLE_ROWS = 16
QKV_WIDTH = 3 * ATTN_WIDTH
QKV_SLABS = QKV_WIDTH // LANES
ACT_WIDTH = POOL_WIDTH + 3 * CONV_CH
ATTN_CLASS_UNROLL = 4
ROW_CHUNKS = 2
VMEM_CAP = 56 * 1024 * 1024


def _vmem_limit(block_bytes, scratch_bytes=0):
    del block_bytes, scratch_bytes
    return VMEM_CAP


def _nbytes(shape, dtype):
    return int(np.prod(shape)) * jnp.dtype(dtype).itemsize


def _rms(x, g):
    return x * lax.rsqrt(jnp.mean(x * x, axis=-1, keepdims=True) + EPS) * g


def _sigmoid(x):
    return 0.5 + 0.5 * jnp.tanh(0.5 * x)


def _gelu_tanh(x):
    return 0.5 * x * (1.0 + jnp.tanh(np.sqrt(2.0 / np.pi) * (x + 0.044715 * (x * x * x))))


def _dot(a, b):
    return jnp.dot(a, b, preferred_element_type=F32)


def _tail2(ref):
    return ref[(0,) * (len(ref.shape) - 2)]


def _norm_view(norm_g, l):
    arr = norm_g.reshape(DEPTH * N_NORM, 1, D_MODEL)
    return arr, lambda k: pl.BlockSpec((1, 1, D_MODEL), lambda *_: (l * N_NORM + k, 0, 0))


def _mod_view(mod_all, l, prompt):
    if prompt:
        arr = mod_all.reshape(DEPTH, mod_all.shape[1], N_MOD, 1, D_MODEL)
        return arr, lambda k: pl.BlockSpec((1, 1, 1, 1, D_MODEL), lambda bi, *_: (l, SAMPLE_ROWS + bi, k, 0, 0))
    return mod_all, lambda k: pl.BlockSpec((1, SAMPLE_ROWS, D_MODEL), lambda bi, *_: (l, 0, k))


def _mod_rows(mod):
    return mod[1](0).block_shape[-2]


def _cast_split_kernel(w_ref, *o_refs, bounds):
    for o_ref, (lo, hi) in zip(o_refs, bounds):
        o_ref[...] = w_ref[:, :, lo:hi].astype(BF16)


def _cast_split(w, bounds, *, tk):
    nl, k, n = w.shape
    blocks = _nbytes((tk, n), F32) + sum(_nbytes((tk, hi - lo), BF16) for lo, hi in bounds)
    return pl.pallas_call(
        functools.partial(_cast_split_kernel, bounds=tuple(bounds)),
        out_shape=[jax.ShapeDtypeStruct((nl, k, hi - lo), BF16) for lo, hi in bounds],
        grid=(nl, k // tk),
        in_specs=[pl.BlockSpec((1, tk, n), lambda li, ki: (li, ki, 0))],
        out_specs=[pl.BlockSpec((1, tk, hi - lo), lambda li, ki: (li, ki, 0)) for lo, hi in bounds],
        compiler_params=pltpu.CompilerParams(
            dimension_semantics=("parallel", "parallel"), vmem_limit_bytes=_vmem_limit(blocks)),
        name="cast_split",
    )(w)


def _ada_kernel(c_ref, w_ref, b_ref, o_ref):
    c = c_ref[...]
    s = (c * _sigmoid(c)).astype(BF16)
    o_ref[0] = _dot(s, w_ref[0].astype(BF16)) + b_ref[0]


def _ada(c_all, w_ada, b_ada):
    rows = c_all.shape[0]
    n = w_ada.shape[-1]
    tn = 1024
    blocks = _nbytes((rows, D_MODEL), F32) + _nbytes((D_MODEL, tn), F32) + _nbytes((rows + 1, tn), F32)
    return pl.pallas_call(
        _ada_kernel,
        out_shape=jax.ShapeDtypeStruct((DEPTH, rows, n), F32),
        grid=(DEPTH, n // tn),
        in_specs=[
            pl.BlockSpec((rows, D_MODEL), lambda l, j: (0, 0)),
            pl.BlockSpec((1, D_MODEL, tn), lambda l, j: (l, 0, j)),
            pl.BlockSpec((1, 1, tn), lambda l, j: (l, 0, j)),
        ],
        out_specs=pl.BlockSpec((1, rows, tn), lambda l, j: (l, 0, j)),
        compiler_params=pltpu.CompilerParams(
            dimension_semantics=("parallel", "parallel"),
            vmem_limit_bytes=_vmem_limit(blocks, _nbytes((D_MODEL, tn), BF16))),
        name="ada",
    )(c_all, w_ada, b_ada.reshape(DEPTH, 1, n))


def _modulate_rows(x_ref, g_ref, sc_ref, sh_ref, h_ref, row0, tm):
    y = _rms(x_ref[0], g_ref[0])
    h_ref[row0:row0 + tm] = (y * (1.0 + _tail2(sc_ref)) + _tail2(sh_ref)).astype(BF16)


def _proj_kernel(x_ref, g_ref, sc_ref, sh_ref, w_ref, o_ref, slab_ref, h_ref, *, slabs, tm, nchunk):
    def project(r):
        res = _dot(h_ref[0, r], w_ref[0])
        o_ref[0, r] = res
        for s in range(slabs):
            slab_ref[0, s, r] = res[:, s * LANES:(s + 1) * LANES]

    @pl.when(pl.program_id(2) == 0)
    def _():
        rc = tm // nchunk
        for c in range(nchunk):
            r = slice(c * rc, (c + 1) * rc)
            y = _rms(x_ref[0, r], g_ref[0])
            h_ref[0, r] = (y * (1.0 + _tail2(sc_ref)) + _tail2(sh_ref)).astype(BF16)
            project(r)

    @pl.when(pl.program_id(2) > 0)
    def _():
        project(slice(None))


def _proj(x, norm, mod, w, l, *, tm, tn):
    b, s, d = x.shape
    n = w.shape[2]
    slabs = tn // LANES
    r = _mod_rows(mod)
    blocks = (_nbytes((tm, d), F32) + _nbytes((2 * r + 1, d), F32) + _nbytes((d, tn), BF16)
              + 2 * _nbytes((tm, tn), F32) + _nbytes((tm, d), BF16))
    return pl.pallas_call(
        functools.partial(_proj_kernel, slabs=slabs, tm=tm, nchunk=2 * ROW_CHUNKS if r == 1 else 1),
        out_shape=[jax.ShapeDtypeStruct((b, s, n), F32),
                   jax.ShapeDtypeStruct((b, n // LANES, s, LANES), F32),
                   jax.ShapeDtypeStruct((b, s, d), BF16)],
        grid=(b, s // tm, n // tn),
        in_specs=[
            pl.BlockSpec((1, tm, d), lambda bi, i, j: (bi, i, 0)),
            norm[1](0), mod[1](1), mod[1](0),
            pl.BlockSpec((1, d, tn), lambda bi, i, j: (l, 0, j)),
        ],
        out_specs=[pl.BlockSpec((1, tm, tn), lambda bi, i, j: (bi, i, j)),
                   pl.BlockSpec((1, slabs, tm, LANES), lambda bi, i, j: (bi, j, i, 0)),
                   pl.BlockSpec((1, tm, d), lambda bi, i, j: (bi, i, 0))],
        compiler_params=pltpu.CompilerParams(
            dimension_semantics=("parallel", "parallel", "arbitrary"),
            vmem_limit_bytes=_vmem_limit(blocks, _nbytes((tm, tn), F32) + _nbytes((tm, d), F32))),
        name="proj_norm",
    )(x, norm[0], mod[0], mod[0], w)


def _matmul_kernel(h_ref, w_ref, o_ref):
    o_ref[0] = _dot(h_ref[0], w_ref[0])


def _matmul(h, w, l, *, tm, tn):
    b, s, d = h.shape
    n = w.shape[2]
    blocks = _nbytes((tm, d), BF16) + _nbytes((d, tn), BF16) + _nbytes((tm, tn), F32)
    return pl.pallas_call(
        _matmul_kernel,
        out_shape=jax.ShapeDtypeStruct((b, s, n), F32),
        grid=(b, s // tm, n // tn),
        in_specs=[pl.BlockSpec((1, tm, d), lambda bi, i, j: (bi, i, 0)),
                  pl.BlockSpec((1, d, tn), lambda bi, i, j: (l, 0, j))],
        out_specs=pl.BlockSpec((1, tm, tn), lambda bi, i, j: (bi, i, j)),
        compiler_params=pltpu.CompilerParams(
            dimension_semantics=("parallel", "parallel", "parallel"),
            vmem_limit_bytes=_vmem_limit(blocks, _nbytes((tm, tn), F32))),
        name="proj_act",
    )(h, w)


def _head_masks(rows):
    lane = lax.broadcasted_iota(jnp.int32, (rows, GROUP_WIDTH), 1)
    return [(lane >= h * HEAD_DIM) & (lane < (h + 1) * HEAD_DIM) for h in range(HEADS_PER_GROUP)]


def _attn_kernel(q_ref, kc_ref, kp_ref, vc_ref, vp_ref, bias_ref, o_ref, lse_ref, edge_ref, *, d, sb):
    i = pl.program_id(1)
    nq = N_DIL_KEYS
    span = nq * d
    ncb = sb // span
    hm = _head_masks(1)
    col = lax.broadcasted_iota(jnp.int32, (1, 2 * nq), 1)
    edge_ref[...] = jnp.where((col < nq) & (i == 0), NEG_INF, bias_ref[0])

    def rows(start):
        return pl.ds(start, nq, stride=d) if d > 1 else pl.ds(start, nq)

    def load(ref, start):
        return jnp.concatenate([ref[0, s, rows(start), :] for s in range(2)], axis=1)

    def one_block(r, jb):
        qs = jb * span + r
        q = load(q_ref, qs) * ATTN_SCALE
        if jb == 0:
            lo = sb - span + r
            k_lo, v_lo = load(kp_ref, lo), load(vp_ref, lo)
        else:
            lo = (jb - 1) * span + r
            k_lo, v_lo = load(kc_ref, lo), load(vc_ref, lo)
        kcat = jnp.concatenate([k_lo, load(kc_ref, qs)], axis=0).astype(BF16)
        vcat = jnp.concatenate([v_lo, load(vc_ref, qs)], axis=0).astype(BF16)
        qm = jnp.concatenate([jnp.where(hm[h], q, 0.0) for h in range(HEADS_PER_GROUP)], axis=0).astype(BF16)
        s = lax.dot_general(qm, kcat, (((1,), (1,)), ((), ())), preferred_element_type=F32)
        s = s + (edge_ref[...] if jb == 0 else bias_ref[0])
        m = jnp.max(s, axis=-1, keepdims=True)
        p = jnp.exp(s - m)
        l = jnp.sum(p, axis=-1, keepdims=True)
        oall = _dot((p * (1.0 / l)).astype(BF16), vcat)
        lse = m + jnp.log(l)
        o = jnp.zeros((nq, GROUP_WIDTH), F32)
        ls = jnp.zeros((nq, GROUP_WIDTH), F32)
        for h in range(HEADS_PER_GROUP):
            o = jnp.where(hm[h], oall[h * nq:(h + 1) * nq], o)
            ls = jnp.where(hm[h], lse[h * nq:(h + 1) * nq], ls)
        for sl in range(2):
            o_ref[0, sl, rows(qs), :] = o[:, sl * LANES:(sl + 1) * LANES]
            lse_ref[0, sl, rows(qs), :] = ls[:, sl * LANES:(sl + 1) * LANES]

    if d == 1:
        for jb in range(ncb):
            one_block(0, jb)
    else:
        def body(r, carry):
            for jb in range(ncb):
                one_block(r, jb)
            return carry
        lax.fori_loop(0, d, body, 0, unroll=ATTN_CLASS_UNROLL)


def _attention(qkv, bias_tab, g):
    b, _, s, _ = qkv.shape
    d = ATTN_GROUPS[g][1]
    sb = max(N_DIL_KEYS * d, 512)
    blk = (1, 2, sb, LANES)
    kslab, vslab = ATTN_WIDTH // GROUP_WIDTH + g, 2 * ATTN_WIDTH // GROUP_WIDTH + g
    prev = lambda i: jnp.maximum(i - 1, 0)
    blocks = 7 * _nbytes(blk, F32) + _nbytes((4 * N_DIL_KEYS, 2 * N_DIL_KEYS), F32)
    out = jax.ShapeDtypeStruct((b, 2, s, LANES), F32)
    return pl.pallas_call(
        functools.partial(_attn_kernel, d=d, sb=sb),
        out_shape=(out, out),
        grid=(b, s // sb),
        in_specs=[
            pl.BlockSpec(blk, lambda bi, i: (bi, g, i, 0)),
            pl.BlockSpec(blk, lambda bi, i: (bi, kslab, i, 0)),
            pl.BlockSpec(blk, lambda bi, i: (bi, kslab, prev(i), 0)),
            pl.BlockSpec(blk, lambda bi, i: (bi, vslab, i, 0)),
            pl.BlockSpec(blk, lambda bi, i: (bi, vslab, prev(i), 0)),
            pl.BlockSpec((1, 4 * N_DIL_KEYS, 2 * N_DIL_KEYS), lambda bi, i: (g, 0, 0)),
        ],
        out_specs=(pl.BlockSpec(blk, lambda bi, i: (bi, 0, i, 0)),
                   pl.BlockSpec(blk, lambda bi, i: (bi, 0, i, 0))),
        scratch_shapes=[pltpu.VMEM((HEADS_PER_GROUP * N_DIL_KEYS, 2 * N_DIL_KEYS), F32)],
        compiler_params=pltpu.CompilerParams(
            dimension_semantics=("parallel", "parallel"),
            vmem_limit_bytes=_vmem_limit(blocks, 8 << 20)),
        name=f"attn_d{d}",
    )(qkv, qkv, qkv, qkv, qkv, bias_tab)


def _attn_step_kernel(qkv_ref, c0_ref, c1_ref, c2_ref, b0_ref, b1_ref, b2_ref, o_ref):
    outs = [[None] * N_ATTN_GROUPS for _ in range(HEADS_PER_GROUP)]
    lses = [[None] * N_ATTN_GROUPS for _ in range(HEADS_PER_GROUP)]
    for g, (c_ref, b_ref) in enumerate(((c0_ref, b0_ref), (c1_ref, b1_ref), (c2_ref, b2_ref))):
        for h in range(HEADS_PER_GROUP):
            q = qkv_ref[0, g, h] * ATTN_SCALE
            kn = qkv_ref[0, N_ATTN_GROUPS + g, h]
            vn = qkv_ref[0, 2 * N_ATTN_GROUPS + g, h]
            s_c = jnp.sum(c_ref[0, 0, 0, h] * q, axis=0, keepdims=True) + b_ref[h:h + 1, :]
            s_n = jnp.sum(kn * q, axis=0, keepdims=True) + b_ref[HEADS_PER_GROUP + h:HEADS_PER_GROUP + h + 1, 0:1]
            m = jnp.maximum(jnp.max(s_c, axis=-1, keepdims=True), s_n)
            p_c = jnp.exp(s_c - m)
            p_n = jnp.exp(s_n - m)
            l = jnp.sum(p_c, axis=-1, keepdims=True) + p_n
            inv = 1.0 / l
            outs[h][g] = jnp.sum(c_ref[0, 0, 1, h] * (p_c * inv), axis=-1, keepdims=True) + (p_n * inv) * vn
            lses[h][g] = m + jnp.log(l)
    for h in range(HEADS_PER_GROUP):
        ls, os_ = lses[h], outs[h]
        mx = jnp.maximum(jnp.maximum(ls[0], ls[1]), ls[2])
        w = [jnp.exp(x - mx) for x in ls]
        tot = w[0] + w[1] + w[2]
        o_ref[0, h] = (w[0] * os_[0] + w[1] * os_[1] + w[2] * os_[2]) * (1.0 / tot)


def _attention_step(qkv, caches, biases, l, nb):
    cache_specs = [pl.BlockSpec((1, 1) + c.shape[2:], lambda bi: (l, bi, 0, 0, 0, 0)) for c in caches]
    bias_specs = [pl.BlockSpec(b.shape, lambda bi: (0, 0)) for b in biases]
    blocks = (sum(_nbytes(c.shape[2:], F32) for c in caches) + sum(_nbytes(b.shape, F32) for b in biases)
              + 13 * HEADS_PER_GROUP * HEAD_DIM * LANES * 4)
    return pl.pallas_call(
        _attn_step_kernel,
        out_shape=jax.ShapeDtypeStruct((nb, HEADS_PER_GROUP, HEAD_DIM, 1), F32),
        grid=(nb,),
        in_specs=[pl.BlockSpec((1,) + qkv.shape[1:], lambda bi: (bi, 0, 0, 0, 0))] + cache_specs + bias_specs,
        out_specs=pl.BlockSpec((1, HEADS_PER_GROUP, HEAD_DIM, 1), lambda bi: (bi, 0, 0, 0)),
        compiler_params=pltpu.CompilerParams(
            dimension_semantics=("parallel",),
            vmem_limit_bytes=_vmem_limit(blocks, 8 << 20)),
        name="attn_step",
    )(qkv, *caches, *biases)


def _branch_kernel(*refs, tm, step):
    if step:
        (act_ref, pst_ref, cst_ref, h_ref, wga_ref, wgb_ref, wgc_ref, at_ref,
         wgrp_ref, pscale_ref, wpb_ref, cw_ref, wcb_ref, wab_ref,
         out_ref, u_ref, pzs_ref, cb_ref, comb_ref) = refs
    else:
        (act_ref, hist_ref, h_ref, wga_ref, wgb_ref, wgc_ref, o0_ref, o1_ref, o2_ref, l0_ref, l1_ref, l2_ref,
         wgrp_ref, pscale_ref, wpb_ref, cw_ref, wcb_ref, wab_ref,
         out_ref, u_ref, pzs_ref, cb_ref, comb_ref, pe_ref, ue_ref) = refs
    c_tile = pl.program_id(1)
    i = pl.program_id(2)
    rows = pl.ds(pl.multiple_of(i * tm, tm), tm)
    pzs_ref, cb_ref, comb_ref = pzs_ref.at[rows], cb_ref.at[rows], comb_ref.at[rows]

    def sequence_mixers():
        p = act_ref[0, :, 0:POOL_WIDTH]
        gate_b = act_ref[0, :, POOL_WIDTH:POOL_WIDTH + CONV_CH]
        u = act_ref[0, :, POOL_WIDTH + CONV_CH:POOL_WIDTH + 2 * CONV_CH] * \
            act_ref[0, :, POOL_WIDTH + 2 * CONV_CH:POOL_WIDTH + 3 * CONV_CH]
        cw = cw_ref[0]
        if step:
            acc = p
            sums = {}
            for k in range(1, max(POOL_WINDOWS)):
                acc = acc + pst_ref[0, POOL_STATE - k]
                sums[k + 1] = acc
            means = [sums[w][:, gi * POOL_GROUP:(gi + 1) * POOL_GROUP] * (1.0 / w)
                     for gi, w in enumerate(POOL_WINDOWS)]
            conv = cst_ref[0, 0] * cw[0:1] + cst_ref[0, 1] * cw[1:2] + u * cw[2:3]
            u_ref[0, 0] = u
            comb_ref[...] = at_ref[0].astype(BF16)
        else:
            first = i == 0
            hist_p = hist_ref[0, :, 0:POOL_WIDTH]
            hist_u = hist_ref[0, :, POOL_WIDTH + CONV_CH:POOL_WIDTH + 2 * CONV_CH] * \
                hist_ref[0, :, POOL_WIDTH + 2 * CONV_CH:POOL_WIDTH + 3 * CONV_CH]
            pe_ref[0:HIST] = jnp.where(first, 0.0, hist_p)
            ue_ref[0:HIST] = jnp.where(first, 0.0, hist_u)
            pe_ref[HIST:HIST + tm] = p
            ue_ref[HIST:HIST + tm] = u
            t = i * tm + lax.broadcasted_iota(jnp.int32, (tm, 1), 0)
            means = []
            for gi, w in enumerate(POOL_WINDOWS):
                cs = slice(gi * POOL_GROUP, (gi + 1) * POOL_GROUP)
                acc = pe_ref[HIST:HIST + tm, cs]
                for k in range(1, w):
                    acc = acc + pe_ref[HIST - k:HIST - k + tm, cs]
                cnt = jnp.minimum(t + 1, w).astype(F32)
                means.append(acc * (1.0 / cnt))
            conv = (ue_ref[HIST - 2:HIST - 2 + tm] * cw[0:1] + ue_ref[HIST - 1:HIST - 1 + tm] * cw[1:2]
                    + u * cw[2:3])
            u_ref[0, 0] = ue_ref[HIST + tm - 8:HIST + tm]
            for sl in range(2):
                ls = [r[0, sl] for r in (l0_ref, l1_ref, l2_ref)]
                os_ = [r[0, sl] for r in (o0_ref, o1_ref, o2_ref)]
                mx = jnp.maximum(jnp.maximum(ls[0], ls[1]), ls[2])
                w_ = [jnp.exp(x - mx) for x in ls]
                tot = w_[0] + w_[1] + w_[2]
                comb = (w_[0] * os_[0] + w_[1] * os_[1] + w_[2] * os_[2]) * (1.0 / tot)
                comb_ref[:, sl * LANES:(sl + 1) * LANES] = comb.astype(BF16)
        pscale = pscale_ref[0]
        for gi in range(len(POOL_WINDOWS)):
            cs = slice(gi * POOL_GROUP, (gi + 1) * POOL_GROUP)
            pm = (means[gi] - p[:, cs]).astype(BF16)
            pz = _dot(pm, wgrp_ref[0, gi])
            pzs_ref[:, cs] = (pz * pscale[:, cs]).astype(BF16)
        cb_ref[...] = (gate_b * conv).astype(BF16)

    def merge_tile():
        h = h_ref[0]
        merged = _sigmoid(_dot(h, wga_ref[0])) * _dot(pzs_ref[...], wpb_ref[0])
        merged += _sigmoid(_dot(h, wgb_ref[0])) * _dot(cb_ref[...], wcb_ref[0])
        merged += _sigmoid(_dot(h, wgc_ref[0])) * _dot(comb_ref[...], wab_ref[0])
        out_ref[0] = merged.astype(BF16)

    @pl.when(c_tile == 0)
    def _():
        sequence_mixers()
        merge_tile()

    @pl.when(c_tile > 0)
    def _():
        merge_tile()


def _branches(act, h, attn, states, weights, l, *, tm, tc, step):
    b, s, _ = act.shape
    wgrp, pscale, wpb, cw, wcb, wab, wgate = weights
    ni = s // tm
    once = lambda c, i: jnp.where(c == 0, i, ni - 1)
    gate_specs = [pl.BlockSpec((1, tm, D_MODEL), lambda bi, c, i: (bi, i, 0))] + [
        pl.BlockSpec((1, D_MODEL, tc), lambda bi, c, i, k=k: (l, 0, k * (D_MODEL // tc) + c)) for k in range(3)]
    w_specs = [
        pl.BlockSpec((1,) + wgrp.shape[1:], lambda bi, c, i: (l, 0, 0, 0)),
        pl.BlockSpec((1,) + pscale.shape[1:], lambda bi, c, i: (l, 0, 0)),
        pl.BlockSpec((1, POOL_WIDTH, tc), lambda bi, c, i: (l, 0, c)),
        pl.BlockSpec((1,) + cw.shape[1:], lambda bi, c, i: (l, 0, 0)),
        pl.BlockSpec((1, CONV_CH, tc), lambda bi, c, i: (l, 0, c)),
        pl.BlockSpec((1, GROUP_WIDTH, tc), lambda bi, c, i: (l, 0, c)),
    ]
    act_spec = pl.BlockSpec((1, tm, ACT_WIDTH), lambda bi, c, i: (bi, once(c, i), 0))
    scratch = [pltpu.VMEM((s, POOL_WIDTH), BF16), pltpu.VMEM((s, CONV_CH), BF16),
               pltpu.VMEM((s, GROUP_WIDTH), BF16)]
    blocks = (_nbytes((tm, ACT_WIDTH), F32) + _nbytes((tm + 3 * tc, D_MODEL), BF16) + _nbytes(wgrp.shape[1:], BF16)
              + _nbytes((POOL_WIDTH + CONV_CH + GROUP_WIDTH, tc), BF16) + _nbytes((tm, tc), BF16))
    if step:
        pst, cst = states
        ins = [act, pst, cst, h, wgate, wgate, wgate, attn]
        in_specs = [act_spec,
                    pl.BlockSpec((1,) + pst.shape[1:], lambda bi, c, i: (l, 0, 0, 0)),
                    pl.BlockSpec((1,) + cst.shape[1:], lambda bi, c, i: (l, 0, 0, 0))] + gate_specs + [
                    pl.BlockSpec((1, tm, GROUP_WIDTH), lambda bi, c, i: (bi, once(c, i), 0))]
        urows = tm
        blocks += _nbytes(pst.shape[1:], F32) + _nbytes(cst.shape[1:], F32)
    else:
        hblk = tm // HIST
        slab = pl.BlockSpec((1, 2, tm, LANES), lambda bi, c, i: (bi, 0, once(c, i), 0))
        ins = [act, act, h, wgate, wgate, wgate] + list(attn)
        in_specs = [act_spec,
                    pl.BlockSpec((1, HIST, ACT_WIDTH),
                                 lambda bi, c, i: (bi, jnp.maximum(once(c, i) * hblk - 1, 0), 0))
                    ] + gate_specs + [slab] * 6
        scratch += [pltpu.VMEM((tm + HIST, POOL_WIDTH), F32), pltpu.VMEM((tm + HIST, CONV_CH), F32)]
        urows = 8
        blocks += 6 * _nbytes((2, tm, LANES), F32) + _nbytes((HIST, ACT_WIDTH), F32)
    scratch_bytes = (_nbytes((s, POOL_WIDTH + CONV_CH + GROUP_WIDTH), BF16) + 2 * _nbytes((tm + HIST, POOL_WIDTH), F32)
                     + 8 * _nbytes((tm, tc), F32))
    return pl.pallas_call(
        functools.partial(_branch_kernel, tm=tm, step=step),
        out_shape=(jax.ShapeDtypeStruct((b, s, D_MODEL), BF16),
                   jax.ShapeDtypeStruct((b, ni, urows, CONV_CH), F32)),
        grid=(b, D_MODEL // tc, ni),
        in_specs=in_specs + w_specs,
        out_specs=(pl.BlockSpec((1, tm, tc), lambda bi, c, i: (bi, i, c)),
                   pl.BlockSpec((1, 1, urows, CONV_CH), lambda bi, c, i: (bi, once(c, i), 0, 0))),
        scratch_shapes=scratch,
        compiler_params=pltpu.CompilerParams(
            dimension_semantics=("parallel", "arbitrary", "arbitrary"),
            vmem_limit_bytes=_vmem_limit(blocks, scratch_bytes)),
        name="branches_step" if step else "branches",
    )(*ins, wgrp, pscale, wpb, cw, wcb, wab)


def _oproj_kernel(m_ref, w_ref, x_ref, g_ref, gate_ref, o_ref):
    mix = _dot(m_ref[0], w_ref[0])
    o_ref[0] = x_ref[0] + _tail2(gate_ref) * _rms(mix, g_ref[0])


def _oproj(merged, w_o, x, norm, mod, l, *, tm):
    b, s, d = x.shape
    blocks = (_nbytes((tm, d), BF16) + _nbytes((d, d), BF16) + 2 * _nbytes((tm, d), F32)
              + _nbytes((_mod_rows(mod) + 1, d), F32))
    return pl.pallas_call(
        _oproj_kernel,
        out_shape=jax.ShapeDtypeStruct((b, s, d), F32),
        grid=(b, s // tm),
        in_specs=[
            pl.BlockSpec((1, tm, d), lambda bi, i: (bi, i, 0)),
            pl.BlockSpec((1, d, d), lambda bi, i: (l, 0, 0)),
            pl.BlockSpec((1, tm, d), lambda bi, i: (bi, i, 0)),
            norm[1](1), mod[1](2),
        ],
        out_specs=pl.BlockSpec((1, tm, d), lambda bi, i: (bi, i, 0)),
        compiler_params=pltpu.CompilerParams(
            dimension_semantics=("parallel", "parallel"),
            vmem_limit_bytes=_vmem_limit(blocks, _nbytes((tm, d), F32))),
        name="oproj",
    )(merged, w_o, x, norm[0], mod[0])


def _ffn_kernel(*refs, tm, step, nchunk):
    if step:
        (x_ref, g2_ref, sc_ref, sh_ref, wg_ref, wv_ref, cwg_ref, cwv_ref, wd_ref, g3_ref, gate_ref,
         stg_ref, stv_ref, o_ref, tg_ref, tv_ref, h_ref) = refs
        hist = 0
    else:
        (x_ref, xp_ref, g2_ref, sc_ref, sh_ref, wg_ref, wv_ref, cwg_ref, cwv_ref, wd_ref, g3_ref, gate_ref,
         o_ref, tg_ref, tv_ref, h_ref, ug_ref, uv_ref) = refs
        hist = HIST
    i = pl.program_id(1)
    j = pl.program_id(2)

    @pl.when(j == 0)
    def _():
        if not step:
            _modulate_rows(xp_ref, g2_ref, sc_ref, sh_ref, h_ref, 0, HIST)
        _modulate_rows(x_ref, g2_ref, sc_ref, sh_ref, h_ref, hist, tm)
        o_ref[...] = jnp.zeros_like(o_ref)

    cwg = cwg_ref[0]
    cwv = cwv_ref[0]
    if step:
        h = h_ref[...]
        up_g = _dot(h, wg_ref[0])
        up_v = _dot(h, wv_ref[0])
        uc_g = stg_ref[0, 0] * cwg[0:1] + stg_ref[0, 1] * cwg[1:2] + up_g * cwg[2:3]
        uc_v = stv_ref[0, 0] * cwv[0:1] + stv_ref[0, 1] * cwv[1:2] + up_v * cwv[2:3]
        tg_ref[0, 0] = up_g
        tv_ref[0, 0] = up_v
        o_ref[0] += _dot((_gelu_tanh(uc_g) * uc_v).astype(BF16), wd_ref[0])
    else:
        first = i == 0
        rc = tm // nchunk
        for c in range(nchunk):
            lo = 0 if c == 0 else HIST + c * rc
            hi = HIST + (c + 1) * rc
            h = h_ref[lo:hi]
            for u_ref, w_ref in ((ug_ref, wg_ref), (uv_ref, wv_ref)):
                up = _dot(h, w_ref[0])
                if c == 0:
                    u_ref[0:HIST] = jnp.where(first, 0.0, up[0:HIST])
                    u_ref[HIST:hi] = up[HIST:]
                else:
                    u_ref[lo:hi] = up

            def conv(ref, cw):
                base = HIST + c * rc
                return (ref[base - 2:base - 2 + rc] * cw[0:1] + ref[base - 1:base - 1 + rc] * cw[1:2]
                        + ref[base:base + rc] * cw[2:3])
            act = (_gelu_tanh(conv(ug_ref, cwg)) * conv(uv_ref, cwv)).astype(BF16)
            o_ref[0, c * rc:(c + 1) * rc] += _dot(act, wd_ref[0])
        tg_ref[0, 0] = ug_ref[HIST + tm - 8:HIST + tm]
        tv_ref[0, 0] = uv_ref[HIST + tm - 8:HIST + tm]

    @pl.when(j == pl.num_programs(2) - 1)
    def _():
        o_ref[0] = x_ref[0] + _tail2(gate_ref) * _rms(o_ref[0], g3_ref[0])


def _ffn(x, norm, mod, w_up, cw, w_down, state, l, *, tm, tf, step):
    b, s, d = x.shape
    ni, nj = s // tm, D_FF // tf
    trows = tm if step else 8
    w_specs = [
        pl.BlockSpec((1, d, tf), lambda bi, i, j: (l, 0, j)),
        pl.BlockSpec((1, d, tf), lambda bi, i, j: (l, 0, nj + j)),
        pl.BlockSpec((1, FFN_K, tf), lambda bi, i, j: (l, 0, j)),
        pl.BlockSpec((1, FFN_K, tf), lambda bi, i, j: (l, 0, nj + j)),
        pl.BlockSpec((1, tf, d), lambda bi, i, j: (l, j, 0)),
    ]
    x_spec = pl.BlockSpec((1, tm, d), lambda bi, i, j: (bi, i, 0), pipeline_mode=pl.Buffered(1))
    hist = 0 if step else HIST
    scratch = [pltpu.VMEM((tm + hist, d), BF16)]
    blocks = (2 * _nbytes((tm, d), F32) + 3 * _nbytes((d, tf), BF16) + 2 * _nbytes((trows, tf), F32)
              + _nbytes((3 * _mod_rows(mod) + 2, d), F32))
    mods = [norm[1](2), mod[1](4), mod[1](3)]
    tailp = [norm[1](3), mod[1](5)]
    if step:
        ins = [x, norm[0], mod[0], mod[0], w_up, w_up, cw, cw, w_down, norm[0], mod[0], state, state]
        in_specs = [x_spec] + mods + w_specs + tailp + [
            pl.BlockSpec((1, FFN_K - 1, tm, tf), lambda bi, i, j: (l, 0, 0, j)),
            pl.BlockSpec((1, FFN_K - 1, tm, tf), lambda bi, i, j: (l, 0, 0, nj + j))]
        blocks += 2 * _nbytes((FFN_K - 1, tm, tf), F32)
    else:
        hblk = tm // HIST
        ins = [x, x, norm[0], mod[0], mod[0], w_up, w_up, cw, cw, w_down, norm[0], mod[0]]
        in_specs = [x_spec,
                    pl.BlockSpec((1, HIST, d), lambda bi, i, j: (bi, jnp.maximum(i * hblk - 1, 0), 0))
                    ] + mods + w_specs + tailp
        scratch += [pltpu.VMEM((tm + HIST, tf), F32), pltpu.VMEM((tm + HIST, tf), F32)]
        blocks += _nbytes((HIST, d), F32)
    scratch_bytes = _nbytes((tm + hist, d), BF16) + 6 * _nbytes((tm + hist, tf), F32)
    tail = jax.ShapeDtypeStruct((b, ni, trows, D_FF), F32)
    tail_spec = lambda: pl.BlockSpec((1, 1, trows, tf), lambda bi, i, j: (bi, i, 0, j))
    return pl.pallas_call(
        functools.partial(_ffn_kernel, tm=tm, step=step, nchunk=ROW_CHUNKS),
        out_shape=(jax.ShapeDtypeStruct((b, s, d), F32), tail, tail),
        grid=(b, ni, nj),
        in_specs=in_specs,
        out_specs=(pl.BlockSpec((1, tm, d), lambda bi, i, j: (bi, i, 0)), tail_spec(), tail_spec()),
        scratch_shapes=scratch,
        compiler_params=pltpu.CompilerParams(
            dimension_semantics=("parallel", "parallel", "arbitrary"),
            vmem_limit_bytes=_vmem_limit(blocks, scratch_bytes)),
        name="ffn_step" if step else "ffn",
    )(*ins)


def _group_rel_bias(rel_bias):
    n = np.arange(N_DIL_KEYS + 1)
    max_exact = N_BUCKETS // 2
    out = []
    for g, (_, d) in enumerate(ATTN_GROUPS):
        dist = n * d
        large = max_exact + (np.log(np.maximum(dist, 1) / max_exact) / np.log(MAX_DISTANCE / max_exact)
                             * (N_BUCKETS - max_exact)).astype(np.int32)
        bucket = np.where(dist < max_exact, dist, np.minimum(large, N_BUCKETS - 1))
        out.append(rel_bias[bucket, g * HEADS_PER_GROUP:(g + 1) * HEADS_PER_GROUP].T)
    return jnp.stack(out)


def _bias_tables(rel_bias):
    bias_g = _group_rel_bias(rel_bias).astype(F32)
    n = N_DIL_KEYS
    gh = (N_ATTN_GROUPS, HEADS_PER_GROUP)
    ext = jnp.concatenate([bias_g[:, :, ::-1], jnp.full(gh + (n,), NEG_INF, F32)], axis=-1)
    tab = jnp.tile(ext, (1, 1, n))[:, :, :n * 2 * n].reshape(N_ATTN_GROUPS, HEADS_PER_GROUP * n, 2 * n)
    step = []
    for g, (_, d) in enumerate(ATTN_GROUPS):
        hit = bias_g[g, :, :0:-1, None]
        row = jnp.concatenate([hit, jnp.full((HEADS_PER_GROUP, n, d - 1), NEG_INF, F32)], axis=-1)
        new = jnp.broadcast_to(bias_g[g, :, 0:1], (HEADS_PER_GROUP, n * d))
        step.append(jnp.concatenate([row.reshape(HEADS_PER_GROUP, n * d), new], axis=0))
    return tab, step


def _kv_pack(qkv, g, nrows):
    b = qkv.shape[0]
    k0 = ATTN_WIDTH + g * GROUP_WIDTH
    v0 = 2 * ATTN_WIDTH + g * GROUP_WIDTH
    k = qkv[:, -nrows:, k0:k0 + GROUP_WIDTH].reshape(b, nrows, HEADS_PER_GROUP, HEAD_DIM)
    v = qkv[:, -nrows:, v0:v0 + GROUP_WIDTH].reshape(b, nrows, HEADS_PER_GROUP, HEAD_DIM)
    return jnp.stack([k, v], axis=2)


def _kv_tail_kernel(*refs):
    n = len(refs) // 3
    for g in range(n):
        k_ref, v_ref, o_ref = refs[2 * g], refs[2 * g + 1], refs[2 * n + g]
        o_ref[0, 0] = k_ref[0].T
        o_ref[0, 1] = v_ref[0].T


def _kv_tails(qkv):
    b, s, _ = qkv.shape
    ins, in_specs, out_shape, out_specs, blocks = [], [], [], [], 0
    for g, (w, _) in enumerate(ATTN_GROUPS):
        nrows = min(w, s)
        assert s % nrows == 0
        for part in (1, 2):
            col = part * (ATTN_WIDTH // GROUP_WIDTH) + g
            ins.append(qkv)
            in_specs.append(pl.BlockSpec((1, nrows, GROUP_WIDTH), lambda bi, col=col, rb=s // nrows - 1: (bi, rb, col)))
        out_shape.append(jax.ShapeDtypeStruct((b, 2, GROUP_WIDTH, nrows), F32))
        out_specs.append(pl.BlockSpec((1, 2, GROUP_WIDTH, nrows), lambda bi: (bi, 0, 0, 0)))
        blocks += 4 * _nbytes((nrows, GROUP_WIDTH), F32)
    outs = pl.pallas_call(
        _kv_tail_kernel,
        out_shape=out_shape,
        grid=(b,),
        in_specs=in_specs,
        out_specs=out_specs,
        compiler_params=pltpu.CompilerParams(
            dimension_semantics=("parallel",), vmem_limit_bytes=_vmem_limit(blocks, blocks // 2)),
        name="kv_tails",
    )(*ins)
    return [o.reshape(b, 2, HEADS_PER_GROUP, HEAD_DIM, o.shape[-1]).transpose(0, 4, 1, 2, 3) for o in outs]


def _prompt_layer(x, l, P):
    b, s, _ = x.shape
    norm, mod = P["norm"](l), P["mod_p"](l)
    tm = 512
    qkv, qkv_slab, h = _proj(x, norm, mod, P["w_qkv"], l, tm=1024, tn=ATTN_WIDTH)
    act = _matmul(h, P["w_act"], l, tm=1024, tn=1024)
    ols = [_attention(qkv_slab, P["bias_tab"], g) for g in range(N_ATTN_GROUPS)]
    attn = [o for o, _ in ols] + [ls for _, ls in ols]
    merged, u_tail = _branches(act, h, attn, None, P["branch"], l, tm=tm, tc=512, step=False)
    x1 = _oproj(merged, P["w_o"], x, norm, mod, l, tm=tm)
    x2, tail_g, tail_v = _ffn(x1, norm, mod, P["w_up"], P["ffn_cw"], P["w_down"], None, l,
                              tm=1024, tf=512, step=False)
    new_kv = _kv_tails(qkv)
    new_pool = act[:, -POOL_STATE:, :POOL_WIDTH]
    new_conv = u_tail[:, -1, -(CONV_K - 1):]
    new_ffn = jnp.concatenate([tail_g[:, -1, -(FFN_K - 1):], tail_v[:, -1, -(FFN_K - 1):]], axis=-1)
    return x2, (new_kv[0], new_kv[1], new_kv[2], new_pool, new_conv, new_ffn)


def _pad_rows(a, axis):
    pad = [(0, 0)] * a.ndim
    pad[axis] = (0, SAMPLE_ROWS - a.shape[axis])
    return jnp.pad(a, pad)


def _sample_layer(x, l, P, S, nb):
    norm, mod = P["norm"](l), P["mod_s"](l)
    tm = SAMPLE_ROWS
    qkv, _, h = _proj(x, norm, mod, P["w_qkv"], l, tm=tm, tn=ATTN_WIDTH)
    act = _matmul(h, P["w_act"], l, tm=tm, tn=1024)
    heads = qkv[0, :nb].reshape(nb, 3 * N_ATTN_GROUPS, HEADS_PER_GROUP, HEAD_DIM, 1)
    attn = _attention_step(heads, S["caches"], P["bias_step"], l, nb)
    attn = _pad_rows(attn.reshape(1, nb, GROUP_WIDTH), 1)
    merged, u = _branches(act, h, attn, (S["pool_t"], S["conv_t"]), P["branch"], l, tm=tm, tc=512, step=True)
    x1 = _oproj(merged, P["w_o"], x, norm, mod, l, tm=tm)
    x2, up_g, up_v = _ffn(x1, norm, mod, P["w_up"], P["ffn_cw"], P["w_down"], S["ffn_t"], l,
                          tm=tm, tf=512, step=True)
    new_kv = [_kv_pack(qkv[0, :nb, None], g, 1) for g in range(N_ATTN_GROUPS)]
    new_pool = jnp.concatenate([S["pool"][l][:, 1:], act[0, :nb, None, :POOL_WIDTH]], axis=1)
    new_conv = jnp.concatenate([S["conv"][l][:, 1:], u[0, 0, :nb, None]], axis=1)
    up_new = jnp.concatenate([up_g[0, 0, :nb], up_v[0, 0, :nb]], axis=-1)
    new_ffn = jnp.concatenate([S["ffn"][l][:, 1:], up_new[:, None]], axis=1)
    return x2, (new_kv[0], new_kv[1], new_kv[2], new_pool, new_conv, new_ffn)


def kernel(x_prompt, x_sample, c_prompt, c_sample, cache_kv_w128, cache_kv_w512, cache_kv_w2048, state_pool, state_conv, state_ffn_conv, rel_bias, norm_g, w_ada, b_ada, w_in, w_attn_br, w_pool_grp, pool_scale, w_pool_br, conv_w, w_conv_br, w_o, w_up, ffn_conv_w, w_down):
    nbp = x_prompt.shape[0]
    nbs, tdec, _ = x_sample.shape
    assert tdec == 1 and nbs <= SAMPLE_ROWS
    caches = (cache_kv_w128, cache_kv_w512, cache_kv_w2048)
    for (w, d), c in zip(ATTN_GROUPS, caches):
        assert c.shape[2] == w == N_DIL_KEYS * d, "cache must hold exactly one window"

    bias_tab, bias_step = _bias_tables(rel_bias)
    c_rows = -(-(SAMPLE_ROWS + nbp) // 8) * 8
    c_all = jnp.zeros((c_rows, D_MODEL), F32).at[:nbs].set(c_sample).at[SAMPLE_ROWS:SAMPLE_ROWS + nbp].set(c_prompt)
    mod_all = _ada(c_all, w_ada, b_ada)

    w_qkv, w_act, w_gate = _cast_split(
        w_in, [(0, QKV_WIDTH), (QKV_WIDTH, QKV_WIDTH + ACT_WIDTH), (QKV_WIDTH + ACT_WIDTH, w_in.shape[2])], tk=256)
    P = dict(
        norm=functools.partial(_norm_view, norm_g),
        mod_p=functools.partial(_mod_view, mod_all, prompt=True),
        mod_s=functools.partial(_mod_view, mod_all, prompt=False),
        w_qkv=w_qkv, w_act=w_act,
        branch=(w_pool_grp.astype(BF16), pool_scale[:, None, :], w_pool_br.astype(BF16),
                conv_w, w_conv_br.astype(BF16), w_attn_br.astype(BF16), w_gate),
        w_o=w_o.astype(BF16), w_up=w_up.astype(BF16), ffn_cw=ffn_conv_w, w_down=w_down.astype(BF16),
        bias_tab=bias_tab, bias_step=bias_step,
    )
    S = dict(
        caches=[c.transpose(0, 1, 3, 4, 5, 2) for c in caches],
        pool=state_pool, conv=state_conv, ffn=state_ffn_conv,
        pool_t=_pad_rows(state_pool.transpose(0, 2, 1, 3), 2),
        conv_t=_pad_rows(state_conv.transpose(0, 2, 1, 3), 2),
        ffn_t=_pad_rows(state_ffn_conv.transpose(0, 2, 1, 3), 2),
    )
    yp = x_prompt
    ys = _pad_rows(x_sample.reshape(1, nbs, D_MODEL), 1)
    st_p, st_s = [], []
    for l in range(DEPTH):
        yp, sp = _prompt_layer(yp, l, P)
        ys, ss = _sample_layer(ys, l, P, S, nbs)
        st_p.append(sp)
        st_s.append(ss)
    outs_p = [jnp.stack([s[k] for s in st_p]) for k in range(6)]
    outs_s = [jnp.stack([s[k] for s in st_s]) for k in range(6)]
    return (yp, ys[0, :nbs, None, :], *outs_p, *outs_s)
```

```python
import functools

import numpy as np
import jax
import jax.numpy as jnp
from jax import lax
from jax.experimental import pallas as pl
from jax.experimental.pallas import tpu as pltpu

F32 = jnp.float32
BF16 = jnp.bfloat16

D_MODEL = 2048
DEPTH = 2
HEAD_DIM = 64
HEADS_PER_GROUP = 4
ATTN_GROUPS = ((128, 1), (512, 4), (2048, 16))
N_ATTN_GROUPS = len(ATTN_GROUPS)
ATTN_WIDTH = N_ATTN_GROUPS * HEADS_PER_GROUP * HEAD_DIM
GROUP_WIDTH = HEADS_PER_GROUP * HEAD_DIM
N_DIL_KEYS = 128
N_BUCKETS = 32
MAX_DISTANCE = 2048
ATTN_SCALE = HEAD_DIM ** -0.5
POOL_WINDOWS = (2, 4, 8, 16)
POOL_GROUP = 128
POOL_WIDTH = 512
POOL_STATE = 15
CONV_CH = 512
CONV_K = 3
D_FF = 5632
FFN_K = 3
N_MOD = 6
N_NORM = 4
EPS = 1e-6
NEG_INF = -1e30

LANES = 128
HIST = 16
SAMPLE_ROWS = 16
QKV_WIDTH = 3 * ATTN_WIDTH
QKV_SLABS = QKV_WIDTH // LANES
ACT_WIDTH = POOL_WIDTH + 3 * CONV_CH
ATTN_CLASS_UNROLL = 4
ROW_CHUNKS = 2
VMEM_CAP = 56 * 1024 * 1024


def _vmem_limit(block_bytes, scratch_bytes=0):
    del block_bytes, scratch_bytes
    return VMEM_CAP


def _nbytes(shape, dtype):
    return int(np.prod(shape)) * jnp.dtype(dtype).itemsize


def _rms(x, g):
    return x * lax.rsqrt(jnp.mean(x * x, axis=-1, keepdims=True) + EPS) * g


def _sigmoid(x):
    return 0.5 + 0.5 * jnp.tanh(0.5 * x)


def _gelu_tanh(x):
    return 0.5 * x * (1.0 + jnp.tanh(np.sqrt(2.0 / np.pi) * (x + 0.044715 * (x * x * x))))


def _dot(a, b):
    return jnp.dot(a, b, preferred_element_type=F32)


def _tail2(ref):
    return ref[(0,) * (len(ref.shape) - 2)]


def _norm_view(norm_g, l):
    arr = norm_g.reshape(DEPTH * N_NORM, 1, D_MODEL)
    return arr, lambda k: pl.BlockSpec((1, 1, D_MODEL), lambda *_: (l * N_NORM + k, 0, 0))


def _mod_view(mod_all, l, prompt):
    if prompt:
        arr = mod_all.reshape(DEPTH, mod_all.shape[1], N_MOD, 1, D_MODEL)
        return arr, lambda k: pl.BlockSpec((1, 1, 1, 1, D_MODEL), lambda bi, *_: (l, SAMPLE_ROWS + bi, k, 0, 0))
    return mod_all, lambda k: pl.BlockSpec((1, SAMPLE_ROWS, D_MODEL), lambda bi, *_: (l, 0, k))


def _mod_rows(mod):
    return mod[1](0).block_shape[-2]


def _cast_split_kernel(w_ref, *o_refs, bounds):
    for o_ref, (lo, hi) in zip(o_refs, bounds):
        o_ref[...] = w_ref[:, :, lo:hi].astype(BF16)


def _cast_split(w, bounds, *, tk):
    nl, k, n = w.shape
    blocks = _nbytes((tk, n), F32) + sum(_nbytes((tk, hi - lo), BF16) for lo, hi in bounds)
    return pl.pallas_call(
        functools.partial(_cast_split_kernel, bounds=tuple(bounds)),
        out_shape=[jax.ShapeDtypeStruct((nl, k, hi - lo), BF16) for lo, hi in bounds],
        grid=(nl, k // tk),
        in_specs=[pl.BlockSpec((1, tk, n), lambda li, ki: (li, ki, 0))],
        out_specs=[pl.BlockSpec((1, tk, hi - lo), lambda li, ki: (li, ki, 0)) for lo, hi in bounds],
        compiler_params=pltpu.CompilerParams(
            dimension_semantics=("parallel", "parallel"), vmem_limit_bytes=_vmem_limit(blocks)),
        name="cast_split",
    )(w)


def _ada_kernel(c_ref, w_ref, b_ref, o_ref):
    c = c_ref[...]
    s = (c * _sigmoid(c)).astype(BF16)
    o_ref[0] = _dot(s, w_ref[0].astype(BF16)) + b_ref[0]


def _ada(c_all, w_ada, b_ada):
    rows = c_all.shape[0]
    n = w_ada.shape[-1]
    tn = 1024
    blocks = _nbytes((rows, D_MODEL), F32) + _nbytes((D_MODEL, tn), F32) + _nbytes((rows + 1, tn), F32)
    return pl.pallas_call(
        _ada_kernel,
        out_shape=jax.ShapeDtypeStruct((DEPTH, rows, n), F32),
        grid=(DEPTH, n // tn),
        in_specs=[
            pl.BlockSpec((rows, D_MODEL), lambda l, j: (0, 0)),
            pl.BlockSpec((1, D_MODEL, tn), lambda l, j: (l, 0, j)),
            pl.BlockSpec((1, 1, tn), lambda l, j: (l, 0, j)),
        ],
        out_specs=pl.BlockSpec((1, rows, tn), lambda l, j: (l, 0, j)),
        compiler_params=pltpu.CompilerParams(
            dimension_semantics=("parallel", "parallel"),
            vmem_limit_bytes=_vmem_limit(blocks, _nbytes((D_MODEL, tn), BF16))),
        name="ada",
    )(c_all, w_ada, b_ada.reshape(DEPTH, 1, n))


def _modulate_rows(x_ref, g_ref, sc_ref, sh_ref, h_ref, row0, tm):
    y = _rms(x_ref[0], g_ref[0])
    h_ref[row0:row0 + tm] = (y * (1.0 + _tail2(sc_ref)) + _tail2(sh_ref)).astype(BF16)


def _proj_kernel(x_ref, g_ref, sc_ref, sh_ref, w_ref, o_ref, slab_ref, h_ref, *, slabs, tm, nchunk):
    def project(r):
        res = _dot(h_ref[0, r], w_ref[0])
        o_ref[0, r] = res
        for s in range(slabs):
            slab_ref[0, s, r] = res[:, s * LANES:(s + 1) * LANES]

    @pl.when(pl.program_id(2) == 0)
    def _():
        rc = tm // nchunk
        for c in range(nchunk):
            r = slice(c * rc, (c + 1) * rc)
            y = _rms(x_ref[0, r], g_ref[0])
            h_ref[0, r] = (y * (1.0 + _tail2(sc_ref)) + _tail2(sh_ref)).astype(BF16)
            project(r)

    @pl.when(pl.program_id(2) > 0)
    def _():
        project(slice(None))


def _proj(x, norm, mod, w, l, *, tm, tn):
    b, s, d = x.shape
    n = w.shape[2]
    slabs = tn // LANES
    r = _mod_rows(mod)
    blocks = (_nbytes((tm, d), F32) + _nbytes((2 * r + 1, d), F32) + _nbytes((d, tn), BF16)
              + 2 * _nbytes((tm, tn), F32) + _nbytes((tm, d), BF16))
    return pl.pallas_call(
        functools.partial(_proj_kernel, slabs=slabs, tm=tm, nchunk=2 * ROW_CHUNKS if r == 1 else 1),
        out_shape=[jax.ShapeDtypeStruct((b, s, n), F32),
                   jax.ShapeDtypeStruct((b, n // LANES, s, LANES), F32),
                   jax.ShapeDtypeStruct((b, s, d), BF16)],
        grid=(b, s // tm, n // tn),
        in_specs=[
            pl.BlockSpec((1, tm, d), lambda bi, i, j: (bi, i, 0)),
            norm[1](0), mod[1](1), mod[1](0),
            pl.BlockSpec((1, d, tn), lambda bi, i, j: (l, 0, j)),
        ],
        out_specs=[pl.BlockSpec((1, tm, tn), lambda bi, i, j: (bi, i, j)),
                   pl.BlockSpec((1, slabs, tm, LANES), lambda bi, i, j: (bi, j, i, 0)),
                   pl.BlockSpec((1, tm, d), lambda bi, i, j: (bi, i, 0))],
        compiler_params=pltpu.CompilerParams(
            dimension_semantics=("parallel", "parallel", "arbitrary"),
            vmem_limit_bytes=_vmem_limit(blocks, _nbytes((tm, tn), F32) + _nbytes((tm, d), F32))),
        name="proj_norm",
    )(x, norm[0], mod[0], mod[0], w)


def _matmul_kernel(h_ref, w_ref, o_ref):
    o_ref[0] = _dot(h_ref[0], w_ref[0])


def _matmul(h, w, l, *, tm, tn):
    b, s, d = h.shape
    n = w.shape[2]
    blocks = _nbytes((tm, d), BF16) + _nbytes((d, tn), BF16) + _nbytes((tm, tn), F32)
    return pl.pallas_call(
        _matmul_kernel,
        out_shape=jax.ShapeDtypeStruct((b, s, n), F32),
        grid=(b, s // tm, n // tn),
        in_specs=[pl.BlockSpec((1, tm, d), lambda bi, i, j: (bi, i, 0)),
                  pl.BlockSpec((1, d, tn), lambda bi, i, j: (l, 0, j))],
        out_specs=pl.BlockSpec((1, tm, tn), lambda bi, i, j: (bi, i, j)),
        compiler_params=pltpu.CompilerParams(
            dimension_semantics=("parallel", "parallel", "parallel"),
            vmem_limit_bytes=_vmem_limit(blocks, _nbytes((tm, tn), F32))),
        name="proj_act",
    )(h, w)


def _head_masks(rows):
    lane = lax.broadcasted_iota(jnp.int32, (rows, GROUP_WIDTH), 1)
    return [(lane >= h * HEAD_DIM) & (lane < (h + 1) * HEAD_DIM) for h in range(HEADS_PER_GROUP)]


def _attn_kernel(q_ref, kc_ref, kp_ref, vc_ref, vp_ref, bias_ref, o_ref, lse_ref, edge_ref, *, d, sb):
    i = pl.program_id(1)
    nq = N_DIL_KEYS
    span = nq * d
    ncb = sb // span
    hm = _head_masks(1)
    col = lax.broadcasted_iota(jnp.int32, (1, 2 * nq), 1)
    edge_ref[...] = jnp.where((col < nq) & (i == 0), NEG_INF, bias_ref[0])

    def rows(start):
        return pl.ds(start, nq, stride=d) if d > 1 else pl.ds(start, nq)

    def load(ref, start):
        return jnp.concatenate([ref[0, s, rows(start), :] for s in range(2)], axis=1)

    def one_block(r, jb):
        qs = jb * span + r
        q = load(q_ref, qs) * ATTN_SCALE
        if jb == 0:
            lo = sb - span + r
            k_lo, v_lo = load(kp_ref, lo), load(vp_ref, lo)
        else:
            lo = (jb - 1) * span + r
            k_lo, v_lo = load(kc_ref, lo), load(vc_ref, lo)
        kcat = jnp.concatenate([k_lo, load(kc_ref, qs)], axis=0).astype(BF16)
        vcat = jnp.concatenate([v_lo, load(vc_ref, qs)], axis=0).astype(BF16)
        qm = jnp.concatenate([jnp.where(hm[h], q, 0.0) for h in range(HEADS_PER_GROUP)], axis=0).astype(BF16)
        s = lax.dot_general(qm, kcat, (((1,), (1,)), ((), ())), preferred_element_type=F32)
        s = s + (edge_ref[...] if jb == 0 else bias_ref[0])
        m = jnp.max(s, axis=-1, keepdims=True)
        p = jnp.exp(s - m)
        l = jnp.sum(p, axis=-1, keepdims=True)
        oall = _dot((p * (1.0 / l)).astype(BF16), vcat)
        lse = m + jnp.log(l)
        o = jnp.zeros((nq, GROUP_WIDTH), F32)
        ls = jnp.zeros((nq, GROUP_WIDTH), F32)
        for h in range(HEADS_PER_GROUP):
            o = jnp.where(hm[h], oall[h * nq:(h + 1) * nq], o)
            ls = jnp.where(hm[h], lse[h * nq:(h + 1) * nq], ls)
        for sl in range(2):
            o_ref[0, sl, rows(qs), :] = o[:, sl * LANES:(sl + 1) * LANES]
            lse_ref[0, sl, rows(qs), :] = ls[:, sl * LANES:(sl + 1) * LANES]

    if d == 1:
        for jb in range(ncb):
            one_block(0, jb)
    else:
        def body(r, carry):
            for jb in range(ncb):
                one_block(r, jb)
            return carry
        lax.fori_loop(0, d, body, 0, unroll=ATTN_CLASS_UNROLL)


def _attention(qkv, bias_tab, g):
    b, _, s, _ = qkv.shape
    d = ATTN_GROUPS[g][1]
    sb = max(N_DIL_KEYS * d, 512)
    blk = (1, 2, sb, LANES)
    kslab, vslab = ATTN_WIDTH // GROUP_WIDTH + g, 2 * ATTN_WIDTH // GROUP_WIDTH + g
    prev = lambda i: jnp.maximum(i - 1, 0)
    blocks = 7 * _nbytes(blk, F32) + _nbytes((4 * N_DIL_KEYS, 2 * N_DIL_KEYS), F32)
    out = jax.ShapeDtypeStruct((b, 2, s, LANES), F32)
    return pl.pallas_call(
        functools.partial(_attn_kernel, d=d, sb=sb),
        out_shape=(out, out),
        grid=(b, s // sb),
        in_specs=[
            pl.BlockSpec(blk, lambda bi, i: (bi, g, i, 0)),
            pl.BlockSpec(blk, lambda bi, i: (bi, kslab, i, 0)),
            pl.BlockSpec(blk, lambda bi, i: (bi, kslab, prev(i), 0)),
            pl.BlockSpec(blk, lambda bi, i: (bi, vslab, i, 0)),
            pl.BlockSpec(blk, lambda bi, i: (bi, vslab, prev(i), 0)),
            pl.BlockSpec((1, 4 * N_DIL_KEYS, 2 * N_DIL_KEYS), lambda bi, i: (g, 0, 0)),
        ],
        out_specs=(pl.BlockSpec(blk, lambda bi, i: (bi, 0, i, 0)),
                   pl.BlockSpec(blk, lambda bi, i: (bi, 0, i, 0))),
        scratch_shapes=[pltpu.VMEM((HEADS_PER_GROUP * N_DIL_KEYS, 2 * N_DIL_KEYS), F32)],
        compiler_params=pltpu.CompilerParams(
            dimension_semantics=("parallel", "parallel"),
            vmem_limit_bytes=_vmem_limit(blocks, 8 << 20)),
        name=f"attn_d{d}",
    )(qkv, qkv, qkv, qkv, qkv, bias_tab)


def _attn_step_kernel(qkv_ref, c0_ref, c1_ref, c2_ref, b0_ref, b1_ref, b2_ref, o_ref):
    outs = [[None] * N_ATTN_GROUPS for _ in range(HEADS_PER_GROUP)]
    lses = [[None] * N_ATTN_GROUPS for _ in range(HEADS_PER_GROUP)]
    for g, (c_ref, b_ref) in enumerate(((c0_ref, b0_ref), (c1_ref, b1_ref), (c2_ref, b2_ref))):
        for h in range(HEADS_PER_GROUP):
            q = qkv_ref[0, g, h] * ATTN_SCALE
            kn = qkv_ref[0, N_ATTN_GROUPS + g, h]
            vn = qkv_ref[0, 2 * N_ATTN_GROUPS + g, h]
            s_c = jnp.sum(c_ref[0, 0, 0, h] * q, axis=0, keepdims=True) + b_ref[h:h + 1, :]
            s_n = jnp.sum(kn * q, axis=0, keepdims=True) + b_ref[HEADS_PER_GROUP + h:HEADS_PER_GROUP + h + 1, 0:1]
            m = jnp.maximum(jnp.max(s_c, axis=-1, keepdims=True), s_n)
            p_c = jnp.exp(s_c - m)
            p_n = jnp.exp(s_n - m)
            l = jnp.sum(p_c, axis=-1, keepdims=True) + p_n
            inv = 1.0 / l
            outs[h][g] = jnp.sum(c_ref[0, 0, 1, h] * (p_c * inv), axis=-1, keepdims=True) + (p_n * inv) * vn
            lses[h][g] = m + jnp.log(l)
    for h in range(HEADS_PER_GROUP):
        ls, os_ = lses[h], outs[h]
        mx = jnp.maximum(jnp.maximum(ls[0], ls[1]), ls[2])
        w = [jnp.exp(x - mx) for x in ls]
        tot = w[0] + w[1] + w[2]
        o_ref[0, h] = (w[0] * os_[0] + w[1] * os_[1] + w[2] * os_[2]) * (1.0 / tot)


def _attention_step(qkv, caches, biases, l, nb):
    cache_specs = [pl.BlockSpec((1, 1) + c.shape[2:], lambda bi: (l, bi, 0, 0, 0, 0)) for c in caches]
    bias_specs = [pl.BlockSpec(b.shape, lambda bi: (0, 0)) for b in biases]
    blocks = (sum(_nbytes(c.shape[2:], F32) for c in caches) + sum(_nbytes(b.shape, F32) for b in biases)
              + 13 * HEADS_PER_GROUP * HEAD_DIM * LANES * 4)
    return pl.pallas_call(
        _attn_step_kernel,
        out_shape=jax.ShapeDtypeStruct((nb, HEADS_PER_GROUP, HEAD_DIM, 1), F32),
        grid=(nb,),
        in_specs=[pl.BlockSpec((1,) + qkv.shape[1:], lambda bi: (bi, 0, 0, 0, 0))] + cache_specs + bias_specs,
        out_specs=pl.BlockSpec((1, HEADS_PER_GROUP, HEAD_DIM, 1), lambda bi: (bi, 0, 0, 0)),
        compiler_params=pltpu.CompilerParams(
            dimension_semantics=("parallel",),
            vmem_limit_bytes=_vmem_limit(blocks, 8 << 20)),
        name="attn_step",
    )(qkv, *caches, *biases)


def _branch_kernel(*refs, tm, step):
    if step:
        (act_ref, pst_ref, cst_ref, h_ref, wga_ref, wgb_ref, wgc_ref, at_ref,
         wgrp_ref, pscale_ref, wpb_ref, cw_ref, wcb_ref, wab_ref,
         out_ref, u_ref, pzs_ref, cb_ref, comb_ref) = refs
    else:
        (act_ref, hist_ref, h_ref, wga_ref, wgb_ref, wgc_ref, o0_ref, o1_ref, o2_ref, l0_ref, l1_ref, l2_ref,
         wgrp_ref, pscale_ref, wpb_ref, cw_ref, wcb_ref, wab_ref,
         out_ref, u_ref, pzs_ref, cb_ref, comb_ref, pe_ref, ue_ref) = refs
    c_tile = pl.program_id(1)
    i = pl.program_id(2)
    rows = pl.ds(pl.multiple_of(i * tm, tm), tm)
    pzs_ref, cb_ref, comb_ref = pzs_ref.at[rows], cb_ref.at[rows], comb_ref.at[rows]

    def sequence_mixers():
        p = act_ref[0, :, 0:POOL_WIDTH]
        gate_b = act_ref[0, :, POOL_WIDTH:POOL_WIDTH + CONV_CH]
        u = act_ref[0, :, POOL_WIDTH + CONV_CH:POOL_WIDTH + 2 * CONV_CH] * \
            act_ref[0, :, POOL_WIDTH + 2 * CONV_CH:POOL_WIDTH + 3 * CONV_CH]
        cw = cw_ref[0]
        if step:
            acc = p
            sums = {}
            for k in range(1, max(POOL_WINDOWS)):
                acc = acc + pst_ref[0, POOL_STATE - k]
                sums[k + 1] = acc
            means = [sums[w][:, gi * POOL_GROUP:(gi + 1) * POOL_GROUP] * (1.0 / w)
                     for gi, w in enumerate(POOL_WINDOWS)]
            conv = cst_ref[0, 0] * cw[0:1] + cst_ref[0, 1] * cw[1:2] + u * cw[2:3]
            u_ref[0, 0] = u
            comb_ref[...] = at_ref[0].astype(BF16)
        else:
            first = i == 0
            hist_p = hist_ref[0, :, 0:POOL_WIDTH]
            hist_u = hist_ref[0, :, POOL_WIDTH + CONV_CH:POOL_WIDTH + 2 * CONV_CH] * \
                hist_ref[0, :, POOL_WIDTH + 2 * CONV_CH:POOL_WIDTH + 3 * CONV_CH]
            pe_ref[0:HIST] = jnp.where(first, 0.0, hist_p)
            ue_ref[0:HIST] = jnp.where(first, 0.0, hist_u)
            pe_ref[HIST:HIST + tm] = p
            ue_ref[HIST:HIST + tm] = u
            t = i * tm + lax.broadcasted_iota(jnp.int32, (tm, 1), 0)
            means = []
            for gi, w in enumerate(POOL_WINDOWS):
                cs = slice(gi * POOL_GROUP, (gi + 1) * POOL_GROUP)
                acc = pe_ref[HIST:HIST + tm, cs]
                for k in range(1, w):
                    acc = acc + pe_ref[HIST - k:HIST - k + tm, cs]
                cnt = jnp.minimum(t + 1, w).astype(F32)
                means.append(acc * (1.0 / cnt))
            conv = (ue_ref[HIST - 2:HIST - 2 + tm] * cw[0:1] + ue_ref[HIST - 1:HIST - 1 + tm] * cw[1:2]
                    + u * cw[2:3])
            u_ref[0, 0] = ue_ref[HIST + tm - 8:HIST + tm]
            for sl in range(2):
                ls = [r[0, sl] for r in (l0_ref, l1_ref, l2_ref)]
                os_ = [r[0, sl] for r in (o0_ref, o1_ref, o2_ref)]
                mx = jnp.maximum(jnp.maximum(ls[0], ls[1]), ls[2])
                w_ = [jnp.exp(x - mx) for x in ls]
                tot = w_[0] + w_[1] + w_[2]
                comb = (w_[0] * os_[0] + w_[1] * os_[1] + w_[2] * os_[2]) * (1.0 / tot)
                comb_ref[:, sl * LANES:(sl + 1) * LANES] = comb.astype(BF16)
        pscale = pscale_ref[0]
        for gi in range(len(POOL_WINDOWS)):
            cs = slice(gi * POOL_GROUP, (gi + 1) * POOL_GROUP)
            pm = (means[gi] - p[:, cs]).astype(BF16)
            pz = _dot(pm, wgrp_ref[0, gi])
            pzs_ref[:, cs] = (pz * pscale[:, cs]).astype(BF16)
        cb_ref[...] = (gate_b * conv).astype(BF16)

    def merge_tile():
        h = h_ref[0]
        merged = _sigmoid(_dot(h, wga_ref[0])) * _dot(pzs_ref[...], wpb_ref[0])
        merged += _sigmoid(_dot(h, wgb_ref[0])) * _dot(cb_ref[...], wcb_ref[0])
        merged += _sigmoid(_dot(h, wgc_ref[0])) * _dot(comb_ref[...], wab_ref[0])
        out_ref[0] = merged.astype(BF16)

    @pl.when(c_tile == 0)
    def _():
        sequence_mixers()
        merge_tile()

    @pl.when(c_tile > 0)
    def _():
        merge_tile()


def _branches(act, h, attn, states, weights, l, *, tm, tc, step):
    b, s, _ = act.shape
    wgrp, pscale, wpb, cw, wcb, wab, wgate = weights
    ni = s // tm
    once = lambda c, i: jnp.where(c == 0, i, ni - 1)
    gate_specs = [pl.BlockSpec((1, tm, D_MODEL), lambda bi, c, i: (bi, i, 0))] + [
        pl.BlockSpec((1, D_MODEL, tc), lambda bi, c, i, k=k: (l, 0, k * (D_MODEL // tc) + c)) for k in range(3)]
    w_specs = [
        pl.BlockSpec((1,) + wgrp.shape[1:], lambda bi, c, i: (l, 0, 0, 0)),
        pl.BlockSpec((1,) + pscale.shape[1:], lambda bi, c, i: (l, 0, 0)),
        pl.BlockSpec((1, POOL_WIDTH, tc), lambda bi, c, i: (l, 0, c)),
        pl.BlockSpec((1,) + cw.shape[1:], lambda bi, c, i: (l, 0, 0)),
        pl.BlockSpec((1, CONV_CH, tc), lambda bi, c, i: (l, 0, c)),
        pl.BlockSpec((1, GROUP_WIDTH, tc), lambda bi, c, i: (l, 0, c)),
    ]
    act_spec = pl.BlockSpec((1, tm, ACT_WIDTH), lambda bi, c, i: (bi, once(c, i), 0))
    scratch = [pltpu.VMEM((s, POOL_WIDTH), BF16), pltpu.VMEM((s, CONV_CH), BF16),
               pltpu.VMEM((s, GROUP_WIDTH), BF16)]
    blocks = (_nbytes((tm, ACT_WIDTH), F32) + _nbytes((tm + 3 * tc, D_MODEL), BF16) + _nbytes(wgrp.shape[1:], BF16)
              + _nbytes((POOL_WIDTH + CONV_CH + GROUP_WIDTH, tc), BF16) + _nbytes((tm, tc), BF16))
    if step:
        pst, cst = states
        ins = [act, pst, cst, h, wgate, wgate, wgate, attn]
        in_specs = [act_spec,
                    pl.BlockSpec((1,) + pst.shape[1:], lambda bi, c, i: (l, 0, 0, 0)),
                    pl.BlockSpec((1,) + cst.shape[1:], lambda bi, c, i: (l, 0, 0, 0))] + gate_specs + [
                    pl.BlockSpec((1, tm, GROUP_WIDTH), lambda bi, c, i: (bi, once(c, i), 0))]
        urows = tm
        blocks += _nbytes(pst.shape[1:], F32) + _nbytes(cst.shape[1:], F32)
    else:
        hblk = tm // HIST
        slab = pl.BlockSpec((1, 2, tm, LANES), lambda bi, c, i: (bi, 0, once(c, i), 0))
        ins = [act, act, h, wgate, wgate, wgate] + list(attn)
        in_specs = [act_spec,
                    pl.BlockSpec((1, HIST, ACT_WIDTH),
                                 lambda bi, c, i: (bi, jnp.maximum(once(c, i) * hblk - 1, 0), 0))
                    ] + gate_specs + [slab] * 6
        scratch += [pltpu.VMEM((tm + HIST, POOL_WIDTH), F32), pltpu.VMEM((tm + HIST, CONV_CH), F32)]
        urows = 8
        blocks += 6 * _nbytes((2, tm, LANES), F32) + _nbytes((HIST, ACT_WIDTH), F32)
    scratch_bytes = (_nbytes((s, POOL_WIDTH + CONV_CH + GROUP_WIDTH), BF16) + 2 * _nbytes((tm + HIST, POOL_WIDTH), F32)
                     + 8 * _nbytes((tm, tc), F32))
    return pl.pallas_call(
        functools.partial(_branch_kernel, tm=tm, step=step),
        out_shape=(jax.ShapeDtypeStruct((b, s, D_MODEL), BF16),
                   jax.ShapeDtypeStruct((b, ni, urows, CONV_CH), F32)),
        grid=(b, D_MODEL // tc, ni),
        in_specs=in_specs + w_specs,
        out_specs=(pl.BlockSpec((1, tm, tc), lambda bi, c, i: (bi, i, c)),
                   pl.BlockSpec((1, 1, urows, CONV_CH), lambda bi, c, i: (bi, once(c, i), 0, 0))),
        scratch_shapes=scratch,
        compiler_params=pltpu.CompilerParams(
            dimension_semantics=("parallel", "arbitrary", "arbitrary"),
            vmem_limit_bytes=_vmem_limit(blocks, scratch_bytes)),
        name="branches_step" if step else "branches",
    )(*ins, wgrp, pscale, wpb, cw, wcb, wab)


def _oproj_kernel(m_ref, w_ref, x_ref, g_ref, gate_ref, o_ref):
    mix = _dot(m_ref[0], w_ref[0])
    o_ref[0] = x_ref[0] + _tail2(gate_ref) * _rms(mix, g_ref[0])


def _oproj(merged, w_o, x, norm, mod, l, *, tm):
    b, s, d = x.shape
    blocks = (_nbytes((tm, d), BF16) + _nbytes((d, d), BF16) + 2 * _nbytes((tm, d), F32)
              + _nbytes((_mod_rows(mod) + 1, d), F32))
    return pl.pallas_call(
        _oproj_kernel,
        out_shape=jax.ShapeDtypeStruct((b, s, d), F32),
        grid=(b, s // tm),
        in_specs=[
            pl.BlockSpec((1, tm, d), lambda bi, i: (bi, i, 0)),
            pl.BlockSpec((1, d, d), lambda bi, i: (l, 0, 0)),
            pl.BlockSpec((1, tm, d), lambda bi, i: (bi, i, 0)),
            norm[1](1), mod[1](2),
        ],
        out_specs=pl.BlockSpec((1, tm, d), lambda bi, i: (bi, i, 0)),
        compiler_params=pltpu.CompilerParams(
            dimension_semantics=("parallel", "parallel"),
            vmem_limit_bytes=_vmem_limit(blocks, _nbytes((tm, d), F32))),
        name="oproj",
    )(merged, w_o, x, norm[0], mod[0])


def _ffn_kernel(*refs, tm, step, nchunk):
    if step:
        (x_ref, g2_ref, sc_ref, sh_ref, wg_ref, wv_ref, cwg_ref, cwv_ref, wd_ref, g3_ref, gate_ref,
         stg_ref, stv_ref, o_ref, tg_ref, tv_ref, h_ref) = refs
        hist = 0
    else:
        (x_ref, xp_ref, g2_ref, sc_ref, sh_ref, wg_ref, wv_ref, cwg_ref, cwv_ref, wd_ref, g3_ref, gate_ref,
         o_ref, tg_ref, tv_ref, h_ref, ug_ref, uv_ref) = refs
        hist = HIST
    i = pl.program_id(1)
    j = pl.program_id(2)
    last = pl.num_programs(2) - 1
    cwg = cwg_ref[0]
    cwv = cwv_ref[0]

    if step:
        @pl.when(j == 0)
        def _():
            _modulate_rows(x_ref, g2_ref, sc_ref, sh_ref, h_ref, 0, tm)
            o_ref[...] = jnp.zeros_like(o_ref)

        h = h_ref[...]
        up_g = _dot(h, wg_ref[0])
        up_v = _dot(h, wv_ref[0])
        uc_g = stg_ref[0, 0] * cwg[0:1] + stg_ref[0, 1] * cwg[1:2] + up_g * cwg[2:3]
        uc_v = stv_ref[0, 0] * cwv[0:1] + stv_ref[0, 1] * cwv[1:2] + up_v * cwv[2:3]
        tg_ref[0, 0] = up_g
        tv_ref[0, 0] = up_v
        o_ref[0] += _dot((_gelu_tanh(uc_g) * uc_v).astype(BF16), wd_ref[0])

        @pl.when(j == last)
        def _():
            o_ref[0] = x_ref[0] + _tail2(gate_ref) * _rms(o_ref[0], g3_ref[0])
        return

    def seq_step(nck, is_first, is_last):
        rc = tm // nck
        if is_first:
            _modulate_rows(xp_ref, g2_ref, sc_ref, sh_ref, h_ref, 0, HIST)
        for c in range(nck):
            r = slice(c * rc, (c + 1) * rc)
            if is_first:
                y = _rms(x_ref[0, r], g2_ref[0])
                h_ref[HIST + c * rc:HIST + (c + 1) * rc] = (y * (1.0 + _tail2(sc_ref)) + _tail2(sh_ref)).astype(BF16)
            lo = 0 if c == 0 else HIST + c * rc
            hi = HIST + (c + 1) * rc
            h = h_ref[lo:hi]
            for u_ref, w_ref in ((ug_ref, wg_ref), (uv_ref, wv_ref)):
                up = _dot(h, w_ref[0])
                if c == 0:
                    u_ref[0:HIST] = jnp.where(i == 0, 0.0, up[0:HIST])
                    u_ref[HIST:hi] = up[HIST:]
                else:
                    u_ref[lo:hi] = up

            def conv(ref, cw):
                base = HIST + c * rc
                return (ref[base - 2:base - 2 + rc] * cw[0:1] + ref[base - 1:base - 1 + rc] * cw[1:2]
                        + ref[base:base + rc] * cw[2:3])
            act = (_gelu_tanh(conv(ug_ref, cwg)) * conv(uv_ref, cwv)).astype(BF16)
            acc = _dot(act, wd_ref[0])
            if not is_first:
                acc = o_ref[0, r] + acc
            if is_last:
                acc = x_ref[0, r] + _tail2(gate_ref) * _rms(acc, g3_ref[0])
            o_ref[0, r] = acc
        tg_ref[0, 0] = ug_ref[HIST + tm - 8:HIST + tm]
        tv_ref[0, 0] = uv_ref[HIST + tm - 8:HIST + tm]

    pl.when(j == 0)(lambda: seq_step(2 * nchunk, True, False))
    pl.when((j > 0) & (j < last))(lambda: seq_step(nchunk, False, False))
    pl.when(j == last)(lambda: seq_step(2 * nchunk, False, True))


def _ffn(x, norm, mod, w_up, cw, w_down, state, l, *, tm, tf, step):
    b, s, d = x.shape
    ni, nj = s // tm, D_FF // tf
    trows = tm if step else 8
    w_specs = [
        pl.BlockSpec((1, d, tf), lambda bi, i, j: (l, 0, j)),
        pl.BlockSpec((1, d, tf), lambda bi, i, j: (l, 0, nj + j)),
        pl.BlockSpec((1, FFN_K, tf), lambda bi, i, j: (l, 0, j)),
        pl.BlockSpec((1, FFN_K, tf), lambda bi, i, j: (l, 0, nj + j)),
        pl.BlockSpec((1, tf, d), lambda bi, i, j: (l, j, 0)),
    ]
    x_spec = pl.BlockSpec((1, tm, d), lambda bi, i, j: (bi, i, 0), pipeline_mode=pl.Buffered(1))
    hist = 0 if step else HIST
    scratch = [pltpu.VMEM((tm + hist, d), BF16)]
    blocks = (2 * _nbytes((tm, d), F32) + 3 * _nbytes((d, tf), BF16) + 2 * _nbytes((trows, tf), F32)
              + _nbytes((3 * _mod_rows(mod) + 2, d), F32))
    mods = [norm[1](2), mod[1](4), mod[1](3)]
    tailp = [norm[1](3), mod[1](5)]
    if step:
        ins = [x, norm[0], mod[0], mod[0], w_up, w_up, cw, cw, w_down, norm[0], mod[0], state, state]
        in_specs = [x_spec] + mods + w_specs + tailp + [
            pl.BlockSpec((1, FFN_K - 1, tm, tf), lambda bi, i, j: (l, 0, 0, j)),
            pl.BlockSpec((1, FFN_K - 1, tm, tf), lambda bi, i, j: (l, 0, 0, nj + j))]
        blocks += 2 * _nbytes((FFN_K - 1, tm, tf), F32)
    else:
        hblk = tm // HIST
        ins = [x, x, norm[0], mod[0], mod[0], w_up, w_up, cw, cw, w_down, norm[0], mod[0]]
        in_specs = [x_spec,
                    pl.BlockSpec((1, HIST, d), lambda bi, i, j: (bi, jnp.maximum(i * hblk - 1, 0), 0))
                    ] + mods + w_specs + tailp
        scratch += [pltpu.VMEM((tm + HIST, tf), F32), pltpu.VMEM((tm + HIST, tf), F32)]
        blocks += _nbytes((HIST, d), F32)
    scratch_bytes = _nbytes((tm + hist, d), BF16) + 6 * _nbytes((tm + hist, tf), F32)
    tail = jax.ShapeDtypeStruct((b, ni, trows, D_FF), F32)
    tail_spec = lambda: pl.BlockSpec((1, 1, trows, tf), lambda bi, i, j: (bi, i, 0, j))
    return pl.pallas_call(
        functools.partial(_ffn_kernel, tm=tm, step=step, nchunk=ROW_CHUNKS),
        out_shape=(jax.ShapeDtypeStruct((b, s, d), F32), tail, tail),
        grid=(b, ni, nj),
        in_specs=in_specs,
        out_specs=(pl.BlockSpec((1, tm, d), lambda bi, i, j: (bi, i, 0)), tail_spec(), tail_spec()),
        scratch_shapes=scratch,
        compiler_params=pltpu.CompilerParams(
            dimension_semantics=("parallel", "parallel", "arbitrary"),
            vmem_limit_bytes=_vmem_limit(blocks, scratch_bytes)),
        name="ffn_step" if step else "ffn",
    )(*ins)


def _group_rel_bias(rel_bias):
    n = np.arange(N_DIL_KEYS + 1)
    max_exact = N_BUCKETS // 2
    out = []
    for g, (_, d) in enumerate(ATTN_GROUPS):
        dist = n * d
        large = max_exact + (np.log(np.maximum(dist, 1) / max_exact) / np.log(MAX_DISTANCE / max_exact)
                             * (N_BUCKETS - max_exact)).astype(np.int32)
        bucket = np.where(dist < max_exact, dist, np.minimum(large, N_BUCKETS - 1))
        out.append(rel_bias[bucket, g * HEADS_PER_GROUP:(g + 1) * HEADS_PER_GROUP].T)
    return jnp.stack(out)


def _bias_tables(rel_bias):
    bias_g = _group_rel_bias(rel_bias).astype(F32)
    n = N_DIL_KEYS
    gh = (N_ATTN_GROUPS, HEADS_PER_GROUP)
    ext = jnp.concatenate([bias_g[:, :, ::-1], jnp.full(gh + (n,), NEG_INF, F32)], axis=-1)
    tab = jnp.tile(ext, (1, 1, n))[:, :, :n * 2 * n].reshape(N_ATTN_GROUPS, HEADS_PER_GROUP * n, 2 * n)
    step = []
    for g, (_, d) in enumerate(ATTN_GROUPS):
        hit = bias_g[g, :, :0:-1, None]
        row = jnp.concatenate([hit, jnp.full((HEADS_PER_GROUP, n, d - 1), NEG_INF, F32)], axis=-1)
        new = jnp.broadcast_to(bias_g[g, :, 0:1], (HEADS_PER_GROUP, n * d))
        step.append(jnp.concatenate([row.reshape(HEADS_PER_GROUP, n * d), new], axis=0))
    return tab, step


def _kv_pack(qkv, g, nrows):
    b = qkv.shape[0]
    k0 = ATTN_WIDTH + g * GROUP_WIDTH
    v0 = 2 * ATTN_WIDTH + g * GROUP_WIDTH
    k = qkv[:, -nrows:, k0:k0 + GROUP_WIDTH].reshape(b, nrows, HEADS_PER_GROUP, HEAD_DIM)
    v = qkv[:, -nrows:, v0:v0 + GROUP_WIDTH].reshape(b, nrows, HEADS_PER_GROUP, HEAD_DIM)
    return jnp.stack([k, v], axis=2)


def _kv_tail_kernel(*refs):
    n = len(refs) // 3
    for g in range(n):
        k_ref, v_ref, o_ref = refs[2 * g], refs[2 * g + 1], refs[2 * n + g]
        o_ref[0, 0] = k_ref[0].T
        o_ref[0, 1] = v_ref[0].T


def _kv_tails(qkv):
    b, s, _ = qkv.shape
    ins, in_specs, out_shape, out_specs, blocks = [], [], [], [], 0
    for g, (w, _) in enumerate(ATTN_GROUPS):
        nrows = min(w, s)
        assert s % nrows == 0
        for part in (1, 2):
            col = part * (ATTN_WIDTH // GROUP_WIDTH) + g
            ins.append(qkv)
            in_specs.append(pl.BlockSpec((1, nrows, GROUP_WIDTH), lambda bi, col=col, rb=s // nrows - 1: (bi, rb, col)))
        out_shape.append(jax.ShapeDtypeStruct((b, 2, GROUP_WIDTH, nrows), F32))
        out_specs.append(pl.BlockSpec((1, 2, GROUP_WIDTH, nrows), lambda bi: (bi, 0, 0, 0)))
        blocks += 4 * _nbytes((nrows, GROUP_WIDTH), F32)
    outs = pl.pallas_call(
        _kv_tail_kernel,
        out_shape=out_shape,
        grid=(b,),
        in_specs=in_specs,
        out_specs=out_specs,
        compiler_params=pltpu.CompilerParams(
            dimension_semantics=("parallel",), vmem_limit_bytes=_vmem_limit(blocks, blocks // 2)),
        name="kv_tails",
    )(*ins)
    return [o.reshape(b, 2, HEADS_PER_GROUP, HEAD_DIM, o.shape[-1]).transpose(0, 4, 1, 2, 3) for o in outs]


def _prompt_layer(x, l, P):
    b, s, _ = x.shape
    norm, mod = P["norm"](l), P["mod_p"](l)
    tm = 512
    qkv, qkv_slab, h = _proj(x, norm, mod, P["w_qkv"], l, tm=1024, tn=ATTN_WIDTH)
    act = _matmul(h, P["w_act"], l, tm=1024, tn=1024)
    ols = [_attention(qkv_slab, P["bias_tab"], g) for g in range(N_ATTN_GROUPS)]
    attn = [o for o, _ in ols] + [ls for _, ls in ols]
    merged, u_tail = _branches(act, h, attn, None, P["branch"], l, tm=tm, tc=512, step=False)
    x1 = _oproj(merged, P["w_o"], x, norm, mod, l, tm=tm)
    x2, tail_g, tail_v = _ffn(x1, norm, mod, P["w_up"], P["ffn_cw"], P["w_down"], None, l,
                              tm=1024, tf=512, step=False)
    new_kv = _kv_tails(qkv)
    new_pool = act[:, -POOL_STATE:, :POOL_WIDTH]
    new_conv = u_tail[:, -1, -(CONV_K - 1):]
    new_ffn = jnp.concatenate([tail_g[:, -1, -(FFN_K - 1):], tail_v[:, -1, -(FFN_K - 1):]], axis=-1)
    return x2, (new_kv[0], new_kv[1], new_kv[2], new_pool, new_conv, new_ffn)


def _pad_rows(a, axis):
    pad = [(0, 0)] * a.ndim
    pad[axis] = (0, SAMPLE_ROWS - a.shape[axis])
    return jnp.pad(a, pad)


def _sample_layer(x, l, P, S, nb):
    norm, mod = P["norm"](l), P["mod_s"](l)
    tm = SAMPLE_ROWS
    qkv, _, h = _proj(x, norm, mod, P["w_qkv"], l, tm=tm, tn=ATTN_WIDTH)
    act = _matmul(h, P["w_act"], l, tm=tm, tn=1024)
    heads = qkv[0, :nb].reshape(nb, 3 * N_ATTN_GROUPS, HEADS_PER_GROUP, HEAD_DIM, 1)
    attn = _attention_step(heads, S["caches"], P["bias_step"], l, nb)
    attn = _pad_rows(attn.reshape(1, nb, GROUP_WIDTH), 1)
    merged, u = _branches(act, h, attn, (S["pool_t"], S["conv_t"]), P["branch"], l, tm=tm, tc=512, step=True)
    x1 = _oproj(merged, P["w_o"], x, norm, mod, l, tm=tm)
    x2, up_g, up_v = _ffn(x1, norm, mod, P["w_up"], P["ffn_cw"], P["w_down"], S["ffn_t"], l,
                          tm=tm, tf=512, step=True)
    new_kv = [_kv_pack(qkv[0, :nb, None], g, 1) for g in range(N_ATTN_GROUPS)]
    new_pool = jnp.concatenate([S["pool"][l][:, 1:], act[0, :nb, None, :POOL_WIDTH]], axis=1)
    new_conv = jnp.concatenate([S["conv"][l][:, 1:], u[0, 0, :nb, None]], axis=1)
    up_new = jnp.concatenate([up_g[0, 0, :nb], up_v[0, 0, :nb]], axis=-1)
    new_ffn = jnp.concatenate([S["ffn"][l][:, 1:], up_new[:, None]], axis=1)
    return x2, (new_kv[0], new_kv[1], new_kv[2], new_pool, new_conv, new_ffn)


def kernel(x_prompt, x_sample, c_prompt, c_sample, cache_kv_w128, cache_kv_w512, cache_kv_w2048, state_pool, state_conv, state_ffn_conv, rel_bias, norm_g, w_ada, b_ada, w_in, w_attn_br, w_pool_grp, pool_scale, w_pool_br, conv_w, w_conv_br, w_o, w_up, ffn_conv_w, w_down):
    nbp = x_prompt.shape[0]
    nbs, tdec, _ = x_sample.shape
    assert tdec == 1 and nbs <= SAMPLE_ROWS
    caches = (cache_kv_w128, cache_kv_w512, cache_kv_w2048)
    for (w, d), c in zip(ATTN_GROUPS, caches):
        assert c.shape[2] == w == N_DIL_KEYS * d, "cache must hold exactly one window"

    bias_tab, bias_step = _bias_tables(rel_bias)
    c_rows = -(-(SAMPLE_ROWS + nbp) // 8) * 8
    c_all = jnp.zeros((c_rows, D_MODEL), F32).at[:nbs].set(c_sample).at[SAMPLE_ROWS:SAMPLE_ROWS + nbp].set(c_prompt)
    mod_all = _ada(c_all, w_ada, b_ada)

    w_qkv, w_act, w_gate = _cast_split(
        w_in, [(0, QKV_WIDTH), (QKV_WIDTH, QKV_WIDTH + ACT_WIDTH), (QKV_WIDTH + ACT_WIDTH, w_in.shape[2])], tk=256)
    P = dict(
        norm=functools.partial(_norm_view, norm_g),
        mod_p=functools.partial(_mod_view, mod_all, prompt=True),
        mod_s=functools.partial(_mod_view, mod_all, prompt=False),
        w_qkv=w_qkv, w_act=w_act,
        branch=(w_pool_grp.astype(BF16), pool_scale[:, None, :], w_pool_br.astype(BF16),
                conv_w, w_conv_br.astype(BF16), w_attn_br.astype(BF16), w_gate),
        w_o=w_o.astype(BF16), w_up=w_up.astype(BF16), ffn_cw=ffn_conv_w, w_down=w_down.astype(BF16),
        bias_tab=bias_tab, bias_step=bias_step,
    )
    S = dict(
        caches=[c.transpose(0, 1, 3, 4, 5, 2) for c in caches],
        pool=state_pool, conv=state_conv, ffn=state_ffn_conv,
        pool_t=_pad_rows(state_pool.transpose(0, 2, 1, 3), 2),
        conv_t=_pad_rows(state_conv.transpose(0, 2, 1, 3), 2),
        ffn_t=_pad_rows(state_ffn_conv.transpose(0, 2, 1, 3), 2),
    )
    yp = x_prompt
    ys = _pad_rows(x_sample.reshape(1, nbs, D_MODEL), 1)
    st_p, st_s = [], []
    for l in range(DEPTH):
        yp, sp = _prompt_layer(yp, l, P)
        ys, ss = _sample_layer(ys, l, P, S, nbs)
        st_p.append(sp)
        st_s.append(ss)
    outs_p = [jnp.stack([s[k] for s in st_p]) for k in range(6)]
    outs_s = [jnp.stack([s[k] for s in st_s]) for k in range(6)]
    return (yp, ys[0, :nbs, None, :], *outs_p, *outs_s)
```

```python
import functools

import numpy as np
import jax
import jax.numpy as jnp
from jax import lax
from jax.experimental import pallas as pl
from jax.experimental.pallas import tpu as pltpu

F32 = jnp.float32
BF16 = jnp.bfloat16

D_MODEL = 2048
DEPTH = 2
HEAD_DIM = 64
HEADS_PER_GROUP = 4
ATTN_GROUPS = ((128, 1), (512, 4), (2048, 16))
N_ATTN_GROUPS = len(ATTN_GROUPS)
ATTN_WIDTH = N_ATTN_GROUPS * HEADS_PER_GROUP * HEAD_DIM
GROUP_WIDTH = HEADS_PER_GROUP * HEAD_DIM
N_DIL_KEYS = 128
N_BUCKETS = 32
MAX_DISTANCE = 2048
ATTN_SCALE = HEAD_DIM ** -0.5
POOL_WINDOWS = (2, 4, 8, 16)
POOL_GROUP = 128
POOL_WIDTH = 512
POOL_STATE = 15
CONV_CH = 512
CONV_K = 3
D_FF = 5632
FFN_K = 3
N_MOD = 6
N_NORM = 4
EPS = 1e-6
NEG_INF = -1e30

LANES = 128
HIST = 16
SAMPLE_ROWS = 16
QKV_WIDTH = 3 * ATTN_WIDTH
QKV_SLABS = QKV_WIDTH // LANES
ACT_WIDTH = POOL_WIDTH + 3 * CONV_CH
ATTN_CLASS_UNROLL = 4
ROW_CHUNKS = 2
VMEM_CAP = 56 * 1024 * 1024


def _vmem_limit(block_bytes, scratch_bytes=0):
    del block_bytes, scratch_bytes
    return VMEM_CAP


def _nbytes(shape, dtype):
    return int(np.prod(shape)) * jnp.dtype(dtype).itemsize


def _rms(x, g):
    return x * lax.rsqrt(jnp.mean(x * x, axis=-1, keepdims=True) + EPS) * g


def _sigmoid(x):
    return 0.5 + 0.5 * jnp.tanh(0.5 * x)


def _gelu_tanh(x):
    return 0.5 * x * (1.0 + jnp.tanh(np.sqrt(2.0 / np.pi) * (x + 0.044715 * (x * x * x))))


def _dot(a, b):
    return jnp.dot(a, b, preferred_element_type=F32)


def _tail2(ref):
    return ref[(0,) * (len(ref.shape) - 2)]


def _norm_view(norm_g, l):
    arr = norm_g.reshape(DEPTH * N_NORM, 1, D_MODEL)
    return arr, lambda k: pl.BlockSpec((1, 1, D_MODEL), lambda *_: (l * N_NORM + k, 0, 0))


def _mod_view(mod_all, l, prompt):
    if prompt:
        arr = mod_all.reshape(DEPTH, mod_all.shape[1], N_MOD, 1, D_MODEL)
        return arr, lambda k: pl.BlockSpec((1, 1, 1, 1, D_MODEL), lambda bi, *_: (l, SAMPLE_ROWS + bi, k, 0, 0))
    return mod_all, lambda k: pl.BlockSpec((1, SAMPLE_ROWS, D_MODEL), lambda bi, *_: (l, 0, k))


def _mod_rows(mod):
    return mod[1](0).block_shape[-2]


def _cast_split_kernel(w_ref, *o_refs, bounds):
    for o_ref, (lo, hi) in zip(o_refs, bounds):
        o_ref[...] = w_ref[:, :, lo:hi].astype(BF16)


def _cast_split(w, bounds, *, tk):
    nl, k, n = w.shape
    blocks = _nbytes((tk, n), F32) + sum(_nbytes((tk, hi - lo), BF16) for lo, hi in bounds)
    return pl.pallas_call(
        functools.partial(_cast_split_kernel, bounds=tuple(bounds)),
        out_shape=[jax.ShapeDtypeStruct((nl, k, hi - lo), BF16) for lo, hi in bounds],
        grid=(nl, k // tk),
        in_specs=[pl.BlockSpec((1, tk, n), lambda li, ki: (li, ki, 0))],
        out_specs=[pl.BlockSpec((1, tk, hi - lo), lambda li, ki: (li, ki, 0)) for lo, hi in bounds],
        compiler_params=pltpu.CompilerParams(
            dimension_semantics=("parallel", "parallel"), vmem_limit_bytes=_vmem_limit(blocks)),
        name="cast_split",
    )(w)


def _ada_kernel(c_ref, w_ref, b_ref, o_ref):
    c = c_ref[...]
    s = (c * _sigmoid(c)).astype(BF16)
    o_ref[0] = _dot(s, w_ref[0].astype(BF16)) + b_ref[0]


def _ada(c_all, w_ada, b_ada):
    rows = c_all.shape[0]
    n = w_ada.shape[-1]
    tn = 1024
    blocks = _nbytes((rows, D_MODEL), F32) + _nbytes((D_MODEL, tn), F32) + _nbytes((rows + 1, tn), F32)
    return pl.pallas_call(
        _ada_kernel,
        out_shape=jax.ShapeDtypeStruct((DEPTH, rows, n), F32),
        grid=(DEPTH, n // tn),
        in_specs=[
            pl.BlockSpec((rows, D_MODEL), lambda l, j: (0, 0)),
            pl.BlockSpec((1, D_MODEL, tn), lambda l, j: (l, 0, j)),
            pl.BlockSpec((1, 1, tn), lambda l, j: (l, 0, j)),
        ],
        out_specs=pl.BlockSpec((1, rows, tn), lambda l, j: (l, 0, j)),
        compiler_params=pltpu.CompilerParams(
            dimension_semantics=("parallel", "parallel"),
            vmem_limit_bytes=_vmem_limit(blocks, _nbytes((D_MODEL, tn), BF16))),
        name="ada",
    )(c_all, w_ada, b_ada.reshape(DEPTH, 1, n))


def _modulate_rows(x_ref, g_ref, sc_ref, sh_ref, h_ref, row0, tm):
    y = _rms(x_ref[0], g_ref[0])
    h_ref[row0:row0 + tm] = (y * (1.0 + _tail2(sc_ref)) + _tail2(sh_ref)).astype(BF16)


def _proj_kernel(x_ref, g_ref, sc_ref, sh_ref, w_ref, o_ref, slab_ref, h_ref, *, slabs, tm, nchunk):
    def project(r):
        res = _dot(h_ref[0, r], w_ref[0])
        o_ref[0, r] = res
        for s in range(slabs):
            slab_ref[0, s, r] = res[:, s * LANES:(s + 1) * LANES]

    @pl.when(pl.program_id(2) == 0)
    def _():
        rc = tm // nchunk
        for c in range(nchunk):
            r = slice(c * rc, (c + 1) * rc)
            y = _rms(x_ref[0, r], g_ref[0])
            h_ref[0, r] = (y * (1.0 + _tail2(sc_ref)) + _tail2(sh_ref)).astype(BF16)
            project(r)

    @pl.when(pl.program_id(2) > 0)
    def _():
        project(slice(None))


def _proj(x, norm, mod, w, l, *, tm, tn):
    b, s, d = x.shape
    n = w.shape[2]
    slabs = tn // LANES
    r = _mod_rows(mod)
    blocks = (_nbytes((tm, d), F32) + _nbytes((2 * r + 1, d), F32) + _nbytes((d, tn), BF16)
              + 2 * _nbytes((tm, tn), F32) + _nbytes((tm, d), BF16))
    return pl.pallas_call(
        functools.partial(_proj_kernel, slabs=slabs, tm=tm, nchunk=2 * ROW_CHUNKS if r == 1 else 1),
        out_shape=[jax.ShapeDtypeStruct((b, s, n), F32),
                   jax.ShapeDtypeStruct((b, n // LANES, s, LANES), F32),
                   jax.ShapeDtypeStruct((b, s, d), BF16)],
        grid=(b, s // tm, n // tn),
        in_specs=[
            pl.BlockSpec((1, tm, d), lambda bi, i, j: (bi, i, 0)),
            norm[1](0), mod[1](1), mod[1](0),
            pl.BlockSpec((1, d, tn), lambda bi, i, j: (l, 0, j)),
        ],
        out_specs=[pl.BlockSpec((1, tm, tn), lambda bi, i, j: (bi, i, j)),
                   pl.BlockSpec((1, slabs, tm, LANES), lambda bi, i, j: (bi, j, i, 0)),
                   pl.BlockSpec((1, tm, d), lambda bi, i, j: (bi, i, 0))],
        compiler_params=pltpu.CompilerParams(
            dimension_semantics=("parallel", "parallel", "arbitrary"),
            vmem_limit_bytes=_vmem_limit(blocks, _nbytes((tm, tn), F32) + _nbytes((tm, d), F32))),
        name="proj_norm",
    )(x, norm[0], mod[0], mod[0], w)


def _matmul_kernel(h_ref, w_ref, o_ref):
    o_ref[0] = _dot(h_ref[0], w_ref[0])


def _matmul(h, w, l, *, tm, tn):
    b, s, d = h.shape
    n = w.shape[2]
    blocks = _nbytes((tm, d), BF16) + _nbytes((d, tn), BF16) + _nbytes((tm, tn), F32)
    return pl.pallas_call(
        _matmul_kernel,
        out_shape=jax.ShapeDtypeStruct((b, s, n), F32),
        grid=(b, s // tm, n // tn),
        in_specs=[pl.BlockSpec((1, tm, d), lambda bi, i, j: (bi, i, 0)),
                  pl.BlockSpec((1, d, tn), lambda bi, i, j: (l, 0, j))],
        out_specs=pl.BlockSpec((1, tm, tn), lambda bi, i, j: (bi, i, j)),
        compiler_params=pltpu.CompilerParams(
            dimension_semantics=("parallel", "parallel", "parallel"),
            vmem_limit_bytes=_vmem_limit(blocks, _nbytes((tm, tn), F32))),
        name="proj_act",
    )(h, w)


def _head_masks(rows):
    lane = lax.broadcasted_iota(jnp.int32, (rows, GROUP_WIDTH), 1)
    return [(lane >= h * HEAD_DIM) & (lane < (h + 1) * HEAD_DIM) for h in range(HEADS_PER_GROUP)]


def _attn_kernel(q_ref, kc_ref, kp_ref, vc_ref, vp_ref, bias_ref, o_ref, lse_ref, edge_ref, *, d, sb):
    i = pl.program_id(1)
    nq = N_DIL_KEYS
    span = nq * d
    ncb = sb // span
    hm = _head_masks(1)
    col = lax.broadcasted_iota(jnp.int32, (1, 2 * nq), 1)
    edge_ref[...] = jnp.where((col < nq) & (i == 0), NEG_INF, bias_ref[0])

    def rows(start):
        return pl.ds(start, nq, stride=d) if d > 1 else pl.ds(start, nq)

    def load(ref, start):
        return jnp.concatenate([ref[0, s, rows(start), :] for s in range(2)], axis=1)

    def one_block(r, jb):
        qs = jb * span + r
        q = load(q_ref, qs) * ATTN_SCALE
        if jb == 0:
            lo = sb - span + r
            k_lo, v_lo = load(kp_ref, lo), load(vp_ref, lo)
        else:
            lo = (jb - 1) * span + r
            k_lo, v_lo = load(kc_ref, lo), load(vc_ref, lo)
        kcat = jnp.concatenate([k_lo, load(kc_ref, qs)], axis=0).astype(BF16)
        vcat = jnp.concatenate([v_lo, load(vc_ref, qs)], axis=0).astype(BF16)
        qm = jnp.concatenate([jnp.where(hm[h], q, 0.0) for h in range(HEADS_PER_GROUP)], axis=0).astype(BF16)
        s = lax.dot_general(qm, kcat, (((1,), (1,)), ((), ())), preferred_element_type=F32)
        s = s + (edge_ref[...] if jb == 0 else bias_ref[0])
        m = jnp.max(s, axis=-1, keepdims=True)
        p = jnp.exp(s - m)
        l = jnp.sum(p, axis=-1, keepdims=True)
        oall = _dot((p * (1.0 / l)).astype(BF16), vcat)
        lse = m + jnp.log(l)
        o = jnp.zeros((nq, GROUP_WIDTH), F32)
        ls = jnp.zeros((nq, GROUP_WIDTH), F32)
        for h in range(HEADS_PER_GROUP):
            o = jnp.where(hm[h], oall[h * nq:(h + 1) * nq], o)
            ls = jnp.where(hm[h], lse[h * nq:(h + 1) * nq], ls)
        for sl in range(2):
            o_ref[0, sl, rows(qs), :] = o[:, sl * LANES:(sl + 1) * LANES]
            lse_ref[0, sl, rows(qs), :] = ls[:, sl * LANES:(sl + 1) * LANES]

    if d == 1:
        for jb in range(ncb):
            one_block(0, jb)
    else:
        def body(r, carry):
            for jb in range(ncb):
                one_block(r, jb)
            return carry
        lax.fori_loop(0, d, body, 0, unroll=ATTN_CLASS_UNROLL)


def _attention(qkv, bias_tab, g):
    b, _, s, _ = qkv.shape
    d = ATTN_GROUPS[g][1]
    sb = max(N_DIL_KEYS * d, 512)
    blk = (1, 2, sb, LANES)
    kslab, vslab = ATTN_WIDTH // GROUP_WIDTH + g, 2 * ATTN_WIDTH // GROUP_WIDTH + g
    prev = lambda i: jnp.maximum(i - 1, 0)
    blocks = 7 * _nbytes(blk, F32) + _nbytes((4 * N_DIL_KEYS, 2 * N_DIL_KEYS), F32)
    out = jax.ShapeDtypeStruct((b, 2, s, LANES), F32)
    return pl.pallas_call(
        functools.partial(_attn_kernel, d=d, sb=sb),
        out_shape=(out, out),
        grid=(b, s // sb),
        in_specs=[
            pl.BlockSpec(blk, lambda bi, i: (bi, g, i, 0)),
            pl.BlockSpec(blk, lambda bi, i: (bi, kslab, i, 0)),
            pl.BlockSpec(blk, lambda bi, i: (bi, kslab, prev(i), 0)),
            pl.BlockSpec(blk, lambda bi, i: (bi, vslab, i, 0)),
            pl.BlockSpec(blk, lambda bi, i: (bi, vslab, prev(i), 0)),
            pl.BlockSpec((1, 4 * N_DIL_KEYS, 2 * N_DIL_KEYS), lambda bi, i: (g, 0, 0)),
        ],
        out_specs=(pl.BlockSpec(blk, lambda bi, i: (bi, 0, i, 0)),
                   pl.BlockSpec(blk, lambda bi, i: (bi, 0, i, 0))),
        scratch_shapes=[pltpu.VMEM((HEADS_PER_GROUP * N_DIL_KEYS, 2 * N_DIL_KEYS), F32)],
        compiler_params=pltpu.CompilerParams(
            dimension_semantics=("parallel", "parallel"),
            vmem_limit_bytes=_vmem_limit(blocks, 8 << 20)),
        name=f"attn_d{d}",
    )(qkv, qkv, qkv, qkv, qkv, bias_tab)


def _attn_step_kernel(qkv_ref, c0_ref, c1_ref, c2_ref, b0_ref, b1_ref, b2_ref, o_ref):
    outs = [[None] * N_ATTN_GROUPS for _ in range(HEADS_PER_GROUP)]
    lses = [[None] * N_ATTN_GROUPS for _ in range(HEADS_PER_GROUP)]
    for g, (c_ref, b_ref) in enumerate(((c0_ref, b0_ref), (c1_ref, b1_ref), (c2_ref, b2_ref))):
        for h in range(HEADS_PER_GROUP):
            col = lambda part: (part * N_ATTN_GROUPS + g) * HEADS_PER_GROUP + h
            q = qkv_ref[0, :, col(0):col(0) + 1] * ATTN_SCALE
            kn = qkv_ref[0, :, col(1):col(1) + 1]
            vn = qkv_ref[0, :, col(2):col(2) + 1]
            s_c = jnp.sum(c_ref[0, 0, 0, h] * q, axis=0, keepdims=True) + b_ref[h:h + 1, :]
            s_n = jnp.sum(kn * q, axis=0, keepdims=True) + b_ref[HEADS_PER_GROUP + h:HEADS_PER_GROUP + h + 1, 0:1]
            m = jnp.maximum(jnp.max(s_c, axis=-1, keepdims=True), s_n)
            p_c = jnp.exp(s_c - m)
            p_n = jnp.exp(s_n - m)
            l = jnp.sum(p_c, axis=-1, keepdims=True) + p_n
            inv = 1.0 / l
            outs[h][g] = jnp.sum(c_ref[0, 0, 1, h] * (p_c * inv), axis=-1, keepdims=True) + (p_n * inv) * vn
            lses[h][g] = m + jnp.log(l)
    for h in range(HEADS_PER_GROUP):
        ls, os_ = lses[h], outs[h]
        mx = jnp.maximum(jnp.maximum(ls[0], ls[1]), ls[2])
        w = [jnp.exp(x - mx) for x in ls]
        tot = w[0] + w[1] + w[2]
        o_ref[0, h] = (w[0] * os_[0] + w[1] * os_[1] + w[2] * os_[2]) * (1.0 / tot)


def _attention_step(qkv, caches, biases, l, nb):
    cache_specs = [pl.BlockSpec((1, 1) + c.shape[2:], lambda bi: (l, bi, 0, 0, 0, 0)) for c in caches]
    bias_specs = [pl.BlockSpec(b.shape, lambda bi: (0, 0)) for b in biases]
    blocks = (sum(_nbytes(c.shape[2:], F32) for c in caches) + sum(_nbytes(b.shape, F32) for b in biases)
              + 13 * HEADS_PER_GROUP * HEAD_DIM * LANES * 4)
    return pl.pallas_call(
        _attn_step_kernel,
        out_shape=jax.ShapeDtypeStruct((nb, HEADS_PER_GROUP, HEAD_DIM, 1), F32),
        grid=(nb,),
        in_specs=[pl.BlockSpec((1,) + qkv.shape[1:], lambda bi: (bi, 0, 0))] + cache_specs + bias_specs,
        out_specs=pl.BlockSpec((1, HEADS_PER_GROUP, HEAD_DIM, 1), lambda bi: (bi, 0, 0, 0)),
        compiler_params=pltpu.CompilerParams(
            dimension_semantics=("parallel",),
            vmem_limit_bytes=_vmem_limit(blocks, 8 << 20)),
        name="attn_step",
    )(qkv, *caches, *biases)


def _branch_kernel(*refs, tm, step):
    if step:
        (act_ref, pst_ref, cst_ref, h_ref, wga_ref, wgb_ref, wgc_ref, at_ref,
         wgrp_ref, pscale_ref, wpb_ref, cw_ref, wcb_ref, wab_ref,
         out_ref, u_ref, pzs_ref, cb_ref, comb_ref) = refs
    else:
        (act_ref, hist_ref, h_ref, wga_ref, wgb_ref, wgc_ref, o0_ref, o1_ref, o2_ref, l0_ref, l1_ref, l2_ref,
         wgrp_ref, pscale_ref, wpb_ref, cw_ref, wcb_ref, wab_ref,
         out_ref, u_ref, pzs_ref, cb_ref, comb_ref, pe_ref, ue_ref) = refs
    c_tile = pl.program_id(1)
    i = pl.program_id(2)
    rows = pl.ds(pl.multiple_of(i * tm, tm), tm)
    pzs_ref, cb_ref, comb_ref = pzs_ref.at[rows], cb_ref.at[rows], comb_ref.at[rows]

    def sequence_mixers():
        p = act_ref[0, :, 0:POOL_WIDTH]
        gate_b = act_ref[0, :, POOL_WIDTH:POOL_WIDTH + CONV_CH]
        u = act_ref[0, :, POOL_WIDTH + CONV_CH:POOL_WIDTH + 2 * CONV_CH] * \
            act_ref[0, :, POOL_WIDTH + 2 * CONV_CH:POOL_WIDTH + 3 * CONV_CH]
        cw = cw_ref[0]
        if step:
            acc = p
            sums = {}
            for k in range(1, max(POOL_WINDOWS)):
                acc = acc + pst_ref[0, POOL_STATE - k]
                sums[k + 1] = acc
            means = [sums[w][:, gi * POOL_GROUP:(gi + 1) * POOL_GROUP] * (1.0 / w)
                     for gi, w in enumerate(POOL_WINDOWS)]
            conv = cst_ref[0, 0] * cw[0:1] + cst_ref[0, 1] * cw[1:2] + u * cw[2:3]
            u_ref[0, 0] = u
            comb_ref[...] = at_ref[0].astype(BF16)
        else:
            first = i == 0
            hist_p = hist_ref[0, :, 0:POOL_WIDTH]
            hist_u = hist_ref[0, :, POOL_WIDTH + CONV_CH:POOL_WIDTH + 2 * CONV_CH] * \
                hist_ref[0, :, POOL_WIDTH + 2 * CONV_CH:POOL_WIDTH + 3 * CONV_CH]
            pe_ref[0:HIST] = jnp.where(first, 0.0, hist_p)
            ue_ref[0:HIST] = jnp.where(first, 0.0, hist_u)
            pe_ref[HIST:HIST + tm] = p
            ue_ref[HIST:HIST + tm] = u
            t = i * tm + lax.broadcasted_iota(jnp.int32, (tm, 1), 0)
            means = []
            for gi, w in enumerate(POOL_WINDOWS):
                cs = slice(gi * POOL_GROUP, (gi + 1) * POOL_GROUP)
                acc = pe_ref[HIST:HIST + tm, cs]
                for k in range(1, w):
                    acc = acc + pe_ref[HIST - k:HIST - k + tm, cs]
                cnt = jnp.minimum(t + 1, w).astype(F32)
                means.append(acc * (1.0 / cnt))
            conv = (ue_ref[HIST - 2:HIST - 2 + tm] * cw[0:1] + ue_ref[HIST - 1:HIST - 1 + tm] * cw[1:2]
                    + u * cw[2:3])
            u_ref[0, 0] = ue_ref[HIST + tm - 8:HIST + tm]
            for sl in range(2):
                ls = [r[0, sl] for r in (l0_ref, l1_ref, l2_ref)]
                os_ = [r[0, sl] for r in (o0_ref, o1_ref, o2_ref)]
                mx = jnp.maximum(jnp.maximum(ls[0], ls[1]), ls[2])
                w_ = [jnp.exp(x - mx) for x in ls]
                tot = w_[0] + w_[1] + w_[2]
                comb = (w_[0] * os_[0] + w_[1] * os_[1] + w_[2] * os_[2]) * (1.0 / tot)
                comb_ref[:, sl * LANES:(sl + 1) * LANES] = comb.astype(BF16)
        pscale = pscale_ref[0]
        for gi in range(len(POOL_WINDOWS)):
            cs = slice(gi * POOL_GROUP, (gi + 1) * POOL_GROUP)
            pm = (means[gi] - p[:, cs]).astype(BF16)
            pz = _dot(pm, wgrp_ref[0, gi])
            pzs_ref[:, cs] = (pz * pscale[:, cs]).astype(BF16)
        cb_ref[...] = (gate_b * conv).astype(BF16)

    def merge_tile():
        h = h_ref[0]
        merged = _sigmoid(_dot(h, wga_ref[0])) * _dot(pzs_ref[...], wpb_ref[0])
        merged += _sigmoid(_dot(h, wgb_ref[0])) * _dot(cb_ref[...], wcb_ref[0])
        merged += _sigmoid(_dot(h, wgc_ref[0])) * _dot(comb_ref[...], wab_ref[0])
        out_ref[0] = merged.astype(BF16)

    @pl.when(c_tile == 0)
    def _():
        sequence_mixers()
        merge_tile()

    @pl.when(c_tile > 0)
    def _():
        merge_tile()


def _branches(act, h, attn, states, weights, l, *, tm, tc, step):
    b, s, _ = act.shape
    wgrp, pscale, wpb, cw, wcb, wab, wgate = weights
    ni = s // tm
    once = lambda c, i: jnp.where(c == 0, i, ni - 1)
    gate_specs = [pl.BlockSpec((1, tm, D_MODEL), lambda bi, c, i: (bi, i, 0))] + [
        pl.BlockSpec((1, D_MODEL, tc), lambda bi, c, i, k=k: (l, 0, k * (D_MODEL // tc) + c)) for k in range(3)]
    w_specs = [
        pl.BlockSpec((1,) + wgrp.shape[1:], lambda bi, c, i: (l, 0, 0, 0)),
        pl.BlockSpec((1,) + pscale.shape[1:], lambda bi, c, i: (l, 0, 0)),
        pl.BlockSpec((1, POOL_WIDTH, tc), lambda bi, c, i: (l, 0, c)),
        pl.BlockSpec((1,) + cw.shape[1:], lambda bi, c, i: (l, 0, 0)),
        pl.BlockSpec((1, CONV_CH, tc), lambda bi, c, i: (l, 0, c)),
        pl.BlockSpec((1, GROUP_WIDTH, tc), lambda bi, c, i: (l, 0, c)),
    ]
    act_spec = pl.BlockSpec((1, tm, ACT_WIDTH), lambda bi, c, i: (bi, once(c, i), 0))
    scratch = [pltpu.VMEM((s, POOL_WIDTH), BF16), pltpu.VMEM((s, CONV_CH), BF16),
               pltpu.VMEM((s, GROUP_WIDTH), BF16)]
    blocks = (_nbytes((tm, ACT_WIDTH), F32) + _nbytes((tm + 3 * tc, D_MODEL), BF16) + _nbytes(wgrp.shape[1:], BF16)
              + _nbytes((POOL_WIDTH + CONV_CH + GROUP_WIDTH, tc), BF16) + _nbytes((tm, tc), BF16))
    if step:
        pst, cst = states
        ins = [act, pst, cst, h, wgate, wgate, wgate, attn]
        in_specs = [act_spec,
                    pl.BlockSpec((1,) + pst.shape[1:], lambda bi, c, i: (l, 0, 0, 0)),
                    pl.BlockSpec((1,) + cst.shape[1:], lambda bi, c, i: (l, 0, 0, 0))] + gate_specs + [
                    pl.BlockSpec((1, tm, GROUP_WIDTH), lambda bi, c, i: (bi, once(c, i), 0))]
        urows = tm
        blocks += _nbytes(pst.shape[1:], F32) + _nbytes(cst.shape[1:], F32)
    else:
        hblk = tm // HIST
        slab = pl.BlockSpec((1, 2, tm, LANES), lambda bi, c, i: (bi, 0, once(c, i), 0))
        ins = [act, act, h, wgate, wgate, wgate] + list(attn)
        in_specs = [act_spec,
                    pl.BlockSpec((1, HIST, ACT_WIDTH),
                                 lambda bi, c, i: (bi, jnp.maximum(once(c, i) * hblk - 1, 0), 0))
                    ] + gate_specs + [slab] * 6
        scratch += [pltpu.VMEM((tm + HIST, POOL_WIDTH), F32), pltpu.VMEM((tm + HIST, CONV_CH), F32)]
        urows = 8
        blocks += 6 * _nbytes((2, tm, LANES), F32) + _nbytes((HIST, ACT_WIDTH), F32)
    scratch_bytes = (_nbytes((s, POOL_WIDTH + CONV_CH + GROUP_WIDTH), BF16) + 2 * _nbytes((tm + HIST, POOL_WIDTH), F32)
                     + 8 * _nbytes((tm, tc), F32))
    return pl.pallas_call(
        functools.partial(_branch_kernel, tm=tm, step=step),
        out_shape=(jax.ShapeDtypeStruct((b, s, D_MODEL), BF16),
                   jax.ShapeDtypeStruct((b, ni, urows, CONV_CH), F32)),
        grid=(b, D_MODEL // tc, ni),
        in_specs=in_specs + w_specs,
        out_specs=(pl.BlockSpec((1, tm, tc), lambda bi, c, i: (bi, i, c)),
                   pl.BlockSpec((1, 1, urows, CONV_CH), lambda bi, c, i: (bi, once(c, i), 0, 0))),
        scratch_shapes=scratch,
        compiler_params=pltpu.CompilerParams(
            dimension_semantics=("parallel", "arbitrary", "arbitrary"),
            vmem_limit_bytes=_vmem_limit(blocks, scratch_bytes)),
        name="branches_step" if step else "branches",
    )(*ins, wgrp, pscale, wpb, cw, wcb, wab)


def _oproj_kernel(m_ref, w_ref, x_ref, g_ref, gate_ref, o_ref):
    mix = _dot(m_ref[0], w_ref[0])
    o_ref[0] = x_ref[0] + _tail2(gate_ref) * _rms(mix, g_ref[0])


def _oproj(merged, w_o, x, norm, mod, l, *, tm):
    b, s, d = x.shape
    blocks = (_nbytes((tm, d), BF16) + _nbytes((d, d), BF16) + 2 * _nbytes((tm, d), F32)
              + _nbytes((_mod_rows(mod) + 1, d), F32))
    return pl.pallas_call(
        _oproj_kernel,
        out_shape=jax.ShapeDtypeStruct((b, s, d), F32),
        grid=(b, s // tm),
        in_specs=[
            pl.BlockSpec((1, tm, d), lambda bi, i: (bi, i, 0)),
            pl.BlockSpec((1, d, d), lambda bi, i: (l, 0, 0)),
            pl.BlockSpec((1, tm, d), lambda bi, i: (bi, i, 0)),
            norm[1](1), mod[1](2),
        ],
        out_specs=pl.BlockSpec((1, tm, d), lambda bi, i: (bi, i, 0)),
        compiler_params=pltpu.CompilerParams(
            dimension_semantics=("parallel", "parallel"),
            vmem_limit_bytes=_vmem_limit(blocks, _nbytes((tm, d), F32))),
        name="oproj",
    )(merged, w_o, x, norm[0], mod[0])


def _ffn_kernel(*refs, tm, step, nchunk):
    if step:
        (x_ref, g2_ref, sc_ref, sh_ref, wg_ref, wv_ref, cwg_ref, cwv_ref, wd_ref, g3_ref, gate_ref,
         stg_ref, stv_ref, o_ref, tg_ref, tv_ref, h_ref) = refs
        hist = 0
    else:
        (x_ref, xp_ref, g2_ref, sc_ref, sh_ref, wg_ref, wv_ref, cwg_ref, cwv_ref, wd_ref, g3_ref, gate_ref,
         o_ref, tg_ref, tv_ref, h_ref, ug_ref, uv_ref) = refs
        hist = HIST
    i = pl.program_id(1)
    j = pl.program_id(2)
    last = pl.num_programs(2) - 1
    tf = wd_ref.shape[1]
    cols = pl.ds(pl.multiple_of(j * tf, tf), tf)
    cwg = cwg_ref[0, :, cols]
    cwv = cwv_ref[0, :, cols]

    if step:
        @pl.when(j == 0)
        def _():
            _modulate_rows(x_ref, g2_ref, sc_ref, sh_ref, h_ref, 0, tm)
            o_ref[...] = jnp.zeros_like(o_ref)

        h = h_ref[...]
        up_g = _dot(h, wg_ref[0])
        up_v = _dot(h, wv_ref[0])
        uc_g = stg_ref[0, 0] * cwg[0:1] + stg_ref[0, 1] * cwg[1:2] + up_g * cwg[2:3]
        uc_v = stv_ref[0, 0] * cwv[0:1] + stv_ref[0, 1] * cwv[1:2] + up_v * cwv[2:3]
        tg_ref[0, 0, :, cols] = up_g
        tv_ref[0, 0, :, cols] = up_v
        o_ref[0] += _dot((_gelu_tanh(uc_g) * uc_v).astype(BF16), wd_ref[0])

        @pl.when(j == last)
        def _():
            o_ref[0] = x_ref[0] + _tail2(gate_ref) * _rms(o_ref[0], g3_ref[0])
        return

    def seq_step(nck, is_first, is_last):
        rc = tm // nck
        if is_first:
            _modulate_rows(xp_ref, g2_ref, sc_ref, sh_ref, h_ref, 0, HIST)
        for c in range(nck):
            r = slice(c * rc, (c + 1) * rc)
            if is_first:
                y = _rms(x_ref[0, r], g2_ref[0])
                h_ref[HIST + c * rc:HIST + (c + 1) * rc] = (y * (1.0 + _tail2(sc_ref)) + _tail2(sh_ref)).astype(BF16)
            lo = 0 if c == 0 else HIST + c * rc
            hi = HIST + (c + 1) * rc
            h = h_ref[lo:hi]
            for u_ref, w_ref in ((ug_ref, wg_ref), (uv_ref, wv_ref)):
                up = _dot(h, w_ref[0])
                if c == 0:
                    u_ref[0:HIST] = jnp.where(i == 0, 0.0, up[0:HIST])
                    u_ref[HIST:hi] = up[HIST:]
                else:
                    u_ref[lo:hi] = up

            def conv(ref, cw):
                base = HIST + c * rc
                return (ref[base - 2:base - 2 + rc] * cw[0:1] + ref[base - 1:base - 1 + rc] * cw[1:2]
                        + ref[base:base + rc] * cw[2:3])
            act = (_gelu_tanh(conv(ug_ref, cwg)) * conv(uv_ref, cwv)).astype(BF16)
            acc = _dot(act, wd_ref[0])
            if not is_first:
                acc = o_ref[0, r] + acc
            if is_last:
                acc = x_ref[0, r] + _tail2(gate_ref) * _rms(acc, g3_ref[0])
            o_ref[0, r] = acc
        tg_ref[0, 0, :, cols] = ug_ref[HIST + tm - 8:HIST + tm]
        tv_ref[0, 0, :, cols] = uv_ref[HIST + tm - 8:HIST + tm]

    pl.when(j == 0)(lambda: seq_step(2 * nchunk, True, False))
    pl.when((j > 0) & (j < last))(lambda: seq_step(nchunk, False, False))
    pl.when(j == last)(lambda: seq_step(2 * nchunk, False, True))


def _ffn(x, norm, mod, w_up, cw, w_down, state, l, *, tm, tf, step):
    b, s, d = x.shape
    ni, nj = s // tm, D_FF // tf
    trows = tm if step else 8
    w_specs = [
        pl.BlockSpec((1, d, tf), lambda bi, i, j: (l, 0, j)),
        pl.BlockSpec((1, d, tf), lambda bi, i, j: (l, 0, nj + j)),
        pl.BlockSpec((1, FFN_K, D_FF), lambda bi, i, j: (l, 0, 0)),
        pl.BlockSpec((1, FFN_K, D_FF), lambda bi, i, j: (l, 0, 1)),
        pl.BlockSpec((1, tf, d), lambda bi, i, j: (l, j, 0)),
    ]
    x_spec = pl.BlockSpec((1, tm, d), lambda bi, i, j: (bi, i, 0), pipeline_mode=pl.Buffered(1))
    hist = 0 if step else HIST
    scratch = [pltpu.VMEM((tm + hist, d), BF16)]
    blocks = (2 * _nbytes((tm, d), F32) + 3 * _nbytes((d, tf), BF16) + 2 * _nbytes((trows, tf), F32)
              + _nbytes((3 * _mod_rows(mod) + 2, d), F32))
    mods = [norm[1](2), mod[1](4), mod[1](3)]
    tailp = [norm[1](3), mod[1](5)]
    if step:
        ins = [x, norm[0], mod[0], mod[0], w_up, w_up, cw, cw, w_down, norm[0], mod[0], state, state]
        in_specs = [x_spec] + mods + w_specs + tailp + [
            pl.BlockSpec((1, FFN_K - 1, tm, tf), lambda bi, i, j: (l, 0, 0, j)),
            pl.BlockSpec((1, FFN_K - 1, tm, tf), lambda bi, i, j: (l, 0, 0, nj + j))]
        blocks += 2 * _nbytes((FFN_K - 1, tm, tf), F32)
    else:
        hblk = tm // HIST
        ins = [x, x, norm[0], mod[0], mod[0], w_up, w_up, cw, cw, w_down, norm[0], mod[0]]
        in_specs = [x_spec,
                    pl.BlockSpec((1, HIST, d), lambda bi, i, j: (bi, jnp.maximum(i * hblk - 1, 0), 0))
                    ] + mods + w_specs + tailp
        scratch += [pltpu.VMEM((tm + HIST, tf), F32), pltpu.VMEM((tm + HIST, tf), F32)]
        blocks += _nbytes((HIST, d), F32)
    scratch_bytes = _nbytes((tm + hist, d), BF16) + 6 * _nbytes((tm + hist, tf), F32)
    tail = jax.ShapeDtypeStruct((b, ni, trows, D_FF), F32)
    tail_spec = lambda: pl.BlockSpec((1, 1, trows, D_FF), lambda bi, i, j: (bi, i, 0, 0))
    return pl.pallas_call(
        functools.partial(_ffn_kernel, tm=tm, step=step, nchunk=ROW_CHUNKS),
        out_shape=(jax.ShapeDtypeStruct((b, s, d), F32), tail, tail),
        grid=(b, ni, nj),
        in_specs=in_specs,
        out_specs=(pl.BlockSpec((1, tm, d), lambda bi, i, j: (bi, i, 0)), tail_spec(), tail_spec()),
        scratch_shapes=scratch,
        compiler_params=pltpu.CompilerParams(
            dimension_semantics=("parallel", "parallel", "arbitrary"),
            vmem_limit_bytes=_vmem_limit(blocks, scratch_bytes)),
        name="ffn_step" if step else "ffn",
    )(*ins)


def _group_rel_bias(rel_bias):
    n = np.arange(N_DIL_KEYS + 1)
    max_exact = N_BUCKETS // 2
    onehot = np.zeros((N_ATTN_GROUPS, N_DIL_KEYS + 1, N_BUCKETS), np.float32)
    for g, (_, d) in enumerate(ATTN_GROUPS):
        dist = n * d
        large = max_exact + (np.log(np.maximum(dist, 1) / max_exact) / np.log(MAX_DISTANCE / max_exact)
                             * (N_BUCKETS - max_exact)).astype(np.int32)
        bucket = np.where(dist < max_exact, dist, np.minimum(large, N_BUCKETS - 1))
        onehot[g, n, bucket] = 1.0
    per_group = rel_bias.reshape(N_BUCKETS, N_ATTN_GROUPS, HEADS_PER_GROUP)
    return jnp.einsum("gnb,bgh->ghn", onehot, per_group, precision=lax.Precision.HIGHEST)


def _bias_tables(rel_bias):
    bias_g = _group_rel_bias(rel_bias).astype(F32)
    n = N_DIL_KEYS
    gh = (N_ATTN_GROUPS, HEADS_PER_GROUP)
    ext = jnp.concatenate([bias_g[:, :, ::-1], jnp.full(gh + (n,), NEG_INF, F32)], axis=-1)
    tab = jnp.tile(ext, (1, 1, n))[:, :, :n * 2 * n].reshape(N_ATTN_GROUPS, HEADS_PER_GROUP * n, 2 * n)
    step = []
    for g, (_, d) in enumerate(ATTN_GROUPS):
        hit = bias_g[g, :, :0:-1, None]
        row = jnp.concatenate([hit, jnp.full((HEADS_PER_GROUP, n, d - 1), NEG_INF, F32)], axis=-1)
        new = jnp.broadcast_to(bias_g[g, :, 0:1], (HEADS_PER_GROUP, n * d))
        step.append(jnp.concatenate([row.reshape(HEADS_PER_GROUP, n * d), new], axis=0))
    return tab, step


def _kv_pack(qkv, g, nrows):
    b = qkv.shape[0]
    k0 = ATTN_WIDTH + g * GROUP_WIDTH
    v0 = 2 * ATTN_WIDTH + g * GROUP_WIDTH
    k = qkv[:, -nrows:, k0:k0 + GROUP_WIDTH].reshape(b, nrows, HEADS_PER_GROUP, HEAD_DIM)
    v = qkv[:, -nrows:, v0:v0 + GROUP_WIDTH].reshape(b, nrows, HEADS_PER_GROUP, HEAD_DIM)
    return jnp.stack([k, v], axis=2)


def _kv_tail_kernel(*refs):
    n = len(refs) // 3
    for g in range(n):
        k_ref, v_ref, o_ref = refs[2 * g], refs[2 * g + 1], refs[2 * n + g]
        o_ref[0, 0] = k_ref[0].T
        o_ref[0, 1] = v_ref[0].T


def _kv_tails(qkv):
    b, s, _ = qkv.shape
    ins, in_specs, out_shape, out_specs, blocks = [], [], [], [], 0
    for g, (w, _) in enumerate(ATTN_GROUPS):
        nrows = min(w, s)
        assert s % nrows == 0
        for part in (1, 2):
            col = part * (ATTN_WIDTH // GROUP_WIDTH) + g
            ins.append(qkv)
            in_specs.append(pl.BlockSpec((1, nrows, GROUP_WIDTH), lambda bi, col=col, rb=s // nrows - 1: (bi, rb, col)))
        out_shape.append(jax.ShapeDtypeStruct((b, 2, GROUP_WIDTH, nrows), F32))
        out_specs.append(pl.BlockSpec((1, 2, GROUP_WIDTH, nrows), lambda bi: (bi, 0, 0, 0)))
        blocks += 4 * _nbytes((nrows, GROUP_WIDTH), F32)
    outs = pl.pallas_call(
        _kv_tail_kernel,
        out_shape=out_shape,
        grid=(b,),
        in_specs=in_specs,
        out_specs=out_specs,
        compiler_params=pltpu.CompilerParams(
            dimension_semantics=("parallel",), vmem_limit_bytes=_vmem_limit(blocks, blocks // 2)),
        name="kv_tails",
    )(*ins)
    return [o.reshape(b, 2, HEADS_PER_GROUP, HEAD_DIM, o.shape[-1]).transpose(0, 4, 1, 2, 3) for o in outs]


def _prompt_layer(x, l, P):
    b, s, _ = x.shape
    norm, mod = P["norm"](l), P["mod_p"](l)
    tm = 512
    qkv, qkv_slab, h = _proj(x, norm, mod, P["w_qkv"], l, tm=1024, tn=ATTN_WIDTH)
    act = _matmul(h, P["w_act"], l, tm=1024, tn=1024)
    ols = [_attention(qkv_slab, P["bias_tab"], g) for g in range(N_ATTN_GROUPS)]
    attn = [o for o, _ in ols] + [ls for _, ls in ols]
    merged, u_tail = _branches(act, h, attn, None, P["branch"], l, tm=tm, tc=512, step=False)
    x1 = _oproj(merged, P["w_o"], x, norm, mod, l, tm=tm)
    x2, tail_g, tail_v = _ffn(x1, norm, mod, P["w_up"], P["ffn_cw"], P["w_down"], None, l,
                              tm=1024, tf=512, step=False)
    new_kv = _kv_tails(qkv)
    new_pool = act[:, -POOL_STATE:, :POOL_WIDTH]
    new_conv = u_tail[:, -1, -(CONV_K - 1):]
    new_ffn = jnp.concatenate([tail_g[:, -1, -(FFN_K - 1):], tail_v[:, -1, -(FFN_K - 1):]], axis=-1)
    return x2, (new_kv[0], new_kv[1], new_kv[2], new_pool, new_conv, new_ffn)


def _pad_rows(a, axis):
    pad = [(0, 0)] * a.ndim
    pad[axis] = (0, SAMPLE_ROWS - a.shape[axis])
    return jnp.pad(a, pad)


def _sample_layer(x, l, P, S, nb):
    norm, mod = P["norm"](l), P["mod_s"](l)
    tm = SAMPLE_ROWS
    qkv, _, h = _proj(x, norm, mod, P["w_qkv"], l, tm=tm, tn=ATTN_WIDTH)
    act = _matmul(h, P["w_act"], l, tm=tm, tn=1024)
    heads = qkv[0, :nb].reshape(nb, 3 * N_ATTN_GROUPS * HEADS_PER_GROUP, HEAD_DIM).transpose(0, 2, 1)
    attn = _attention_step(heads, S["caches"], P["bias_step"], l, nb)
    attn = _pad_rows(attn.reshape(1, nb, GROUP_WIDTH), 1)
    merged, u = _branches(act, h, attn, (S["pool_t"], S["conv_t"]), P["branch"], l, tm=tm, tc=512, step=True)
    x1 = _oproj(merged, P["w_o"], x, norm, mod, l, tm=tm)
    x2, up_g, up_v = _ffn(x1, norm, mod, P["w_up"], P["ffn_cw"], P["w_down"], S["ffn_t"], l,
                          tm=tm, tf=512, step=True)
    new_kv = [_kv_pack(qkv[0, :nb, None], g, 1) for g in range(N_ATTN_GROUPS)]
    new_pool = jnp.concatenate([S["pool"][l][:, 1:], act[0, :nb, None, :POOL_WIDTH]], axis=1)
    new_conv = jnp.concatenate([S["conv"][l][:, 1:], u[0, 0, :nb, None]], axis=1)
    up_new = jnp.concatenate([up_g[0, 0, :nb], up_v[0, 0, :nb]], axis=-1)
    new_ffn = jnp.concatenate([S["ffn"][l][:, 1:], up_new[:, None]], axis=1)
    return x2, (new_kv[0], new_kv[1], new_kv[2], new_pool, new_conv, new_ffn)


def kernel(x_prompt, x_sample, c_prompt, c_sample, cache_kv_w128, cache_kv_w512, cache_kv_w2048, state_pool, state_conv, state_ffn_conv, rel_bias, norm_g, w_ada, b_ada, w_in, w_attn_br, w_pool_grp, pool_scale, w_pool_br, conv_w, w_conv_br, w_o, w_up, ffn_conv_w, w_down):
    nbp = x_prompt.shape[0]
    nbs, tdec, _ = x_sample.shape
    assert tdec == 1 and nbs <= SAMPLE_ROWS
    caches = (cache_kv_w128, cache_kv_w512, cache_kv_w2048)
    for (w, d), c in zip(ATTN_GROUPS, caches):
        assert c.shape[2] == w == N_DIL_KEYS * d, "cache must hold exactly one window"

    bias_tab, bias_step = _bias_tables(rel_bias)
    c_rows = -(-(SAMPLE_ROWS + nbp) // 8) * 8
    c_all = jnp.zeros((c_rows, D_MODEL), F32).at[:nbs].set(c_sample).at[SAMPLE_ROWS:SAMPLE_ROWS + nbp].set(c_prompt)
    mod_all = _ada(c_all, w_ada, b_ada)

    w_qkv, w_act, w_gate = _cast_split(
        w_in, [(0, QKV_WIDTH), (QKV_WIDTH, QKV_WIDTH + ACT_WIDTH), (QKV_WIDTH + ACT_WIDTH, w_in.shape[2])], tk=256)
    P = dict(
        norm=functools.partial(_norm_view, norm_g),
        mod_p=functools.partial(_mod_view, mod_all, prompt=True),
        mod_s=functools.partial(_mod_view, mod_all, prompt=False),
        w_qkv=w_qkv, w_act=w_act,
        branch=(w_pool_grp.astype(BF16), pool_scale[:, None, :], w_pool_br.astype(BF16),
                conv_w, w_conv_br.astype(BF16), w_attn_br.astype(BF16), w_gate),
        w_o=w_o.astype(BF16), w_up=w_up.astype(BF16), ffn_cw=ffn_conv_w, w_down=w_down.astype(BF16),
        bias_tab=bias_tab, bias_step=bias_step,
    )
    S = dict(
        caches=[c.transpose(0, 1, 3, 4, 5, 2) for c in caches],
        pool=state_pool, conv=state_conv, ffn=state_ffn_conv,
        pool_t=_pad_rows(state_pool.transpose(0, 2, 1, 3), 2),
        conv_t=_pad_rows(state_conv.transpose(0, 2, 1, 3), 2),
        ffn_t=_pad_rows(state_ffn_conv.transpose(0, 2, 1, 3), 2),
    )
    yp = x_prompt
    ys = _pad_rows(x_sample.reshape(1, nbs, D_MODEL), 1)
    st_p, st_s = [], []
    for l in range(DEPTH):
        yp, sp = _prompt_layer(yp, l, P)
        ys, ss = _sample_layer(ys, l, P, S, nbs)
        st_p.append(sp)
        st_s.append(ss)
    outs_p = [jnp.stack([s[k] for s in st_p]) for k in range(6)]
    outs_s = [jnp.stack([s[k] for s in st_s]) for k in range(6)]
    return (yp, ys[0, :nbs, None, :], *outs_p, *outs_s)
```

```python
import functools

import numpy as np
import jax
import jax.numpy as jnp
from jax import lax
from jax.experimental import pallas as pl
from jax.experimental.pallas import tpu as pltpu

F32 = jnp.float32
BF16 = jnp.bfloat16

D_MODEL = 2048
DEPTH = 2
HEAD_DIM = 64
HEADS_PER_GROUP = 4
ATTN_GROUPS = ((128, 1), (512, 4), (2048, 16))
N_ATTN_GROUPS = len(ATTN_GROUPS)
ATTN_WIDTH = N_ATTN_GROUPS * HEADS_PER_GROUP * HEAD_DIM
GROUP_WIDTH = HEADS_PER_GROUP * HEAD_DIM
N_DIL_KEYS = 128
N_BUCKETS = 32
MAX_DISTANCE = 2048
ATTN_SCALE = HEAD_DIM ** -0.5
POOL_WINDOWS = (2, 4, 8, 16)
POOL_GROUP = 128
POOL_WIDTH = 512
POOL_STATE = 15
CONV_CH = 512
CONV_K = 3
D_FF = 5632
FFN_K = 3
N_MOD = 6
N_NORM = 4
EPS = 1e-6
NEG_INF = -1e30

LANES = 128
HIST = 16
SAMPLE_ROWS = 16
QKV_WIDTH = 3 * ATTN_WIDTH
QKV_SLABS = QKV_WIDTH // LANES
ACT_WIDTH = POOL_WIDTH + 3 * CONV_CH
ATTN_CLASS_UNROLL = 8
ROW_CHUNKS = 2
VMEM_CAP = 56 * 1024 * 1024


def _vmem_limit(block_bytes, scratch_bytes=0):
    del block_bytes, scratch_bytes
    return VMEM_CAP


def _nbytes(shape, dtype):
    return int(np.prod(shape)) * jnp.dtype(dtype).itemsize


def _rms(x, g):
    return x * lax.rsqrt(jnp.mean(x * x, axis=-1, keepdims=True) + EPS) * g


def _sigmoid(x):
    return 0.5 + 0.5 * jnp.tanh(0.5 * x)


def _gelu_tanh(x):
    return 0.5 * x * (1.0 + jnp.tanh(np.sqrt(2.0 / np.pi) * (x + 0.044715 * (x * x * x))))


def _dot(a, b):
    return jnp.dot(a, b, preferred_element_type=F32)


def _tail2(ref):
    return ref[(0,) * (len(ref.shape) - 2)]


def _norm_view(norm_g, l):
    arr = norm_g.reshape(DEPTH * N_NORM, 1, D_MODEL)
    return arr, lambda k: pl.BlockSpec((1, 1, D_MODEL), lambda *_: (l * N_NORM + k, 0, 0))


def _mod_view(mod_all, l, prompt):
    if prompt:
        arr = mod_all.reshape(DEPTH, mod_all.shape[1], N_MOD, 1, D_MODEL)
        return arr, lambda k: pl.BlockSpec((1, 1, 1, 1, D_MODEL), lambda bi, *_: (l, SAMPLE_ROWS + bi, k, 0, 0))
    return mod_all, lambda k: pl.BlockSpec((1, SAMPLE_ROWS, D_MODEL), lambda bi, *_: (l, 0, k))


def _mod_rows(mod):
    return mod[1](0).block_shape[-2]


def _cast_split_kernel(w_ref, *o_refs, bounds):
    for o_ref, (lo, hi) in zip(o_refs, bounds):
        o_ref[...] = w_ref[:, :, lo:hi].astype(BF16)


def _cast_split(w, bounds, *, tk):
    nl, k, n = w.shape
    blocks = _nbytes((tk, n), F32) + sum(_nbytes((tk, hi - lo), BF16) for lo, hi in bounds)
    return pl.pallas_call(
        functools.partial(_cast_split_kernel, bounds=tuple(bounds)),
        out_shape=[jax.ShapeDtypeStruct((nl, k, hi - lo), BF16) for lo, hi in bounds],
        grid=(nl, k // tk),
        in_specs=[pl.BlockSpec((1, tk, n), lambda li, ki: (li, ki, 0))],
        out_specs=[pl.BlockSpec((1, tk, hi - lo), lambda li, ki: (li, ki, 0)) for lo, hi in bounds],
        compiler_params=pltpu.CompilerParams(
            dimension_semantics=("parallel", "parallel"), vmem_limit_bytes=_vmem_limit(blocks)),
        name="cast_split",
    )(w)


def _ada_kernel(c_ref, w_ref, b_ref, o_ref):
    c = c_ref[...]
    s = (c * _sigmoid(c)).astype(BF16)
    o_ref[0] = _dot(s, w_ref[0].astype(BF16)) + b_ref[0]


def _ada(c_all, w_ada, b_ada):
    rows = c_all.shape[0]
    n = w_ada.shape[-1]
    tn = 1024
    blocks = _nbytes((rows, D_MODEL), F32) + _nbytes((D_MODEL, tn), F32) + _nbytes((rows + 1, tn), F32)
    return pl.pallas_call(
        _ada_kernel,
        out_shape=jax.ShapeDtypeStruct((DEPTH, rows, n), F32),
        grid=(DEPTH, n // tn),
        in_specs=[
            pl.BlockSpec((rows, D_MODEL), lambda l, j: (0, 0)),
            pl.BlockSpec((1, D_MODEL, tn), lambda l, j: (l, 0, j)),
            pl.BlockSpec((1, 1, tn), lambda l, j: (l, 0, j)),
        ],
        out_specs=pl.BlockSpec((1, rows, tn), lambda l, j: (l, 0, j)),
        compiler_params=pltpu.CompilerParams(
            dimension_semantics=("parallel", "parallel"),
            vmem_limit_bytes=_vmem_limit(blocks, _nbytes((D_MODEL, tn), BF16))),
        name="ada",
    )(c_all, w_ada, b_ada.reshape(DEPTH, 1, n))


def _modulate_rows(x_ref, g_ref, sc_ref, sh_ref, h_ref, row0, tm):
    y = _rms(x_ref[0], g_ref[0])
    h_ref[row0:row0 + tm] = (y * (1.0 + _tail2(sc_ref)) + _tail2(sh_ref)).astype(BF16)


def _proj_kernel(x_ref, g_ref, sc_ref, sh_ref, w_ref, o_ref, h_ref, *, slabs, tm, nchunk):
    def project(r):
        res = _dot(h_ref[0, r], w_ref[0])
        if slabs:
            for s in range(slabs):
                o_ref[0, s, r] = res[:, s * LANES:(s + 1) * LANES]
        else:
            o_ref[0, r] = res

    @pl.when(pl.program_id(2) == 0)
    def _():
        rc = tm // nchunk
        for c in range(nchunk):
            r = slice(c * rc, (c + 1) * rc)
            y = _rms(x_ref[0, r], g_ref[0])
            h_ref[0, r] = (y * (1.0 + _tail2(sc_ref)) + _tail2(sh_ref)).astype(BF16)
            project(r)

    @pl.when(pl.program_id(2) > 0)
    def _():
        project(slice(None))


def _proj(x, norm, mod, w, l, *, tm, tn, slab_out):
    b, s, d = x.shape
    n = w.shape[2]
    slabs = tn // LANES if slab_out else 0
    r = _mod_rows(mod)
    blocks = (_nbytes((tm, d), F32) + _nbytes((2 * r + 1, d), F32) + _nbytes((d, tn), BF16)
              + _nbytes((tm, tn), F32) + _nbytes((tm, d), BF16))
    if slab_out:
        out = jax.ShapeDtypeStruct((b, n // LANES, s, LANES), F32)
        out_spec = pl.BlockSpec((1, slabs, tm, LANES), lambda bi, i, j: (bi, j, i, 0))
    else:
        out = jax.ShapeDtypeStruct((b, s, n), F32)
        out_spec = pl.BlockSpec((1, tm, tn), lambda bi, i, j: (bi, i, j))
    return pl.pallas_call(
        functools.partial(_proj_kernel, slabs=slabs, tm=tm, nchunk=2 * ROW_CHUNKS if r == 1 else 1),
        out_shape=[out, jax.ShapeDtypeStruct((b, s, d), BF16)],
        grid=(b, s // tm, n // tn),
        in_specs=[
            pl.BlockSpec((1, tm, d), lambda bi, i, j: (bi, i, 0)),
            norm[1](0), mod[1](1), mod[1](0),
            pl.BlockSpec((1, d, tn), lambda bi, i, j: (l, 0, j)),
        ],
        out_specs=[out_spec, pl.BlockSpec((1, tm, d), lambda bi, i, j: (bi, i, 0))],
        compiler_params=pltpu.CompilerParams(
            dimension_semantics=("parallel", "parallel", "arbitrary"),
            vmem_limit_bytes=_vmem_limit(blocks, _nbytes((tm, tn), F32) + _nbytes((tm, d), F32))),
        name="proj_norm",
    )(x, norm[0], mod[0], mod[0], w)


def _matmul_kernel(h_ref, w_ref, o_ref):
    o_ref[0] = _dot(h_ref[0], w_ref[0])


def _matmul(h, w, l, *, tm, tn):
    b, s, d = h.shape
    n = w.shape[2]
    blocks = _nbytes((tm, d), BF16) + _nbytes((d, tn), BF16) + _nbytes((tm, tn), F32)
    return pl.pallas_call(
        _matmul_kernel,
        out_shape=jax.ShapeDtypeStruct((b, s, n), F32),
        grid=(b, s // tm, n // tn),
        in_specs=[pl.BlockSpec((1, tm, d), lambda bi, i, j: (bi, i, 0)),
                  pl.BlockSpec((1, d, tn), lambda bi, i, j: (l, 0, j))],
        out_specs=pl.BlockSpec((1, tm, tn), lambda bi, i, j: (bi, i, j)),
        compiler_params=pltpu.CompilerParams(
            dimension_semantics=("parallel", "parallel", "parallel"),
            vmem_limit_bytes=_vmem_limit(blocks, _nbytes((tm, tn), F32))),
        name="proj_act",
    )(h, w)


def _head_masks(rows):
    lane = lax.broadcasted_iota(jnp.int32, (rows, GROUP_WIDTH), 1)
    return [(lane >= h * HEAD_DIM) & (lane < (h + 1) * HEAD_DIM) for h in range(HEADS_PER_GROUP)]


def _attn_kernel(q_ref, kc_ref, kp_ref, vc_ref, vp_ref, bias_ref, o_ref, lse_ref, edge_ref, *, d, sb):
    i = pl.program_id(1)
    nq = N_DIL_KEYS
    span = nq * d
    ncb = sb // span
    hm = _head_masks(1)
    col = lax.broadcasted_iota(jnp.int32, (1, 2 * nq), 1)
    edge_ref[...] = jnp.where((col < nq) & (i == 0), NEG_INF, bias_ref[0])

    def rows(start):
        return pl.ds(start, nq, stride=d) if d > 1 else pl.ds(start, nq)

    def load(ref, start):
        return jnp.concatenate([ref[0, s, rows(start), :] for s in range(2)], axis=1)

    def one_block(r, jb):
        qs = jb * span + r
        q = load(q_ref, qs) * ATTN_SCALE
        if jb == 0:
            lo = sb - span + r
            k_lo, v_lo = load(kp_ref, lo), load(vp_ref, lo)
        else:
            lo = (jb - 1) * span + r
            k_lo, v_lo = load(kc_ref, lo), load(vc_ref, lo)
        kcat = jnp.concatenate([k_lo, load(kc_ref, qs)], axis=0).astype(BF16)
        vcat = jnp.concatenate([v_lo, load(vc_ref, qs)], axis=0).astype(BF16)
        qm = jnp.concatenate([jnp.where(hm[h], q, 0.0) for h in range(HEADS_PER_GROUP)], axis=0).astype(BF16)
        s = lax.dot_general(qm, kcat, (((1,), (1,)), ((), ())), preferred_element_type=F32)
        s = s + (edge_ref[...] if jb == 0 else bias_ref[0])
        m = jnp.max(s, axis=-1, keepdims=True)
        p = jnp.exp(s - m)
        l = jnp.sum(p, axis=-1, keepdims=True)
        oall = _dot((p * (1.0 / l)).astype(BF16), vcat)
        lse = m + jnp.log(l)
        o = jnp.zeros((nq, GROUP_WIDTH), F32)
        ls = jnp.zeros((nq, GROUP_WIDTH), F32)
        for h in range(HEADS_PER_GROUP):
            o = jnp.where(hm[h], oall[h * nq:(h + 1) * nq], o)
            ls = jnp.where(hm[h], lse[h * nq:(h + 1) * nq], ls)
        for sl in range(2):
            o_ref[0, sl, rows(qs), :] = o[:, sl * LANES:(sl + 1) * LANES]
            lse_ref[0, sl, rows(qs), :] = ls[:, sl * LANES:(sl + 1) * LANES]

    if d == 1:
        for jb in range(ncb):
            one_block(0, jb)
    else:
        def body(r, carry):
            for jb in range(ncb):
                one_block(r, jb)
            return carry
        lax.fori_loop(0, d, body, 0, unroll=min(d, ATTN_CLASS_UNROLL))


def _attention(qkv, bias_tab, g):
    b, _, s, _ = qkv.shape
    d = ATTN_GROUPS[g][1]
    sb = max(N_DIL_KEYS * d, 512)
    blk = (1, 2, sb, LANES)
    kslab, vslab = ATTN_WIDTH // GROUP_WIDTH + g, 2 * ATTN_WIDTH // GROUP_WIDTH + g
    prev = lambda i: jnp.maximum(i - 1, 0)
    blocks = 7 * _nbytes(blk, F32) + _nbytes((4 * N_DIL_KEYS, 2 * N_DIL_KEYS), F32)
    out = jax.ShapeDtypeStruct((b, 2, s, LANES), F32)
    return pl.pallas_call(
        functools.partial(_attn_kernel, d=d, sb=sb),
        out_shape=(out, out),
        grid=(b, s // sb),
        in_specs=[
            pl.BlockSpec(blk, lambda bi, i: (bi, g, i, 0)),
            pl.BlockSpec(blk, lambda bi, i: (bi, kslab, i, 0)),
            pl.BlockSpec(blk, lambda bi, i: (bi, kslab, prev(i), 0)),
            pl.BlockSpec(blk, lambda bi, i: (bi, vslab, i, 0)),
            pl.BlockSpec(blk, lambda bi, i: (bi, vslab, prev(i), 0)),
            pl.BlockSpec((1, 4 * N_DIL_KEYS, 2 * N_DIL_KEYS), lambda bi, i: (g, 0, 0)),
        ],
        out_specs=(pl.BlockSpec(blk, lambda bi, i: (bi, 0, i, 0)),
                   pl.BlockSpec(blk, lambda bi, i: (bi, 0, i, 0))),
        scratch_shapes=[pltpu.VMEM((HEADS_PER_GROUP * N_DIL_KEYS, 2 * N_DIL_KEYS), F32)],
        compiler_params=pltpu.CompilerParams(
            dimension_semantics=("parallel", "parallel"),
            vmem_limit_bytes=_vmem_limit(blocks, 8 << 20)),
        name=f"attn_d{d}",
    )(qkv, qkv, qkv, qkv, qkv, bias_tab)


def _attn_step_kernel(qkv_ref, c0_ref, c1_ref, c2_ref, b0_ref, b1_ref, b2_ref, o_ref):
    outs = [[None] * N_ATTN_GROUPS for _ in range(HEADS_PER_GROUP)]
    lses = [[None] * N_ATTN_GROUPS for _ in range(HEADS_PER_GROUP)]
    for g, (c_ref, b_ref) in enumerate(((c0_ref, b0_ref), (c1_ref, b1_ref), (c2_ref, b2_ref))):
        for h in range(HEADS_PER_GROUP):
            col = lambda part: (part * N_ATTN_GROUPS + g) * HEADS_PER_GROUP + h
            q = qkv_ref[0, :, col(0):col(0) + 1] * ATTN_SCALE
            kn = qkv_ref[0, :, col(1):col(1) + 1]
            vn = qkv_ref[0, :, col(2):col(2) + 1]
            s_c = jnp.sum(c_ref[0, 0, 0, h] * q, axis=0, keepdims=True) + b_ref[h:h + 1, :]
            s_n = jnp.sum(kn * q, axis=0, keepdims=True) + b_ref[HEADS_PER_GROUP + h:HEADS_PER_GROUP + h + 1, 0:1]
            m = jnp.maximum(jnp.max(s_c, axis=-1, keepdims=True), s_n)
            p_c = jnp.exp(s_c - m)
            p_n = jnp.exp(s_n - m)
            l = jnp.sum(p_c, axis=-1, keepdims=True) + p_n
            inv = 1.0 / l
            outs[h][g] = jnp.sum(c_ref[0, 0, 1, h] * (p_c * inv), axis=-1, keepdims=True) + (p_n * inv) * vn
            lses[h][g] = m + jnp.log(l)
    for h in range(HEADS_PER_GROUP):
        ls, os_ = lses[h], outs[h]
        mx = jnp.maximum(jnp.maximum(ls[0], ls[1]), ls[2])
        w = [jnp.exp(x - mx) for x in ls]
        tot = w[0] + w[1] + w[2]
        o_ref[0, h] = (w[0] * os_[0] + w[1] * os_[1] + w[2] * os_[2]) * (1.0 / tot)


def _attention_step(qkv, caches, biases, l, nb):
    cache_specs = [pl.BlockSpec((1, 1) + c.shape[2:], lambda bi: (l, bi, 0, 0, 0, 0)) for c in caches]
    bias_specs = [pl.BlockSpec(b.shape, lambda bi: (0, 0)) for b in biases]
    blocks = (sum(_nbytes(c.shape[2:], F32) for c in caches) + sum(_nbytes(b.shape, F32) for b in biases)
              + 13 * HEADS_PER_GROUP * HEAD_DIM * LANES * 4)
    return pl.pallas_call(
        _attn_step_kernel,
        out_shape=jax.ShapeDtypeStruct((nb, HEADS_PER_GROUP, HEAD_DIM, 1), F32),
        grid=(nb,),
        in_specs=[pl.BlockSpec((1,) + qkv.shape[1:], lambda bi: (bi, 0, 0))] + cache_specs + bias_specs,
        out_specs=pl.BlockSpec((1, HEADS_PER_GROUP, HEAD_DIM, 1), lambda bi: (bi, 0, 0, 0)),
        compiler_params=pltpu.CompilerParams(
            dimension_semantics=("parallel",),
            vmem_limit_bytes=_vmem_limit(blocks, 8 << 20)),
        name="attn_step",
    )(qkv, *caches, *biases)


def _branch_kernel(*refs, tm, step):
    if step:
        (act_ref, pst_ref, cst_ref, h_ref, wga_ref, wgb_ref, wgc_ref, at_ref,
         wgrp_ref, pscale_ref, wpb_ref, cw_ref, wcb_ref, wab_ref,
         out_ref, u_ref, pzs_ref, cb_ref, comb_ref) = refs
    else:
        (act_ref, hist_ref, h_ref, wga_ref, wgb_ref, wgc_ref, o0_ref, o1_ref, o2_ref, l0_ref, l1_ref, l2_ref,
         wgrp_ref, pscale_ref, wpb_ref, cw_ref, wcb_ref, wab_ref,
         out_ref, u_ref, pzs_ref, cb_ref, comb_ref, pe_ref, ue_ref) = refs
    c_tile = pl.program_id(1)
    i = pl.program_id(2)
    rows = pl.ds(pl.multiple_of(i * tm, tm), tm)
    pzs_ref, cb_ref, comb_ref = pzs_ref.at[rows], cb_ref.at[rows], comb_ref.at[rows]

    def sequence_mixers():
        p = act_ref[0, :, 0:POOL_WIDTH]
        gate_b = act_ref[0, :, POOL_WIDTH:POOL_WIDTH + CONV_CH]
        u = act_ref[0, :, POOL_WIDTH + CONV_CH:POOL_WIDTH + 2 * CONV_CH] * \
            act_ref[0, :, POOL_WIDTH + 2 * CONV_CH:POOL_WIDTH + 3 * CONV_CH]
        cw = cw_ref[0]
        if step:
            acc = p
            sums = {}
            for k in range(1, max(POOL_WINDOWS)):
                acc = acc + pst_ref[0, POOL_STATE - k]
                sums[k + 1] = acc
            means = [sums[w][:, gi * POOL_GROUP:(gi + 1) * POOL_GROUP] * (1.0 / w)
                     for gi, w in enumerate(POOL_WINDOWS)]
            conv = cst_ref[0, 0] * cw[0:1] + cst_ref[0, 1] * cw[1:2] + u * cw[2:3]
            u_ref[0, 0] = u
            comb_ref[...] = at_ref[0].astype(BF16)
        else:
            first = i == 0
            hist_p = hist_ref[0, :, 0:POOL_WIDTH]
            hist_u = hist_ref[0, :, POOL_WIDTH + CONV_CH:POOL_WIDTH + 2 * CONV_CH] * \
                hist_ref[0, :, POOL_WIDTH + 2 * CONV_CH:POOL_WIDTH + 3 * CONV_CH]
            pe_ref[0:HIST] = jnp.where(first, 0.0, hist_p)
            ue_ref[0:HIST] = jnp.where(first, 0.0, hist_u)
            pe_ref[HIST:HIST + tm] = p
            ue_ref[HIST:HIST + tm] = u
            t = i * tm + lax.broadcasted_iota(jnp.int32, (tm, 1), 0)
            means = []
            for gi, w in enumerate(POOL_WINDOWS):
                cs = slice(gi * POOL_GROUP, (gi + 1) * POOL_GROUP)
                acc = pe_ref[HIST:HIST + tm, cs]
                for k in range(1, w):
                    acc = acc + pe_ref[HIST - k:HIST - k + tm, cs]
                cnt = jnp.minimum(t + 1, w).astype(F32)
                means.append(acc * (1.0 / cnt))
            conv = (ue_ref[HIST - 2:HIST - 2 + tm] * cw[0:1] + ue_ref[HIST - 1:HIST - 1 + tm] * cw[1:2]
                    + u * cw[2:3])
            u_ref[0, 0] = ue_ref[HIST + tm - 8:HIST + tm]
            for sl in range(2):
                ls = [r[0, sl] for r in (l0_ref, l1_ref, l2_ref)]
                os_ = [r[0, sl] for r in (o0_ref, o1_ref, o2_ref)]
                mx = jnp.maximum(jnp.maximum(ls[0], ls[1]), ls[2])
                w_ = [jnp.exp(x - mx) for x in ls]
                tot = w_[0] + w_[1] + w_[2]
                comb = (w_[0] * os_[0] + w_[1] * os_[1] + w_[2] * os_[2]) * (1.0 / tot)
                comb_ref[:, sl * LANES:(sl + 1) * LANES] = comb.astype(BF16)
        pscale = pscale_ref[0]
        for gi in range(len(POOL_WINDOWS)):
            cs = slice(gi * POOL_GROUP, (gi + 1) * POOL_GROUP)
            pm = (means[gi] - p[:, cs]).astype(BF16)
            pz = _dot(pm, wgrp_ref[0, gi])
            pzs_ref[:, cs] = (pz * pscale[:, cs]).astype(BF16)
        cb_ref[...] = (gate_b * conv).astype(BF16)

    def merge_tile():
        h = h_ref[0]
        merged = _sigmoid(_dot(h, wga_ref[0])) * _dot(pzs_ref[...], wpb_ref[0])
        merged += _sigmoid(_dot(h, wgb_ref[0])) * _dot(cb_ref[...], wcb_ref[0])
        merged += _sigmoid(_dot(h, wgc_ref[0])) * _dot(comb_ref[...], wab_ref[0])
        out_ref[0] = merged.astype(BF16)

    @pl.when(c_tile == 0)
    def _():
        sequence_mixers()
        merge_tile()

    @pl.when(c_tile > 0)
    def _():
        merge_tile()


def _branches(act, h, attn, states, weights, l, *, tm, tc, step):
    b, s, _ = act.shape
    wgrp, pscale, wpb, cw, wcb, wab, wgate = weights
    ni = s // tm
    once = lambda c, i: jnp.where(c == 0, i, ni - 1)
    gate_specs = [pl.BlockSpec((1, tm, D_MODEL), lambda bi, c, i: (bi, i, 0))] + [
        pl.BlockSpec((1, D_MODEL, tc), lambda bi, c, i, k=k: (l, 0, k * (D_MODEL // tc) + c)) for k in range(3)]
    w_specs = [
        pl.BlockSpec((1,) + wgrp.shape[1:], lambda bi, c, i: (l, 0, 0, 0)),
        pl.BlockSpec((1,) + pscale.shape[1:], lambda bi, c, i: (l, 0, 0)),
        pl.BlockSpec((1, POOL_WIDTH, tc), lambda bi, c, i: (l, 0, c)),
        pl.BlockSpec((1,) + cw.shape[1:], lambda bi, c, i: (l, 0, 0)),
        pl.BlockSpec((1, CONV_CH, tc), lambda bi, c, i: (l, 0, c)),
        pl.BlockSpec((1, GROUP_WIDTH, tc), lambda bi, c, i: (l, 0, c)),
    ]
    act_spec = pl.BlockSpec((1, tm, ACT_WIDTH), lambda bi, c, i: (bi, once(c, i), 0))
    scratch = [pltpu.VMEM((s, POOL_WIDTH), BF16), pltpu.VMEM((s, CONV_CH), BF16),
               pltpu.VMEM((s, GROUP_WIDTH), BF16)]
    blocks = (_nbytes((tm, ACT_WIDTH), F32) + _nbytes((tm + 3 * tc, D_MODEL), BF16) + _nbytes(wgrp.shape[1:], BF16)
              + _nbytes((POOL_WIDTH + CONV_CH + GROUP_WIDTH, tc), BF16) + _nbytes((tm, tc), BF16))
    if step:
        pst, cst = states
        ins = [act, pst, cst, h, wgate, wgate, wgate, attn]
        in_specs = [act_spec,
                    pl.BlockSpec((1,) + pst.shape[1:], lambda bi, c, i: (l, 0, 0, 0)),
                    pl.BlockSpec((1,) + cst.shape[1:], lambda bi, c, i: (l, 0, 0, 0))] + gate_specs + [
                    pl.BlockSpec((1, tm, GROUP_WIDTH), lambda bi, c, i: (bi, once(c, i), 0))]
        urows = tm
        blocks += _nbytes(pst.shape[1:], F32) + _nbytes(cst.shape[1:], F32)
    else:
        hblk = tm // HIST
        slab = pl.BlockSpec((1, 2, tm, LANES), lambda bi, c, i: (bi, 0, once(c, i), 0))
        ins = [act, act, h, wgate, wgate, wgate] + list(attn)
        in_specs = [act_spec,
                    pl.BlockSpec((1, HIST, ACT_WIDTH),
                                 lambda bi, c, i: (bi, jnp.maximum(once(c, i) * hblk - 1, 0), 0))
                    ] + gate_specs + [slab] * 6
        scratch += [pltpu.VMEM((tm + HIST, POOL_WIDTH), F32), pltpu.VMEM((tm + HIST, CONV_CH), F32)]
        urows = 8
        blocks += 6 * _nbytes((2, tm, LANES), F32) + _nbytes((HIST, ACT_WIDTH), F32)
    scratch_bytes = (_nbytes((s, POOL_WIDTH + CONV_CH + GROUP_WIDTH), BF16) + 2 * _nbytes((tm + HIST, POOL_WIDTH), F32)
                     + 8 * _nbytes((tm, tc), F32))
    return pl.pallas_call(
        functools.partial(_branch_kernel, tm=tm, step=step),
        out_shape=(jax.ShapeDtypeStruct((b, s, D_MODEL), BF16),
                   jax.ShapeDtypeStruct((b, ni, urows, CONV_CH), F32)),
        grid=(b, D_MODEL // tc, ni),
        in_specs=in_specs + w_specs,
        out_specs=(pl.BlockSpec((1, tm, tc), lambda bi, c, i: (bi, i, c)),
                   pl.BlockSpec((1, 1, urows, CONV_CH), lambda bi, c, i: (bi, once(c, i), 0, 0))),
        scratch_shapes=scratch,
        compiler_params=pltpu.CompilerParams(
            dimension_semantics=("parallel", "arbitrary", "arbitrary"),
            vmem_limit_bytes=_vmem_limit(blocks, scratch_bytes)),
        name="branches_step" if step else "branches",
    )(*ins, wgrp, pscale, wpb, cw, wcb, wab)


def _oproj_kernel(m_ref, w_ref, x_ref, g_ref, gate_ref, o_ref):
    mix = _dot(m_ref[0], w_ref[0])
    o_ref[0] = x_ref[0] + _tail2(gate_ref) * _rms(mix, g_ref[0])


def _oproj(merged, w_o, x, norm, mod, l, *, tm):
    b, s, d = x.shape
    blocks = (_nbytes((tm, d), BF16) + _nbytes((d, d), BF16) + 2 * _nbytes((tm, d), F32)
              + _nbytes((_mod_rows(mod) + 1, d), F32))
    return pl.pallas_call(
        _oproj_kernel,
        out_shape=jax.ShapeDtypeStruct((b, s, d), F32),
        grid=(b, s // tm),
        in_specs=[
            pl.BlockSpec((1, tm, d), lambda bi, i: (bi, i, 0)),
            pl.BlockSpec((1, d, d), lambda bi, i: (l, 0, 0)),
            pl.BlockSpec((1, tm, d), lambda bi, i: (bi, i, 0)),
            norm[1](1), mod[1](2),
        ],
        out_specs=pl.BlockSpec((1, tm, d), lambda bi, i: (bi, i, 0)),
        compiler_params=pltpu.CompilerParams(
            dimension_semantics=("parallel", "parallel"),
            vmem_limit_bytes=_vmem_limit(blocks, _nbytes((tm, d), F32))),
        name="oproj",
    )(merged, w_o, x, norm[0], mod[0])


def _ffn_kernel(*refs, tm, step, nchunk):
    if step:
        (x_ref, g2_ref, sc_ref, sh_ref, wg_ref, wv_ref, cwg_ref, cwv_ref, wd_ref, g3_ref, gate_ref,
         stg_ref, stv_ref, o_ref, tg_ref, tv_ref, h_ref) = refs
        hist = 0
    else:
        (x_ref, xp_ref, g2_ref, sc_ref, sh_ref, wg_ref, wv_ref, cwg_ref, cwv_ref, wd_ref, g3_ref, gate_ref,
         o_ref, tg_ref, tv_ref, h_ref, ug_ref, uv_ref) = refs
        hist = HIST
    i = pl.program_id(1)
    j = pl.program_id(2)
    last = pl.num_programs(2) - 1
    tf = wd_ref.shape[1]
    cols = pl.ds(pl.multiple_of(j * tf, tf), tf)
    cwg = cwg_ref[0, :, cols]
    cwv = cwv_ref[0, :, cols]

    if step:
        @pl.when(j == 0)
        def _():
            _modulate_rows(x_ref, g2_ref, sc_ref, sh_ref, h_ref, 0, tm)
            o_ref[...] = jnp.zeros_like(o_ref)

        h = h_ref[...]
        up_g = _dot(h, wg_ref[0])
        up_v = _dot(h, wv_ref[0])
        uc_g = stg_ref[0, 0] * cwg[0:1] + stg_ref[0, 1] * cwg[1:2] + up_g * cwg[2:3]
        uc_v = stv_ref[0, 0] * cwv[0:1] + stv_ref[0, 1] * cwv[1:2] + up_v * cwv[2:3]
        tg_ref[0, 0, :, cols] = up_g
        tv_ref[0, 0, :, cols] = up_v
        o_ref[0] += _dot((_gelu_tanh(uc_g) * uc_v).astype(BF16), wd_ref[0])

        @pl.when(j == last)
        def _():
            o_ref[0] = x_ref[0] + _tail2(gate_ref) * _rms(o_ref[0], g3_ref[0])
        return

    def seq_step(nck, is_first, is_last):
        rc = tm // nck
        if is_first:
            _modulate_rows(xp_ref, g2_ref, sc_ref, sh_ref, h_ref, 0, HIST)
        for c in range(nck):
            r = slice(c * rc, (c + 1) * rc)
            if is_first:
                y = _rms(x_ref[0, r], g2_ref[0])
                h_ref[HIST + c * rc:HIST + (c + 1) * rc] = (y * (1.0 + _tail2(sc_ref)) + _tail2(sh_ref)).astype(BF16)
            lo = 0 if c == 0 else HIST + c * rc
            hi = HIST + (c + 1) * rc
            h = h_ref[lo:hi]
            for u_ref, w_ref in ((ug_ref, wg_ref), (uv_ref, wv_ref)):
                up = _dot(h, w_ref[0])
                if c == 0:
                    u_ref[0:HIST] = jnp.where(i == 0, 0.0, up[0:HIST])
                    u_ref[HIST:hi] = up[HIST:]
                else:
                    u_ref[lo:hi] = up

            def conv(ref, cw):
                base = HIST + c * rc
                return (ref[base - 2:base - 2 + rc] * cw[0:1] + ref[base - 1:base - 1 + rc] * cw[1:2]
                        + ref[base:base + rc] * cw[2:3])
            act = (_gelu_tanh(conv(ug_ref, cwg)) * conv(uv_ref, cwv)).astype(BF16)
            acc = _dot(act, wd_ref[0])
            if not is_first:
                acc = o_ref[0, r] + acc
            if is_last:
                acc = x_ref[0, r] + _tail2(gate_ref) * _rms(acc, g3_ref[0])
            o_ref[0, r] = acc
        tg_ref[0, 0, :, cols] = ug_ref[HIST + tm - 8:HIST + tm]
        tv_ref[0, 0, :, cols] = uv_ref[HIST + tm - 8:HIST + tm]

    pl.when(j == 0)(lambda: seq_step(2 * nchunk, True, False))
    pl.when((j > 0) & (j < last))(lambda: seq_step(nchunk, False, False))
    pl.when(j == last)(lambda: seq_step(2 * nchunk, False, True))


def _ffn(x, norm, mod, w_up, cw, w_down, state, l, *, tm, tf, step):
    b, s, d = x.shape
    ni, nj = s // tm, D_FF // tf
    trows = tm if step else 8
    w_specs = [
        pl.BlockSpec((1, d, tf), lambda bi, i, j: (l, 0, j)),
        pl.BlockSpec((1, d, tf), lambda bi, i, j: (l, 0, nj + j)),
        pl.BlockSpec((1, FFN_K, D_FF), lambda bi, i, j: (l, 0, 0)),
        pl.BlockSpec((1, FFN_K, D_FF), lambda bi, i, j: (l, 0, 1)),
        pl.BlockSpec((1, tf, d), lambda bi, i, j: (l, j, 0)),
    ]
    x_spec = pl.BlockSpec((1, tm, d), lambda bi, i, j: (bi, i, 0), pipeline_mode=pl.Buffered(1))
    hist = 0 if step else HIST
    scratch = [pltpu.VMEM((tm + hist, d), BF16)]
    blocks = (2 * _nbytes((tm, d), F32) + 3 * _nbytes((d, tf), BF16) + 2 * _nbytes((trows, tf), F32)
              + _nbytes((3 * _mod_rows(mod) + 2, d), F32))
    mods = [norm[1](2), mod[1](4), mod[1](3)]
    tailp = [norm[1](3), mod[1](5)]
    if step:
        ins = [x, norm[0], mod[0], mod[0], w_up, w_up, cw, cw, w_down, norm[0], mod[0], state, state]
        in_specs = [x_spec] + mods + w_specs + tailp + [
            pl.BlockSpec((1, FFN_K - 1, tm, tf), lambda bi, i, j: (l, 0, 0, j)),
            pl.BlockSpec((1, FFN_K - 1, tm, tf), lambda bi, i, j: (l, 0, 0, nj + j))]
        blocks += 2 * _nbytes((FFN_K - 1, tm, tf), F32)
    else:
        hblk = tm // HIST
        ins = [x, x, norm[0], mod[0], mod[0], w_up, w_up, cw, cw, w_down, norm[0], mod[0]]
        in_specs = [x_spec,
                    pl.BlockSpec((1, HIST, d), lambda bi, i, j: (bi, jnp.maximum(i * hblk - 1, 0), 0))
                    ] + mods + w_specs + tailp
        scratch += [pltpu.VMEM((tm + HIST, tf), F32), pltpu.VMEM((tm + HIST, tf), F32)]
        blocks += _nbytes((HIST, d), F32)
    scratch_bytes = _nbytes((tm + hist, d), BF16) + 6 * _nbytes((tm + hist, tf), F32)
    tail = jax.ShapeDtypeStruct((b, ni, trows, D_FF), F32)
    tail_spec = lambda: pl.BlockSpec((1, 1, trows, D_FF), lambda bi, i, j: (bi, i, 0, 0))
    return pl.pallas_call(
        functools.partial(_ffn_kernel, tm=tm, step=step, nchunk=ROW_CHUNKS),
        out_shape=(jax.ShapeDtypeStruct((b, s, d), F32), tail, tail),
        grid=(b, ni, nj),
        in_specs=in_specs,
        out_specs=(pl.BlockSpec((1, tm, d), lambda bi, i, j: (bi, i, 0)), tail_spec(), tail_spec()),
        scratch_shapes=scratch,
        compiler_params=pltpu.CompilerParams(
            dimension_semantics=("parallel", "parallel", "arbitrary"),
            vmem_limit_bytes=_vmem_limit(blocks, scratch_bytes)),
        name="ffn_step" if step else "ffn",
    )(*ins)


def _group_rel_bias(rel_bias):
    n = np.arange(N_DIL_KEYS + 1)
    max_exact = N_BUCKETS // 2
    onehot = np.zeros((N_ATTN_GROUPS, N_DIL_KEYS + 1, N_BUCKETS), np.float32)
    for g, (_, d) in enumerate(ATTN_GROUPS):
        dist = n * d
        large = max_exact + (np.log(np.maximum(dist, 1) / max_exact) / np.log(MAX_DISTANCE / max_exact)
                             * (N_BUCKETS - max_exact)).astype(np.int32)
        bucket = np.where(dist < max_exact, dist, np.minimum(large, N_BUCKETS - 1))
        onehot[g, n, bucket] = 1.0
    per_group = rel_bias.reshape(N_BUCKETS, N_ATTN_GROUPS, HEADS_PER_GROUP)
    return jnp.einsum("gnb,bgh->ghn", onehot, per_group, precision=lax.Precision.HIGHEST)


def _bias_tables(rel_bias):
    bias_g = _group_rel_bias(rel_bias).astype(F32)
    n = N_DIL_KEYS
    gh = (N_ATTN_GROUPS, HEADS_PER_GROUP)
    ext = jnp.concatenate([bias_g[:, :, ::-1], jnp.full(gh + (n,), NEG_INF, F32)], axis=-1)
    tab = jnp.tile(ext, (1, 1, n))[:, :, :n * 2 * n].reshape(N_ATTN_GROUPS, HEADS_PER_GROUP * n, 2 * n)
    step = []
    for g, (_, d) in enumerate(ATTN_GROUPS):
        hit = bias_g[g, :, :0:-1, None]
        row = jnp.concatenate([hit, jnp.full((HEADS_PER_GROUP, n, d - 1), NEG_INF, F32)], axis=-1)
        new = jnp.broadcast_to(bias_g[g, :, 0:1], (HEADS_PER_GROUP, n * d))
        step.append(jnp.concatenate([row.reshape(HEADS_PER_GROUP, n * d), new], axis=0))
    return tab, step


def _kv_pack(qkv, g, nrows):
    b = qkv.shape[0]
    k0 = ATTN_WIDTH + g * GROUP_WIDTH
    v0 = 2 * ATTN_WIDTH + g * GROUP_WIDTH
    k = qkv[:, -nrows:, k0:k0 + GROUP_WIDTH].reshape(b, nrows, HEADS_PER_GROUP, HEAD_DIM)
    v = qkv[:, -nrows:, v0:v0 + GROUP_WIDTH].reshape(b, nrows, HEADS_PER_GROUP, HEAD_DIM)
    return jnp.stack([k, v], axis=2)


def _kv_tail_kernel(*refs):
    n = len(refs) // 3
    for g in range(n):
        o_ref = refs[2 * n + g]
        for part in range(2):
            for s in range(2):
                o_ref[0, part, s * LANES:(s + 1) * LANES, :] = refs[2 * g + part][0, s].T


def _kv_tails(qkv):
    b, _, s, _ = qkv.shape
    ins, in_specs, out_shape, out_specs, blocks = [], [], [], [], 0
    for g, (w, _) in enumerate(ATTN_GROUPS):
        nrows = min(w, s)
        assert s % nrows == 0
        for part in (1, 2):
            col = part * (ATTN_WIDTH // GROUP_WIDTH) + g
            ins.append(qkv)
            in_specs.append(pl.BlockSpec((1, 2, nrows, LANES), lambda bi, col=col, rb=s // nrows - 1: (bi, col, rb, 0)))
        out_shape.append(jax.ShapeDtypeStruct((b, 2, GROUP_WIDTH, nrows), F32))
        out_specs.append(pl.BlockSpec((1, 2, GROUP_WIDTH, nrows), lambda bi: (bi, 0, 0, 0)))
        blocks += 4 * _nbytes((nrows, GROUP_WIDTH), F32)
    outs = pl.pallas_call(
        _kv_tail_kernel,
        out_shape=out_shape,
        grid=(b,),
        in_specs=in_specs,
        out_specs=out_specs,
        compiler_params=pltpu.CompilerParams(
            dimension_semantics=("parallel",), vmem_limit_bytes=_vmem_limit(blocks, blocks // 2)),
        name="kv_tails",
    )(*ins)
    return [o.reshape(b, 2, HEADS_PER_GROUP, HEAD_DIM, o.shape[-1]).transpose(0, 4, 1, 2, 3) for o in outs]


def _prompt_layer(x, l, P):
    b, s, _ = x.shape
    norm, mod = P["norm"](l), P["mod_p"](l)
    tm = 512
    qkv_slab, h = _proj(x, norm, mod, P["w_qkv"], l, tm=1024, tn=ATTN_WIDTH, slab_out=True)
    act = _matmul(h, P["w_act"], l, tm=1024, tn=ACT_WIDTH)
    ols = [_attention(qkv_slab, P["bias_tab"], g) for g in range(N_ATTN_GROUPS)]
    attn = [o for o, _ in ols] + [ls for _, ls in ols]
    merged, u_tail = _branches(act, h, attn, None, P["branch"], l, tm=tm, tc=512, step=False)
    x1 = _oproj(merged, P["w_o"], x, norm, mod, l, tm=tm)
    x2, tail_g, tail_v = _ffn(x1, norm, mod, P["w_up"], P["ffn_cw"], P["w_down"], None, l,
                              tm=1024, tf=512, step=False)
    new_kv = _kv_tails(qkv_slab)
    new_pool = act[:, -POOL_STATE:, :POOL_WIDTH]
    new_conv = u_tail[:, -1, -(CONV_K - 1):]
    new_ffn = jnp.concatenate([tail_g[:, -1, -(FFN_K - 1):], tail_v[:, -1, -(FFN_K - 1):]], axis=-1)
    return x2, (new_kv[0], new_kv[1], new_kv[2], new_pool, new_conv, new_ffn)


def _pad_rows(a, axis):
    pad = [(0, 0)] * a.ndim
    pad[axis] = (0, SAMPLE_ROWS - a.shape[axis])
    return jnp.pad(a, pad)


def _sample_layer(x, l, P, S, nb):
    norm, mod = P["norm"](l), P["mod_s"](l)
    tm = SAMPLE_ROWS
    qkv, h = _proj(x, norm, mod, P["w_qkv"], l, tm=tm, tn=ATTN_WIDTH, slab_out=False)
    act = _matmul(h, P["w_act"], l, tm=tm, tn=1024)
    heads = qkv[0, :nb].reshape(nb, 3 * N_ATTN_GROUPS * HEADS_PER_GROUP, HEAD_DIM).transpose(0, 2, 1)
    attn = _attention_step(heads, S["caches"], P["bias_step"], l, nb)
    attn = _pad_rows(attn.reshape(1, nb, GROUP_WIDTH), 1)
    merged, u = _branches(act, h, attn, (S["pool_t"], S["conv_t"]), P["branch"], l, tm=tm, tc=512, step=True)
    x1 = _oproj(merged, P["w_o"], x, norm, mod, l, tm=tm)
    x2, up_g, up_v = _ffn(x1, norm, mod, P["w_up"], P["ffn_cw"], P["w_down"], S["ffn_t"], l,
                          tm=tm, tf=512, step=True)
    new_kv = [_kv_pack(qkv[0, :nb, None], g, 1) for g in range(N_ATTN_GROUPS)]
    new_pool = jnp.concatenate([S["pool"][l][:, 1:], act[0, :nb, None, :POOL_WIDTH]], axis=1)
    new_conv = jnp.concatenate([S["conv"][l][:, 1:], u[0, 0, :nb, None]], axis=1)
    up_new = jnp.concatenate([up_g[0, 0, :nb], up_v[0, 0, :nb]], axis=-1)
    new_ffn = jnp.concatenate([S["ffn"][l][:, 1:], up_new[:, None]], axis=1)
    return x2, (new_kv[0], new_kv[1], new_kv[2], new_pool, new_conv, new_ffn)


def kernel(x_prompt, x_sample, c_prompt, c_sample, cache_kv_w128, cache_kv_w512, cache_kv_w2048, state_pool, state_conv, state_ffn_conv, rel_bias, norm_g, w_ada, b_ada, w_in, w_attn_br, w_pool_grp, pool_scale, w_pool_br, conv_w, w_conv_br, w_o, w_up, ffn_conv_w, w_down):
    nbp = x_prompt.shape[0]
    nbs, tdec, _ = x_sample.shape
    assert tdec == 1 and nbs <= SAMPLE_ROWS
    caches = (cache_kv_w128, cache_kv_w512, cache_kv_w2048)
    for (w, d), c in zip(ATTN_GROUPS, caches):
        assert c.shape[2] == w == N_DIL_KEYS * d, "cache must hold exactly one window"

    bias_tab, bias_step = _bias_tables(rel_bias)
    c_rows = -(-(SAMPLE_ROWS + nbp) // 8) * 8
    c_all = jnp.zeros((c_rows, D_MODEL), F32).at[:nbs].set(c_sample).at[SAMPLE_ROWS:SAMPLE_ROWS + nbp].set(c_prompt)
    mod_all = _ada(c_all, w_ada, b_ada)

    w_qkv, w_act, w_gate = _cast_split(
        w_in, [(0, QKV_WIDTH), (QKV_WIDTH, QKV_WIDTH + ACT_WIDTH), (QKV_WIDTH + ACT_WIDTH, w_in.shape[2])], tk=256)
    P = dict(
        norm=functools.partial(_norm_view, norm_g),
        mod_p=functools.partial(_mod_view, mod_all, prompt=True),
        mod_s=functools.partial(_mod_view, mod_all, prompt=False),
        w_qkv=w_qkv, w_act=w_act,
        branch=(w_pool_grp.astype(BF16), pool_scale[:, None, :], w_pool_br.astype(BF16),
                conv_w, w_conv_br.astype(BF16), w_attn_br.astype(BF16), w_gate),
        w_o=w_o.astype(BF16), w_up=w_up.astype(BF16), ffn_cw=ffn_conv_w, w_down=w_down.astype(BF16),
        bias_tab=bias_tab, bias_step=bias_step,
    )
    S = dict(
        caches=[c.transpose(0, 1, 3, 4, 5, 2) for c in caches],
        pool=state_pool, conv=state_conv, ffn=state_ffn_conv,
        pool_t=_pad_rows(state_pool.transpose(0, 2, 1, 3), 2),
        conv_t=_pad_rows(state_conv.transpose(0, 2, 1, 3), 2),
        ffn_t=_pad_rows(state_ffn_conv.transpose(0, 2, 1, 3), 2),
    )
    yp = x_prompt
    ys = _pad_rows(x_sample.reshape(1, nbs, D_MODEL), 1)
    st_p, st_s = [], []
    for l in range(DEPTH):
        yp, sp = _prompt_layer(yp, l, P)
        ys, ss = _sample_layer(ys, l, P, S, nbs)
        st_p.append(sp)
        st_s.append(ss)
    outs_p = [jnp.stack([s[k] for s in st_p]) for k in range(6)]
    outs_s = [jnp.stack([s[k] for s in st_s]) for k in range(6)]
    return (yp, ys[0, :nbs, None, :], *outs_p, *outs_s)
```

```python
import functools

import numpy as np
import jax
import jax.numpy as jnp
from jax import lax
from jax.experimental import pallas as pl
from jax.experimental.pallas import tpu as pltpu

F32 = jnp.float32
BF16 = jnp.bfloat16

D_MODEL = 2048
DEPTH = 2
HEAD_DIM = 64
HEADS_PER_GROUP = 4
ATTN_GROUPS = ((128, 1), (512, 4), (2048, 16))
N_ATTN_GROUPS = len(ATTN_GROUPS)
ATTN_WIDTH = N_ATTN_GROUPS * HEADS_PER_GROUP * HEAD_DIM
GROUP_WIDTH = HEADS_PER_GROUP * HEAD_DIM
N_DIL_KEYS = 128
N_BUCKETS = 32
MAX_DISTANCE = 2048
ATTN_SCALE = HEAD_DIM ** -0.5
POOL_WINDOWS = (2, 4, 8, 16)
POOL_GROUP = 128
POOL_WIDTH = 512
POOL_STATE = 15
CONV_CH = 512
CONV_K = 3
D_FF = 5632
FFN_K = 3
N_MOD = 6
N_NORM = 4
EPS = 1e-6
NEG_INF = -1e30

LANES = 128
HIST = 16
SAMPLE_ROWS = 16
QKV_WIDTH = 3 * ATTN_WIDTH
QKV_SLABS = QKV_WIDTH // LANES
ACT_WIDTH = POOL_WIDTH + 3 * CONV_CH
ATTN_SUPER_BLOCK = 2048
ATTN_CLASS_UNROLL = 8
ROW_CHUNKS = 2
VMEM_CAP = 56 * 1024 * 1024


def _vmem_limit(block_bytes, scratch_bytes=0):
    del block_bytes, scratch_bytes
    return VMEM_CAP


def _nbytes(shape, dtype):
    return int(np.prod(shape)) * jnp.dtype(dtype).itemsize


def _rms(x, g):
    return x * lax.rsqrt(jnp.mean(x * x, axis=-1, keepdims=True) + EPS) * g


def _sigmoid(x):
    return 0.5 + 0.5 * jnp.tanh(0.5 * x)


def _gelu_tanh(x):
    return 0.5 * x * (1.0 + jnp.tanh(np.sqrt(2.0 / np.pi) * (x + 0.044715 * (x * x * x))))


def _dot(a, b):
    return jnp.dot(a, b, preferred_element_type=F32)


def _tail2(ref):
    return ref[(0,) * (len(ref.shape) - 2)]


def _norm_view(norm_g, l):
    arr = norm_g.reshape(DEPTH * N_NORM, 1, D_MODEL)
    return arr, lambda k: pl.BlockSpec((1, 1, D_MODEL), lambda *_: (l * N_NORM + k, 0, 0))


def _mod_view(mod_all, l, prompt):
    if prompt:
        arr = mod_all.reshape(DEPTH, mod_all.shape[1], N_MOD, 1, D_MODEL)
        return arr, lambda k: pl.BlockSpec((1, 1, 1, 1, D_MODEL), lambda bi, *_: (l, SAMPLE_ROWS + bi, k, 0, 0))
    return mod_all, lambda k: pl.BlockSpec((1, SAMPLE_ROWS, D_MODEL), lambda bi, *_: (l, 0, k))


def _mod_rows(mod):
    return mod[1](0).block_shape[-2]


def _cast_split_kernel(w_ref, *o_refs, bounds):
    for o_ref, (lo, hi) in zip(o_refs, bounds):
        o_ref[...] = w_ref[:, :, lo:hi].astype(BF16)


def _cast_split(w, bounds, *, tk):
    nl, k, n = w.shape
    blocks = _nbytes((tk, n), F32) + sum(_nbytes((tk, hi - lo), BF16) for lo, hi in bounds)
    return pl.pallas_call(
        functools.partial(_cast_split_kernel, bounds=tuple(bounds)),
        out_shape=[jax.ShapeDtypeStruct((nl, k, hi - lo), BF16) for lo, hi in bounds],
        grid=(nl, k // tk),
        in_specs=[pl.BlockSpec((1, tk, n), lambda li, ki: (li, ki, 0))],
        out_specs=[pl.BlockSpec((1, tk, hi - lo), lambda li, ki: (li, ki, 0)) for lo, hi in bounds],
        compiler_params=pltpu.CompilerParams(
            dimension_semantics=("parallel", "parallel"), vmem_limit_bytes=_vmem_limit(blocks)),
        name="cast_split",
    )(w)


def _ada_kernel(c_ref, w_ref, b_ref, o_ref):
    c = c_ref[...]
    s = (c * _sigmoid(c)).astype(BF16)
    o_ref[0] = _dot(s, w_ref[0].astype(BF16)) + b_ref[0]


def _ada(c_all, w_ada, b_ada):
    rows = c_all.shape[0]
    n = w_ada.shape[-1]
    tn = 1024
    blocks = _nbytes((rows, D_MODEL), F32) + _nbytes((D_MODEL, tn), F32) + _nbytes((rows + 1, tn), F32)
    return pl.pallas_call(
        _ada_kernel,
        out_shape=jax.ShapeDtypeStruct((DEPTH, rows, n), F32),
        grid=(DEPTH, n // tn),
        in_specs=[
            pl.BlockSpec((rows, D_MODEL), lambda l, j: (0, 0)),
            pl.BlockSpec((1, D_MODEL, tn), lambda l, j: (l, 0, j)),
            pl.BlockSpec((1, 1, tn), lambda l, j: (l, 0, j)),
        ],
        out_specs=pl.BlockSpec((1, rows, tn), lambda l, j: (l, 0, j)),
        compiler_params=pltpu.CompilerParams(
            dimension_semantics=("parallel", "parallel"),
            vmem_limit_bytes=_vmem_limit(blocks, _nbytes((D_MODEL, tn), BF16))),
        name="ada",
    )(c_all, w_ada, b_ada.reshape(DEPTH, 1, n))


def _modulate_rows(x_ref, g_ref, sc_ref, sh_ref, h_ref, row0, tm):
    y = _rms(x_ref[0], g_ref[0])
    h_ref[row0:row0 + tm] = (y * (1.0 + _tail2(sc_ref)) + _tail2(sh_ref)).astype(BF16)


def _proj_kernel(x_ref, g_ref, sc_ref, sh_ref, w_ref, o_ref, h_ref, *, slabs, tm, nchunk):
    def project(r):
        res = _dot(h_ref[0, r], w_ref[0])
        if slabs:
            for s in range(slabs):
                o_ref[0, s, r] = res[:, s * LANES:(s + 1) * LANES]
        else:
            o_ref[0, r] = res

    @pl.when(pl.program_id(2) == 0)
    def _():
        rc = tm // nchunk
        for c in range(nchunk):
            r = slice(c * rc, (c + 1) * rc)
            y = _rms(x_ref[0, r], g_ref[0])
            h_ref[0, r] = (y * (1.0 + _tail2(sc_ref)) + _tail2(sh_ref)).astype(BF16)
            project(r)

    @pl.when(pl.program_id(2) > 0)
    def _():
        project(slice(None))


def _proj(x, norm, mod, w, l, *, tm, tn, slab_out):
    b, s, d = x.shape
    n = w.shape[2]
    slabs = tn // LANES if slab_out else 0
    r = _mod_rows(mod)
    blocks = (_nbytes((tm, d), F32) + _nbytes((2 * r + 1, d), F32) + _nbytes((d, tn), BF16)
              + _nbytes((tm, tn), F32) + _nbytes((tm, d), BF16))
    if slab_out:
        out = jax.ShapeDtypeStruct((b, n // LANES, s, LANES), F32)
        out_spec = pl.BlockSpec((1, slabs, tm, LANES), lambda bi, i, j: (bi, j, i, 0))
    else:
        out = jax.ShapeDtypeStruct((b, s, n), F32)
        out_spec = pl.BlockSpec((1, tm, tn), lambda bi, i, j: (bi, i, j))
    return pl.pallas_call(
        functools.partial(_proj_kernel, slabs=slabs, tm=tm, nchunk=2 * ROW_CHUNKS if r == 1 else 1),
        out_shape=[out, jax.ShapeDtypeStruct((b, s, d), BF16)],
        grid=(b, s // tm, n // tn),
        in_specs=[
            pl.BlockSpec((1, tm, d), lambda bi, i, j: (bi, i, 0)),
            norm[1](0), mod[1](1), mod[1](0),
            pl.BlockSpec((1, d, tn), lambda bi, i, j: (l, 0, j)),
        ],
        out_specs=[out_spec, pl.BlockSpec((1, tm, d), lambda bi, i, j: (bi, i, 0))],
        compiler_params=pltpu.CompilerParams(
            dimension_semantics=("parallel", "parallel", "arbitrary"),
            vmem_limit_bytes=_vmem_limit(blocks, _nbytes((tm, tn), F32) + _nbytes((tm, d), F32))),
        name="proj_norm",
    )(x, norm[0], mod[0], mod[0], w)


def _matmul_kernel(h_ref, w_ref, o_ref):
    o_ref[0] = _dot(h_ref[0], w_ref[0])


def _matmul(h, w, l, *, tm, tn):
    b, s, d = h.shape
    n = w.shape[2]
    blocks = _nbytes((tm, d), BF16) + _nbytes((d, tn), BF16) + _nbytes((tm, tn), F32)
    return pl.pallas_call(
        _matmul_kernel,
        out_shape=jax.ShapeDtypeStruct((b, s, n), F32),
        grid=(b, s // tm, n // tn),
        in_specs=[pl.BlockSpec((1, tm, d), lambda bi, i, j: (bi, i, 0)),
                  pl.BlockSpec((1, d, tn), lambda bi, i, j: (l, 0, j))],
        out_specs=pl.BlockSpec((1, tm, tn), lambda bi, i, j: (bi, i, j)),
        compiler_params=pltpu.CompilerParams(
            dimension_semantics=("parallel", "parallel", "parallel"),
            vmem_limit_bytes=_vmem_limit(blocks, _nbytes((tm, tn), F32))),
        name="proj_act",
    )(h, w)


def _head_masks(rows):
    lane = lax.broadcasted_iota(jnp.int32, (rows, GROUP_WIDTH), 1)
    return [(lane >= h * HEAD_DIM) & (lane < (h + 1) * HEAD_DIM) for h in range(HEADS_PER_GROUP)]


def _attn_kernel(q_ref, kc_ref, kp_ref, vc_ref, vp_ref, bias_ref, o_ref, lse_ref, edge_ref, *, d, sb):
    i = pl.program_id(1)
    nq = N_DIL_KEYS
    span = nq * d
    ncb = sb // span
    hm = _head_masks(1)
    col = lax.broadcasted_iota(jnp.int32, (1, 2 * nq), 1)
    edge_ref[...] = jnp.where((col < nq) & (i == 0), NEG_INF, bias_ref[0])

    def rows(start):
        return pl.ds(start, nq, stride=d) if d > 1 else pl.ds(start, nq)

    def load(ref, start):
        return jnp.concatenate([ref[0, s, rows(start), :] for s in range(2)], axis=1)

    def one_block(r, jb):
        qs = jb * span + r
        q = load(q_ref, qs) * ATTN_SCALE
        if jb == 0:
            lo = sb - span + r
            k_lo, v_lo = load(kp_ref, lo), load(vp_ref, lo)
        else:
            lo = (jb - 1) * span + r
            k_lo, v_lo = load(kc_ref, lo), load(vc_ref, lo)
        kcat = jnp.concatenate([k_lo, load(kc_ref, qs)], axis=0).astype(BF16)
        vcat = jnp.concatenate([v_lo, load(vc_ref, qs)], axis=0).astype(BF16)
        qm = jnp.concatenate([jnp.where(hm[h], q, 0.0) for h in range(HEADS_PER_GROUP)], axis=0).astype(BF16)
        s = lax.dot_general(qm, kcat, (((1,), (1,)), ((), ())), preferred_element_type=F32)
        s = s + (edge_ref[...] if jb == 0 else bias_ref[0])
        m = jnp.max(s, axis=-1, keepdims=True)
        p = jnp.exp(s - m)
        l = jnp.sum(p, axis=-1, keepdims=True)
        oall = _dot((p * (1.0 / l)).astype(BF16), vcat)
        lse = m + jnp.log(l)
        o = jnp.zeros((nq, GROUP_WIDTH), F32)
        ls = jnp.zeros((nq, GROUP_WIDTH), F32)
        for h in range(HEADS_PER_GROUP):
            o = jnp.where(hm[h], oall[h * nq:(h + 1) * nq], o)
            ls = jnp.where(hm[h], lse[h * nq:(h + 1) * nq], ls)
        for sl in range(2):
            o_ref[0, sl, rows(qs), :] = o[:, sl * LANES:(sl + 1) * LANES]
            lse_ref[0, sl, rows(qs), :] = ls[:, sl * LANES:(sl + 1) * LANES]

    if d == 1:
        for jb in range(ncb):
            one_block(0, jb)
    else:
        def body(r, carry):
            for jb in range(ncb):
                one_block(r, jb)
            return carry
        lax.fori_loop(0, d, body, 0, unroll=min(d, ATTN_CLASS_UNROLL))


def _attention(qkv, bias_tab, g):
    b, _, s, _ = qkv.shape
    d = ATTN_GROUPS[g][1]
    sb = max(N_DIL_KEYS * d, ATTN_SUPER_BLOCK)
    blk = (1, 2, sb, LANES)
    kslab, vslab = ATTN_WIDTH // GROUP_WIDTH + g, 2 * ATTN_WIDTH // GROUP_WIDTH + g
    prev = lambda i: jnp.maximum(i - 1, 0)
    blocks = 7 * _nbytes(blk, F32) + _nbytes((4 * N_DIL_KEYS, 2 * N_DIL_KEYS), F32)
    out = jax.ShapeDtypeStruct((b, 2, s, LANES), F32)
    return pl.pallas_call(
        functools.partial(_attn_kernel, d=d, sb=sb),
        out_shape=(out, out),
        grid=(b, s // sb),
        in_specs=[
            pl.BlockSpec(blk, lambda bi, i: (bi, g, i, 0)),
            pl.BlockSpec(blk, lambda bi, i: (bi, kslab, i, 0)),
            pl.BlockSpec(blk, lambda bi, i: (bi, kslab, prev(i), 0)),
            pl.BlockSpec(blk, lambda bi, i: (bi, vslab, i, 0)),
            pl.BlockSpec(blk, lambda bi, i: (bi, vslab, prev(i), 0)),
            pl.BlockSpec((1, 4 * N_DIL_KEYS, 2 * N_DIL_KEYS), lambda bi, i: (g, 0, 0)),
        ],
        out_specs=(pl.BlockSpec(blk, lambda bi, i: (bi, 0, i, 0)),
                   pl.BlockSpec(blk, lambda bi, i: (bi, 0, i, 0))),
        scratch_shapes=[pltpu.VMEM((HEADS_PER_GROUP * N_DIL_KEYS, 2 * N_DIL_KEYS), F32)],
        compiler_params=pltpu.CompilerParams(
            dimension_semantics=("parallel", "parallel"),
            vmem_limit_bytes=_vmem_limit(blocks, 8 << 20)),
        name=f"attn_d{d}",
    )(qkv, qkv, qkv, qkv, qkv, bias_tab)


def _attn_step_kernel(qkv_ref, c0_ref, c1_ref, c2_ref, b0_ref, b1_ref, b2_ref, o_ref):
    b = pl.program_id(0)
    rows = SAMPLE_ROWS
    lane = lax.broadcasted_iota(jnp.int32, (rows, GROUP_WIDTH), 1)
    row = lax.broadcasted_iota(jnp.int32, (rows, GROUP_WIDTH), 0)
    sel = (lane >= row * HEAD_DIM) & (lane < (row + 1) * HEAD_DIM)

    def rowvec(col0):
        return qkv_ref[0, pl.ds(b, 1), col0:col0 + GROUP_WIDTH]

    outs, lses = [], []
    for g, (c_ref, b_ref) in enumerate(((c0_ref, b0_ref), (c1_ref, b1_ref), (c2_ref, b2_ref))):
        q = rowvec(g * GROUP_WIDTH) * ATTN_SCALE
        kn = rowvec(ATTN_WIDTH + g * GROUP_WIDTH).astype(BF16).astype(F32)
        vn = rowvec(2 * ATTN_WIDTH + g * GROUP_WIDTH).astype(BF16).astype(F32)
        q4 = jnp.where(sel, jnp.broadcast_to(q, (rows, GROUP_WIDTH)), 0.0).astype(BF16)
        s_c = _dot(q4, c_ref[0, 0, 0].astype(BF16)) + b_ref[0:rows, :]
        s_n = jnp.sum(q4.astype(F32) * kn, axis=-1, keepdims=True) + b_ref[rows:2 * rows, 0:1]
        m = jnp.maximum(jnp.max(s_c, axis=-1, keepdims=True), s_n)
        p_c = jnp.exp(s_c - m)
        p_n = jnp.exp(s_n - m)
        l = jnp.sum(p_c, axis=-1, keepdims=True) + p_n
        inv = 1.0 / l
        oc = lax.dot_general((p_c * inv).astype(BF16), c_ref[0, 0, 1].astype(BF16), (((1,), (1,)), ((), ())),
                             preferred_element_type=F32)
        oc = oc + (p_n * inv).astype(BF16).astype(F32) * vn
        lse = m + jnp.log(l)
        outs.append(jnp.sum(jnp.where(sel, oc, 0.0), axis=0, keepdims=True))
        lses.append(jnp.sum(jnp.where(sel, lse, 0.0), axis=0, keepdims=True))
    mx = jnp.maximum(jnp.maximum(lses[0], lses[1]), lses[2])
    w = [jnp.exp(ls - mx) for ls in lses]
    tot = w[0] + w[1] + w[2]
    o_ref[0] = (w[0] * outs[0] + w[1] * outs[1] + w[2] * outs[2]) * (1.0 / tot)


def _attention_step(qkv, caches, biases, l, nb):
    cache_specs = [pl.BlockSpec((1, 1) + c.shape[2:], lambda bi: (l, bi, 0, 0, 0)) for c in caches]
    bias_specs = [pl.BlockSpec(b.shape, lambda bi: (0, 0)) for b in biases]
    blocks = (sum(_nbytes(c.shape[2:], F32) for c in caches) + sum(_nbytes(b.shape, F32) for b in biases)
              + _nbytes(qkv.shape, F32))
    return pl.pallas_call(
        _attn_step_kernel,
        out_shape=jax.ShapeDtypeStruct((nb, 1, GROUP_WIDTH), F32),
        grid=(nb,),
        in_specs=[pl.BlockSpec(qkv.shape, lambda bi: (0, 0, 0))] + cache_specs + bias_specs,
        out_specs=pl.BlockSpec((1, 1, GROUP_WIDTH), lambda bi: (bi, 0, 0)),
        compiler_params=pltpu.CompilerParams(
            dimension_semantics=("parallel",),
            vmem_limit_bytes=_vmem_limit(blocks, 8 << 20)),
        name="attn_step",
    )(qkv, *caches, *biases)


def _branch_kernel(*refs, tm, step):
    if step:
        (act_ref, pst_ref, cst_ref, h_ref, wga_ref, wgb_ref, wgc_ref, at_ref,
         wgrp_ref, pscale_ref, wpb_ref, cw_ref, wcb_ref, wab_ref,
         out_ref, u_ref, pzs_ref, cb_ref, comb_ref) = refs
    else:
        (act_ref, hist_ref, h_ref, wga_ref, wgb_ref, wgc_ref, o0_ref, o1_ref, o2_ref, l0_ref, l1_ref, l2_ref,
         wgrp_ref, pscale_ref, wpb_ref, cw_ref, wcb_ref, wab_ref,
         out_ref, u_ref, pzs_ref, cb_ref, comb_ref, pe_ref, ue_ref) = refs
    c_tile = pl.program_id(1)
    i = pl.program_id(2)
    rows = pl.ds(pl.multiple_of(i * tm, tm), tm)
    pzs_ref, cb_ref, comb_ref = pzs_ref.at[rows], cb_ref.at[rows], comb_ref.at[rows]

    def sequence_mixers():
        p = act_ref[0, :, 0:POOL_WIDTH]
        gate_b = act_ref[0, :, POOL_WIDTH:POOL_WIDTH + CONV_CH]
        u = act_ref[0, :, POOL_WIDTH + CONV_CH:POOL_WIDTH + 2 * CONV_CH] * \
            act_ref[0, :, POOL_WIDTH + 2 * CONV_CH:POOL_WIDTH + 3 * CONV_CH]
        cw = cw_ref[0]
        if step:
            acc = p
            sums = {}
            for k in range(1, max(POOL_WINDOWS)):
                acc = acc + pst_ref[0, POOL_STATE - k]
                sums[k + 1] = acc
            means = [sums[w][:, gi * POOL_GROUP:(gi + 1) * POOL_GROUP] * (1.0 / w)
                     for gi, w in enumerate(POOL_WINDOWS)]
            conv = cst_ref[0, 0] * cw[0:1] + cst_ref[0, 1] * cw[1:2] + u * cw[2:3]
            u_ref[0, 0] = u
            comb_ref[...] = at_ref[0].astype(BF16)
        else:
            first = i == 0
            hist_p = hist_ref[0, :, 0:POOL_WIDTH]
            hist_u = hist_ref[0, :, POOL_WIDTH + CONV_CH:POOL_WIDTH + 2 * CONV_CH] * \
                hist_ref[0, :, POOL_WIDTH + 2 * CONV_CH:POOL_WIDTH + 3 * CONV_CH]
            pe_ref[0:HIST] = jnp.where(first, 0.0, hist_p)
            ue_ref[0:HIST] = jnp.where(first, 0.0, hist_u)
            pe_ref[HIST:HIST + tm] = p
            ue_ref[HIST:HIST + tm] = u
            t = i * tm + lax.broadcasted_iota(jnp.int32, (tm, 1), 0)
            means = []
            for gi, w in enumerate(POOL_WINDOWS):
                cs = slice(gi * POOL_GROUP, (gi + 1) * POOL_GROUP)
                acc = pe_ref[HIST:HIST + tm, cs]
                for k in range(1, w):
                    acc = acc + pe_ref[HIST - k:HIST - k + tm, cs]
                cnt = jnp.minimum(t + 1, w).astype(F32)
                means.append(acc * (1.0 / cnt))
            conv = (ue_ref[HIST - 2:HIST - 2 + tm] * cw[0:1] + ue_ref[HIST - 1:HIST - 1 + tm] * cw[1:2]
                    + u * cw[2:3])
            u_ref[0, 0] = ue_ref[HIST + tm - 8:HIST + tm]
            for sl in range(2):
                ls = [r[0, sl] for r in (l0_ref, l1_ref, l2_ref)]
                os_ = [r[0, sl] for r in (o0_ref, o1_ref, o2_ref)]
                mx = jnp.maximum(jnp.maximum(ls[0], ls[1]), ls[2])
                w_ = [jnp.exp(x - mx) for x in ls]
                tot = w_[0] + w_[1] + w_[2]
                comb = (w_[0] * os_[0] + w_[1] * os_[1] + w_[2] * os_[2]) * (1.0 / tot)
                comb_ref[:, sl * LANES:(sl + 1) * LANES] = comb.astype(BF16)
        pscale = pscale_ref[0]
        for gi in range(len(POOL_WINDOWS)):
            cs = slice(gi * POOL_GROUP, (gi + 1) * POOL_GROUP)
            pm = (means[gi] - p[:, cs]).astype(BF16)
            pz = _dot(pm, wgrp_ref[0, gi])
            pzs_ref[:, cs] = (pz * pscale[:, cs]).astype(BF16)
        cb_ref[...] = (gate_b * conv).astype(BF16)

    def merge_tile():
        h = h_ref[0]
        merged = _sigmoid(_dot(h, wga_ref[0])) * _dot(pzs_ref[...], wpb_ref[0])
        merged += _sigmoid(_dot(h, wgb_ref[0])) * _dot(cb_ref[...], wcb_ref[0])
        merged += _sigmoid(_dot(h, wgc_ref[0])) * _dot(comb_ref[...], wab_ref[0])
        out_ref[0] = merged.astype(BF16)

    @pl.when(c_tile == 0)
    def _():
        sequence_mixers()
        merge_tile()

    @pl.when(c_tile > 0)
    def _():
        merge_tile()


def _branches(act, h, attn, states, weights, l, *, tm, tc, step):
    b, s, _ = act.shape
    wgrp, pscale, wpb, cw, wcb, wab, wgate = weights
    ni = s // tm
    once = lambda c, i: jnp.where(c == 0, i, ni - 1)
    gate_specs = [pl.BlockSpec((1, tm, D_MODEL), lambda bi, c, i: (bi, i, 0))] + [
        pl.BlockSpec((1, D_MODEL, tc), lambda bi, c, i, k=k: (l, 0, k * (D_MODEL // tc) + c)) for k in range(3)]
    w_specs = [
        pl.BlockSpec((1,) + wgrp.shape[1:], lambda bi, c, i: (l, 0, 0, 0)),
        pl.BlockSpec((1,) + pscale.shape[1:], lambda bi, c, i: (l, 0, 0)),
        pl.BlockSpec((1, POOL_WIDTH, tc), lambda bi, c, i: (l, 0, c)),
        pl.BlockSpec((1,) + cw.shape[1:], lambda bi, c, i: (l, 0, 0)),
        pl.BlockSpec((1, CONV_CH, tc), lambda bi, c, i: (l, 0, c)),
        pl.BlockSpec((1, GROUP_WIDTH, tc), lambda bi, c, i: (l, 0, c)),
    ]
    act_spec = pl.BlockSpec((1, tm, ACT_WIDTH), lambda bi, c, i: (bi, once(c, i), 0))
    scratch = [pltpu.VMEM((s, POOL_WIDTH), BF16), pltpu.VMEM((s, CONV_CH), BF16),
               pltpu.VMEM((s, GROUP_WIDTH), BF16)]
    blocks = (_nbytes((tm, ACT_WIDTH), F32) + _nbytes((tm + 3 * tc, D_MODEL), BF16) + _nbytes(wgrp.shape[1:], BF16)
              + _nbytes((POOL_WIDTH + CONV_CH + GROUP_WIDTH, tc), BF16) + _nbytes((tm, tc), BF16))
    if step:
        pst, cst = states
        ins = [act, pst, cst, h, wgate, wgate, wgate, attn]
        in_specs = [act_spec,
                    pl.BlockSpec((1,) + pst.shape[1:], lambda bi, c, i: (l, 0, 0, 0)),
                    pl.BlockSpec((1,) + cst.shape[1:], lambda bi, c, i: (l, 0, 0, 0))] + gate_specs + [
                    pl.BlockSpec((1, tm, GROUP_WIDTH), lambda bi, c, i: (bi, once(c, i), 0))]
        urows = tm
        blocks += _nbytes(pst.shape[1:], F32) + _nbytes(cst.shape[1:], F32)
    else:
        hblk = tm // HIST
        slab = pl.BlockSpec((1, 2, tm, LANES), lambda bi, c, i: (bi, 0, once(c, i), 0))
        ins = [act, act, h, wgate, wgate, wgate] + list(attn)
        in_specs = [act_spec,
                    pl.BlockSpec((1, HIST, ACT_WIDTH),
                                 lambda bi, c, i: (bi, jnp.maximum(once(c, i) * hblk - 1, 0), 0))
                    ] + gate_specs + [slab] * 6
        scratch += [pltpu.VMEM((tm + HIST, POOL_WIDTH), F32), pltpu.VMEM((tm + HIST, CONV_CH), F32)]
        urows = 8
        blocks += 6 * _nbytes((2, tm, LANES), F32) + _nbytes((HIST, ACT_WIDTH), F32)
    scratch_bytes = (_nbytes((s, POOL_WIDTH + CONV_CH + GROUP_WIDTH), BF16) + 2 * _nbytes((tm + HIST, POOL_WIDTH), F32)
                     + 8 * _nbytes((tm, tc), F32))
    return pl.pallas_call(
        functools.partial(_branch_kernel, tm=tm, step=step),
        out_shape=(jax.ShapeDtypeStruct((b, s, D_MODEL), BF16),
                   jax.ShapeDtypeStruct((b, ni, urows, CONV_CH), F32)),
        grid=(b, D_MODEL // tc, ni),
        in_specs=in_specs + w_specs,
        out_specs=(pl.BlockSpec((1, tm, tc), lambda bi, c, i: (bi, i, c)),
                   pl.BlockSpec((1, 1, urows, CONV_CH), lambda bi, c, i: (bi, once(c, i), 0, 0))),
        scratch_shapes=scratch,
        compiler_params=pltpu.CompilerParams(
            dimension_semantics=("parallel", "arbitrary", "arbitrary"),
            vmem_limit_bytes=_vmem_limit(blocks, scratch_bytes)),
        name="branches_step" if step else "branches",
    )(*ins, wgrp, pscale, wpb, cw, wcb, wab)


def _oproj_kernel(m_ref, w_ref, x_ref, g_ref, gate_ref, o_ref):
    mix = _dot(m_ref[0], w_ref[0])
    o_ref[0] = x_ref[0] + _tail2(gate_ref) * _rms(mix, g_ref[0])


def _oproj(merged, w_o, x, norm, mod, l, *, tm):
    b, s, d = x.shape
    blocks = (_nbytes((tm, d), BF16) + _nbytes((d, d), BF16) + 2 * _nbytes((tm, d), F32)
              + _nbytes((_mod_rows(mod) + 1, d), F32))
    return pl.pallas_call(
        _oproj_kernel,
        out_shape=jax.ShapeDtypeStruct((b, s, d), F32),
        grid=(b, s // tm),
        in_specs=[
            pl.BlockSpec((1, tm, d), lambda bi, i: (bi, i, 0)),
            pl.BlockSpec((1, d, d), lambda bi, i: (l, 0, 0)),
            pl.BlockSpec((1, tm, d), lambda bi, i: (bi, i, 0)),
            norm[1](1), mod[1](2),
        ],
        out_specs=pl.BlockSpec((1, tm, d), lambda bi, i: (bi, i, 0)),
        compiler_params=pltpu.CompilerParams(
            dimension_semantics=("parallel", "parallel"),
            vmem_limit_bytes=_vmem_limit(blocks, _nbytes((tm, d), F32))),
        name="oproj",
    )(merged, w_o, x, norm[0], mod[0])


def _ffn_kernel(*refs, tm, step, nchunk):
    if step:
        (x_ref, g2_ref, sc_ref, sh_ref, wg_ref, wv_ref, cwg_ref, cwv_ref, wd_ref, g3_ref, gate_ref,
         stg_ref, stv_ref, o_ref, tg_ref, tv_ref, h_ref) = refs
        hist = 0
    else:
        (x_ref, xp_ref, g2_ref, sc_ref, sh_ref, wg_ref, wv_ref, cwg_ref, cwv_ref, wd_ref, g3_ref, gate_ref,
         o_ref, tg_ref, tv_ref, h_ref, ug_ref, uv_ref) = refs
        hist = HIST
    i = pl.program_id(1)
    j = pl.program_id(2)
    last = pl.num_programs(2) - 1
    tf = wd_ref.shape[1]
    cols = pl.ds(pl.multiple_of(j * tf, tf), tf)
    cwg = cwg_ref[0, :, cols]
    cwv = cwv_ref[0, :, cols]

    if step:
        @pl.when(j == 0)
        def _():
            _modulate_rows(x_ref, g2_ref, sc_ref, sh_ref, h_ref, 0, tm)
            o_ref[...] = jnp.zeros_like(o_ref)

        h = h_ref[...]
        up_g = _dot(h, wg_ref[0])
        up_v = _dot(h, wv_ref[0])
        uc_g = stg_ref[0, 0] * cwg[0:1] + stg_ref[0, 1] * cwg[1:2] + up_g * cwg[2:3]
        uc_v = stv_ref[0, 0] * cwv[0:1] + stv_ref[0, 1] * cwv[1:2] + up_v * cwv[2:3]
        tg_ref[0, 0, :, cols] = up_g
        tv_ref[0, 0, :, cols] = up_v
        o_ref[0] += _dot((_gelu_tanh(uc_g) * uc_v).astype(BF16), wd_ref[0])

        @pl.when(j == last)
        def _():
            o_ref[0] = x_ref[0] + _tail2(gate_ref) * _rms(o_ref[0], g3_ref[0])
        return

    def seq_step(nck, is_first, is_last):
        rc = tm // nck
        if is_first:
            _modulate_rows(xp_ref, g2_ref, sc_ref, sh_ref, h_ref, 0, HIST)
        for c in range(nck):
            r = slice(c * rc, (c + 1) * rc)
            if is_first:
                y = _rms(x_ref[0, r], g2_ref[0])
                h_ref[HIST + c * rc:HIST + (c + 1) * rc] = (y * (1.0 + _tail2(sc_ref)) + _tail2(sh_ref)).astype(BF16)
            lo = 0 if c == 0 else HIST + c * rc
            hi = HIST + (c + 1) * rc
            h = h_ref[lo:hi]
            for u_ref, w_ref in ((ug_ref, wg_ref), (uv_ref, wv_ref)):
                up = _dot(h, w_ref[0])
                if c == 0:
                    u_ref[0:HIST] = jnp.where(i == 0, 0.0, up[0:HIST])
                    u_ref[HIST:hi] = up[HIST:]
                else:
                    u_ref[lo:hi] = up

            def conv(ref, cw):
                base = HIST + c * rc
                return (ref[base - 2:base - 2 + rc] * cw[0:1] + ref[base - 1:base - 1 + rc] * cw[1:2]
                        + ref[base:base + rc] * cw[2:3])
            act = (_gelu_tanh(conv(ug_ref, cwg)) * conv(uv_ref, cwv)).astype(BF16)
            acc = _dot(act, wd_ref[0])
            if not is_first:
                acc = o_ref[0, r] + acc
            if is_last:
                acc = x_ref[0, r] + _tail2(gate_ref) * _rms(acc, g3_ref[0])
            o_ref[0, r] = acc
        tg_ref[0, 0, :, cols] = ug_ref[HIST + tm - 8:HIST + tm]
        tv_ref[0, 0, :, cols] = uv_ref[HIST + tm - 8:HIST + tm]

    pl.when(j == 0)(lambda: seq_step(2 * nchunk, True, False))
    pl.when((j > 0) & (j < last))(lambda: seq_step(nchunk, False, False))
    pl.when(j == last)(lambda: seq_step(2 * nchunk, False, True))


def _ffn(x, norm, mod, w_up, cw, w_down, state, l, *, tm, tf, step):
    b, s, d = x.shape
    ni, nj = s // tm, D_FF // tf
    trows = tm if step else 8
    w_specs = [
        pl.BlockSpec((1, d, tf), lambda bi, i, j: (l, 0, j)),
        pl.BlockSpec((1, d, tf), lambda bi, i, j: (l, 0, nj + j)),
        pl.BlockSpec((1, FFN_K, D_FF), lambda bi, i, j: (l, 0, 0)),
        pl.BlockSpec((1, FFN_K, D_FF), lambda bi, i, j: (l, 0, 1)),
        pl.BlockSpec((1, tf, d), lambda bi, i, j: (l, j, 0)),
    ]
    x_spec = pl.BlockSpec((1, tm, d), lambda bi, i, j: (bi, i, 0), pipeline_mode=pl.Buffered(1))
    hist = 0 if step else HIST
    scratch = [pltpu.VMEM((tm + hist, d), BF16)]
    blocks = (2 * _nbytes((tm, d), F32) + 3 * _nbytes((d, tf), BF16) + 2 * _nbytes((trows, tf), F32)
              + _nbytes((3 * _mod_rows(mod) + 2, d), F32))
    mods = [norm[1](2), mod[1](4), mod[1](3)]
    tailp = [norm[1](3), mod[1](5)]
    if step:
        ins = [x, norm[0], mod[0], mod[0], w_up, w_up, cw, cw, w_down, norm[0], mod[0], state, state]
        in_specs = [x_spec] + mods + w_specs + tailp + [
            pl.BlockSpec((1, FFN_K - 1, tm, tf), lambda bi, i, j: (l, 0, 0, j)),
            pl.BlockSpec((1, FFN_K - 1, tm, tf), lambda bi, i, j: (l, 0, 0, nj + j))]
        blocks += 2 * _nbytes((FFN_K - 1, tm, tf), F32)
    else:
        hblk = tm // HIST
        ins = [x, x, norm[0], mod[0], mod[0], w_up, w_up, cw, cw, w_down, norm[0], mod[0]]
        in_specs = [x_spec,
                    pl.BlockSpec((1, HIST, d), lambda bi, i, j: (bi, jnp.maximum(i * hblk - 1, 0), 0))
                    ] + mods + w_specs + tailp
        scratch += [pltpu.VMEM((tm + HIST, tf), F32), pltpu.VMEM((tm + HIST, tf), F32)]
        blocks += _nbytes((HIST, d), F32)
    scratch_bytes = _nbytes((tm + hist, d), BF16) + 6 * _nbytes((tm + hist, tf), F32)
    tail = jax.ShapeDtypeStruct((b, ni, trows, D_FF), F32)
    tail_spec = lambda: pl.BlockSpec((1, 1, trows, D_FF), lambda bi, i, j: (bi, i, 0, 0))
    return pl.pallas_call(
        functools.partial(_ffn_kernel, tm=tm, step=step, nchunk=ROW_CHUNKS),
        out_shape=(jax.ShapeDtypeStruct((b, s, d), F32), tail, tail),
        grid=(b, ni, nj),
        in_specs=in_specs,
        out_specs=(pl.BlockSpec((1, tm, d), lambda bi, i, j: (bi, i, 0)), tail_spec(), tail_spec()),
        scratch_shapes=scratch,
        compiler_params=pltpu.CompilerParams(
            dimension_semantics=("parallel", "parallel", "arbitrary"),
            vmem_limit_bytes=_vmem_limit(blocks, scratch_bytes)),
        name="ffn_step" if step else "ffn",
    )(*ins)


def _group_rel_bias(rel_bias):
    n = np.arange(N_DIL_KEYS + 1)
    max_exact = N_BUCKETS // 2
    onehot = np.zeros((N_ATTN_GROUPS, N_DIL_KEYS + 1, N_BUCKETS), np.float32)
    for g, (_, d) in enumerate(ATTN_GROUPS):
        dist = n * d
        large = max_exact + (np.log(np.maximum(dist, 1) / max_exact) / np.log(MAX_DISTANCE / max_exact)
                             * (N_BUCKETS - max_exact)).astype(np.int32)
        bucket = np.where(dist < max_exact, dist, np.minimum(large, N_BUCKETS - 1))
        onehot[g, n, bucket] = 1.0
    per_group = rel_bias.reshape(N_BUCKETS, N_ATTN_GROUPS, HEADS_PER_GROUP)
    return jnp.einsum("gnb,bgh->ghn", onehot, per_group, precision=lax.Precision.HIGHEST)


def _bias_tables(rel_bias):
    bias_g = _group_rel_bias(rel_bias).astype(F32)
    n = N_DIL_KEYS
    gh = (N_ATTN_GROUPS, HEADS_PER_GROUP)
    ext = jnp.concatenate([bias_g[:, :, ::-1], jnp.full(gh + (n,), NEG_INF, F32)], axis=-1)
    tab = jnp.tile(ext, (1, 1, n))[:, :, :n * 2 * n].reshape(N_ATTN_GROUPS, HEADS_PER_GROUP * n, 2 * n)
    step = []
    for g, (_, d) in enumerate(ATTN_GROUPS):
        hit = bias_g[g, :, :0:-1, None]
        row = jnp.concatenate([hit, jnp.full((HEADS_PER_GROUP, n, d - 1), NEG_INF, F32)], axis=-1)
        new = jnp.broadcast_to(bias_g[g, :, 0:1], (HEADS_PER_GROUP, n * d))
        pad = jnp.zeros((SAMPLE_ROWS - HEADS_PER_GROUP, n * d), F32)
        step.append(jnp.concatenate([row.reshape(HEADS_PER_GROUP, n * d), pad, new, pad], axis=0))
    return tab, step


def _kv_pack(qkv, g, nrows):
    b = qkv.shape[0]
    k0 = ATTN_WIDTH + g * GROUP_WIDTH
    v0 = 2 * ATTN_WIDTH + g * GROUP_WIDTH
    k = qkv[:, -nrows:, k0:k0 + GROUP_WIDTH].reshape(b, nrows, HEADS_PER_GROUP, HEAD_DIM)
    v = qkv[:, -nrows:, v0:v0 + GROUP_WIDTH].reshape(b, nrows, HEADS_PER_GROUP, HEAD_DIM)
    return jnp.stack([k, v], axis=2)


def _kv_tail_kernel(*refs):
    n = len(refs) // 3
    for g in range(n):
        o_ref = refs[2 * n + g]
        for part in range(2):
            for s in range(2):
                o_ref[0, part, s * LANES:(s + 1) * LANES, :] = refs[2 * g + part][0, s].T


def _kv_tails(qkv):
    b, _, s, _ = qkv.shape
    ins, in_specs, out_shape, out_specs, blocks = [], [], [], [], 0
    for g, (w, _) in enumerate(ATTN_GROUPS):
        nrows = min(w, s)
        assert s % nrows == 0
        for part in (1, 2):
            col = part * (ATTN_WIDTH // GROUP_WIDTH) + g
            ins.append(qkv)
            in_specs.append(pl.BlockSpec((1, 2, nrows, LANES), lambda bi, col=col, rb=s // nrows - 1: (bi, col, rb, 0)))
        out_shape.append(jax.ShapeDtypeStruct((b, 2, GROUP_WIDTH, nrows), F32))
        out_specs.append(pl.BlockSpec((1, 2, GROUP_WIDTH, nrows), lambda bi: (bi, 0, 0, 0)))
        blocks += 4 * _nbytes((nrows, GROUP_WIDTH), F32)
    outs = pl.pallas_call(
        _kv_tail_kernel,
        out_shape=out_shape,
        grid=(b,),
        in_specs=in_specs,
        out_specs=out_specs,
        compiler_params=pltpu.CompilerParams(
            dimension_semantics=("parallel",), vmem_limit_bytes=_vmem_limit(blocks, blocks // 2)),
        name="kv_tails",
    )(*ins)
    return [o.reshape(b, 2, HEADS_PER_GROUP, HEAD_DIM, o.shape[-1]).transpose(0, 4, 1, 2, 3) for o in outs]


def _prompt_layer(x, l, P):
    b, s, _ = x.shape
    norm, mod = P["norm"](l), P["mod_p"](l)
    tm = 512
    qkv_slab, h = _proj(x, norm, mod, P["w_qkv"], l, tm=1024, tn=ATTN_WIDTH, slab_out=True)
    act = _matmul(h, P["w_act"], l, tm=1024, tn=ACT_WIDTH)
    ols = [_attention(qkv_slab, P["bias_tab"], g) for g in range(N_ATTN_GROUPS)]
    attn = [o for o, _ in ols] + [ls for _, ls in ols]
    merged, u_tail = _branches(act, h, attn, None, P["branch"], l, tm=tm, tc=512, step=False)
    x1 = _oproj(merged, P["w_o"], x, norm, mod, l, tm=tm)
    x2, tail_g, tail_v = _ffn(x1, norm, mod, P["w_up"], P["ffn_cw"], P["w_down"], None, l,
                              tm=1024, tf=512, step=False)
    new_kv = _kv_tails(qkv_slab)
    new_pool = act[:, -POOL_STATE:, :POOL_WIDTH]
    new_conv = u_tail[:, -1, -(CONV_K - 1):]
    new_ffn = jnp.concatenate([tail_g[:, -1, -(FFN_K - 1):], tail_v[:, -1, -(FFN_K - 1):]], axis=-1)
    return x2, (new_kv[0], new_kv[1], new_kv[2], new_pool, new_conv, new_ffn)


def _pad_rows(a, axis):
    pad = [(0, 0)] * a.ndim
    pad[axis] = (0, SAMPLE_ROWS - a.shape[axis])
    return jnp.pad(a, pad)


def _sample_layer(x, l, P, S, nb):
    norm, mod = P["norm"](l), P["mod_s"](l)
    tm = SAMPLE_ROWS
    qkv, h = _proj(x, norm, mod, P["w_qkv"], l, tm=tm, tn=ATTN_WIDTH, slab_out=False)
    act = _matmul(h, P["w_act"], l, tm=tm, tn=1024)
    attn = _attention_step(qkv, S["caches"], P["bias_step"], l, nb)
    attn = _pad_rows(attn.reshape(1, nb, GROUP_WIDTH), 1)
    merged, u = _branches(act, h, attn, (S["pool_t"], S["conv_t"]), P["branch"], l, tm=tm, tc=512, step=True)
    x1 = _oproj(merged, P["w_o"], x, norm, mod, l, tm=tm)
    x2, up_g, up_v = _ffn(x1, norm, mod, P["w_up"], P["ffn_cw"], P["w_down"], S["ffn_t"], l,
                          tm=tm, tf=512, step=True)
    new_kv = [_kv_pack(qkv[0, :nb, None], g, 1) for g in range(N_ATTN_GROUPS)]
    new_pool = jnp.concatenate([S["pool"][l][:, 1:], act[0, :nb, None, :POOL_WIDTH]], axis=1)
    new_conv = jnp.concatenate([S["conv"][l][:, 1:], u[0, 0, :nb, None]], axis=1)
    up_new = jnp.concatenate([up_g[0, 0, :nb], up_v[0, 0, :nb]], axis=-1)
    new_ffn = jnp.concatenate([S["ffn"][l][:, 1:], up_new[:, None]], axis=1)
    return x2, (new_kv[0], new_kv[1], new_kv[2], new_pool, new_conv, new_ffn)


def kernel(x_prompt, x_sample, c_prompt, c_sample, cache_kv_w128, cache_kv_w512, cache_kv_w2048, state_pool, state_conv, state_ffn_conv, rel_bias, norm_g, w_ada, b_ada, w_in, w_attn_br, w_pool_grp, pool_scale, w_pool_br, conv_w, w_conv_br, w_o, w_up, ffn_conv_w, w_down):
    nbp = x_prompt.shape[0]
    nbs, tdec, _ = x_sample.shape
    assert tdec == 1 and nbs <= SAMPLE_ROWS
    caches = (cache_kv_w128, cache_kv_w512, cache_kv_w2048)
    for (w, d), c in zip(ATTN_GROUPS, caches):
        assert c.shape[2] == w == N_DIL_KEYS * d, "cache must hold exactly one window"

    bias_tab, bias_step = _bias_tables(rel_bias)
    c_rows = -(-(SAMPLE_ROWS + nbp) // 8) * 8
    c_all = jnp.zeros((c_rows, D_MODEL), F32).at[:nbs].set(c_sample).at[SAMPLE_ROWS:SAMPLE_ROWS + nbp].set(c_prompt)
    mod_all = _ada(c_all, w_ada, b_ada)

    w_qkv, w_act, w_gate = _cast_split(
        w_in, [(0, QKV_WIDTH), (QKV_WIDTH, QKV_WIDTH + ACT_WIDTH), (QKV_WIDTH + ACT_WIDTH, w_in.shape[2])], tk=256)
    P = dict(
        norm=functools.partial(_norm_view, norm_g),
        mod_p=functools.partial(_mod_view, mod_all, prompt=True),
        mod_s=functools.partial(_mod_view, mod_all, prompt=False),
        w_qkv=w_qkv, w_act=w_act,
        branch=(w_pool_grp.astype(BF16), pool_scale[:, None, :], w_pool_br.astype(BF16),
                conv_w, w_conv_br.astype(BF16), w_attn_br.astype(BF16), w_gate),
        w_o=w_o.astype(BF16), w_up=w_up.astype(BF16), ffn_cw=ffn_conv_w, w_down=w_down.astype(BF16),
        bias_tab=bias_tab, bias_step=bias_step,
    )
    S = dict(
        caches=[c.transpose(0, 1, 3, 4, 5, 2).reshape(DEPTH, nbs, 2, GROUP_WIDTH, -1) for c in caches],
        pool=state_pool, conv=state_conv, ffn=state_ffn_conv,
        pool_t=_pad_rows(state_pool.transpose(0, 2, 1, 3), 2),
        conv_t=_pad_rows(state_conv.transpose(0, 2, 1, 3), 2),
        ffn_t=_pad_rows(state_ffn_conv.transpose(0, 2, 1, 3), 2),
    )
    yp = x_prompt
    ys = _pad_rows(x_sample.reshape(1, nbs, D_MODEL), 1)
    st_p, st_s = [], []
    for l in range(DEPTH):
        yp, sp = _prompt_layer(yp, l, P)
        ys, ss = _sample_layer(ys, l, P, S, nbs)
        st_p.append(sp)
        st_s.append(ss)
    outs_p = [jnp.stack([s[k] for s in st_p]) for k in range(6)]
    outs_s = [jnp.stack([s[k] for s in st_s]) for k in range(6)]
    return (yp, ys[0, :nbs, None, :], *outs_p, *outs_s)
```

```python
import functools

import numpy as np
import jax
import jax.numpy as jnp
from jax import lax
from jax.experimental import pallas as pl
from jax.experimental.pallas import tpu as pltpu

F32 = jnp.float32
BF16 = jnp.bfloat16

D_MODEL = 2048
DEPTH = 2
HEAD_DIM = 64
HEADS_PER_GROUP = 4
ATTN_GROUPS = ((128, 1), (512, 4), (2048, 16))
N_ATTN_GROUPS = len(ATTN_GROUPS)
ATTN_WIDTH = N_ATTN_GROUPS * HEADS_PER_GROUP * HEAD_DIM
GROUP_WIDTH = HEADS_PER_GROUP * HEAD_DIM
N_DIL_KEYS = 128
N_BUCKETS = 32
MAX_DISTANCE = 2048
ATTN_SCALE = HEAD_DIM ** -0.5
POOL_WINDOWS = (2, 4, 8, 16)
POOL_GROUP = 128
POOL_WIDTH = 512
POOL_STATE = 15
CONV_CH = 512
CONV_K = 3
D_FF = 5632
FFN_K = 3
N_MOD = 6
N_NORM = 4
EPS = 1e-6
NEG_INF = -1e30

LANES = 128
HIST = 16
SAMPLE_ROWS = 16
QKV_WIDTH = 3 * ATTN_WIDTH
QKV_SLABS = QKV_WIDTH // LANES
ACT_WIDTH = POOL_WIDTH + 3 * CONV_CH
ATTN_SUPER_BLOCK = 2048
ATTN_CLASS_UNROLL = 8
FFN_TILE = 512
ROW_CHUNKS = 2
VMEM_CAP = 56 * 1024 * 1024


def _vmem_limit(block_bytes, scratch_bytes=0):
    del block_bytes, scratch_bytes
    return VMEM_CAP


def _nbytes(shape, dtype):
    return int(np.prod(shape)) * jnp.dtype(dtype).itemsize


def _rms(x, g):
    return x * lax.rsqrt(jnp.mean(x * x, axis=-1, keepdims=True) + EPS) * g


def _sigmoid(x):
    return 0.5 + 0.5 * jnp.tanh(0.5 * x)


def _gelu_tanh(x):
    return 0.5 * x * (1.0 + jnp.tanh(np.sqrt(2.0 / np.pi) * (x + 0.044715 * (x * x * x))))


def _dot(a, b):
    return jnp.dot(a, b, preferred_element_type=F32)


def _tail2(ref):
    return ref[(0,) * (len(ref.shape) - 2)]


def _norm_view(norm_g, l):
    arr = norm_g.reshape(DEPTH * N_NORM, 1, D_MODEL)
    return arr, lambda k: pl.BlockSpec((1, 1, D_MODEL), lambda *_: (l * N_NORM + k, 0, 0))


def _mod_view(mod_all, l, prompt):
    if prompt:
        arr = mod_all.reshape(DEPTH, mod_all.shape[1], N_MOD, 1, D_MODEL)
        return arr, lambda k: pl.BlockSpec((1, 1, 1, 1, D_MODEL), lambda bi, *_: (l, SAMPLE_ROWS + bi, k, 0, 0))
    return mod_all, lambda k: pl.BlockSpec((1, SAMPLE_ROWS, D_MODEL), lambda bi, *_: (l, 0, k))


def _mod_rows(mod):
    return mod[1](0).block_shape[-2]


def _cast_split_kernel(w_ref, *o_refs, bounds):
    for o_ref, (lo, hi) in zip(o_refs, bounds):
        o_ref[...] = w_ref[:, :, lo:hi].astype(BF16)


def _cast_split(w, bounds, *, tk):
    nl, k, n = w.shape
    blocks = _nbytes((tk, n), F32) + sum(_nbytes((tk, hi - lo), BF16) for lo, hi in bounds)
    return pl.pallas_call(
        functools.partial(_cast_split_kernel, bounds=tuple(bounds)),
        out_shape=[jax.ShapeDtypeStruct((nl, k, hi - lo), BF16) for lo, hi in bounds],
        grid=(nl, k // tk),
        in_specs=[pl.BlockSpec((1, tk, n), lambda li, ki: (li, ki, 0))],
        out_specs=[pl.BlockSpec((1, tk, hi - lo), lambda li, ki: (li, ki, 0)) for lo, hi in bounds],
        compiler_params=pltpu.CompilerParams(
            dimension_semantics=("parallel", "parallel"), vmem_limit_bytes=_vmem_limit(blocks)),
        name="cast_split",
    )(w)


def _ada_kernel(c_ref, w_ref, b_ref, o_ref):
    c = c_ref[...]
    s = (c * _sigmoid(c)).astype(BF16)
    o_ref[0] = _dot(s, w_ref[0].astype(BF16)) + b_ref[0]


def _ada(c_all, w_ada, b_ada):
    rows = c_all.shape[0]
    n = w_ada.shape[-1]
    tn = 1024
    blocks = _nbytes((rows, D_MODEL), F32) + _nbytes((D_MODEL, tn), F32) + _nbytes((rows + 1, tn), F32)
    return pl.pallas_call(
        _ada_kernel,
        out_shape=jax.ShapeDtypeStruct((DEPTH, rows, n), F32),
        grid=(DEPTH, n // tn),
        in_specs=[
            pl.BlockSpec((rows, D_MODEL), lambda l, j: (0, 0)),
            pl.BlockSpec((1, D_MODEL, tn), lambda l, j: (l, 0, j)),
            pl.BlockSpec((1, 1, tn), lambda l, j: (l, 0, j)),
        ],
        out_specs=pl.BlockSpec((1, rows, tn), lambda l, j: (l, 0, j)),
        compiler_params=pltpu.CompilerParams(
            dimension_semantics=("parallel", "parallel"),
            vmem_limit_bytes=_vmem_limit(blocks, _nbytes((D_MODEL, tn), BF16))),
        name="ada",
    )(c_all, w_ada, b_ada.reshape(DEPTH, 1, n))


def _modulate_rows(x_ref, g_ref, sc_ref, sh_ref, h_ref, row0, tm):
    y = _rms(x_ref[0], g_ref[0])
    h_ref[row0:row0 + tm] = (y * (1.0 + _tail2(sc_ref)) + _tail2(sh_ref)).astype(BF16)


def _proj_kernel(x_ref, g_ref, sc_ref, sh_ref, w_ref, o_ref, h_ref, *, slabs, tm, nchunk):
    def project(r):
        res = _dot(h_ref[0, r], w_ref[0])
        if slabs:
            for s in range(slabs):
                o_ref[0, s, r] = res[:, s * LANES:(s + 1) * LANES]
        else:
            o_ref[0, r] = res

    @pl.when(pl.program_id(2) == 0)
    def _():
        rc = tm // nchunk
        for c in range(nchunk):
            r = slice(c * rc, (c + 1) * rc)
            y = _rms(x_ref[0, r], g_ref[0])
            h_ref[0, r] = (y * (1.0 + _tail2(sc_ref)) + _tail2(sh_ref)).astype(BF16)
            project(r)

    @pl.when(pl.program_id(2) > 0)
    def _():
        project(slice(None))


def _proj(x, norm, mod, w, l, *, tm, tn, slab_out):
    b, s, d = x.shape
    n = w.shape[2]
    slabs = tn // LANES if slab_out else 0
    r = _mod_rows(mod)
    blocks = (_nbytes((tm, d), F32) + _nbytes((2 * r + 1, d), F32) + _nbytes((d, tn), BF16)
              + _nbytes((tm, tn), F32) + _nbytes((tm, d), BF16))
    if slab_out:
        out = jax.ShapeDtypeStruct((b, n // LANES, s, LANES), F32)
        out_spec = pl.BlockSpec((1, slabs, tm, LANES), lambda bi, i, j: (bi, j, i, 0))
    else:
        out = jax.ShapeDtypeStruct((b, s, n), F32)
        out_spec = pl.BlockSpec((1, tm, tn), lambda bi, i, j: (bi, i, j))
    return pl.pallas_call(
        functools.partial(_proj_kernel, slabs=slabs, tm=tm, nchunk=2 * ROW_CHUNKS if r == 1 else 1),
        out_shape=[out, jax.ShapeDtypeStruct((b, s, d), BF16)],
        grid=(b, s // tm, n // tn),
        in_specs=[
            pl.BlockSpec((1, tm, d), lambda bi, i, j: (bi, i, 0)),
            norm[1](0), mod[1](1), mod[1](0),
            pl.BlockSpec((1, d, tn), lambda bi, i, j: (l, 0, j)),
        ],
        out_specs=[out_spec, pl.BlockSpec((1, tm, d), lambda bi, i, j: (bi, i, 0))],
        compiler_params=pltpu.CompilerParams(
            dimension_semantics=("parallel", "parallel", "arbitrary"),
            vmem_limit_bytes=_vmem_limit(blocks, _nbytes((tm, tn), F32) + _nbytes((tm, d), F32))),
        name="proj_norm",
    )(x, norm[0], mod[0], mod[0], w)


def _matmul_kernel(h_ref, w_ref, o_ref):
    o_ref[0] = _dot(h_ref[0], w_ref[0])


def _matmul(h, w, l, *, tm, tn):
    b, s, d = h.shape
    n = w.shape[2]
    blocks = _nbytes((tm, d), BF16) + _nbytes((d, tn), BF16) + _nbytes((tm, tn), F32)
    return pl.pallas_call(
        _matmul_kernel,
        out_shape=jax.ShapeDtypeStruct((b, s, n), F32),
        grid=(b, s // tm, n // tn),
        in_specs=[pl.BlockSpec((1, tm, d), lambda bi, i, j: (bi, i, 0)),
                  pl.BlockSpec((1, d, tn), lambda bi, i, j: (l, 0, j))],
        out_specs=pl.BlockSpec((1, tm, tn), lambda bi, i, j: (bi, i, j)),
        compiler_params=pltpu.CompilerParams(
            dimension_semantics=("parallel", "parallel", "parallel"),
            vmem_limit_bytes=_vmem_limit(blocks, _nbytes((tm, tn), F32))),
        name="proj_act",
    )(h, w)


def _head_masks(rows):
    lane = lax.broadcasted_iota(jnp.int32, (rows, GROUP_WIDTH), 1)
    return [(lane >= h * HEAD_DIM) & (lane < (h + 1) * HEAD_DIM) for h in range(HEADS_PER_GROUP)]


def _attn_kernel(q_ref, kc_ref, kp_ref, vc_ref, vp_ref, bias_ref, o_ref, lse_ref, edge_ref, *, d, sb):
    i = pl.program_id(1)
    nq = N_DIL_KEYS
    span = nq * d
    ncb = sb // span
    hm = _head_masks(1)
    col = lax.broadcasted_iota(jnp.int32, (1, 2 * nq), 1)
    edge_ref[...] = jnp.where((col < nq) & (i == 0), NEG_INF, bias_ref[0])

    def rows(start):
        return pl.ds(start, nq, stride=d) if d > 1 else pl.ds(start, nq)

    def load(ref, start):
        return jnp.concatenate([ref[0, s, rows(start), :] for s in range(2)], axis=1)

    def one_block(r, jb):
        qs = jb * span + r
        q = load(q_ref, qs) * ATTN_SCALE
        if jb == 0:
            lo = sb - span + r
            k_lo, v_lo = load(kp_ref, lo), load(vp_ref, lo)
        else:
            lo = (jb - 1) * span + r
            k_lo, v_lo = load(kc_ref, lo), load(vc_ref, lo)
        kcat = jnp.concatenate([k_lo, load(kc_ref, qs)], axis=0).astype(BF16)
        vcat = jnp.concatenate([v_lo, load(vc_ref, qs)], axis=0).astype(BF16)
        qm = jnp.concatenate([jnp.where(hm[h], q, 0.0) for h in range(HEADS_PER_GROUP)], axis=0).astype(BF16)
        s = lax.dot_general(qm, kcat, (((1,), (1,)), ((), ())), preferred_element_type=F32)
        s = s + (edge_ref[...] if jb == 0 else bias_ref[0])
        m = jnp.max(s, axis=-1, keepdims=True)
        p = jnp.exp(s - m)
        l = jnp.sum(p, axis=-1, keepdims=True)
        oall = _dot((p * (1.0 / l)).astype(BF16), vcat)
        lse = m + jnp.log(l)
        o = jnp.zeros((nq, GROUP_WIDTH), F32)
        ls = jnp.zeros((nq, GROUP_WIDTH), F32)
        for h in range(HEADS_PER_GROUP):
            o = jnp.where(hm[h], oall[h * nq:(h + 1) * nq], o)
            ls = jnp.where(hm[h], lse[h * nq:(h + 1) * nq], ls)
        for sl in range(2):
            o_ref[0, sl, rows(qs), :] = o[:, sl * LANES:(sl + 1) * LANES]
            lse_ref[0, sl, rows(qs), :] = ls[:, sl * LANES:(sl + 1) * LANES]

    if d == 1:
        for jb in range(ncb):
            one_block(0, jb)
    else:
        def body(r, carry):
            for jb in range(ncb):
                one_block(r, jb)
            return carry
        lax.fori_loop(0, d, body, 0, unroll=min(d, ATTN_CLASS_UNROLL))


def _attention(qkv, bias_tab, g):
    b, _, s, _ = qkv.shape
    d = ATTN_GROUPS[g][1]
    sb = max(N_DIL_KEYS * d, ATTN_SUPER_BLOCK)
    blk = (1, 2, sb, LANES)
    kslab, vslab = ATTN_WIDTH // GROUP_WIDTH + g, 2 * ATTN_WIDTH // GROUP_WIDTH + g
    prev = lambda i: jnp.maximum(i - 1, 0)
    blocks = 7 * _nbytes(blk, F32) + _nbytes((4 * N_DIL_KEYS, 2 * N_DIL_KEYS), F32)
    out = jax.ShapeDtypeStruct((b, 2, s, LANES), F32)
    return pl.pallas_call(
        functools.partial(_attn_kernel, d=d, sb=sb),
        out_shape=(out, out),
        grid=(b, s // sb),
        in_specs=[
            pl.BlockSpec(blk, lambda bi, i: (bi, g, i, 0)),
            pl.BlockSpec(blk, lambda bi, i: (bi, kslab, i, 0)),
            pl.BlockSpec(blk, lambda bi, i: (bi, kslab, prev(i), 0)),
            pl.BlockSpec(blk, lambda bi, i: (bi, vslab, i, 0)),
            pl.BlockSpec(blk, lambda bi, i: (bi, vslab, prev(i), 0)),
            pl.BlockSpec((1, 4 * N_DIL_KEYS, 2 * N_DIL_KEYS), lambda bi, i: (g, 0, 0)),
        ],
        out_specs=(pl.BlockSpec(blk, lambda bi, i: (bi, 0, i, 0)),
                   pl.BlockSpec(blk, lambda bi, i: (bi, 0, i, 0))),
        scratch_shapes=[pltpu.VMEM((HEADS_PER_GROUP * N_DIL_KEYS, 2 * N_DIL_KEYS), F32)],
        compiler_params=pltpu.CompilerParams(
            dimension_semantics=("parallel", "parallel"),
            vmem_limit_bytes=_vmem_limit(blocks, 8 << 20)),
        name=f"attn_d{d}",
    )(qkv, qkv, qkv, qkv, qkv, bias_tab)


def _attn_step_kernel(qkv_ref, c0_ref, c1_ref, c2_ref, b0_ref, b1_ref, b2_ref, o_ref):
    b = pl.program_id(0)
    rows = SAMPLE_ROWS
    lane = lax.broadcasted_iota(jnp.int32, (rows, GROUP_WIDTH), 1)
    row = lax.broadcasted_iota(jnp.int32, (rows, GROUP_WIDTH), 0)
    sel = (lane >= row * HEAD_DIM) & (lane < (row + 1) * HEAD_DIM)

    def rowvec(col0):
        return qkv_ref[0, pl.ds(b, 1), col0:col0 + GROUP_WIDTH]

    outs, lses = [], []
    for g, (c_ref, b_ref) in enumerate(((c0_ref, b0_ref), (c1_ref, b1_ref), (c2_ref, b2_ref))):
        q = rowvec(g * GROUP_WIDTH) * ATTN_SCALE
        kn = rowvec(ATTN_WIDTH + g * GROUP_WIDTH).astype(BF16).astype(F32)
        vn = rowvec(2 * ATTN_WIDTH + g * GROUP_WIDTH).astype(BF16).astype(F32)
        q4 = jnp.where(sel, jnp.broadcast_to(q, (rows, GROUP_WIDTH)), 0.0).astype(BF16)
        s_c = _dot(q4, c_ref[0, 0, 0].astype(BF16)) + b_ref[0:rows, :]
        s_n = jnp.sum(q4.astype(F32) * kn, axis=-1, keepdims=True) + b_ref[rows:2 * rows, 0:1]
        m = jnp.maximum(jnp.max(s_c, axis=-1, keepdims=True), s_n)
        p_c = jnp.exp(s_c - m)
        p_n = jnp.exp(s_n - m)
        l = jnp.sum(p_c, axis=-1, keepdims=True) + p_n
        inv = 1.0 / l
        oc = lax.dot_general((p_c * inv).astype(BF16), c_ref[0, 0, 1].astype(BF16), (((1,), (1,)), ((), ())),
                             preferred_element_type=F32)
        oc = oc + (p_n * inv).astype(BF16).astype(F32) * vn
        lse = m + jnp.log(l)
        outs.append(jnp.sum(jnp.where(sel, oc, 0.0), axis=0, keepdims=True))
        lses.append(jnp.sum(jnp.where(sel, lse, 0.0), axis=0, keepdims=True))
    mx = jnp.maximum(jnp.maximum(lses[0], lses[1]), lses[2])
    w = [jnp.exp(ls - mx) for ls in lses]
    tot = w[0] + w[1] + w[2]
    o_ref[0] = (w[0] * outs[0] + w[1] * outs[1] + w[2] * outs[2]) * (1.0 / tot)


def _attention_step(qkv, caches, biases, l, nb):
    cache_specs = [pl.BlockSpec((1, 1) + c.shape[2:], lambda bi: (l, bi, 0, 0, 0)) for c in caches]
    bias_specs = [pl.BlockSpec(b.shape, lambda bi: (0, 0)) for b in biases]
    blocks = (sum(_nbytes(c.shape[2:], F32) for c in caches) + sum(_nbytes(b.shape, F32) for b in biases)
              + _nbytes(qkv.shape, F32))
    return pl.pallas_call(
        _attn_step_kernel,
        out_shape=jax.ShapeDtypeStruct((nb, 1, GROUP_WIDTH), F32),
        grid=(nb,),
        in_specs=[pl.BlockSpec(qkv.shape, lambda bi: (0, 0, 0))] + cache_specs + bias_specs,
        out_specs=pl.BlockSpec((1, 1, GROUP_WIDTH), lambda bi: (bi, 0, 0)),
        compiler_params=pltpu.CompilerParams(
            dimension_semantics=("parallel",),
            vmem_limit_bytes=_vmem_limit(blocks, 8 << 20)),
        name="attn_step",
    )(qkv, *caches, *biases)


def _branch_kernel(*refs, tm, step):
    if step:
        (act_ref, pst_ref, cst_ref, h_ref, wga_ref, wgb_ref, wgc_ref, at_ref,
         wgrp_ref, pscale_ref, wpb_ref, cw_ref, wcb_ref, wab_ref,
         out_ref, u_ref, pzs_ref, cb_ref, comb_ref) = refs
    else:
        (act_ref, hist_ref, h_ref, wga_ref, wgb_ref, wgc_ref, o0_ref, o1_ref, o2_ref, l0_ref, l1_ref, l2_ref,
         wgrp_ref, pscale_ref, wpb_ref, cw_ref, wcb_ref, wab_ref,
         out_ref, u_ref, pzs_ref, cb_ref, comb_ref, pe_ref, ue_ref) = refs
    c_tile = pl.program_id(1)
    i = pl.program_id(2)
    rows = pl.ds(pl.multiple_of(i * tm, tm), tm)
    pzs_ref, cb_ref, comb_ref = pzs_ref.at[rows], cb_ref.at[rows], comb_ref.at[rows]

    def sequence_mixers():
        p = act_ref[0, :, 0:POOL_WIDTH]
        gate_b = act_ref[0, :, POOL_WIDTH:POOL_WIDTH + CONV_CH]
        u = act_ref[0, :, POOL_WIDTH + CONV_CH:POOL_WIDTH + 2 * CONV_CH] * \
            act_ref[0, :, POOL_WIDTH + 2 * CONV_CH:POOL_WIDTH + 3 * CONV_CH]
        cw = cw_ref[0]
        if step:
            acc = p
            sums = {}
            for k in range(1, max(POOL_WINDOWS)):
                acc = acc + pst_ref[0, POOL_STATE - k]
                sums[k + 1] = acc
            means = [sums[w][:, gi * POOL_GROUP:(gi + 1) * POOL_GROUP] * (1.0 / w)
                     for gi, w in enumerate(POOL_WINDOWS)]
            conv = cst_ref[0, 0] * cw[0:1] + cst_ref[0, 1] * cw[1:2] + u * cw[2:3]
            u_ref[0, 0] = u
            comb_ref[...] = at_ref[0].astype(BF16)
        else:
            first = i == 0
            hist_p = hist_ref[0, :, 0:POOL_WIDTH]
            hist_u = hist_ref[0, :, POOL_WIDTH + CONV_CH:POOL_WIDTH + 2 * CONV_CH] * \
                hist_ref[0, :, POOL_WIDTH + 2 * CONV_CH:POOL_WIDTH + 3 * CONV_CH]
            pe_ref[0:HIST] = jnp.where(first, 0.0, hist_p)
            ue_ref[0:HIST] = jnp.where(first, 0.0, hist_u)
            pe_ref[HIST:HIST + tm] = p
            ue_ref[HIST:HIST + tm] = u
            t = i * tm + lax.broadcasted_iota(jnp.int32, (tm, 1), 0)
            means = []
            for gi, w in enumerate(POOL_WINDOWS):
                cs = slice(gi * POOL_GROUP, (gi + 1) * POOL_GROUP)
                acc = pe_ref[HIST:HIST + tm, cs]
                for k in range(1, w):
                    acc = acc + pe_ref[HIST - k:HIST - k + tm, cs]
                cnt = jnp.minimum(t + 1, w).astype(F32)
                means.append(acc * (1.0 / cnt))
            conv = (ue_ref[HIST - 2:HIST - 2 + tm] * cw[0:1] + ue_ref[HIST - 1:HIST - 1 + tm] * cw[1:2]
                    + u * cw[2:3])
            u_ref[0, 0] = ue_ref[HIST + tm - 8:HIST + tm]
            for sl in range(2):
                ls = [r[0, sl] for r in (l0_ref, l1_ref, l2_ref)]
                os_ = [r[0, sl] for r in (o0_ref, o1_ref, o2_ref)]
                mx = jnp.maximum(jnp.maximum(ls[0], ls[1]), ls[2])
                w_ = [jnp.exp(x - mx) for x in ls]
                tot = w_[0] + w_[1] + w_[2]
                comb = (w_[0] * os_[0] + w_[1] * os_[1] + w_[2] * os_[2]) * (1.0 / tot)
                comb_ref[:, sl * LANES:(sl + 1) * LANES] = comb.astype(BF16)
        pscale = pscale_ref[0]
        for gi in range(len(POOL_WINDOWS)):
            cs = slice(gi * POOL_GROUP, (gi + 1) * POOL_GROUP)
            pm = (means[gi] - p[:, cs]).astype(BF16)
            pz = _dot(pm, wgrp_ref[0, gi])
            pzs_ref[:, cs] = (pz * pscale[:, cs]).astype(BF16)
        cb_ref[...] = (gate_b * conv).astype(BF16)

    def merge_tile():
        h = h_ref[0]
        merged = _sigmoid(_dot(h, wga_ref[0])) * _dot(pzs_ref[...], wpb_ref[0])
        merged += _sigmoid(_dot(h, wgb_ref[0])) * _dot(cb_ref[...], wcb_ref[0])
        merged += _sigmoid(_dot(h, wgc_ref[0])) * _dot(comb_ref[...], wab_ref[0])
        out_ref[0] = merged.astype(BF16)

    @pl.when(c_tile == 0)
    def _():
        sequence_mixers()
        merge_tile()

    @pl.when(c_tile > 0)
    def _():
        merge_tile()


def _branches(act, h, attn, states, weights, l, *, tm, tc, step):
    b, s, _ = act.shape
    wgrp, pscale, wpb, cw, wcb, wab, wgate = weights
    ni = s // tm
    once = lambda c, i: jnp.where(c == 0, i, ni - 1)
    gate_specs = [pl.BlockSpec((1, tm, D_MODEL), lambda bi, c, i: (bi, i, 0))] + [
        pl.BlockSpec((1, D_MODEL, tc), lambda bi, c, i, k=k: (l, 0, k * (D_MODEL // tc) + c)) for k in range(3)]
    w_specs = [
        pl.BlockSpec((1,) + wgrp.shape[1:], lambda bi, c, i: (l, 0, 0, 0)),
        pl.BlockSpec((1,) + pscale.shape[1:], lambda bi, c, i: (l, 0, 0)),
        pl.BlockSpec((1, POOL_WIDTH, tc), lambda bi, c, i: (l, 0, c)),
        pl.BlockSpec((1,) + cw.shape[1:], lambda bi, c, i: (l, 0, 0)),
        pl.BlockSpec((1, CONV_CH, tc), lambda bi, c, i: (l, 0, c)),
        pl.BlockSpec((1, GROUP_WIDTH, tc), lambda bi, c, i: (l, 0, c)),
    ]
    act_spec = pl.BlockSpec((1, tm, ACT_WIDTH), lambda bi, c, i: (bi, once(c, i), 0))
    scratch = [pltpu.VMEM((s, POOL_WIDTH), BF16), pltpu.VMEM((s, CONV_CH), BF16),
               pltpu.VMEM((s, GROUP_WIDTH), BF16)]
    blocks = (_nbytes((tm, ACT_WIDTH), F32) + _nbytes((tm + 3 * tc, D_MODEL), BF16) + _nbytes(wgrp.shape[1:], BF16)
              + _nbytes((POOL_WIDTH + CONV_CH + GROUP_WIDTH, tc), BF16) + _nbytes((tm, tc), BF16))
    if step:
        pst, cst = states
        ins = [act, pst, cst, h, wgate, wgate, wgate, attn]
        in_specs = [act_spec,
                    pl.BlockSpec((1,) + pst.shape[1:], lambda bi, c, i: (l, 0, 0, 0)),
                    pl.BlockSpec((1,) + cst.shape[1:], lambda bi, c, i: (l, 0, 0, 0))] + gate_specs + [
                    pl.BlockSpec((1, tm, GROUP_WIDTH), lambda bi, c, i: (bi, once(c, i), 0))]
        urows = tm
        blocks += _nbytes(pst.shape[1:], F32) + _nbytes(cst.shape[1:], F32)
    else:
        hblk = tm // HIST
        slab = pl.BlockSpec((1, 2, tm, LANES), lambda bi, c, i: (bi, 0, once(c, i), 0))
        ins = [act, act, h, wgate, wgate, wgate] + list(attn)
        in_specs = [act_spec,
                    pl.BlockSpec((1, HIST, ACT_WIDTH),
                                 lambda bi, c, i: (bi, jnp.maximum(once(c, i) * hblk - 1, 0), 0))
                    ] + gate_specs + [slab] * 6
        scratch += [pltpu.VMEM((tm + HIST, POOL_WIDTH), F32), pltpu.VMEM((tm + HIST, CONV_CH), F32)]
        urows = 8
        blocks += 6 * _nbytes((2, tm, LANES), F32) + _nbytes((HIST, ACT_WIDTH), F32)
    scratch_bytes = (_nbytes((s, POOL_WIDTH + CONV_CH + GROUP_WIDTH), BF16) + 2 * _nbytes((tm + HIST, POOL_WIDTH), F32)
                     + 8 * _nbytes((tm, tc), F32))
    return pl.pallas_call(
        functools.partial(_branch_kernel, tm=tm, step=step),
        out_shape=(jax.ShapeDtypeStruct((b, s, D_MODEL), BF16),
                   jax.ShapeDtypeStruct((b, ni, urows, CONV_CH), F32)),
        grid=(b, D_MODEL // tc, ni),
        in_specs=in_specs + w_specs,
        out_specs=(pl.BlockSpec((1, tm, tc), lambda bi, c, i: (bi, i, c)),
                   pl.BlockSpec((1, 1, urows, CONV_CH), lambda bi, c, i: (bi, once(c, i), 0, 0))),
        scratch_shapes=scratch,
        compiler_params=pltpu.CompilerParams(
            dimension_semantics=("parallel", "arbitrary", "arbitrary"),
            vmem_limit_bytes=_vmem_limit(blocks, scratch_bytes)),
        name="branches_step" if step else "branches",
    )(*ins, wgrp, pscale, wpb, cw, wcb, wab)


def _oproj_kernel(m_ref, w_ref, x_ref, g_ref, gate_ref, o_ref):
    mix = _dot(m_ref[0], w_ref[0])
    o_ref[0] = x_ref[0] + _tail2(gate_ref) * _rms(mix, g_ref[0])


def _oproj(merged, w_o, x, norm, mod, l, *, tm):
    b, s, d = x.shape
    blocks = (_nbytes((tm, d), BF16) + _nbytes((d, d), BF16) + 2 * _nbytes((tm, d), F32)
              + _nbytes((_mod_rows(mod) + 1, d), F32))
    return pl.pallas_call(
        _oproj_kernel,
        out_shape=jax.ShapeDtypeStruct((b, s, d), F32),
        grid=(b, s // tm),
        in_specs=[
            pl.BlockSpec((1, tm, d), lambda bi, i: (bi, i, 0)),
            pl.BlockSpec((1, d, d), lambda bi, i: (l, 0, 0)),
            pl.BlockSpec((1, tm, d), lambda bi, i: (bi, i, 0)),
            norm[1](1), mod[1](2),
        ],
        out_specs=pl.BlockSpec((1, tm, d), lambda bi, i: (bi, i, 0)),
        compiler_params=pltpu.CompilerParams(
            dimension_semantics=("parallel", "parallel"),
            vmem_limit_bytes=_vmem_limit(blocks, _nbytes((tm, d), F32))),
        name="oproj",
    )(merged, w_o, x, norm[0], mod[0])


def _ffn_kernel(*refs, tm, step, nchunk):
    if step:
        (x_ref, g2_ref, sc_ref, sh_ref, wg_ref, wv_ref, cwg_ref, cwv_ref, wd_ref, g3_ref, gate_ref,
         stg_ref, stv_ref, o_ref, tg_ref, tv_ref, h_ref) = refs
        hist = 0
    else:
        (x_ref, xp_ref, g2_ref, sc_ref, sh_ref, wg_ref, wv_ref, cwg_ref, cwv_ref, wd_ref, g3_ref, gate_ref,
         o_ref, tg_ref, tv_ref, h_ref, ug_ref, uv_ref) = refs
        hist = HIST
    i = pl.program_id(1)
    j = pl.program_id(2)
    last = pl.num_programs(2) - 1
    tf = wd_ref.shape[1]
    cols = pl.ds(pl.multiple_of(j * tf, tf), tf)
    cwg = cwg_ref[0, :, cols]
    cwv = cwv_ref[0, :, cols]

    if step:
        @pl.when(j == 0)
        def _():
            _modulate_rows(x_ref, g2_ref, sc_ref, sh_ref, h_ref, 0, tm)
            o_ref[...] = jnp.zeros_like(o_ref)

        h = h_ref[...]
        up_g = _dot(h, wg_ref[0, 0])
        up_v = _dot(h, wv_ref[0, 0])
        uc_g = stg_ref[0, 0] * cwg[0:1] + stg_ref[0, 1] * cwg[1:2] + up_g * cwg[2:3]
        uc_v = stv_ref[0, 0] * cwv[0:1] + stv_ref[0, 1] * cwv[1:2] + up_v * cwv[2:3]
        tg_ref[0, 0, :, cols] = up_g
        tv_ref[0, 0, :, cols] = up_v
        o_ref[0] += _dot((_gelu_tanh(uc_g) * uc_v).astype(BF16), wd_ref[0])

        @pl.when(j == last)
        def _():
            o_ref[0] = x_ref[0] + _tail2(gate_ref) * _rms(o_ref[0], g3_ref[0])
        return

    def seq_step(nck, is_first, is_last):
        rc = tm // nck
        if is_first:
            _modulate_rows(xp_ref, g2_ref, sc_ref, sh_ref, h_ref, 0, HIST)
        for c in range(nck):
            r = slice(c * rc, (c + 1) * rc)
            if is_first:
                y = _rms(x_ref[0, r], g2_ref[0])
                h_ref[HIST + c * rc:HIST + (c + 1) * rc] = (y * (1.0 + _tail2(sc_ref)) + _tail2(sh_ref)).astype(BF16)
            lo = 0 if c == 0 else HIST + c * rc
            hi = HIST + (c + 1) * rc
            h = h_ref[lo:hi]
            for u_ref, w_ref in ((ug_ref, wg_ref), (uv_ref, wv_ref)):
                up = _dot(h, w_ref[0, 0])
                if c == 0:
                    u_ref[0:HIST] = jnp.where(i == 0, 0.0, up[0:HIST])
                    u_ref[HIST:hi] = up[HIST:]
                else:
                    u_ref[lo:hi] = up

            def conv(ref, cw):
                base = HIST + c * rc
                return (ref[base - 2:base - 2 + rc] * cw[0:1] + ref[base - 1:base - 1 + rc] * cw[1:2]
                        + ref[base:base + rc] * cw[2:3])
            act = (_gelu_tanh(conv(ug_ref, cwg)) * conv(uv_ref, cwv)).astype(BF16)
            acc = _dot(act, wd_ref[0])
            if not is_first:
                acc = o_ref[0, r] + acc
            if is_last:
                acc = x_ref[0, r] + _tail2(gate_ref) * _rms(acc, g3_ref[0])
            o_ref[0, r] = acc
        tg_ref[0, 0, :, cols] = ug_ref[HIST + tm - 8:HIST + tm]
        tv_ref[0, 0, :, cols] = uv_ref[HIST + tm - 8:HIST + tm]

    pl.when(j == 0)(lambda: seq_step(2 * nchunk, True, False))
    pl.when((j > 0) & (j < last))(lambda: seq_step(nchunk, False, False))
    pl.when(j == last)(lambda: seq_step(2 * nchunk, False, True))


def _ffn(x, norm, mod, w_up, cw, w_down, state, l, *, tm, tf, step):
    b, s, d = x.shape
    ni, nj = s // tm, D_FF // tf
    trows = tm if step else 8
    w_specs = [
        pl.BlockSpec((1, 1, d, tf), lambda bi, i, j: (l, j, 0, 0)),
        pl.BlockSpec((1, 1, d, tf), lambda bi, i, j: (l, nj + j, 0, 0)),
        pl.BlockSpec((1, FFN_K, D_FF), lambda bi, i, j: (l, 0, 0)),
        pl.BlockSpec((1, FFN_K, D_FF), lambda bi, i, j: (l, 0, 1)),
        pl.BlockSpec((1, tf, d), lambda bi, i, j: (l, j, 0)),
    ]
    x_spec = pl.BlockSpec((1, tm, d), lambda bi, i, j: (bi, i, 0), pipeline_mode=pl.Buffered(1))
    hist = 0 if step else HIST
    scratch = [pltpu.VMEM((tm + hist, d), BF16)]
    blocks = (2 * _nbytes((tm, d), F32) + 3 * _nbytes((d, tf), BF16) + 2 * _nbytes((trows, tf), F32)
              + _nbytes((3 * _mod_rows(mod) + 2, d), F32))
    mods = [norm[1](2), mod[1](4), mod[1](3)]
    tailp = [norm[1](3), mod[1](5)]
    if step:
        ins = [x, norm[0], mod[0], mod[0], w_up, w_up, cw, cw, w_down, norm[0], mod[0], state, state]
        in_specs = [x_spec] + mods + w_specs + tailp + [
            pl.BlockSpec((1, FFN_K - 1, tm, tf), lambda bi, i, j: (l, 0, 0, j)),
            pl.BlockSpec((1, FFN_K - 1, tm, tf), lambda bi, i, j: (l, 0, 0, nj + j))]
        blocks += 2 * _nbytes((FFN_K - 1, tm, tf), F32)
    else:
        hblk = tm // HIST
        ins = [x, x, norm[0], mod[0], mod[0], w_up, w_up, cw, cw, w_down, norm[0], mod[0]]
        in_specs = [x_spec,
                    pl.BlockSpec((1, HIST, d), lambda bi, i, j: (bi, jnp.maximum(i * hblk - 1, 0), 0))
                    ] + mods + w_specs + tailp
        scratch += [pltpu.VMEM((tm + HIST, tf), F32), pltpu.VMEM((tm + HIST, tf), F32)]
        blocks += _nbytes((HIST, d), F32)
    scratch_bytes = _nbytes((tm + hist, d), BF16) + 6 * _nbytes((tm + hist, tf), F32)
    tail = jax.ShapeDtypeStruct((b, ni, trows, D_FF), F32)
    tail_spec = lambda: pl.BlockSpec((1, 1, trows, D_FF), lambda bi, i, j: (bi, i, 0, 0))
    return pl.pallas_call(
        functools.partial(_ffn_kernel, tm=tm, step=step, nchunk=ROW_CHUNKS),
        out_shape=(jax.ShapeDtypeStruct((b, s, d), F32), tail, tail),
        grid=(b, ni, nj),
        in_specs=in_specs,
        out_specs=(pl.BlockSpec((1, tm, d), lambda bi, i, j: (bi, i, 0)), tail_spec(), tail_spec()),
        scratch_shapes=scratch,
        compiler_params=pltpu.CompilerParams(
            dimension_semantics=("parallel", "parallel", "arbitrary"),
            vmem_limit_bytes=_vmem_limit(blocks, scratch_bytes)),
        name="ffn_step" if step else "ffn",
    )(*ins)


def _group_rel_bias(rel_bias):
    n = np.arange(N_DIL_KEYS + 1)
    max_exact = N_BUCKETS // 2
    onehot = np.zeros((N_ATTN_GROUPS, N_DIL_KEYS + 1, N_BUCKETS), np.float32)
    for g, (_, d) in enumerate(ATTN_GROUPS):
        dist = n * d
        large = max_exact + (np.log(np.maximum(dist, 1) / max_exact) / np.log(MAX_DISTANCE / max_exact)
                             * (N_BUCKETS - max_exact)).astype(np.int32)
        bucket = np.where(dist < max_exact, dist, np.minimum(large, N_BUCKETS - 1))
        onehot[g, n, bucket] = 1.0
    per_group = rel_bias.reshape(N_BUCKETS, N_ATTN_GROUPS, HEADS_PER_GROUP)
    return jnp.einsum("gnb,bgh->ghn", onehot, per_group, precision=lax.Precision.HIGHEST)


def _bias_tables(rel_bias):
    bias_g = _group_rel_bias(rel_bias).astype(F32)
    n = N_DIL_KEYS
    gh = (N_ATTN_GROUPS, HEADS_PER_GROUP)
    ext = jnp.concatenate([bias_g[:, :, ::-1], jnp.full(gh + (n,), NEG_INF, F32)], axis=-1)
    tab = jnp.tile(ext, (1, 1, n))[:, :, :n * 2 * n].reshape(N_ATTN_GROUPS, HEADS_PER_GROUP * n, 2 * n)
    step = []
    for g, (_, d) in enumerate(ATTN_GROUPS):
        hit = bias_g[g, :, :0:-1, None]
        row = jnp.concatenate([hit, jnp.full((HEADS_PER_GROUP, n, d - 1), NEG_INF, F32)], axis=-1)
        new = jnp.broadcast_to(bias_g[g, :, 0:1], (HEADS_PER_GROUP, n * d))
        pad = jnp.zeros((SAMPLE_ROWS - HEADS_PER_GROUP, n * d), F32)
        step.append(jnp.concatenate([row.reshape(HEADS_PER_GROUP, n * d), pad, new, pad], axis=0))
    return tab, step


def _kv_pack(qkv, g, nrows):
    b = qkv.shape[0]
    k0 = ATTN_WIDTH + g * GROUP_WIDTH
    v0 = 2 * ATTN_WIDTH + g * GROUP_WIDTH
    k = qkv[:, -nrows:, k0:k0 + GROUP_WIDTH].reshape(b, nrows, HEADS_PER_GROUP, HEAD_DIM)
    v = qkv[:, -nrows:, v0:v0 + GROUP_WIDTH].reshape(b, nrows, HEADS_PER_GROUP, HEAD_DIM)
    return jnp.stack([k, v], axis=2)


def _kv_tail_kernel(*refs):
    n = len(refs) // 3
    for g in range(n):
        o_ref = refs[2 * n + g]
        for part in range(2):
            for s in range(2):
                o_ref[0, part, s * LANES:(s + 1) * LANES, :] = refs[2 * g + part][0, s].T


def _kv_tails(qkv):
    b, _, s, _ = qkv.shape
    ins, in_specs, out_shape, out_specs, blocks = [], [], [], [], 0
    for g, (w, _) in enumerate(ATTN_GROUPS):
        nrows = min(w, s)
        assert s % nrows == 0
        for part in (1, 2):
            col = part * (ATTN_WIDTH // GROUP_WIDTH) + g
            ins.append(qkv)
            in_specs.append(pl.BlockSpec((1, 2, nrows, LANES), lambda bi, col=col, rb=s // nrows - 1: (bi, col, rb, 0)))
        out_shape.append(jax.ShapeDtypeStruct((b, 2, GROUP_WIDTH, nrows), F32))
        out_specs.append(pl.BlockSpec((1, 2, GROUP_WIDTH, nrows), lambda bi: (bi, 0, 0, 0)))
        blocks += 4 * _nbytes((nrows, GROUP_WIDTH), F32)
    outs = pl.pallas_call(
        _kv_tail_kernel,
        out_shape=out_shape,
        grid=(b,),
        in_specs=in_specs,
        out_specs=out_specs,
        compiler_params=pltpu.CompilerParams(
            dimension_semantics=("parallel",), vmem_limit_bytes=_vmem_limit(blocks, blocks // 2)),
        name="kv_tails",
    )(*ins)
    return [o.reshape(b, 2, HEADS_PER_GROUP, HEAD_DIM, o.shape[-1]).transpose(0, 4, 1, 2, 3) for o in outs]


def _prompt_layer(x, l, P):
    b, s, _ = x.shape
    norm, mod = P["norm"](l), P["mod_p"](l)
    tm = 512
    qkv_slab, h = _proj(x, norm, mod, P["w_qkv"], l, tm=1024, tn=ATTN_WIDTH, slab_out=True)
    act = _matmul(h, P["w_act"], l, tm=1024, tn=ACT_WIDTH)
    ols = [_attention(qkv_slab, P["bias_tab"], g) for g in range(N_ATTN_GROUPS)]
    attn = [o for o, _ in ols] + [ls for _, ls in ols]
    merged, u_tail = _branches(act, h, attn, None, P["branch"], l, tm=tm, tc=512, step=False)
    x1 = _oproj(merged, P["w_o"], x, norm, mod, l, tm=tm)
    x2, tail_g, tail_v = _ffn(x1, norm, mod, P["w_up"], P["ffn_cw"], P["w_down"], None, l,
                              tm=1024, tf=FFN_TILE, step=False)
    new_kv = _kv_tails(qkv_slab)
    new_pool = act[:, -POOL_STATE:, :POOL_WIDTH]
    new_conv = u_tail[:, -1, -(CONV_K - 1):]
    new_ffn = jnp.concatenate([tail_g[:, -1, -(FFN_K - 1):], tail_v[:, -1, -(FFN_K - 1):]], axis=-1)
    return x2, (new_kv[0], new_kv[1], new_kv[2], new_pool, new_conv, new_ffn)


def _tile_major(w, tn):
    nl, k, n = w.shape
    return w.reshape(nl, k, n // tn, tn).transpose(0, 2, 1, 3)


def _pad_rows(a, axis):
    pad = [(0, 0)] * a.ndim
    pad[axis] = (0, SAMPLE_ROWS - a.shape[axis])
    return jnp.pad(a, pad)


def _sample_layer(x, l, P, S, nb):
    norm, mod = P["norm"](l), P["mod_s"](l)
    tm = SAMPLE_ROWS
    qkv, h = _proj(x, norm, mod, P["w_qkv"], l, tm=tm, tn=ATTN_WIDTH, slab_out=False)
    act = _matmul(h, P["w_act"], l, tm=tm, tn=1024)
    attn = _attention_step(qkv, S["caches"], P["bias_step"], l, nb)
    attn = _pad_rows(attn.reshape(1, nb, GROUP_WIDTH), 1)
    merged, u = _branches(act, h, attn, (S["pool_t"], S["conv_t"]), P["branch"], l, tm=tm, tc=512, step=True)
    x1 = _oproj(merged, P["w_o"], x, norm, mod, l, tm=tm)
    x2, up_g, up_v = _ffn(x1, norm, mod, P["w_up"], P["ffn_cw"], P["w_down"], S["ffn_t"], l,
                          tm=tm, tf=FFN_TILE, step=True)
    new_kv = [_kv_pack(qkv[0, :nb, None], g, 1) for g in range(N_ATTN_GROUPS)]
    new_pool = jnp.concatenate([S["pool"][l][:, 1:], act[0, :nb, None, :POOL_WIDTH]], axis=1)
    new_conv = jnp.concatenate([S["conv"][l][:, 1:], u[0, 0, :nb, None]], axis=1)
    up_new = jnp.concatenate([up_g[0, 0, :nb], up_v[0, 0, :nb]], axis=-1)
    new_ffn = jnp.concatenate([S["ffn"][l][:, 1:], up_new[:, None]], axis=1)
    return x2, (new_kv[0], new_kv[1], new_kv[2], new_pool, new_conv, new_ffn)


def kernel(x_prompt, x_sample, c_prompt, c_sample, cache_kv_w128, cache_kv_w512, cache_kv_w2048, state_pool, state_conv, state_ffn_conv, rel_bias, norm_g, w_ada, b_ada, w_in, w_attn_br, w_pool_grp, pool_scale, w_pool_br, conv_w, w_conv_br, w_o, w_up, ffn_conv_w, w_down):
    nbp = x_prompt.shape[0]
    nbs, tdec, _ = x_sample.shape
    assert tdec == 1 and nbs <= SAMPLE_ROWS
    caches = (cache_kv_w128, cache_kv_w512, cache_kv_w2048)
    for (w, d), c in zip(ATTN_GROUPS, caches):
        assert c.shape[2] == w == N_DIL_KEYS * d, "cache must hold exactly one window"

    bias_tab, bias_step = _bias_tables(rel_bias)
    c_rows = -(-(SAMPLE_ROWS + nbp) // 8) * 8
    c_all = jnp.zeros((c_rows, D_MODEL), F32).at[:nbs].set(c_sample).at[SAMPLE_ROWS:SAMPLE_ROWS + nbp].set(c_prompt)
    mod_all = _ada(c_all, w_ada, b_ada)

    w_qkv, w_act, w_gate = _cast_split(
        w_in, [(0, QKV_WIDTH), (QKV_WIDTH, QKV_WIDTH + ACT_WIDTH), (QKV_WIDTH + ACT_WIDTH, w_in.shape[2])], tk=256)
    P = dict(
        norm=functools.partial(_norm_view, norm_g),
        mod_p=functools.partial(_mod_view, mod_all, prompt=True),
        mod_s=functools.partial(_mod_view, mod_all, prompt=False),
        w_qkv=w_qkv, w_act=w_act,
        branch=(w_pool_grp.astype(BF16), pool_scale[:, None, :], w_pool_br.astype(BF16),
                conv_w, w_conv_br.astype(BF16), w_attn_br.astype(BF16), w_gate),
        w_o=w_o.astype(BF16), w_up=_tile_major(w_up, FFN_TILE).astype(BF16), ffn_cw=ffn_conv_w, w_down=w_down.astype(BF16),
        bias_tab=bias_tab, bias_step=bias_step,
    )
    S = dict(
        caches=[c.transpose(0, 1, 3, 4, 5, 2).reshape(DEPTH, nbs, 2, GROUP_WIDTH, -1) for c in caches],
        pool=state_pool, conv=state_conv, ffn=state_ffn_conv,
        pool_t=_pad_rows(state_pool.transpose(0, 2, 1, 3), 2),
        conv_t=_pad_rows(state_conv.transpose(0, 2, 1, 3), 2),
        ffn_t=_pad_rows(state_ffn_conv.transpose(0, 2, 1, 3), 2),
    )
    yp = x_prompt
    ys = _pad_rows(x_sample.reshape(1, nbs, D_MODEL), 1)
    st_p, st_s = [], []
    for l in range(DEPTH):
        yp, sp = _prompt_layer(yp, l, P)
        ys, ss = _sample_layer(ys, l, P, S, nbs)
        st_p.append(sp)
        st_s.append(ss)
    outs_p = [jnp.stack([s[k] for s in st_p]) for k in range(6)]
    outs_s = [jnp.stack([s[k] for s in st_s]) for k in range(6)]
    return (yp, ys[0, :nbs, None, :], *outs_p, *outs_s)
```

```python
import functools

import numpy as np
import jax
import jax.numpy as jnp
from jax import lax
from jax.experimental import pallas as pl
from jax.experimental.pallas import tpu as pltpu

F32 = jnp.float32
BF16 = jnp.bfloat16

D_MODEL = 2048
DEPTH = 2
HEAD_DIM = 64
HEADS_PER_GROUP = 4
ATTN_GROUPS = ((128, 1), (512, 4), (2048, 16))
N_ATTN_GROUPS = len(ATTN_GROUPS)
ATTN_WIDTH = N_ATTN_GROUPS * HEADS_PER_GROUP * HEAD_DIM
GROUP_WIDTH = HEADS_PER_GROUP * HEAD_DIM
N_DIL_KEYS = 128
N_BUCKETS = 32
MAX_DISTANCE = 2048
ATTN_SCALE = HEAD_DIM ** -0.5
POOL_WINDOWS = (2, 4, 8, 16)
POOL_GROUP = 128
POOL_WIDTH = 512
POOL_STATE = 15
CONV_CH = 512
CONV_K = 3
D_FF = 5632
FFN_K = 3
N_MOD = 6
N_NORM = 4
EPS = 1e-6
NEG_INF = -1e30

LANES = 128
HIST = 16
SAMPLE_ROWS = 16
QKV_WIDTH = 3 * ATTN_WIDTH
QKV_SLABS = QKV_WIDTH // LANES
ACT_WIDTH = POOL_WIDTH + 3 * CONV_CH
ATTN_SUPER_BLOCK = 2048
ATTN_CLASS_UNROLL = 8
ROW_CHUNKS = 2
VMEM_CAP = 56 * 1024 * 1024


def _vmem_limit(block_bytes, scratch_bytes=0):
    del block_bytes, scratch_bytes
    return VMEM_CAP


def _nbytes(shape, dtype):
    return int(np.prod(shape)) * jnp.dtype(dtype).itemsize


def _rms(x, g):
    return x * lax.rsqrt(jnp.mean(x * x, axis=-1, keepdims=True) + EPS) * g


def _sigmoid(x):
    return 0.5 + 0.5 * jnp.tanh(0.5 * x)


def _gelu_tanh(x):
    return 0.5 * x * (1.0 + jnp.tanh(np.sqrt(2.0 / np.pi) * (x + 0.044715 * (x * x * x))))


def _dot(a, b):
    return jnp.dot(a, b, preferred_element_type=F32)


def _tail2(ref):
    return ref[(0,) * (len(ref.shape) - 2)]


def _norm_view(norm_g, l):
    arr = norm_g.reshape(DEPTH * N_NORM, 1, D_MODEL)
    return arr, lambda k: pl.BlockSpec((1, 1, D_MODEL), lambda *_: (l * N_NORM + k, 0, 0))


def _mod_view(mod_all, l, prompt):
    if prompt:
        arr = mod_all.reshape(DEPTH, mod_all.shape[1], N_MOD, 1, D_MODEL)
        return arr, lambda k: pl.BlockSpec((1, 1, 1, 1, D_MODEL), lambda bi, *_: (l, SAMPLE_ROWS + bi, k, 0, 0))
    return mod_all, lambda k: pl.BlockSpec((1, SAMPLE_ROWS, D_MODEL), lambda bi, *_: (l, 0, k))


def _mod_rows(mod):
    return mod[1](0).block_shape[-2]


def _cast_split_kernel(w_ref, *o_refs, bounds):
    for o_ref, (lo, hi) in zip(o_refs, bounds):
        o_ref[...] = w_ref[:, :, lo:hi].astype(BF16)


def _cast_split(w, bounds, *, tk):
    nl, k, n = w.shape
    blocks = _nbytes((tk, n), F32) + sum(_nbytes((tk, hi - lo), BF16) for lo, hi in bounds)
    return pl.pallas_call(
        functools.partial(_cast_split_kernel, bounds=tuple(bounds)),
        out_shape=[jax.ShapeDtypeStruct((nl, k, hi - lo), BF16) for lo, hi in bounds],
        grid=(nl, k // tk),
        in_specs=[pl.BlockSpec((1, tk, n), lambda li, ki: (li, ki, 0))],
        out_specs=[pl.BlockSpec((1, tk, hi - lo), lambda li, ki: (li, ki, 0)) for lo, hi in bounds],
        compiler_params=pltpu.CompilerParams(
            dimension_semantics=("parallel", "parallel"), vmem_limit_bytes=_vmem_limit(blocks)),
        name="cast_split",
    )(w)


def _ada_kernel(c_ref, w_ref, b_ref, o_ref):
    c = c_ref[...]
    s = (c * _sigmoid(c)).astype(BF16)
    o_ref[0] = _dot(s, w_ref[0].astype(BF16)) + b_ref[0]


def _ada(c_all, w_ada, b_ada):
    rows = c_all.shape[0]
    n = w_ada.shape[-1]
    tn = 1024
    blocks = _nbytes((rows, D_MODEL), F32) + _nbytes((D_MODEL, tn), F32) + _nbytes((rows + 1, tn), F32)
    return pl.pallas_call(
        _ada_kernel,
        out_shape=jax.ShapeDtypeStruct((DEPTH, rows, n), F32),
        grid=(DEPTH, n // tn),
        in_specs=[
            pl.BlockSpec((rows, D_MODEL), lambda l, j: (0, 0)),
            pl.BlockSpec((1, D_MODEL, tn), lambda l, j: (l, 0, j)),
            pl.BlockSpec((1, 1, tn), lambda l, j: (l, 0, j)),
        ],
        out_specs=pl.BlockSpec((1, rows, tn), lambda l, j: (l, 0, j)),
        compiler_params=pltpu.CompilerParams(
            dimension_semantics=("parallel", "parallel"),
            vmem_limit_bytes=_vmem_limit(blocks, _nbytes((D_MODEL, tn), BF16))),
        name="ada",
    )(c_all, w_ada, b_ada.reshape(DEPTH, 1, n))


def _modulate_rows(x_ref, g_ref, sc_ref, sh_ref, h_ref, row0, tm):
    y = _rms(x_ref[0], g_ref[0])
    h_ref[row0:row0 + tm] = (y * (1.0 + _tail2(sc_ref)) + _tail2(sh_ref)).astype(BF16)


def _proj_kernel(x_ref, g_ref, sc_ref, sh_ref, w_ref, wa_ref, o_ref, h_ref, a_ref, *, slabs, tm, nchunk, nq):
    j = pl.program_id(2)

    def project(r):
        res = _dot(h_ref[0, r], w_ref[0])
        if slabs:
            for s in range(slabs):
                o_ref[0, s, r] = res[:, s * LANES:(s + 1) * LANES]
        else:
            o_ref[0, r] = res

    @pl.when(j == 0)
    def _():
        rc = tm // nchunk
        for c in range(nchunk):
            r = slice(c * rc, (c + 1) * rc)
            y = _rms(x_ref[0, r], g_ref[0])
            h_ref[0, r] = (y * (1.0 + _tail2(sc_ref)) + _tail2(sh_ref)).astype(BF16)
            project(r)

    @pl.when((j > 0) & (j < nq))
    def _():
        project(slice(None))

    @pl.when(j >= nq)
    def _():
        a_ref[0] = _dot(h_ref[0], wa_ref[0])


def _proj(x, norm, mod, w, w_act, l, *, tm, tn, ta, slab_out):
    b, s, d = x.shape
    n, na = w.shape[2], w_act.shape[2]
    nq = n // tn
    slabs = tn // LANES if slab_out else 0
    r = _mod_rows(mod)
    blocks = (_nbytes((tm, d), F32) + _nbytes((2 * r + 1, d), F32) + _nbytes((d, tn + ta), BF16)
              + _nbytes((tm, tn + ta), F32) + _nbytes((tm, d), BF16))
    qcol = lambda j: jnp.minimum(j, nq - 1)
    acol = lambda j: jnp.maximum(j - nq, 0)
    if slab_out:
        out = jax.ShapeDtypeStruct((b, n // LANES, s, LANES), F32)
        out_spec = pl.BlockSpec((1, slabs, tm, LANES), lambda bi, i, j: (bi, qcol(j), i, 0))
    else:
        out = jax.ShapeDtypeStruct((b, s, n), F32)
        out_spec = pl.BlockSpec((1, tm, tn), lambda bi, i, j: (bi, i, qcol(j)))
    return pl.pallas_call(
        functools.partial(_proj_kernel, slabs=slabs, tm=tm, nchunk=2 * ROW_CHUNKS if r == 1 else 1, nq=nq),
        out_shape=[out, jax.ShapeDtypeStruct((b, s, d), BF16), jax.ShapeDtypeStruct((b, s, na), F32)],
        grid=(b, s // tm, nq + na // ta),
        in_specs=[
            pl.BlockSpec((1, tm, d), lambda bi, i, j: (bi, i, 0), pipeline_mode=pl.Buffered(1)),
            norm[1](0), mod[1](1), mod[1](0),
            pl.BlockSpec((1, d, tn), lambda bi, i, j: (l, 0, qcol(j))),
            pl.BlockSpec((1, d, ta), lambda bi, i, j: (l, 0, acol(j))),
        ],
        out_specs=[out_spec, pl.BlockSpec((1, tm, d), lambda bi, i, j: (bi, i, 0)),
                   pl.BlockSpec((1, tm, ta), lambda bi, i, j: (bi, i, acol(j)))],
        compiler_params=pltpu.CompilerParams(
            dimension_semantics=("parallel", "parallel", "arbitrary"),
            vmem_limit_bytes=_vmem_limit(blocks, _nbytes((tm, tn), F32) + _nbytes((tm, d), F32))),
        name="proj_norm",
    )(x, norm[0], mod[0], mod[0], w, w_act)


def _head_masks(rows):
    lane = lax.broadcasted_iota(jnp.int32, (rows, GROUP_WIDTH), 1)
    return [(lane >= h * HEAD_DIM) & (lane < (h + 1) * HEAD_DIM) for h in range(HEADS_PER_GROUP)]


def _attn_kernel(q_ref, kc_ref, kp_ref, vc_ref, vp_ref, bias_ref, o_ref, lse_ref, edge_ref, *, d, sb):
    i = pl.program_id(1)
    nq = N_DIL_KEYS
    span = nq * d
    ncb = sb // span
    hm = _head_masks(1)
    col = lax.broadcasted_iota(jnp.int32, (1, 2 * nq), 1)
    edge_ref[...] = jnp.where((col < nq) & (i == 0), NEG_INF, bias_ref[0])

    def rows(start):
        return pl.ds(start, nq, stride=d) if d > 1 else pl.ds(start, nq)

    def load(ref, start):
        return jnp.concatenate([ref[0, s, rows(start), :] for s in range(2)], axis=1)

    def one_block(r, jb):
        qs = jb * span + r
        q = load(q_ref, qs) * ATTN_SCALE
        if jb == 0:
            lo = sb - span + r
            k_lo, v_lo = load(kp_ref, lo), load(vp_ref, lo)
        else:
            lo = (jb - 1) * span + r
            k_lo, v_lo = load(kc_ref, lo), load(vc_ref, lo)
        kcat = jnp.concatenate([k_lo, load(kc_ref, qs)], axis=0).astype(BF16)
        vcat = jnp.concatenate([v_lo, load(vc_ref, qs)], axis=0).astype(BF16)
        qm = jnp.concatenate([jnp.where(hm[h], q, 0.0) for h in range(HEADS_PER_GROUP)], axis=0).astype(BF16)
        s = lax.dot_general(qm, kcat, (((1,), (1,)), ((), ())), preferred_element_type=F32)
        s = s + (edge_ref[...] if jb == 0 else bias_ref[0])
        m = jnp.max(s, axis=-1, keepdims=True)
        p = jnp.exp(s - m)
        l = jnp.sum(p, axis=-1, keepdims=True)
        oall = _dot((p * (1.0 / l)).astype(BF16), vcat)
        lse = m + jnp.log(l)
        o = jnp.zeros((nq, GROUP_WIDTH), F32)
        ls = jnp.zeros((nq, GROUP_WIDTH), F32)
        for h in range(HEADS_PER_GROUP):
            o = jnp.where(hm[h], oall[h * nq:(h + 1) * nq], o)
            ls = jnp.where(hm[h], lse[h * nq:(h + 1) * nq], ls)
        for sl in range(2):
            o_ref[0, sl, rows(qs), :] = o[:, sl * LANES:(sl + 1) * LANES]
            lse_ref[0, sl, rows(qs), :] = ls[:, sl * LANES:(sl + 1) * LANES]

    if d == 1:
        for jb in range(ncb):
            one_block(0, jb)
    else:
        def body(r, carry):
            for jb in range(ncb):
                one_block(r, jb)
            return carry
        lax.fori_loop(0, d, body, 0, unroll=min(d, ATTN_CLASS_UNROLL))


def _attention(qkv, bias_tab, g):
    b, _, s, _ = qkv.shape
    d = ATTN_GROUPS[g][1]
    sb = max(N_DIL_KEYS * d, ATTN_SUPER_BLOCK)
    blk = (1, 2, sb, LANES)
    kslab, vslab = ATTN_WIDTH // GROUP_WIDTH + g, 2 * ATTN_WIDTH // GROUP_WIDTH + g
    prev = lambda i: jnp.maximum(i - 1, 0)
    blocks = 7 * _nbytes(blk, F32) + _nbytes((4 * N_DIL_KEYS, 2 * N_DIL_KEYS), F32)
    out = jax.ShapeDtypeStruct((b, 2, s, LANES), F32)
    return pl.pallas_call(
        functools.partial(_attn_kernel, d=d, sb=sb),
        out_shape=(out, out),
        grid=(b, s // sb),
        in_specs=[
            pl.BlockSpec(blk, lambda bi, i: (bi, g, i, 0)),
            pl.BlockSpec(blk, lambda bi, i: (bi, kslab, i, 0)),
            pl.BlockSpec(blk, lambda bi, i: (bi, kslab, prev(i), 0)),
            pl.BlockSpec(blk, lambda bi, i: (bi, vslab, i, 0)),
            pl.BlockSpec(blk, lambda bi, i: (bi, vslab, prev(i), 0)),
            pl.BlockSpec((1, 4 * N_DIL_KEYS, 2 * N_DIL_KEYS), lambda bi, i: (g, 0, 0)),
        ],
        out_specs=(pl.BlockSpec(blk, lambda bi, i: (bi, 0, i, 0)),
                   pl.BlockSpec(blk, lambda bi, i: (bi, 0, i, 0))),
        scratch_shapes=[pltpu.VMEM((HEADS_PER_GROUP * N_DIL_KEYS, 2 * N_DIL_KEYS), F32)],
        compiler_params=pltpu.CompilerParams(
            dimension_semantics=("parallel", "parallel"),
            vmem_limit_bytes=_vmem_limit(blocks, 8 << 20)),
        name=f"attn_d{d}",
    )(qkv, qkv, qkv, qkv, qkv, bias_tab)


def _attn_step_kernel(qkv_ref, c0_ref, c1_ref, c2_ref, b0_ref, b1_ref, b2_ref, o_ref):
    b = pl.program_id(0)
    rows = SAMPLE_ROWS
    lane = lax.broadcasted_iota(jnp.int32, (rows, GROUP_WIDTH), 1)
    row = lax.broadcasted_iota(jnp.int32, (rows, GROUP_WIDTH), 0)
    sel = (lane >= row * HEAD_DIM) & (lane < (row + 1) * HEAD_DIM)

    def rowvec(col0):
        return qkv_ref[0, pl.ds(b, 1), col0:col0 + GROUP_WIDTH]

    outs, lses = [], []
    for g, (c_ref, b_ref) in enumerate(((c0_ref, b0_ref), (c1_ref, b1_ref), (c2_ref, b2_ref))):
        q = rowvec(g * GROUP_WIDTH) * ATTN_SCALE
        kn = rowvec(ATTN_WIDTH + g * GROUP_WIDTH).astype(BF16).astype(F32)
        vn = rowvec(2 * ATTN_WIDTH + g * GROUP_WIDTH).astype(BF16).astype(F32)
        q4 = jnp.where(sel, jnp.broadcast_to(q, (rows, GROUP_WIDTH)), 0.0).astype(BF16)
        s_c = _dot(q4, c_ref[0, 0, 0].astype(BF16)) + b_ref[0:rows, :]
        s_n = jnp.sum(q4.astype(F32) * kn, axis=-1, keepdims=True) + b_ref[rows:2 * rows, 0:1]
        m = jnp.maximum(jnp.max(s_c, axis=-1, keepdims=True), s_n)
        p_c = jnp.exp(s_c - m)
        p_n = jnp.exp(s_n - m)
        l = jnp.sum(p_c, axis=-1, keepdims=True) + p_n
        inv = 1.0 / l
        oc = lax.dot_general((p_c * inv).astype(BF16), c_ref[0, 0, 1].astype(BF16), (((1,), (1,)), ((), ())),
                             preferred_element_type=F32)
        oc = oc + (p_n * inv).astype(BF16).astype(F32) * vn
        lse = m + jnp.log(l)
        outs.append(jnp.sum(jnp.where(sel, oc, 0.0), axis=0, keepdims=True))
        lses.append(jnp.sum(jnp.where(sel, lse, 0.0), axis=0, keepdims=True))
    mx = jnp.maximum(jnp.maximum(lses[0], lses[1]), lses[2])
    w = [jnp.exp(ls - mx) for ls in lses]
    tot = w[0] + w[1] + w[2]
    o_ref[0] = (w[0] * outs[0] + w[1] * outs[1] + w[2] * outs[2]) * (1.0 / tot)


def _attention_step(qkv, caches, biases, l, nb):
    cache_specs = [pl.BlockSpec((1, 1) + c.shape[2:], lambda bi: (l, bi, 0, 0, 0)) for c in caches]
    bias_specs = [pl.BlockSpec(b.shape, lambda bi: (0, 0)) for b in biases]
    blocks = (sum(_nbytes(c.shape[2:], F32) for c in caches) + sum(_nbytes(b.shape, F32) for b in biases)
              + _nbytes(qkv.shape, F32))
    return pl.pallas_call(
        _attn_step_kernel,
        out_shape=jax.ShapeDtypeStruct((nb, 1, GROUP_WIDTH), F32),
        grid=(nb,),
        in_specs=[pl.BlockSpec(qkv.shape, lambda bi: (0, 0, 0))] + cache_specs + bias_specs,
        out_specs=pl.BlockSpec((1, 1, GROUP_WIDTH), lambda bi: (bi, 0, 0)),
        compiler_params=pltpu.CompilerParams(
            dimension_semantics=("parallel",),
            vmem_limit_bytes=_vmem_limit(blocks, 8 << 20)),
        name="attn_step",
    )(qkv, *caches, *biases)


def _branch_kernel(*refs, tm, step):
    if step:
        (act_ref, pst_ref, cst_ref, h_ref, wga_ref, wgb_ref, wgc_ref, at_ref,
         wgrp_ref, pscale_ref, wpb_ref, cw_ref, wcb_ref, wab_ref,
         out_ref, u_ref, pzs_ref, cb_ref, comb_ref) = refs
    else:
        (act_ref, hist_ref, h_ref, wga_ref, wgb_ref, wgc_ref, o0_ref, o1_ref, o2_ref, l0_ref, l1_ref, l2_ref,
         wgrp_ref, pscale_ref, wpb_ref, cw_ref, wcb_ref, wab_ref,
         out_ref, u_ref, pzs_ref, cb_ref, comb_ref, pe_ref, ue_ref) = refs
    c_tile = pl.program_id(1)
    i = pl.program_id(2)
    rows = pl.ds(pl.multiple_of(i * tm, tm), tm)
    pzs_ref, cb_ref, comb_ref = pzs_ref.at[rows], cb_ref.at[rows], comb_ref.at[rows]

    def sequence_mixers():
        p = act_ref[0, :, 0:POOL_WIDTH]
        gate_b = act_ref[0, :, POOL_WIDTH:POOL_WIDTH + CONV_CH]
        u = act_ref[0, :, POOL_WIDTH + CONV_CH:POOL_WIDTH + 2 * CONV_CH] * \
            act_ref[0, :, POOL_WIDTH + 2 * CONV_CH:POOL_WIDTH + 3 * CONV_CH]
        cw = cw_ref[0]
        if step:
            acc = p
            sums = {}
            for k in range(1, max(POOL_WINDOWS)):
                acc = acc + pst_ref[0, POOL_STATE - k]
                sums[k + 1] = acc
            means = [sums[w][:, gi * POOL_GROUP:(gi + 1) * POOL_GROUP] * (1.0 / w)
                     for gi, w in enumerate(POOL_WINDOWS)]
            conv = cst_ref[0, 0] * cw[0:1] + cst_ref[0, 1] * cw[1:2] + u * cw[2:3]
            u_ref[0, 0] = u
            comb_ref[...] = at_ref[0].astype(BF16)
        else:
            first = i == 0
            hist_p = hist_ref[0, :, 0:POOL_WIDTH]
            hist_u = hist_ref[0, :, POOL_WIDTH + CONV_CH:POOL_WIDTH + 2 * CONV_CH] * \
                hist_ref[0, :, POOL_WIDTH + 2 * CONV_CH:POOL_WIDTH + 3 * CONV_CH]
            pe_ref[0:HIST] = jnp.where(first, 0.0, hist_p)
            ue_ref[0:HIST] = jnp.where(first, 0.0, hist_u)
            pe_ref[HIST:HIST + tm] = p
            ue_ref[HIST:HIST + tm] = u
            t = i * tm + lax.broadcasted_iota(jnp.int32, (tm, 1), 0)
            means = []
            for gi, w in enumerate(POOL_WINDOWS):
                cs = slice(gi * POOL_GROUP, (gi + 1) * POOL_GROUP)
                acc = pe_ref[HIST:HIST + tm, cs]
                for k in range(1, w):
                    acc = acc + pe_ref[HIST - k:HIST - k + tm, cs]
                cnt = jnp.minimum(t + 1, w).astype(F32)
                means.append(acc * (1.0 / cnt))
            conv = (ue_ref[HIST - 2:HIST - 2 + tm] * cw[0:1] + ue_ref[HIST - 1:HIST - 1 + tm] * cw[1:2]
                    + u * cw[2:3])
            u_ref[0, 0] = ue_ref[HIST + tm - 8:HIST + tm]
            for sl in range(2):
                ls = [r[0, sl] for r in (l0_ref, l1_ref, l2_ref)]
                os_ = [r[0, sl] for r in (o0_ref, o1_ref, o2_ref)]
                mx = jnp.maximum(jnp.maximum(ls[0], ls[1]), ls[2])
                w_ = [jnp.exp(x - mx) for x in ls]
                tot = w_[0] + w_[1] + w_[2]
                comb = (w_[0] * os_[0] + w_[1] * os_[1] + w_[2] * os_[2]) * (1.0 / tot)
                comb_ref[:, sl * LANES:(sl + 1) * LANES] = comb.astype(BF16)
        pscale = pscale_ref[0]
        for gi in range(len(POOL_WINDOWS)):
            cs = slice(gi * POOL_GROUP, (gi + 1) * POOL_GROUP)
            pm = (means[gi] - p[:, cs]).astype(BF16)
            pz = _dot(pm, wgrp_ref[0, gi])
            pzs_ref[:, cs] = (pz * pscale[:, cs]).astype(BF16)
        cb_ref[...] = (gate_b * conv).astype(BF16)

    def merge_tile():
        h = h_ref[0]
        merged = _sigmoid(_dot(h, wga_ref[0])) * _dot(pzs_ref[...], wpb_ref[0])
        merged += _sigmoid(_dot(h, wgb_ref[0])) * _dot(cb_ref[...], wcb_ref[0])
        merged += _sigmoid(_dot(h, wgc_ref[0])) * _dot(comb_ref[...], wab_ref[0])
        out_ref[0] = merged.astype(BF16)

    @pl.when(c_tile == 0)
    def _():
        sequence_mixers()
        merge_tile()

    @pl.when(c_tile > 0)
    def _():
        merge_tile()


def _branches(act, h, attn, states, weights, l, *, tm, tc, step):
    b, s, _ = act.shape
    wgrp, pscale, wpb, cw, wcb, wab, wgate = weights
    ni = s // tm
    once = lambda c, i: jnp.where(c == 0, i, ni - 1)
    gate_specs = [pl.BlockSpec((1, tm, D_MODEL), lambda bi, c, i: (bi, i, 0))] + [
        pl.BlockSpec((1, D_MODEL, tc), lambda bi, c, i, k=k: (l, 0, k * (D_MODEL // tc) + c)) for k in range(3)]
    w_specs = [
        pl.BlockSpec((1,) + wgrp.shape[1:], lambda bi, c, i: (l, 0, 0, 0)),
        pl.BlockSpec((1,) + pscale.shape[1:], lambda bi, c, i: (l, 0, 0)),
        pl.BlockSpec((1, POOL_WIDTH, tc), lambda bi, c, i: (l, 0, c)),
        pl.BlockSpec((1,) + cw.shape[1:], lambda bi, c, i: (l, 0, 0)),
        pl.BlockSpec((1, CONV_CH, tc), lambda bi, c, i: (l, 0, c)),
        pl.BlockSpec((1, GROUP_WIDTH, tc), lambda bi, c, i: (l, 0, c)),
    ]
    act_spec = pl.BlockSpec((1, tm, ACT_WIDTH), lambda bi, c, i: (bi, once(c, i), 0))
    scratch = [pltpu.VMEM((s, POOL_WIDTH), BF16), pltpu.VMEM((s, CONV_CH), BF16),
               pltpu.VMEM((s, GROUP_WIDTH), BF16)]
    blocks = (_nbytes((tm, ACT_WIDTH), F32) + _nbytes((tm + 3 * tc, D_MODEL), BF16) + _nbytes(wgrp.shape[1:], BF16)
              + _nbytes((POOL_WIDTH + CONV_CH + GROUP_WIDTH, tc), BF16) + _nbytes((tm, tc), BF16))
    if step:
        pst, cst = states
        ins = [act, pst, cst, h, wgate, wgate, wgate, attn]
        in_specs = [act_spec,
                    pl.BlockSpec((1,) + pst.shape[1:], lambda bi, c, i: (l, 0, 0, 0)),
                    pl.BlockSpec((1,) + cst.shape[1:], lambda bi, c, i: (l, 0, 0, 0))] + gate_specs + [
                    pl.BlockSpec((1, tm, GROUP_WIDTH), lambda bi, c, i: (bi, once(c, i), 0))]
        urows = tm
        blocks += _nbytes(pst.shape[1:], F32) + _nbytes(cst.shape[1:], F32)
    else:
        hblk = tm // HIST
        slab = pl.BlockSpec((1, 2, tm, LANES), lambda bi, c, i: (bi, 0, once(c, i), 0))
        ins = [act, act, h, wgate, wgate, wgate] + list(attn)
        in_specs = [act_spec,
                    pl.BlockSpec((1, HIST, ACT_WIDTH),
                                 lambda bi, c, i: (bi, jnp.maximum(once(c, i) * hblk - 1, 0), 0))
                    ] + gate_specs + [slab] * 6
        scratch += [pltpu.VMEM((tm + HIST, POOL_WIDTH), F32), pltpu.VMEM((tm + HIST, CONV_CH), F32)]
        urows = 8
        blocks += 6 * _nbytes((2, tm, LANES), F32) + _nbytes((HIST, ACT_WIDTH), F32)
    scratch_bytes = (_nbytes((s, POOL_WIDTH + CONV_CH + GROUP_WIDTH), BF16) + 2 * _nbytes((tm + HIST, POOL_WIDTH), F32)
                     + 8 * _nbytes((tm, tc), F32))
    return pl.pallas_call(
        functools.partial(_branch_kernel, tm=tm, step=step),
        out_shape=(jax.ShapeDtypeStruct((b, s, D_MODEL), BF16),
                   jax.ShapeDtypeStruct((b, ni, urows, CONV_CH), F32)),
        grid=(b, D_MODEL // tc, ni),
        in_specs=in_specs + w_specs,
        out_specs=(pl.BlockSpec((1, tm, tc), lambda bi, c, i: (bi, i, c)),
                   pl.BlockSpec((1, 1, urows, CONV_CH), lambda bi, c, i: (bi, once(c, i), 0, 0))),
        scratch_shapes=scratch,
        compiler_params=pltpu.CompilerParams(
            dimension_semantics=("parallel", "arbitrary", "arbitrary"),
            vmem_limit_bytes=_vmem_limit(blocks, scratch_bytes)),
        name="branches_step" if step else "branches",
    )(*ins, wgrp, pscale, wpb, cw, wcb, wab)


def _oproj_kernel(m_ref, w_ref, x_ref, g_ref, gate_ref, o_ref):
    mix = _dot(m_ref[0], w_ref[0])
    o_ref[0] = x_ref[0] + _tail2(gate_ref) * _rms(mix, g_ref[0])


def _oproj(merged, w_o, x, norm, mod, l, *, tm):
    b, s, d = x.shape
    blocks = (_nbytes((tm, d), BF16) + _nbytes((d, d), BF16) + 2 * _nbytes((tm, d), F32)
              + _nbytes((_mod_rows(mod) + 1, d), F32))
    return pl.pallas_call(
        _oproj_kernel,
        out_shape=jax.ShapeDtypeStruct((b, s, d), F32),
        grid=(b, s // tm),
        in_specs=[
            pl.BlockSpec((1, tm, d), lambda bi, i: (bi, i, 0)),
            pl.BlockSpec((1, d, d), lambda bi, i: (l, 0, 0)),
            pl.BlockSpec((1, tm, d), lambda bi, i: (bi, i, 0)),
            norm[1](1), mod[1](2),
        ],
        out_specs=pl.BlockSpec((1, tm, d), lambda bi, i: (bi, i, 0)),
        compiler_params=pltpu.CompilerParams(
            dimension_semantics=("parallel", "parallel"),
            vmem_limit_bytes=_vmem_limit(blocks, _nbytes((tm, d), F32))),
        name="oproj",
    )(merged, w_o, x, norm[0], mod[0])


def _ffn_kernel(*refs, tm, step, nchunk):
    if step:
        (x_ref, g2_ref, sc_ref, sh_ref, wg_ref, wv_ref, cwg_ref, cwv_ref, wd_ref, g3_ref, gate_ref,
         stg_ref, stv_ref, o_ref, tg_ref, tv_ref, h_ref) = refs
        hist = 0
    else:
        (x_ref, xp_ref, g2_ref, sc_ref, sh_ref, wg_ref, wv_ref, cwg_ref, cwv_ref, wd_ref, g3_ref, gate_ref,
         o_ref, tg_ref, tv_ref, h_ref, ug_ref, uv_ref) = refs
        hist = HIST
    i = pl.program_id(1)
    j = pl.program_id(2)
    last = pl.num_programs(2) - 1
    tf = wd_ref.shape[1]
    cols = pl.ds(pl.multiple_of(j * tf, tf), tf)
    cwg = cwg_ref[0, :, cols]
    cwv = cwv_ref[0, :, cols]

    if step:
        @pl.when(j == 0)
        def _():
            _modulate_rows(x_ref, g2_ref, sc_ref, sh_ref, h_ref, 0, tm)
            o_ref[...] = jnp.zeros_like(o_ref)

        h = h_ref[...]
        up_g = _dot(h, wg_ref[0])
        up_v = _dot(h, wv_ref[0])
        uc_g = stg_ref[0, 0] * cwg[0:1] + stg_ref[0, 1] * cwg[1:2] + up_g * cwg[2:3]
        uc_v = stv_ref[0, 0] * cwv[0:1] + stv_ref[0, 1] * cwv[1:2] + up_v * cwv[2:3]
        tg_ref[0, 0, :, cols] = up_g
        tv_ref[0, 0, :, cols] = up_v
        o_ref[0] += _dot((_gelu_tanh(uc_g) * uc_v).astype(BF16), wd_ref[0])

        @pl.when(j == last)
        def _():
            o_ref[0] = x_ref[0] + _tail2(gate_ref) * _rms(o_ref[0], g3_ref[0])
        return

    def seq_step(nck, is_first, is_last):
        rc = tm // nck
        if is_first:
            _modulate_rows(xp_ref, g2_ref, sc_ref, sh_ref, h_ref, 0, HIST)
        for c in range(nck):
            r = slice(c * rc, (c + 1) * rc)
            if is_first:
                y = _rms(x_ref[0, r], g2_ref[0])
                h_ref[HIST + c * rc:HIST + (c + 1) * rc] = (y * (1.0 + _tail2(sc_ref)) + _tail2(sh_ref)).astype(BF16)
            lo = 0 if c == 0 else HIST + c * rc
            hi = HIST + (c + 1) * rc
            h = h_ref[lo:hi]
            for u_ref, w_ref in ((ug_ref, wg_ref), (uv_ref, wv_ref)):
                up = _dot(h, w_ref[0])
                if c == 0:
                    u_ref[0:HIST] = jnp.where(i == 0, 0.0, up[0:HIST])
                    u_ref[HIST:hi] = up[HIST:]
                else:
                    u_ref[lo:hi] = up

            def conv(ref, cw):
                base = HIST + c * rc
                return (ref[base - 2:base - 2 + rc] * cw[0:1] + ref[base - 1:base - 1 + rc] * cw[1:2]
                        + ref[base:base + rc] * cw[2:3])
            act = (_gelu_tanh(conv(ug_ref, cwg)) * conv(uv_ref, cwv)).astype(BF16)
            acc = _dot(act, wd_ref[0])
            if not is_first:
                acc = o_ref[0, r] + acc
            if is_last:
                acc = x_ref[0, r] + _tail2(gate_ref) * _rms(acc, g3_ref[0])
            o_ref[0, r] = acc
        tg_ref[0, 0, :, cols] = ug_ref[HIST + tm - 8:HIST + tm]
        tv_ref[0, 0, :, cols] = uv_ref[HIST + tm - 8:HIST + tm]

    pl.when(j == 0)(lambda: seq_step(2 * nchunk, True, False))
    pl.when((j > 0) & (j < last))(lambda: seq_step(nchunk, False, False))
    pl.when(j == last)(lambda: seq_step(2 * nchunk, False, True))


def _ffn(x, norm, mod, w_up, cw, w_down, state, l, *, tm, tf, step):
    b, s, d = x.shape
    ni, nj = s // tm, D_FF // tf
    trows = tm if step else 8
    w_specs = [
        pl.BlockSpec((1, d, tf), lambda bi, i, j: (l, 0, j)),
        pl.BlockSpec((1, d, tf), lambda bi, i, j: (l, 0, nj + j)),
        pl.BlockSpec((1, FFN_K, D_FF), lambda bi, i, j: (l, 0, 0)),
        pl.BlockSpec((1, FFN_K, D_FF), lambda bi, i, j: (l, 0, 1)),
        pl.BlockSpec((1, tf, d), lambda bi, i, j: (l, j, 0)),
    ]
    x_spec = pl.BlockSpec((1, tm, d), lambda bi, i, j: (bi, i, 0), pipeline_mode=pl.Buffered(1))
    hist = 0 if step else HIST
    scratch = [pltpu.VMEM((tm + hist, d), BF16)]
    blocks = (2 * _nbytes((tm, d), F32) + 3 * _nbytes((d, tf), BF16) + 2 * _nbytes((trows, tf), F32)
              + _nbytes((3 * _mod_rows(mod) + 2, d), F32))
    mods = [norm[1](2), mod[1](4), mod[1](3)]
    tailp = [norm[1](3), mod[1](5)]
    if step:
        ins = [x, norm[0], mod[0], mod[0], w_up, w_up, cw, cw, w_down, norm[0], mod[0], state, state]
        in_specs = [x_spec] + mods + w_specs + tailp + [
            pl.BlockSpec((1, FFN_K - 1, tm, tf), lambda bi, i, j: (l, 0, 0, j)),
            pl.BlockSpec((1, FFN_K - 1, tm, tf), lambda bi, i, j: (l, 0, 0, nj + j))]
        blocks += 2 * _nbytes((FFN_K - 1, tm, tf), F32)
    else:
        hblk = tm // HIST
        ins = [x, x, norm[0], mod[0], mod[0], w_up, w_up, cw, cw, w_down, norm[0], mod[0]]
        in_specs = [x_spec,
                    pl.BlockSpec((1, HIST, d), lambda bi, i, j: (bi, jnp.maximum(i * hblk - 1, 0), 0))
                    ] + mods + w_specs + tailp
        scratch += [pltpu.VMEM((tm + HIST, tf), F32), pltpu.VMEM((tm + HIST, tf), F32)]
        blocks += _nbytes((HIST, d), F32)
    scratch_bytes = _nbytes((tm + hist, d), BF16) + 6 * _nbytes((tm + hist, tf), F32)
    tail = jax.ShapeDtypeStruct((b, ni, trows, D_FF), F32)
    tail_spec = lambda: pl.BlockSpec((1, 1, trows, D_FF), lambda bi, i, j: (bi, i, 0, 0))
    return pl.pallas_call(
        functools.partial(_ffn_kernel, tm=tm, step=step, nchunk=ROW_CHUNKS),
        out_shape=(jax.ShapeDtypeStruct((b, s, d), F32), tail, tail),
        grid=(b, ni, nj),
        in_specs=in_specs,
        out_specs=(pl.BlockSpec((1, tm, d), lambda bi, i, j: (bi, i, 0)), tail_spec(), tail_spec()),
        scratch_shapes=scratch,
        compiler_params=pltpu.CompilerParams(
            dimension_semantics=("parallel", "parallel", "arbitrary"),
            vmem_limit_bytes=_vmem_limit(blocks, scratch_bytes)),
        name="ffn_step" if step else "ffn",
    )(*ins)


def _group_rel_bias(rel_bias):
    n = np.arange(N_DIL_KEYS + 1)
    max_exact = N_BUCKETS // 2
    onehot = np.zeros((N_ATTN_GROUPS, N_DIL_KEYS + 1, N_BUCKETS), np.float32)
    for g, (_, d) in enumerate(ATTN_GROUPS):
        dist = n * d
        large = max_exact + (np.log(np.maximum(dist, 1) / max_exact) / np.log(MAX_DISTANCE / max_exact)
                             * (N_BUCKETS - max_exact)).astype(np.int32)
        bucket = np.where(dist < max_exact, dist, np.minimum(large, N_BUCKETS - 1))
        onehot[g, n, bucket] = 1.0
    per_group = rel_bias.reshape(N_BUCKETS, N_ATTN_GROUPS, HEADS_PER_GROUP)
    return jnp.einsum("gnb,bgh->ghn", onehot, per_group, precision=lax.Precision.HIGHEST)


def _bias_tables(rel_bias):
    bias_g = _group_rel_bias(rel_bias).astype(F32)
    n = N_DIL_KEYS
    gh = (N_ATTN_GROUPS, HEADS_PER_GROUP)
    ext = jnp.concatenate([bias_g[:, :, ::-1], jnp.full(gh + (n,), NEG_INF, F32)], axis=-1)
    tab = jnp.tile(ext, (1, 1, n))[:, :, :n * 2 * n].reshape(N_ATTN_GROUPS, HEADS_PER_GROUP * n, 2 * n)
    step = []
    for g, (_, d) in enumerate(ATTN_GROUPS):
        hit = bias_g[g, :, :0:-1, None]
        row = jnp.concatenate([hit, jnp.full((HEADS_PER_GROUP, n, d - 1), NEG_INF, F32)], axis=-1)
        new = jnp.broadcast_to(bias_g[g, :, 0:1], (HEADS_PER_GROUP, n * d))
        pad = jnp.zeros((SAMPLE_ROWS - HEADS_PER_GROUP, n * d), F32)
        step.append(jnp.concatenate([row.reshape(HEADS_PER_GROUP, n * d), pad, new, pad], axis=0))
    return tab, step


def _kv_pack(qkv, g, nrows):
    b = qkv.shape[0]
    k0 = ATTN_WIDTH + g * GROUP_WIDTH
    v0 = 2 * ATTN_WIDTH + g * GROUP_WIDTH
    k = qkv[:, -nrows:, k0:k0 + GROUP_WIDTH].reshape(b, nrows, HEADS_PER_GROUP, HEAD_DIM)
    v = qkv[:, -nrows:, v0:v0 + GROUP_WIDTH].reshape(b, nrows, HEADS_PER_GROUP, HEAD_DIM)
    return jnp.stack([k, v], axis=2)


def _kv_tail_kernel(*refs):
    n = len(refs) // 3
    for g in range(n):
        o_ref = refs[2 * n + g]
        for part in range(2):
            for s in range(2):
                o_ref[0, part, s * LANES:(s + 1) * LANES, :] = refs[2 * g + part][0, s].T


def _kv_tails(qkv):
    b, _, s, _ = qkv.shape
    ins, in_specs, out_shape, out_specs, blocks = [], [], [], [], 0
    for g, (w, _) in enumerate(ATTN_GROUPS):
        nrows = min(w, s)
        assert s % nrows == 0
        for part in (1, 2):
            col = part * (ATTN_WIDTH // GROUP_WIDTH) + g
            ins.append(qkv)
            in_specs.append(pl.BlockSpec((1, 2, nrows, LANES), lambda bi, col=col, rb=s // nrows - 1: (bi, col, rb, 0)))
        out_shape.append(jax.ShapeDtypeStruct((b, 2, GROUP_WIDTH, nrows), F32))
        out_specs.append(pl.BlockSpec((1, 2, GROUP_WIDTH, nrows), lambda bi: (bi, 0, 0, 0)))
        blocks += 4 * _nbytes((nrows, GROUP_WIDTH), F32)
    outs = pl.pallas_call(
        _kv_tail_kernel,
        out_shape=out_shape,
        grid=(b,),
        in_specs=in_specs,
        out_specs=out_specs,
        compiler_params=pltpu.CompilerParams(
            dimension_semantics=("parallel",), vmem_limit_bytes=_vmem_limit(blocks, blocks // 2)),
        name="kv_tails",
    )(*ins)
    return [o.reshape(b, 2, HEADS_PER_GROUP, HEAD_DIM, o.shape[-1]).transpose(0, 4, 1, 2, 3) for o in outs]


def _prompt_layer(x, l, P):
    b, s, _ = x.shape
    norm, mod = P["norm"](l), P["mod_p"](l)
    tm = 512
    qkv_slab, h, act = _proj(x, norm, mod, P["w_qkv"], P["w_act"], l, tm=1024, tn=ATTN_WIDTH, ta=1024, slab_out=True)
    ols = [_attention(qkv_slab, P["bias_tab"], g) for g in range(N_ATTN_GROUPS)]
    attn = [o for o, _ in ols] + [ls for _, ls in ols]
    merged, u_tail = _branches(act, h, attn, None, P["branch"], l, tm=tm, tc=512, step=False)
    x1 = _oproj(merged, P["w_o"], x, norm, mod, l, tm=tm)
    x2, tail_g, tail_v = _ffn(x1, norm, mod, P["w_up"], P["ffn_cw"], P["w_down"], None, l,
                              tm=1024, tf=512, step=False)
    new_kv = _kv_tails(qkv_slab)
    new_pool = act[:, -POOL_STATE:, :POOL_WIDTH]
    new_conv = u_tail[:, -1, -(CONV_K - 1):]
    new_ffn = jnp.concatenate([tail_g[:, -1, -(FFN_K - 1):], tail_v[:, -1, -(FFN_K - 1):]], axis=-1)
    return x2, (new_kv[0], new_kv[1], new_kv[2], new_pool, new_conv, new_ffn)


def _pad_rows(a, axis):
    pad = [(0, 0)] * a.ndim
    pad[axis] = (0, SAMPLE_ROWS - a.shape[axis])
    return jnp.pad(a, pad)


def _sample_layer(x, l, P, S, nb):
    norm, mod = P["norm"](l), P["mod_s"](l)
    tm = SAMPLE_ROWS
    qkv, h, act = _proj(x, norm, mod, P["w_qkv"], P["w_act"], l, tm=tm, tn=ATTN_WIDTH, ta=1024, slab_out=False)
    attn = _attention_step(qkv, S["caches"], P["bias_step"], l, nb)
    attn = _pad_rows(attn.reshape(1, nb, GROUP_WIDTH), 1)
    merged, u = _branches(act, h, attn, (S["pool_t"], S["conv_t"]), P["branch"], l, tm=tm, tc=512, step=True)
    x1 = _oproj(merged, P["w_o"], x, norm, mod, l, tm=tm)
    x2, up_g, up_v = _ffn(x1, norm, mod, P["w_up"], P["ffn_cw"], P["w_down"], S["ffn_t"], l,
                          tm=tm, tf=512, step=True)
    new_kv = [_kv_pack(qkv[0, :nb, None], g, 1) for g in range(N_ATTN_GROUPS)]
    new_pool = jnp.concatenate([S["pool"][l][:, 1:], act[0, :nb, None, :POOL_WIDTH]], axis=1)
    new_conv = jnp.concatenate([S["conv"][l][:, 1:], u[0, 0, :nb, None]], axis=1)
    up_new = jnp.concatenate([up_g[0, 0, :nb], up_v[0, 0, :nb]], axis=-1)
    new_ffn = jnp.concatenate([S["ffn"][l][:, 1:], up_new[:, None]], axis=1)
    return x2, (new_kv[0], new_kv[1], new_kv[2], new_pool, new_conv, new_ffn)


def kernel(x_prompt, x_sample, c_prompt, c_sample, cache_kv_w128, cache_kv_w512, cache_kv_w2048, state_pool, state_conv, state_ffn_conv, rel_bias, norm_g, w_ada, b_ada, w_in, w_attn_br, w_pool_grp, pool_scale, w_pool_br, conv_w, w_conv_br, w_o, w_up, ffn_conv_w, w_down):
    nbp = x_prompt.shape[0]
    nbs, tdec, _ = x_sample.shape
    assert tdec == 1 and nbs <= SAMPLE_ROWS
    caches = (cache_kv_w128, cache_kv_w512, cache_kv_w2048)
    for (w, d), c in zip(ATTN_GROUPS, caches):
        assert c.shape[2] == w == N_DIL_KEYS * d, "cache must hold exactly one window"

    bias_tab, bias_step = _bias_tables(rel_bias)
    c_rows = -(-(SAMPLE_ROWS + nbp) // 8) * 8
    c_all = jnp.zeros((c_rows, D_MODEL), F32).at[:nbs].set(c_sample).at[SAMPLE_ROWS:SAMPLE_ROWS + nbp].set(c_prompt)
    mod_all = _ada(c_all, w_ada, b_ada)

    w_qkv, w_act, w_gate = _cast_split(
        w_in, [(0, QKV_WIDTH), (QKV_WIDTH, QKV_WIDTH + ACT_WIDTH), (QKV_WIDTH + ACT_WIDTH, w_in.shape[2])], tk=256)
    P = dict(
        norm=functools.partial(_norm_view, norm_g),
        mod_p=functools.partial(_mod_view, mod_all, prompt=True),
        mod_s=functools.partial(_mod_view, mod_all, prompt=False),
        w_qkv=w_qkv, w_act=w_act,
        branch=(w_pool_grp.astype(BF16), pool_scale[:, None, :], w_pool_br.astype(BF16),
                conv_w, w_conv_br.astype(BF16), w_attn_br.astype(BF16), w_gate),
        w_o=w_o.astype(BF16), w_up=w_up.astype(BF16), ffn_cw=ffn_conv_w, w_down=w_down.astype(BF16),
        bias_tab=bias_tab, bias_step=bias_step,
    )
    S = dict(
        caches=[c.transpose(0, 1, 3, 4, 5, 2).reshape(DEPTH, nbs, 2, GROUP_WIDTH, -1) for c in caches],
        pool=state_pool, conv=state_conv, ffn=state_ffn_conv,
        pool_t=_pad_rows(state_pool.transpose(0, 2, 1, 3), 2),
        conv_t=_pad_rows(state_conv.transpose(0, 2, 1, 3), 2),
        ffn_t=_pad_rows(state_ffn_conv.transpose(0, 2, 1, 3), 2),
    )
    yp = x_prompt
    ys = _pad_rows(x_sample.reshape(1, nbs, D_MODEL), 1)
    st_p, st_s = [], []
    for l in range(DEPTH):
        yp, sp = _prompt_layer(yp, l, P)
        ys, ss = _sample_layer(ys, l, P, S, nbs)
        st_p.append(sp)
        st_s.append(ss)
    outs_p = [jnp.stack([s[k] for s in st_p]) for k in range(6)]
    outs_s = [jnp.stack([s[k] for s in st_s]) for k in range(6)]
    return (yp, ys[0, :nbs, None, :], *outs_p, *outs_s)
```

```python
import functools

import numpy as np
import jax
import jax.numpy as jnp
from jax import lax
from jax.experimental import pallas as pl
from jax.experimental.pallas import tpu as pltpu

F32 = jnp.float32
BF16 = jnp.bfloat16

D_MODEL = 2048
DEPTH = 2
HEAD_DIM = 64
HEADS_PER_GROUP = 4
ATTN_GROUPS = ((128, 1), (512, 4), (2048, 16))
N_ATTN_GROUPS = len(ATTN_GROUPS)
ATTN_WIDTH = N_ATTN_GROUPS * HEADS_PER_GROUP * HEAD_DIM
GROUP_WIDTH = HEADS_PER_GROUP * HEAD_DIM
N_DIL_KEYS = 128
N_BUCKETS = 32
MAX_DISTANCE = 2048
ATTN_SCALE = HEAD_DIM ** -0.5
POOL_WINDOWS = (2, 4, 8, 16)
POOL_GROUP = 128
POOL_WIDTH = 512
POOL_STATE = 15
CONV_CH = 512
CONV_K = 3
D_FF = 5632
FFN_K = 3
N_MOD = 6
N_NORM = 4
EPS = 1e-6
NEG_INF = -1e30

LANES = 128
HIST = 16
SAMPLE_ROWS = 16
QKV_WIDTH = 3 * ATTN_WIDTH
QKV_SLABS = QKV_WIDTH // LANES
ACT_WIDTH = POOL_WIDTH + 3 * CONV_CH
ATTN_SUPER_BLOCK = 2048
ATTN_CLASS_UNROLL = 8
ROW_CHUNKS = 2
VMEM_CAP = 56 * 1024 * 1024


def _vmem_limit(block_bytes, scratch_bytes=0):
    del block_bytes, scratch_bytes
    return VMEM_CAP


def _nbytes(shape, dtype):
    return int(np.prod(shape)) * jnp.dtype(dtype).itemsize


def _rms(x, g):
    return x * lax.rsqrt(jnp.mean(x * x, axis=-1, keepdims=True) + EPS) * g


def _sigmoid(x):
    return 0.5 + 0.5 * jnp.tanh(0.5 * x)


def _gelu_tanh(x):
    return 0.5 * x * (1.0 + jnp.tanh(np.sqrt(2.0 / np.pi) * (x + 0.044715 * (x * x * x))))


def _dot(a, b):
    return jnp.dot(a, b, preferred_element_type=F32)


def _tail2(ref):
    return ref[(0,) * (len(ref.shape) - 2)]


def _norm_view(norm_g, l):
    arr = norm_g.reshape(DEPTH * N_NORM, 1, D_MODEL)
    return arr, lambda k: pl.BlockSpec((1, 1, D_MODEL), lambda *_: (l * N_NORM + k, 0, 0))


def _mod_view(mod_all, l, prompt):
    if prompt:
        arr = mod_all.reshape(DEPTH, mod_all.shape[1], N_MOD, 1, D_MODEL)
        return arr, lambda k: pl.BlockSpec((1, 1, 1, 1, D_MODEL), lambda bi, *_: (l, SAMPLE_ROWS + bi, k, 0, 0))
    return mod_all, lambda k: pl.BlockSpec((1, SAMPLE_ROWS, D_MODEL), lambda bi, *_: (l, 0, k))


def _mod_rows(mod):
    return mod[1](0).block_shape[-2]


def _cast_split_kernel(w_ref, *o_refs, bounds):
    for o_ref, (lo, hi) in zip(o_refs, bounds):
        o_ref[...] = w_ref[:, :, lo:hi].astype(BF16)


def _cast_split(w, bounds, *, tk):
    nl, k, n = w.shape
    blocks = _nbytes((tk, n), F32) + sum(_nbytes((tk, hi - lo), BF16) for lo, hi in bounds)
    return pl.pallas_call(
        functools.partial(_cast_split_kernel, bounds=tuple(bounds)),
        out_shape=[jax.ShapeDtypeStruct((nl, k, hi - lo), BF16) for lo, hi in bounds],
        grid=(nl, k // tk),
        in_specs=[pl.BlockSpec((1, tk, n), lambda li, ki: (li, ki, 0))],
        out_specs=[pl.BlockSpec((1, tk, hi - lo), lambda li, ki: (li, ki, 0)) for lo, hi in bounds],
        compiler_params=pltpu.CompilerParams(
            dimension_semantics=("parallel", "parallel"), vmem_limit_bytes=_vmem_limit(blocks)),
        name="cast_split",
    )(w)


def _ada_kernel(c_ref, w_ref, b_ref, o_ref):
    c = c_ref[...]
    s = (c * _sigmoid(c)).astype(BF16)
    o_ref[0] = _dot(s, w_ref[0].astype(BF16)) + b_ref[0]


def _ada(c_all, w_ada, b_ada):
    rows = c_all.shape[0]
    n = w_ada.shape[-1]
    tn = 2048
    blocks = _nbytes((rows, D_MODEL), F32) + _nbytes((D_MODEL, tn), F32) + _nbytes((rows + 1, tn), F32)
    return pl.pallas_call(
        _ada_kernel,
        out_shape=jax.ShapeDtypeStruct((DEPTH, rows, n), F32),
        grid=(DEPTH, n // tn),
        in_specs=[
            pl.BlockSpec((rows, D_MODEL), lambda l, j: (0, 0)),
            pl.BlockSpec((1, D_MODEL, tn), lambda l, j: (l, 0, j)),
            pl.BlockSpec((1, 1, tn), lambda l, j: (l, 0, j)),
        ],
        out_specs=pl.BlockSpec((1, rows, tn), lambda l, j: (l, 0, j)),
        compiler_params=pltpu.CompilerParams(
            dimension_semantics=("parallel", "parallel"),
            vmem_limit_bytes=_vmem_limit(blocks, _nbytes((D_MODEL, tn), BF16))),
        name="ada",
    )(c_all, w_ada, b_ada.reshape(DEPTH, 1, n))


def _modulate_rows(x_ref, g_ref, sc_ref, sh_ref, h_ref, row0, tm):
    y = _rms(x_ref[0], g_ref[0])
    h_ref[row0:row0 + tm] = (y * (1.0 + _tail2(sc_ref)) + _tail2(sh_ref)).astype(BF16)


def _proj_kernel(x_ref, g_ref, sc_ref, sh_ref, w_ref, o_ref, h_ref, *, slabs, tm, nchunk):
    def project(r):
        res = _dot(h_ref[0, r], w_ref[0])
        if slabs:
            for s in range(slabs):
                o_ref[0, s, r] = res[:, s * LANES:(s + 1) * LANES]
        else:
            o_ref[0, r] = res

    @pl.when(pl.program_id(2) == 0)
    def _():
        rc = tm // nchunk
        for c in range(nchunk):
            r = slice(c * rc, (c + 1) * rc)
            y = _rms(x_ref[0, r], g_ref[0])
            h_ref[0, r] = (y * (1.0 + _tail2(sc_ref)) + _tail2(sh_ref)).astype(BF16)
            project(r)

    @pl.when(pl.program_id(2) > 0)
    def _():
        project(slice(None))


def _proj(x, norm, mod, w, l, *, tm, tn, slab_out):
    b, s, d = x.shape
    n = w.shape[2]
    slabs = tn // LANES if slab_out else 0
    r = _mod_rows(mod)
    blocks = (_nbytes((tm, d), F32) + _nbytes((2 * r + 1, d), F32) + _nbytes((d, tn), BF16)
              + _nbytes((tm, tn), F32) + _nbytes((tm, d), BF16))
    if slab_out:
        out = jax.ShapeDtypeStruct((b, n // LANES, s, LANES), F32)
        out_spec = pl.BlockSpec((1, slabs, tm, LANES), lambda bi, i, j: (bi, j, i, 0))
    else:
        out = jax.ShapeDtypeStruct((b, s, n), F32)
        out_spec = pl.BlockSpec((1, tm, tn), lambda bi, i, j: (bi, i, j))
    return pl.pallas_call(
        functools.partial(_proj_kernel, slabs=slabs, tm=tm, nchunk=2 * ROW_CHUNKS if r == 1 else 1),
        out_shape=[out, jax.ShapeDtypeStruct((b, s, d), BF16)],
        grid=(b, s // tm, n // tn),
        in_specs=[
            pl.BlockSpec((1, tm, d), lambda bi, i, j: (bi, i, 0)),
            norm[1](0), mod[1](1), mod[1](0),
            pl.BlockSpec((1, d, tn), lambda bi, i, j: (l, 0, j)),
        ],
        out_specs=[out_spec, pl.BlockSpec((1, tm, d), lambda bi, i, j: (bi, i, 0))],
        compiler_params=pltpu.CompilerParams(
            dimension_semantics=("parallel", "parallel", "arbitrary"),
            vmem_limit_bytes=_vmem_limit(blocks, _nbytes((tm, tn), F32) + _nbytes((tm, d), F32))),
        name="proj_norm",
    )(x, norm[0], mod[0], mod[0], w)


def _matmul_kernel(h_ref, w_ref, o_ref):
    o_ref[0] = _dot(h_ref[0], w_ref[0])


def _matmul(h, w, l, *, tm, tn):
    b, s, d = h.shape
    n = w.shape[2]
    blocks = _nbytes((tm, d), BF16) + _nbytes((d, tn), BF16) + _nbytes((tm, tn), F32)
    return pl.pallas_call(
        _matmul_kernel,
        out_shape=jax.ShapeDtypeStruct((b, s, n), F32),
        grid=(b, s // tm, n // tn),
        in_specs=[pl.BlockSpec((1, tm, d), lambda bi, i, j: (bi, i, 0)),
                  pl.BlockSpec((1, d, tn), lambda bi, i, j: (l, 0, j))],
        out_specs=pl.BlockSpec((1, tm, tn), lambda bi, i, j: (bi, i, j)),
        compiler_params=pltpu.CompilerParams(
            dimension_semantics=("parallel", "parallel", "parallel"),
            vmem_limit_bytes=_vmem_limit(blocks, _nbytes((tm, tn), F32))),
        name="proj_act",
    )(h, w)


def _head_masks(rows):
    lane = lax.broadcasted_iota(jnp.int32, (rows, GROUP_WIDTH), 1)
    return [(lane >= h * HEAD_DIM) & (lane < (h + 1) * HEAD_DIM) for h in range(HEADS_PER_GROUP)]


def _attn_kernel(q_ref, kc_ref, kp_ref, vc_ref, vp_ref, bias_ref, o_ref, lse_ref, edge_ref, *, d, sb):
    i = pl.program_id(1)
    nq = N_DIL_KEYS
    span = nq * d
    ncb = sb // span
    hm = _head_masks(1)
    col = lax.broadcasted_iota(jnp.int32, (1, 2 * nq), 1)
    edge_ref[...] = jnp.where((col < nq) & (i == 0), NEG_INF, bias_ref[0])

    def rows(start):
        return pl.ds(start, nq, stride=d) if d > 1 else pl.ds(start, nq)

    def load(ref, start):
        return jnp.concatenate([ref[0, s, rows(start), :] for s in range(2)], axis=1)

    def one_block(r, jb):
        qs = jb * span + r
        q = load(q_ref, qs) * ATTN_SCALE
        if jb == 0:
            lo = sb - span + r
            k_lo, v_lo = load(kp_ref, lo), load(vp_ref, lo)
        else:
            lo = (jb - 1) * span + r
            k_lo, v_lo = load(kc_ref, lo), load(vc_ref, lo)
        kcat = jnp.concatenate([k_lo, load(kc_ref, qs)], axis=0).astype(BF16)
        vcat = jnp.concatenate([v_lo, load(vc_ref, qs)], axis=0).astype(BF16)
        qm = jnp.concatenate([jnp.where(hm[h], q, 0.0) for h in range(HEADS_PER_GROUP)], axis=0).astype(BF16)
        s = lax.dot_general(qm, kcat, (((1,), (1,)), ((), ())), preferred_element_type=F32)
        s = s + (edge_ref[...] if jb == 0 else bias_ref[0])
        m = jnp.max(s, axis=-1, keepdims=True)
        p = jnp.exp(s - m)
        l = jnp.sum(p, axis=-1, keepdims=True)
        oall = _dot((p * (1.0 / l)).astype(BF16), vcat)
        lse = m + jnp.log(l)
        o = jnp.zeros((nq, GROUP_WIDTH), F32)
        ls = jnp.zeros((nq, GROUP_WIDTH), F32)
        for h in range(HEADS_PER_GROUP):
            o = jnp.where(hm[h], oall[h * nq:(h + 1) * nq], o)
            ls = jnp.where(hm[h], lse[h * nq:(h + 1) * nq], ls)
        for sl in range(2):
            o_ref[0, sl, rows(qs), :] = o[:, sl * LANES:(sl + 1) * LANES]
            lse_ref[0, sl, rows(qs), :] = ls[:, sl * LANES:(sl + 1) * LANES]

    if d == 1:
        for jb in range(ncb):
            one_block(0, jb)
    else:
        def body(r, carry):
            for jb in range(ncb):
                one_block(r, jb)
            return carry
        lax.fori_loop(0, d, body, 0, unroll=min(d, ATTN_CLASS_UNROLL))


def _attention(qkv, bias_tab, g):
    b, _, s, _ = qkv.shape
    d = ATTN_GROUPS[g][1]
    sb = max(N_DIL_KEYS * d, ATTN_SUPER_BLOCK)
    blk = (1, 2, sb, LANES)
    kslab, vslab = ATTN_WIDTH // GROUP_WIDTH + g, 2 * ATTN_WIDTH // GROUP_WIDTH + g
    prev = lambda i: jnp.maximum(i - 1, 0)
    blocks = 7 * _nbytes(blk, F32) + _nbytes((4 * N_DIL_KEYS, 2 * N_DIL_KEYS), F32)
    out = jax.ShapeDtypeStruct((b, 2, s, LANES), F32)
    return pl.pallas_call(
        functools.partial(_attn_kernel, d=d, sb=sb),
        out_shape=(out, out),
        grid=(b, s // sb),
        in_specs=[
            pl.BlockSpec(blk, lambda bi, i: (bi, g, i, 0)),
            pl.BlockSpec(blk, lambda bi, i: (bi, kslab, i, 0)),
            pl.BlockSpec(blk, lambda bi, i: (bi, kslab, prev(i), 0)),
            pl.BlockSpec(blk, lambda bi, i: (bi, vslab, i, 0)),
            pl.BlockSpec(blk, lambda bi, i: (bi, vslab, prev(i), 0)),
            pl.BlockSpec((1, 4 * N_DIL_KEYS, 2 * N_DIL_KEYS), lambda bi, i: (g, 0, 0)),
        ],
        out_specs=(pl.BlockSpec(blk, lambda bi, i: (bi, 0, i, 0)),
                   pl.BlockSpec(blk, lambda bi, i: (bi, 0, i, 0))),
        scratch_shapes=[pltpu.VMEM((HEADS_PER_GROUP * N_DIL_KEYS, 2 * N_DIL_KEYS), F32)],
        compiler_params=pltpu.CompilerParams(
            dimension_semantics=("parallel", "parallel"),
            vmem_limit_bytes=_vmem_limit(blocks, 8 << 20)),
        name=f"attn_d{d}",
    )(qkv, qkv, qkv, qkv, qkv, bias_tab)


def _attn_step_kernel(qkv_ref, c0_ref, c1_ref, c2_ref, b0_ref, b1_ref, b2_ref, o_ref):
    b = pl.program_id(0)
    rows = SAMPLE_ROWS
    lane = lax.broadcasted_iota(jnp.int32, (rows, GROUP_WIDTH), 1)
    row = lax.broadcasted_iota(jnp.int32, (rows, GROUP_WIDTH), 0)
    sel = (lane >= row * HEAD_DIM) & (lane < (row + 1) * HEAD_DIM)

    def rowvec(col0):
        return qkv_ref[0, pl.ds(b, 1), col0:col0 + GROUP_WIDTH]

    outs, lses = [], []
    for g, (c_ref, b_ref) in enumerate(((c0_ref, b0_ref), (c1_ref, b1_ref), (c2_ref, b2_ref))):
        q = rowvec(g * GROUP_WIDTH) * ATTN_SCALE
        kn = rowvec(ATTN_WIDTH + g * GROUP_WIDTH).astype(BF16).astype(F32)
        vn = rowvec(2 * ATTN_WIDTH + g * GROUP_WIDTH).astype(BF16).astype(F32)
        q4 = jnp.where(sel, jnp.broadcast_to(q, (rows, GROUP_WIDTH)), 0.0).astype(BF16)
        s_c = _dot(q4, c_ref[0, 0, 0].astype(BF16)) + b_ref[0:rows, :]
        s_n = jnp.sum(q4.astype(F32) * kn, axis=-1, keepdims=True) + b_ref[rows:2 * rows, 0:1]
        m = jnp.maximum(jnp.max(s_c, axis=-1, keepdims=True), s_n)
        p_c = jnp.exp(s_c - m)
        p_n = jnp.exp(s_n - m)
        l = jnp.sum(p_c, axis=-1, keepdims=True) + p_n
        inv = 1.0 / l
        oc = lax.dot_general((p_c * inv).astype(BF16), c_ref[0, 0, 1].astype(BF16), (((1,), (1,)), ((), ())),
                             preferred_element_type=F32)
        oc = oc + (p_n * inv).astype(BF16).astype(F32) * vn
        lse = m + jnp.log(l)
        outs.append(jnp.sum(jnp.where(sel, oc, 0.0), axis=0, keepdims=True))
        lses.append(jnp.sum(jnp.where(sel, lse, 0.0), axis=0, keepdims=True))
    mx = jnp.maximum(jnp.maximum(lses[0], lses[1]), lses[2])
    w = [jnp.exp(ls - mx) for ls in lses]
    tot = w[0] + w[1] + w[2]
    o_ref[0] = (w[0] * outs[0] + w[1] * outs[1] + w[2] * outs[2]) * (1.0 / tot)


def _attention_step(qkv, caches, biases, l, nb):
    cache_specs = [pl.BlockSpec((1, 1) + c.shape[2:], lambda bi: (l, bi, 0, 0, 0)) for c in caches]
    bias_specs = [pl.BlockSpec(b.shape, lambda bi: (0, 0)) for b in biases]
    blocks = (sum(_nbytes(c.shape[2:], F32) for c in caches) + sum(_nbytes(b.shape, F32) for b in biases)
              + _nbytes(qkv.shape, F32))
    return pl.pallas_call(
        _attn_step_kernel,
        out_shape=jax.ShapeDtypeStruct((nb, 1, GROUP_WIDTH), F32),
        grid=(nb,),
        in_specs=[pl.BlockSpec(qkv.shape, lambda bi: (0, 0, 0))] + cache_specs + bias_specs,
        out_specs=pl.BlockSpec((1, 1, GROUP_WIDTH), lambda bi: (bi, 0, 0)),
        compiler_params=pltpu.CompilerParams(
            dimension_semantics=("parallel",),
            vmem_limit_bytes=_vmem_limit(blocks, 8 << 20)),
        name="attn_step",
    )(qkv, *caches, *biases)


def _branch_kernel(*refs, tm, step):
    if step:
        (act_ref, pst_ref, cst_ref, h_ref, wga_ref, wgb_ref, wgc_ref, at_ref,
         wgrp_ref, pscale_ref, wpb_ref, cw_ref, wcb_ref, wab_ref,
         out_ref, u_ref, pzs_ref, cb_ref, comb_ref) = refs
    else:
        (act_ref, hist_ref, h_ref, wga_ref, wgb_ref, wgc_ref, o0_ref, o1_ref, o2_ref, l0_ref, l1_ref, l2_ref,
         wgrp_ref, pscale_ref, wpb_ref, cw_ref, wcb_ref, wab_ref,
         out_ref, u_ref, pzs_ref, cb_ref, comb_ref, pe_ref, ue_ref) = refs
    c_tile = pl.program_id(1)
    i = pl.program_id(2)
    rows = pl.ds(pl.multiple_of(i * tm, tm), tm)
    pzs_ref, cb_ref, comb_ref = pzs_ref.at[rows], cb_ref.at[rows], comb_ref.at[rows]

    def sequence_mixers():
        p = act_ref[0, :, 0:POOL_WIDTH]
        gate_b = act_ref[0, :, POOL_WIDTH:POOL_WIDTH + CONV_CH]
        u = act_ref[0, :, POOL_WIDTH + CONV_CH:POOL_WIDTH + 2 * CONV_CH] * \
            act_ref[0, :, POOL_WIDTH + 2 * CONV_CH:POOL_WIDTH + 3 * CONV_CH]
        cw = cw_ref[0]
        if step:
            acc = p
            sums = {}
            for k in range(1, max(POOL_WINDOWS)):
                acc = acc + pst_ref[0, POOL_STATE - k]
                sums[k + 1] = acc
            means = [sums[w][:, gi * POOL_GROUP:(gi + 1) * POOL_GROUP] * (1.0 / w)
                     for gi, w in enumerate(POOL_WINDOWS)]
            conv = cst_ref[0, 0] * cw[0:1] + cst_ref[0, 1] * cw[1:2] + u * cw[2:3]
            u_ref[0, 0] = u
            comb_ref[...] = at_ref[0].astype(BF16)
        else:
            first = i == 0
            hist_p = hist_ref[0, :, 0:POOL_WIDTH]
            hist_u = hist_ref[0, :, POOL_WIDTH + CONV_CH:POOL_WIDTH + 2 * CONV_CH] * \
                hist_ref[0, :, POOL_WIDTH + 2 * CONV_CH:POOL_WIDTH + 3 * CONV_CH]
            pe_ref[0:HIST] = jnp.where(first, 0.0, hist_p)
            ue_ref[0:HIST] = jnp.where(first, 0.0, hist_u)
            pe_ref[HIST:HIST + tm] = p
            ue_ref[HIST:HIST + tm] = u
            t = i * tm + lax.broadcasted_iota(jnp.int32, (tm, 1), 0)
            means = []
            for gi, w in enumerate(POOL_WINDOWS):
                cs = slice(gi * POOL_GROUP, (gi + 1) * POOL_GROUP)
                acc = pe_ref[HIST:HIST + tm, cs]
                for k in range(1, w):
                    acc = acc + pe_ref[HIST - k:HIST - k + tm, cs]
                cnt = jnp.minimum(t + 1, w).astype(F32)
                means.append(acc * (1.0 / cnt))
            conv = (ue_ref[HIST - 2:HIST - 2 + tm] * cw[0:1] + ue_ref[HIST - 1:HIST - 1 + tm] * cw[1:2]
                    + u * cw[2:3])
            u_ref[0, 0] = ue_ref[HIST + tm - 8:HIST + tm]
            for sl in range(2):
                ls = [r[0, sl] for r in (l0_ref, l1_ref, l2_ref)]
                os_ = [r[0, sl] for r in (o0_ref, o1_ref, o2_ref)]
                mx = jnp.maximum(jnp.maximum(ls[0], ls[1]), ls[2])
                w_ = [jnp.exp(x - mx) for x in ls]
                tot = w_[0] + w_[1] + w_[2]
                comb = (w_[0] * os_[0] + w_[1] * os_[1] + w_[2] * os_[2]) * (1.0 / tot)
                comb_ref[:, sl * LANES:(sl + 1) * LANES] = comb.astype(BF16)
        pscale = pscale_ref[0]
        for gi in range(len(POOL_WINDOWS)):
            cs = slice(gi * POOL_GROUP, (gi + 1) * POOL_GROUP)
            pm = (means[gi] - p[:, cs]).astype(BF16)
            pz = _dot(pm, wgrp_ref[0, gi])
            pzs_ref[:, cs] = (pz * pscale[:, cs]).astype(BF16)
        cb_ref[...] = (gate_b * conv).astype(BF16)

    def merge_tile():
        h = h_ref[0]
        merged = _sigmoid(_dot(h, wga_ref[0])) * _dot(pzs_ref[...], wpb_ref[0])
        merged += _sigmoid(_dot(h, wgb_ref[0])) * _dot(cb_ref[...], wcb_ref[0])
        merged += _sigmoid(_dot(h, wgc_ref[0])) * _dot(comb_ref[...], wab_ref[0])
        out_ref[0] = merged.astype(BF16)

    @pl.when(c_tile == 0)
    def _():
        sequence_mixers()
        merge_tile()

    @pl.when(c_tile > 0)
    def _():
        merge_tile()


def _branches(act, h, attn, states, weights, l, *, tm, tc, step):
    b, s, _ = act.shape
    wgrp, pscale, wpb, cw, wcb, wab, wgate = weights
    ni = s // tm
    once = lambda c, i: jnp.where(c == 0, i, ni - 1)
    gate_specs = [pl.BlockSpec((1, tm, D_MODEL), lambda bi, c, i: (bi, i, 0))] + [
        pl.BlockSpec((1, D_MODEL, tc), lambda bi, c, i, k=k: (l, 0, k * (D_MODEL // tc) + c)) for k in range(3)]
    w_specs = [
        pl.BlockSpec((1,) + wgrp.shape[1:], lambda bi, c, i: (l, 0, 0, 0)),
        pl.BlockSpec((1,) + pscale.shape[1:], lambda bi, c, i: (l, 0, 0)),
        pl.BlockSpec((1, POOL_WIDTH, tc), lambda bi, c, i: (l, 0, c)),
        pl.BlockSpec((1,) + cw.shape[1:], lambda bi, c, i: (l, 0, 0)),
        pl.BlockSpec((1, CONV_CH, tc), lambda bi, c, i: (l, 0, c)),
        pl.BlockSpec((1, GROUP_WIDTH, tc), lambda bi, c, i: (l, 0, c)),
    ]
    act_spec = pl.BlockSpec((1, tm, ACT_WIDTH), lambda bi, c, i: (bi, once(c, i), 0))
    scratch = [pltpu.VMEM((s, POOL_WIDTH), BF16), pltpu.VMEM((s, CONV_CH), BF16),
               pltpu.VMEM((s, GROUP_WIDTH), BF16)]
    blocks = (_nbytes((tm, ACT_WIDTH), F32) + _nbytes((tm + 3 * tc, D_MODEL), BF16) + _nbytes(wgrp.shape[1:], BF16)
              + _nbytes((POOL_WIDTH + CONV_CH + GROUP_WIDTH, tc), BF16) + _nbytes((tm, tc), BF16))
    if step:
        pst, cst = states
        ins = [act, pst, cst, h, wgate, wgate, wgate, attn]
        in_specs = [act_spec,
                    pl.BlockSpec((1,) + pst.shape[1:], lambda bi, c, i: (l, 0, 0, 0)),
                    pl.BlockSpec((1,) + cst.shape[1:], lambda bi, c, i: (l, 0, 0, 0))] + gate_specs + [
                    pl.BlockSpec((1, tm, GROUP_WIDTH), lambda bi, c, i: (bi, once(c, i), 0))]
        urows = tm
        blocks += _nbytes(pst.shape[1:], F32) + _nbytes(cst.shape[1:], F32)
    else:
        hblk = tm // HIST
        slab = pl.BlockSpec((1, 2, tm, LANES), lambda bi, c, i: (bi, 0, once(c, i), 0))
        ins = [act, act, h, wgate, wgate, wgate] + list(attn)
        in_specs = [act_spec,
                    pl.BlockSpec((1, HIST, ACT_WIDTH),
                                 lambda bi, c, i: (bi, jnp.maximum(once(c, i) * hblk - 1, 0), 0))
                    ] + gate_specs + [slab] * 6
        scratch += [pltpu.VMEM((tm + HIST, POOL_WIDTH), F32), pltpu.VMEM((tm + HIST, CONV_CH), F32)]
        urows = 8
        blocks += 6 * _nbytes((2, tm, LANES), F32) + _nbytes((HIST, ACT_WIDTH), F32)
    scratch_bytes = (_nbytes((s, POOL_WIDTH + CONV_CH + GROUP_WIDTH), BF16) + 2 * _nbytes((tm + HIST, POOL_WIDTH), F32)
                     + 8 * _nbytes((tm, tc), F32))
    return pl.pallas_call(
        functools.partial(_branch_kernel, tm=tm, step=step),
        out_shape=(jax.ShapeDtypeStruct((b, s, D_MODEL), BF16),
                   jax.ShapeDtypeStruct((b, ni, urows, CONV_CH), F32)),
        grid=(b, D_MODEL // tc, ni),
        in_specs=in_specs + w_specs,
        out_specs=(pl.BlockSpec((1, tm, tc), lambda bi, c, i: (bi, i, c)),
                   pl.BlockSpec((1, 1, urows, CONV_CH), lambda bi, c, i: (bi, once(c, i), 0, 0))),
        scratch_shapes=scratch,
        compiler_params=pltpu.CompilerParams(
            dimension_semantics=("parallel", "arbitrary", "arbitrary"),
            vmem_limit_bytes=_vmem_limit(blocks, scratch_bytes)),
        name="branches_step" if step else "branches",
    )(*ins, wgrp, pscale, wpb, cw, wcb, wab)


def _oproj_kernel(m_ref, w_ref, x_ref, g_ref, gate_ref, o_ref):
    mix = _dot(m_ref[0], w_ref[0])
    o_ref[0] = x_ref[0] + _tail2(gate_ref) * _rms(mix, g_ref[0])


def _oproj(merged, w_o, x, norm, mod, l, *, tm):
    b, s, d = x.shape
    blocks = (_nbytes((tm, d), BF16) + _nbytes((d, d), BF16) + 2 * _nbytes((tm, d), F32)
              + _nbytes((_mod_rows(mod) + 1, d), F32))
    return pl.pallas_call(
        _oproj_kernel,
        out_shape=jax.ShapeDtypeStruct((b, s, d), F32),
        grid=(b, s // tm),
        in_specs=[
            pl.BlockSpec((1, tm, d), lambda bi, i: (bi, i, 0)),
            pl.BlockSpec((1, d, d), lambda bi, i: (l, 0, 0)),
            pl.BlockSpec((1, tm, d), lambda bi, i: (bi, i, 0)),
            norm[1](1), mod[1](2),
        ],
        out_specs=pl.BlockSpec((1, tm, d), lambda bi, i: (bi, i, 0)),
        compiler_params=pltpu.CompilerParams(
            dimension_semantics=("parallel", "parallel"),
            vmem_limit_bytes=_vmem_limit(blocks, _nbytes((tm, d), F32))),
        name="oproj",
    )(merged, w_o, x, norm[0], mod[0])


def _ffn_kernel(*refs, tm, step, nchunk):
    if step:
        (x_ref, g2_ref, sc_ref, sh_ref, wg_ref, wv_ref, cwg_ref, cwv_ref, wd_ref, g3_ref, gate_ref,
         stg_ref, stv_ref, o_ref, tg_ref, tv_ref, h_ref) = refs
        hist = 0
    else:
        (x_ref, xp_ref, g2_ref, sc_ref, sh_ref, wg_ref, wv_ref, cwg_ref, cwv_ref, wd_ref, g3_ref, gate_ref,
         o_ref, tg_ref, tv_ref, h_ref, ug_ref, uv_ref) = refs
        hist = HIST
    i = pl.program_id(1)
    j = pl.program_id(2)
    last = pl.num_programs(2) - 1
    tf = wd_ref.shape[1]
    cols = pl.ds(pl.multiple_of(j * tf, tf), tf)
    cwg = cwg_ref[0, :, cols]
    cwv = cwv_ref[0, :, cols]

    if step:
        @pl.when(j == 0)
        def _():
            _modulate_rows(x_ref, g2_ref, sc_ref, sh_ref, h_ref, 0, tm)
            o_ref[...] = jnp.zeros_like(o_ref)

        h = h_ref[...]
        up_g = _dot(h, wg_ref[0])
        up_v = _dot(h, wv_ref[0])
        uc_g = stg_ref[0, 0] * cwg[0:1] + stg_ref[0, 1] * cwg[1:2] + up_g * cwg[2:3]
        uc_v = stv_ref[0, 0] * cwv[0:1] + stv_ref[0, 1] * cwv[1:2] + up_v * cwv[2:3]
        tg_ref[0, 0, :, cols] = up_g
        tv_ref[0, 0, :, cols] = up_v
        o_ref[0] += _dot((_gelu_tanh(uc_g) * uc_v).astype(BF16), wd_ref[0])

        @pl.when(j == last)
        def _():
            o_ref[0] = x_ref[0] + _tail2(gate_ref) * _rms(o_ref[0], g3_ref[0])
        return

    def seq_step(nck, is_first, is_last):
        rc = tm // nck
        if is_first:
            _modulate_rows(xp_ref, g2_ref, sc_ref, sh_ref, h_ref, 0, HIST)
        for c in range(nck):
            r = slice(c * rc, (c + 1) * rc)
            if is_first:
                y = _rms(x_ref[0, r], g2_ref[0])
                h_ref[HIST + c * rc:HIST + (c + 1) * rc] = (y * (1.0 + _tail2(sc_ref)) + _tail2(sh_ref)).astype(BF16)
            lo = 0 if c == 0 else HIST + c * rc
            hi = HIST + (c + 1) * rc
            h = h_ref[lo:hi]
            for u_ref, w_ref in ((ug_ref, wg_ref), (uv_ref, wv_ref)):
                up = _dot(h, w_ref[0])
                if c == 0:
                    u_ref[0:HIST] = jnp.where(i == 0, 0.0, up[0:HIST])
                    u_ref[HIST:hi] = up[HIST:]
                else:
                    u_ref[lo:hi] = up

            def conv(ref, cw):
                base = HIST + c * rc
                return (ref[base - 2:base - 2 + rc] * cw[0:1] + ref[base - 1:base - 1 + rc] * cw[1:2]
                        + ref[base:base + rc] * cw[2:3])
            act = (_gelu_tanh(conv(ug_ref, cwg)) * conv(uv_ref, cwv)).astype(BF16)
            acc = _dot(act, wd_ref[0])
            if not is_first:
                acc = o_ref[0, r] + acc
            if is_last:
                acc = x_ref[0, r] + _tail2(gate_ref) * _rms(acc, g3_ref[0])
            o_ref[0, r] = acc
        tg_ref[0, 0, :, cols] = ug_ref[HIST + tm - 8:HIST + tm]
        tv_ref[0, 0, :, cols] = uv_ref[HIST + tm - 8:HIST + tm]

    pl.when(j == 0)(lambda: seq_step(2 * nchunk, True, False))
    pl.when((j > 0) & (j < last))(lambda: seq_step(nchunk, False, False))
    pl.when(j == last)(lambda: seq_step(2 * nchunk, False, True))


def _ffn(x, norm, mod, w_up, cw, w_down, state, l, *, tm, tf, step):
    b, s, d = x.shape
    ni, nj = s // tm, D_FF // tf
    trows = tm if step else 8
    w_specs = [
        pl.BlockSpec((1, d, tf), lambda bi, i, j: (l, 0, j)),
        pl.BlockSpec((1, d, tf), lambda bi, i, j: (l, 0, nj + j)),
        pl.BlockSpec((1, FFN_K, D_FF), lambda bi, i, j: (l, 0, 0)),
        pl.BlockSpec((1, FFN_K, D_FF), lambda bi, i, j: (l, 0, 1)),
        pl.BlockSpec((1, tf, d), lambda bi, i, j: (l, j, 0)),
    ]
    x_spec = pl.BlockSpec((1, tm, d), lambda bi, i, j: (bi, i, 0), pipeline_mode=pl.Buffered(1))
    hist = 0 if step else HIST
    scratch = [pltpu.VMEM((tm + hist, d), BF16)]
    blocks = (2 * _nbytes((tm, d), F32) + 3 * _nbytes((d, tf), BF16) + 2 * _nbytes((trows, tf), F32)
              + _nbytes((3 * _mod_rows(mod) + 2, d), F32))
    mods = [norm[1](2), mod[1](4), mod[1](3)]
    tailp = [norm[1](3), mod[1](5)]
    if step:
        ins = [x, norm[0], mod[0], mod[0], w_up, w_up, cw, cw, w_down, norm[0], mod[0], state, state]
        in_specs = [x_spec] + mods + w_specs + tailp + [
            pl.BlockSpec((1, FFN_K - 1, tm, tf), lambda bi, i, j: (l, 0, 0, j)),
            pl.BlockSpec((1, FFN_K - 1, tm, tf), lambda bi, i, j: (l, 0, 0, nj + j))]
        blocks += 2 * _nbytes((FFN_K - 1, tm, tf), F32)
    else:
        hblk = tm // HIST
        ins = [x, x, norm[0], mod[0], mod[0], w_up, w_up, cw, cw, w_down, norm[0], mod[0]]
        in_specs = [x_spec,
                    pl.BlockSpec((1, HIST, d), lambda bi, i, j: (bi, jnp.maximum(i * hblk - 1, 0), 0))
                    ] + mods + w_specs + tailp
        scratch += [pltpu.VMEM((tm + HIST, tf), F32), pltpu.VMEM((tm + HIST, tf), F32)]
        blocks += _nbytes((HIST, d), F32)
    scratch_bytes = _nbytes((tm + hist, d), BF16) + 6 * _nbytes((tm + hist, tf), F32)
    tail = jax.ShapeDtypeStruct((b, ni, trows, D_FF), F32)
    tail_spec = lambda: pl.BlockSpec((1, 1, trows, D_FF), lambda bi, i, j: (bi, i, 0, 0))
    return pl.pallas_call(
        functools.partial(_ffn_kernel, tm=tm, step=step, nchunk=ROW_CHUNKS),
        out_shape=(jax.ShapeDtypeStruct((b, s, d), F32), tail, tail),
        grid=(b, ni, nj),
        in_specs=in_specs,
        out_specs=(pl.BlockSpec((1, tm, d), lambda bi, i, j: (bi, i, 0)), tail_spec(), tail_spec()),
        scratch_shapes=scratch,
        compiler_params=pltpu.CompilerParams(
            dimension_semantics=("parallel", "parallel", "arbitrary"),
            vmem_limit_bytes=_vmem_limit(blocks, scratch_bytes)),
        name="ffn_step" if step else "ffn",
    )(*ins)


def _group_rel_bias(rel_bias):
    n = np.arange(N_DIL_KEYS + 1)
    max_exact = N_BUCKETS // 2
    onehot = np.zeros((N_ATTN_GROUPS, N_DIL_KEYS + 1, N_BUCKETS), np.float32)
    for g, (_, d) in enumerate(ATTN_GROUPS):
        dist = n * d
        large = max_exact + (np.log(np.maximum(dist, 1) / max_exact) / np.log(MAX_DISTANCE / max_exact)
                             * (N_BUCKETS - max_exact)).astype(np.int32)
        bucket = np.where(dist < max_exact, dist, np.minimum(large, N_BUCKETS - 1))
        onehot[g, n, bucket] = 1.0
    per_group = rel_bias.reshape(N_BUCKETS, N_ATTN_GROUPS, HEADS_PER_GROUP)
    return jnp.einsum("gnb,bgh->ghn", onehot, per_group, precision=lax.Precision.HIGHEST)


def _bias_tables(rel_bias):
    bias_g = _group_rel_bias(rel_bias).astype(F32)
    n = N_DIL_KEYS
    gh = (N_ATTN_GROUPS, HEADS_PER_GROUP)
    ext = jnp.concatenate([bias_g[:, :, ::-1], jnp.full(gh + (n,), NEG_INF, F32)], axis=-1)
    tab = jnp.tile(ext, (1, 1, n))[:, :, :n * 2 * n].reshape(N_ATTN_GROUPS, HEADS_PER_GROUP * n, 2 * n)
    step = []
    for g, (_, d) in enumerate(ATTN_GROUPS):
        hit = bias_g[g, :, :0:-1, None]
        row = jnp.concatenate([hit, jnp.full((HEADS_PER_GROUP, n, d - 1), NEG_INF, F32)], axis=-1)
        new = jnp.broadcast_to(bias_g[g, :, 0:1], (HEADS_PER_GROUP, n * d))
        pad = jnp.zeros((SAMPLE_ROWS - HEADS_PER_GROUP, n * d), F32)
        step.append(jnp.concatenate([row.reshape(HEADS_PER_GROUP, n * d), pad, new, pad], axis=0))
    return tab, step


def _kv_pack(qkv, g, nrows):
    b = qkv.shape[0]
    k0 = ATTN_WIDTH + g * GROUP_WIDTH
    v0 = 2 * ATTN_WIDTH + g * GROUP_WIDTH
    k = qkv[:, -nrows:, k0:k0 + GROUP_WIDTH].reshape(b, nrows, HEADS_PER_GROUP, HEAD_DIM)
    v = qkv[:, -nrows:, v0:v0 + GROUP_WIDTH].reshape(b, nrows, HEADS_PER_GROUP, HEAD_DIM)
    return jnp.stack([k, v], axis=2)


def _kv_tail_kernel(*refs):
    n = len(refs) // 3
    for g in range(n):
        o_ref = refs[2 * n + g]
        for part in range(2):
            for s in range(2):
                o_ref[0, part, s * LANES:(s + 1) * LANES, :] = refs[2 * g + part][0, s].T


def _kv_tails(qkv):
    b, _, s, _ = qkv.shape
    ins, in_specs, out_shape, out_specs, blocks = [], [], [], [], 0
    for g, (w, _) in enumerate(ATTN_GROUPS):
        nrows = min(w, s)
        assert s % nrows == 0
        for part in (1, 2):
            col = part * (ATTN_WIDTH // GROUP_WIDTH) + g
            ins.append(qkv)
            in_specs.append(pl.BlockSpec((1, 2, nrows, LANES), lambda bi, col=col, rb=s // nrows - 1: (bi, col, rb, 0)))
        out_shape.append(jax.ShapeDtypeStruct((b, 2, GROUP_WIDTH, nrows), F32))
        out_specs.append(pl.BlockSpec((1, 2, GROUP_WIDTH, nrows), lambda bi: (bi, 0, 0, 0)))
        blocks += 4 * _nbytes((nrows, GROUP_WIDTH), F32)
    outs = pl.pallas_call(
        _kv_tail_kernel,
        out_shape=out_shape,
        grid=(b,),
        in_specs=in_specs,
        out_specs=out_specs,
        compiler_params=pltpu.CompilerParams(
            dimension_semantics=("parallel",), vmem_limit_bytes=_vmem_limit(blocks, blocks // 2)),
        name="kv_tails",
    )(*ins)
    return [o.reshape(b, 2, HEADS_PER_GROUP, HEAD_DIM, o.shape[-1]).transpose(0, 4, 1, 2, 3) for o in outs]


def _prompt_layer(x, l, P):
    b, s, _ = x.shape
    norm, mod = P["norm"](l), P["mod_p"](l)
    tm = 512
    qkv_slab, h = _proj(x, norm, mod, P["w_qkv"], l, tm=1024, tn=ATTN_WIDTH, slab_out=True)
    act = _matmul(h, P["w_act"], l, tm=1024, tn=ACT_WIDTH)
    ols = [_attention(qkv_slab, P["bias_tab"], g) for g in range(N_ATTN_GROUPS)]
    attn = [o for o, _ in ols] + [ls for _, ls in ols]
    merged, u_tail = _branches(act, h, attn, None, P["branch"], l, tm=tm, tc=512, step=False)
    x1 = _oproj(merged, P["w_o"], x, norm, mod, l, tm=tm)
    x2, tail_g, tail_v = _ffn(x1, norm, mod, P["w_up"], P["ffn_cw"], P["w_down"], None, l,
                              tm=1024, tf=512, step=False)
    new_kv = _kv_tails(qkv_slab)
    new_pool = act[:, -POOL_STATE:, :POOL_WIDTH]
    new_conv = u_tail[:, -1, -(CONV_K - 1):]
    new_ffn = jnp.concatenate([tail_g[:, -1, -(FFN_K - 1):], tail_v[:, -1, -(FFN_K - 1):]], axis=-1)
    return x2, (new_kv[0], new_kv[1], new_kv[2], new_pool, new_conv, new_ffn)


def _pad_rows(a, axis):
    pad = [(0, 0)] * a.ndim
    pad[axis] = (0, SAMPLE_ROWS - a.shape[axis])
    return jnp.pad(a, pad)


def _sample_layer(x, l, P, S, nb):
    norm, mod = P["norm"](l), P["mod_s"](l)
    tm = SAMPLE_ROWS
    qkv, h = _proj(x, norm, mod, P["w_qkv"], l, tm=tm, tn=ATTN_WIDTH, slab_out=False)
    act = _matmul(h, P["w_act"], l, tm=tm, tn=1024)
    attn = _attention_step(qkv, S["caches"], P["bias_step"], l, nb)
    attn = _pad_rows(attn.reshape(1, nb, GROUP_WIDTH), 1)
    merged, u = _branches(act, h, attn, (S["pool_t"], S["conv_t"]), P["branch"], l, tm=tm, tc=512, step=True)
    x1 = _oproj(merged, P["w_o"], x, norm, mod, l, tm=tm)
    x2, up_g, up_v = _ffn(x1, norm, mod, P["w_up"], P["ffn_cw"], P["w_down"], S["ffn_t"], l,
                          tm=tm, tf=512, step=True)
    new_kv = [_kv_pack(qkv[0, :nb, None], g, 1) for g in range(N_ATTN_GROUPS)]
    new_pool = jnp.concatenate([S["pool"][l][:, 1:], act[0, :nb, None, :POOL_WIDTH]], axis=1)
    new_conv = jnp.concatenate([S["conv"][l][:, 1:], u[0, 0, :nb, None]], axis=1)
    up_new = jnp.concatenate([up_g[0, 0, :nb], up_v[0, 0, :nb]], axis=-1)
    new_ffn = jnp.concatenate([S["ffn"][l][:, 1:], up_new[:, None]], axis=1)
    return x2, (new_kv[0], new_kv[1], new_kv[2], new_pool, new_conv, new_ffn)


def kernel(x_prompt, x_sample, c_prompt, c_sample, cache_kv_w128, cache_kv_w512, cache_kv_w2048, state_pool, state_conv, state_ffn_conv, rel_bias, norm_g, w_ada, b_ada, w_in, w_attn_br, w_pool_grp, pool_scale, w_pool_br, conv_w, w_conv_br, w_o, w_up, ffn_conv_w, w_down):
    nbp = x_prompt.shape[0]
    nbs, tdec, _ = x_sample.shape
    assert tdec == 1 and nbs <= SAMPLE_ROWS
    caches = (cache_kv_w128, cache_kv_w512, cache_kv_w2048)
    for (w, d), c in zip(ATTN_GROUPS, caches):
        assert c.shape[2] == w == N_DIL_KEYS * d, "cache must hold exactly one window"

    bias_tab, bias_step = _bias_tables(rel_bias)
    c_rows = -(-(SAMPLE_ROWS + nbp) // 8) * 8
    c_all = jnp.zeros((c_rows, D_MODEL), F32).at[:nbs].set(c_sample).at[SAMPLE_ROWS:SAMPLE_ROWS + nbp].set(c_prompt)
    mod_all = _ada(c_all, w_ada, b_ada)

    w_qkv, w_act, w_gate = _cast_split(
        w_in, [(0, QKV_WIDTH), (QKV_WIDTH, QKV_WIDTH + ACT_WIDTH), (QKV_WIDTH + ACT_WIDTH, w_in.shape[2])], tk=256)
    P = dict(
        norm=functools.partial(_norm_view, norm_g),
        mod_p=functools.partial(_mod_view, mod_all, prompt=True),
        mod_s=functools.partial(_mod_view, mod_all, prompt=False),
        w_qkv=w_qkv, w_act=w_act,
        branch=(w_pool_grp.astype(BF16), pool_scale[:, None, :], w_pool_br.astype(BF16),
                conv_w, w_conv_br.astype(BF16), w_attn_br.astype(BF16), w_gate),
        w_o=w_o.astype(BF16), w_up=w_up.astype(BF16), ffn_cw=ffn_conv_w, w_down=w_down.astype(BF16),
        bias_tab=bias_tab, bias_step=bias_step,
    )
    S = dict(
        caches=[c.transpose(0, 1, 3, 4, 5, 2).reshape(DEPTH, nbs, 2, GROUP_WIDTH, -1) for c in caches],
        pool=state_pool, conv=state_conv, ffn=state_ffn_conv,
        pool_t=_pad_rows(state_pool.transpose(0, 2, 1, 3), 2),
        conv_t=_pad_rows(state_conv.transpose(0, 2, 1, 3), 2),
        ffn_t=_pad_rows(state_ffn_conv.transpose(0, 2, 1, 3), 2),
    )
    yp = x_prompt
    ys = _pad_rows(x_sample.reshape(1, nbs, D_MODEL), 1)
    st_p, st_s = [], []
    for l in range(DEPTH):
        yp, sp = _prompt_layer(yp, l, P)
        ys, ss = _sample_layer(ys, l, P, S, nbs)
        st_p.append(sp)
        st_s.append(ss)
    outs_p = [jnp.stack([s[k] for s in st_p]) for k in range(6)]
    outs_s = [jnp.stack([s[k] for s in st_s]) for k in range(6)]
    return (yp, ys[0, :nbs, None, :], *outs_p, *outs_s)
```
